```python
import math
import jax, jax.numpy as jnp
from jax import lax
import numpy as np

D_MODEL = 1024
BATCH = 8
SEQ = 2048
DEPTH = 4
DEC_BATCH = 32
DEC_SEQ = 4
PAST_LEN = 8192
PAGE_SIZE = 128

N_SSM_LAYERS = (DEPTH + 1) // 2
N_ATTN_LAYERS = DEPTH // 2
SSM_P = 16
SSM_GROUPS = D_MODEL // SSM_P
SSM_N = 64
SSM_DT_MIN = 1e-3
SSM_DT_MAX = 1e-1
N_HEADS = 8
HEAD_DIM = 64
WINDOWS = (128, 512, 2048)
DILATIONS = (1, 4, 16)
N_GROUPS = len(WINDOWS)
ROPE_DIM = HEAD_DIM // 4
ROPE_THETA = 500000.0
PEER_HEADS = 8
PEER_N_KEYS = 128
PEER_N_EXPERTS = PEER_N_KEYS * PEER_N_KEYS
PEER_D_KEY = 128
PEER_TOPK = 16
PEER_TOKEN_BLOCK = 128
EPS = 1e-6
NEG_INF = -1e30

kernel_name = 'hybrid_s5_dilated_attn_peer_step'


def rms_norm(x, gain=None):
    x32 = x.astype(jnp.float32)
    y = x32 * lax.rsqrt(jnp.mean(x32 * x32, axis=-1, keepdims=True) + EPS)
    if gain is not None:
        y = y * gain.astype(jnp.float32)
    return y.astype(x.dtype)


def adaln_params(c, w, b):
    return jnp.split(jax.nn.silu(c) @ w + b, 6, axis=-1)


def modulate(x, gain, shift, scale):
    return rms_norm(x, gain) * (1.0 + scale[:, None, :]) + shift[:, None, :]


def rope_partial(x, pos):
    half = ROPE_DIM // 2
    inv = ROPE_THETA ** (-jnp.arange(half, dtype=jnp.float32) / half)
    ang = pos.astype(jnp.float32)[:, None] * inv[None, :]
    shp = (1, pos.shape[0]) + (1,) * (x.ndim - 3) + (half,)
    cos, sin = jnp.cos(ang).reshape(shp), jnp.sin(ang).reshape(shp)
    x32 = x.astype(jnp.float32)
    x1, x2 = x32[..., :half], x32[..., half:ROPE_DIM]
    out = jnp.concatenate([x1 * cos - x2 * sin, x1 * sin + x2 * cos, x32[..., ROPE_DIM:]], axis=-1)
    return out.astype(x.dtype)


def softmax_lse(s):
    m = jnp.max(s, axis=-1, keepdims=True)
    p = jnp.exp(s - m)
    den = jnp.sum(p, axis=-1, keepdims=True)
    return p / den, (m + jnp.log(den))[..., 0]


def _linear_recurrence_combine(e1, e2):
    a1, b1 = e1
    a2, b2 = e2
    return a1 * a2, a2 * b1 + b2


def s5_mixer(h, h0, lam_re, lam_im, log_dt, b_re, b_im, c_re, c_im, d_skip, w_glu):
    f32 = jnp.float32
    Bt, S, D = h.shape
    lam = lax.complex(lam_re.astype(f32), lam_im.astype(f32))
    dt = jnp.exp(log_dt.astype(f32))[:, None]
    a_bar = jnp.exp(lam * dt)
    b_bar = ((a_bar - 1.0) / lam)[..., None] * lax.complex(b_re.astype(f32), b_im.astype(f32))
    c_mat = lax.complex(c_re.astype(f32), c_im.astype(f32))
    u = h.astype(f32).reshape(Bt, S, SSM_GROUPS, SSM_P)
    bu = jnp.einsum('bsgp,gnp->bsgn', u.astype(jnp.complex64), b_bar)
    bu = bu.at[:, 0].add(a_bar[None] * h0)
    a = jnp.broadcast_to(a_bar[None, None], (1, S, SSM_GROUPS, SSM_N))
    _, states = lax.associative_scan(_linear_recurrence_combine, (a, bu), axis=1)
    y = jnp.real(jnp.einsum('bsgn,gpn->bsgp', states, c_mat)) \
        + d_skip.astype(f32).reshape(SSM_GROUPS, SSM_P) * u
    z = y.reshape(Bt, S, D).astype(h.dtype) @ w_glu
    out = z[..., :D] * jax.nn.sigmoid(z[..., D:])
    last = states[:, -1]
    return out, jnp.stack([jnp.real(last), jnp.imag(last)], axis=1).astype(h.dtype)


def qkv_heads(h, pos, w_qkv, q_gain, k_gain):
    Bt, S, _ = h.shape
    qkv = (h @ w_qkv).reshape(Bt, S, N_GROUPS, 3, N_HEADS, HEAD_DIM)
    q = rope_partial(rms_norm(qkv[:, :, :, 0], q_gain), pos)
    k = rope_partial(rms_norm(qkv[:, :, :, 1], k_gain), pos)
    return q, k, qkv[:, :, :, 2]


def band_attention(q, k, v, dil, n_back):
    Bt, S, H, E = q.shape
    L = S // dil
    nb = -(-L // n_back)
    Lp = nb * n_back

    def sub(t):
        t = t.reshape(Bt, L, dil, H, E).transpose(0, 2, 1, 3, 4)
        return jnp.pad(t, ((0, 0), (0, 0), (0, Lp - L), (0, 0), (0, 0)))

    def band(t):
        t = jnp.pad(sub(t), ((0, 0), (0, 0), (n_back, 0), (0, 0), (0, 0)))
        t = t.reshape(Bt, dil, nb + 1, n_back, H, E)
        return jnp.concatenate([t[:, :, :-1], t[:, :, 1:]], axis=3)

    qs = sub(q).reshape(Bt, dil, nb, n_back, H, E)
    kb, vb = band(k), band(v)
    s = jnp.einsum('brnqhe,brnkhe->brnhqk', qs, kb, preferred_element_type=jnp.float32) * (E ** -0.5)
    iq = jnp.arange(n_back)[:, None]
    ik = jnp.arange(2 * n_back)[None, :]
    dist = iq + n_back - ik
    key_sub = jnp.arange(nb)[:, None, None] * n_back + ik - n_back
    mask = (dist >= 0) & (dist <= n_back) & (key_sub >= 0)
    s = jnp.where(mask[None, None, :, None], s, NEG_INF)
    p, lse = softmax_lse(s)
    o = jnp.einsum('brnhqk,brnkhe->brnqhe', p, vb.astype(jnp.float32))
    o = o.reshape(Bt, dil, Lp, H, E)[:, :, :L].transpose(0, 2, 1, 3, 4).reshape(Bt, S, H, E)
    lse = lse.transpose(0, 1, 2, 4, 3).reshape(Bt, dil, Lp, H)[:, :, :L]
    lse = lse.transpose(0, 2, 1, 3).reshape(Bt, S, H)
    return o, lse


def strided_gather_attention(q, k_new, v_new, k_buf, v_buf, dil, n_back):
    W, T, E = k_buf.shape[1], q.shape[1], q.shape[-1]
    k_ext = jnp.concatenate([k_buf, k_new.astype(k_buf.dtype)], axis=1)
    v_ext = jnp.concatenate([v_buf, v_new.astype(v_buf.dtype)], axis=1)
    idx = W + jnp.arange(T)[:, None] - dil * jnp.arange(n_back + 1)[None, :]
    valid = idx >= 0
    idx = jnp.maximum(idx, 0)
    kg, vg = k_ext[:, idx], v_ext[:, idx]
    s = jnp.einsum('bthe,btjhe->bhtj', q, kg.astype(q.dtype), preferred_element_type=jnp.float32) * (E ** -0.5)
    s = jnp.where(valid[None, None], s, NEG_INF)
    p, lse = softmax_lse(s)
    o = jnp.einsum('bhtj,btjhe->bthe', p, vg.astype(jnp.float32))
    return o, lse.transpose(0, 2, 1)


def merge_groups(outs, lses, w_o, dtype):
    wts = jax.nn.softmax(jnp.stack(lses, axis=0), axis=0)
    o = jnp.sum(wts[..., None] * jnp.stack(outs, axis=0), axis=0)
    Bt, S = o.shape[:2]
    return o.reshape(Bt, S, N_HEADS * HEAD_DIM).astype(dtype) @ w_o


def dilated_attn_prompt(h, pos, w_qkv, q_gain, k_gain, w_o):
    S = h.shape[1]
    q, k, v = qkv_heads(h, pos, w_qkv, q_gain, k_gain)
    outs, lses, kv_rows = [], [], []
    for g in range(N_GROUPS):
        o, l = band_attention(q[:, :, g], k[:, :, g], v[:, :, g], DILATIONS[g], WINDOWS[g] // DILATIONS[g])
        outs.append(o)
        lses.append(l)
        keep = min(WINDOWS[g], S)
        kv_rows.append(jnp.stack([k[:, S - keep:, g], v[:, S - keep:, g]], axis=1))
    return merge_groups(outs, lses, w_o, h.dtype), kv_rows


def dilated_attn_sample(h, pos, bufs, w_qkv, q_gain, k_gain, w_o):
    q, k, v = qkv_heads(h, pos, w_qkv, q_gain, k_gain)
    outs, lses, kv_rows = [], [], []
    for g in range(N_GROUPS):
        buf = bufs[g]
        o, l = strided_gather_attention(q[:, :, g], k[:, :, g], v[:, :, g], buf[:, 0], buf[:, 1],
                                        DILATIONS[g], WINDOWS[g] // DILATIONS[g])
        outs.append(o)
        lses.append(l)
        kv_rows.append(jnp.stack([k[:, :, g], v[:, :, g]], axis=1))
    return merge_groups(outs, lses, w_o, h.dtype), kv_rows


def peer_ffn(h, w_q, sub_keys, u_tab, v_tab):
    Bt, S, D = h.shape
    t = h.reshape(Bt * S, D)
    n = t.shape[0]
    q = rms_norm((t @ w_q).reshape(n, PEER_HEADS, 2, PEER_D_KEY // 2))
    s = jnp.einsum('thcd,hckd->thck', q, sub_keys, preferred_element_type=jnp.float32)
    v1, i1 = lax.top_k(s[:, :, 0], PEER_TOPK)
    v2, i2 = lax.top_k(s[:, :, 1], PEER_TOPK)
    cand = (v1[..., :, None] + v2[..., None, :]).reshape(n, PEER_HEADS, PEER_TOPK * PEER_TOPK)
    cidx = (i1[..., :, None] * PEER_N_KEYS + i2[..., None, :]).reshape(n, PEER_HEADS, PEER_TOPK * PEER_TOPK)
    best, sel = lax.top_k(cand, PEER_TOPK)
    eidx = jnp.take_along_axis(cidx, sel, axis=-1)
    gates = jax.nn.softmax(best, axis=-1)
    nb = -(-n // PEER_TOKEN_BLOCK)
    pad = nb * PEER_TOKEN_BLOCK - n
    tp = jnp.pad(t, ((0, pad), (0, 0))).reshape(nb, PEER_TOKEN_BLOCK, D)
    ep = jnp.pad(eidx, ((0, pad), (0, 0), (0, 0))).reshape(nb, PEER_TOKEN_BLOCK, PEER_HEADS, PEER_TOPK)
    gp = jnp.pad(gates, ((0, pad), (0, 0), (0, 0))).reshape(nb, PEER_TOKEN_BLOCK, PEER_HEADS, PEER_TOPK)

    def block(args):
        tb, eb, gb = args
        act = jax.nn.gelu(jnp.einsum('td,thkd->thk', tb, u_tab[eb], preferred_element_type=jnp.float32),
                          approximate=False)
        return jnp.einsum('thk,thkd->td', (gb * act).astype(tb.dtype), v_tab[eb])

    y = lax.map(block, (tp, ep, gp)).reshape(nb * PEER_TOKEN_BLOCK, D)[:n]
    return y.reshape(Bt, S, D)


def setup_inputs(seed: int = 0) -> dict:
    key = jax.random.key(seed)
    keys = iter(jax.random.split(key, 40))
    f32 = jnp.float32
    D = D_MODEL
    A = N_HEADS * HEAD_DIM

    def nrm(shape, scale=1.0):
        return jax.random.normal(next(keys), shape, f32) * scale

    n_buf = [min(w, PAST_LEN) for w in WINDOWS]
    lam_im0 = jnp.pi * jnp.arange(SSM_N, dtype=f32)
    kv_shape = lambda w: (N_ATTN_LAYERS, DEC_BATCH, 2, w, N_HEADS, HEAD_DIM)
    return {
        'x_prompt': nrm((BATCH, SEQ, D)),
        'x_sample': nrm((DEC_BATCH, DEC_SEQ, D)),
        'c_prompt': nrm((BATCH, D)),
        'c_sample': nrm((DEC_BATCH, D)),
        'state_ssm': nrm((N_SSM_LAYERS, DEC_BATCH, 2, SSM_GROUPS, SSM_N), 0.1),
        'cache_kv_w128': nrm(kv_shape(n_buf[0])),
        'cache_kv_w512': nrm(kv_shape(n_buf[1])),
        'cache_kv_w2048': nrm(kv_shape(n_buf[2])),
        'norm_mix': 1.0 + nrm((DEPTH, D), 0.05),
        'norm_ffn': 1.0 + nrm((DEPTH, D), 0.05),
        'ada_w': nrm((DEPTH, D, 6 * D), 0.5 * D ** -0.5),
        'ada_b': nrm((DEPTH, 6 * D), 0.02),
        'ssm_lam_re': -0.5 + nrm((N_SSM_LAYERS, SSM_GROUPS, SSM_N), 0.01),
        'ssm_lam_im': lam_im0 + nrm((N_SSM_LAYERS, SSM_GROUPS, SSM_N), 0.01),
        'ssm_log_dt': jax.random.uniform(next(keys), (N_SSM_LAYERS, SSM_GROUPS), f32,
                                         math.log(SSM_DT_MIN), math.log(SSM_DT_MAX)),
        'ssm_b_re': nrm((N_SSM_LAYERS, SSM_GROUPS, SSM_N, SSM_P), (2 * SSM_P) ** -0.5),
        'ssm_b_im': nrm((N_SSM_LAYERS, SSM_GROUPS, SSM_N, SSM_P), (2 * SSM_P) ** -0.5),
        'ssm_c_re': nrm((N_SSM_LAYERS, SSM_GROUPS, SSM_P, SSM_N), (2 * SSM_N) ** -0.5),
        'ssm_c_im': nrm((N_SSM_LAYERS, SSM_GROUPS, SSM_P, SSM_N), (2 * SSM_N) ** -0.5),
        'ssm_d': nrm((N_SSM_LAYERS, D)),
        'ssm_w_glu': nrm((N_SSM_LAYERS, D, 2 * D), D ** -0.5),
        'attn_w_qkv': nrm((N_ATTN_LAYERS, D, N_GROUPS * 3 * A), D ** -0.5),
        'attn_q_norm': 1.0 + nrm((N_ATTN_LAYERS, HEAD_DIM), 0.05),
        'attn_k_norm': 1.0 + nrm((N_ATTN_LAYERS, HEAD_DIM), 0.05),
        'attn_w_o': nrm((N_ATTN_LAYERS, A, D), A ** -0.5),
        'peer_w_q': nrm((DEPTH, D, PEER_HEADS * PEER_D_KEY), D ** -0.5),
        'peer_sub_keys': nrm((DEPTH, PEER_HEADS, 2, PEER_N_KEYS, PEER_D_KEY // 2), (PEER_D_KEY // 2) ** -0.5),
        'peer_u': nrm((DEPTH, PEER_N_EXPERTS, D), D ** -0.5),
        'peer_v': nrm((DEPTH, PEER_N_EXPERTS, D), 0.5),
    }


def reference(x_prompt, x_sample, c_prompt, c_sample, state_ssm, cache_kv_w128, cache_kv_w512,
              cache_kv_w2048, norm_mix, norm_ffn, ada_w, ada_b, ssm_lam_re, ssm_lam_im, ssm_log_dt,
              ssm_b_re, ssm_b_im, ssm_c_re, ssm_c_im, ssm_d, ssm_w_glu, attn_w_qkv, attn_q_norm,
              attn_k_norm, attn_w_o, peer_w_q, peer_sub_keys, peer_u, peer_v):
    f32 = jnp.float32
    S, T = x_prompt.shape[1], x_sample.shape[1]
    pos_p = jnp.arange(S, dtype=jnp.int32)
    pos_s = PAST_LEN + jnp.arange(T, dtype=jnp.int32)
    sample_bufs = (cache_kv_w128, cache_kv_w512, cache_kv_w2048)
    h0_p = jnp.zeros((x_prompt.shape[0], SSM_GROUPS, SSM_N), jnp.complex64)
    xp, xs = x_prompt, x_sample
    ssm_new_p, ssm_new_s = [], []
    kv_new_p = [[] for _ in range(N_GROUPS)]
    kv_new_s = [[] for _ in range(N_GROUPS)]
    for i in range(DEPTH):
        j = i // 2
        mp = adaln_params(c_prompt, ada_w[i], ada_b[i])
        ms = adaln_params(c_sample, ada_w[i], ada_b[i])
        hp = modulate(xp, norm_mix[i], mp[0], mp[1])
        hs = modulate(xs, norm_mix[i], ms[0], ms[1])
        if i % 2 == 0:
            sw = (ssm_lam_re[j], ssm_lam_im[j], ssm_log_dt[j], ssm_b_re[j], ssm_b_im[j],
                  ssm_c_re[j], ssm_c_im[j], ssm_d[j], ssm_w_glu[j])
            h0_s = lax.complex(state_ssm[j, :, 0].astype(f32), state_ssm[j, :, 1].astype(f32))
            op, st_p = s5_mixer(hp, h0_p, *sw)
            os_, st_s = s5_mixer(hs, h0_s, *sw)
            ssm_new_p.append(st_p)
            ssm_new_s.append(st_s)
        else:
            aw = (attn_w_qkv[j], attn_q_norm[j], attn_k_norm[j], attn_w_o[j])
            op, kv_p = dilated_attn_prompt(hp, pos_p, *aw)
            os_, kv_s = dilated_attn_sample(hs, pos_s, [b[j] for b in sample_bufs], *aw)
            for g in range(N_GROUPS):
                kv_new_p[g].append(kv_p[g])
                kv_new_s[g].append(kv_s[g])
        xp = xp + mp[2][:, None, :] * op
        xs = xs + ms[2][:, None, :] * os_
        pw = (peer_w_q[i], peer_sub_keys[i], peer_u[i], peer_v[i])
        xp = xp + mp[5][:, None, :] * peer_ffn(modulate(xp, norm_ffn[i], mp[3], mp[4]), *pw)
        xs = xs + ms[5][:, None, :] * peer_ffn(modulate(xs, norm_ffn[i], ms[3], ms[4]), *pw)
    return (xp, xs,
            jnp.stack(ssm_new_p), jnp.stack(kv_new_p[0]), jnp.stack(kv_new_p[1]), jnp.stack(kv_new_p[2]),
            jnp.stack(ssm_new_s), jnp.stack(kv_new_s[0]), jnp.stack(kv_new_s[1]), jnp.stack(kv_new_s[2]))
```

```python
import functools
import math

import jax
import jax.numpy as jnp
import numpy as np
from jax import lax
from jax.experimental import pallas as pl
from jax.experimental.pallas import tpu as pltpu

F32 = jnp.float32
BF16 = jnp.bfloat16

D_MODEL = 1024
DEPTH = 4
PAST_LEN = 8192
EPS = 1e-6
NEG_INF = -1e30

SSM_P = 16
SSM_GROUPS = D_MODEL // SSM_P
SSM_N = 64
SSM_GB = 4
SSM_GB_CH = D_MODEL // SSM_GB
SSM_GB_ST = SSM_GROUPS * SSM_N // SSM_GB
SSM_STATE_W = 2 * SSM_GROUPS * SSM_N

N_HEADS = 8
HEAD_DIM = 64
ATT = N_HEADS * HEAD_DIM
WINDOWS = (128, 512, 2048)
DILATIONS = (1, 4, 16)
N_GROUPS = 3
N_BACK = 128
ROPE_DIM = HEAD_DIM // 4
ROPE_THETA = 500000.0

PEER_HEADS = 8
PEER_N_KEYS = 128
PEER_HALF = 64
PEER_TOPK = 16
PEER_SEL_TOK = 128
PEER_EXP_TILE = 1024
PEER_I1_TILE = PEER_EXP_TILE // PEER_N_KEYS

VMEM_LIMIT = 56 * 1024 * 1024


def _cparams(sem):
    return pltpu.CompilerParams(dimension_semantics=sem, vmem_limit_bytes=VMEM_LIMIT)


def _modulate(x, gain, shift, scale):
    ms = jnp.mean(x * x, axis=-1, keepdims=True)
    return x * lax.rsqrt(ms + EPS) * gain * (1.0 + scale) + shift


def _seg_mean(x2, ones_bd):
    hi = x2.astype(BF16)
    lo = (x2 - hi.astype(F32)).astype(BF16)
    s = jnp.dot(hi, ones_bd, preferred_element_type=F32) + jnp.dot(lo, ones_bd, preferred_element_type=F32)
    return s * (1.0 / PEER_HALF)


def _ones_blockdiag(n):
    seg = np.arange(n) // 64
    return jnp.asarray((seg[:, None] == seg[None, :]).astype(np.float32), dtype=BF16)


def _adaln_kernel(c_ref, w_ref, b_ref, o_ref):
    s = jax.nn.silu(c_ref[...])
    o_ref[0] = jnp.dot(s.astype(BF16), w_ref[0].astype(BF16), preferred_element_type=F32) + b_ref[0]


def _adaln(c_all, ada_w, ada_b):
    nb = c_all.shape[0]
    nt = 6 * D_MODEL // 1024
    return pl.pallas_call(
        _adaln_kernel,
        grid=(DEPTH, nt),
        in_specs=[
            pl.BlockSpec((nb, D_MODEL), lambda l, j: (0, 0)),
            pl.BlockSpec((1, D_MODEL, 1024), lambda l, j: (l, 0, j)),
            pl.BlockSpec((1, 1, 1024), lambda l, j: (l, 0, j)),
        ],
        out_specs=pl.BlockSpec((1, nb, 1024), lambda l, j: (l, 0, j)),
        out_shape=jax.ShapeDtypeStruct((DEPTH, nb, 6 * D_MODEL), F32),
        compiler_params=_cparams(("parallel", "parallel")),
        name="adaln",
    )(c_all, ada_w, ada_b.reshape(DEPTH, 1, 6 * D_MODEL))


def _s5_kernel(x_ref, shift_ref, scale_ref, gate_ref, gain_ref, h0_ref, bm_ref, cm_ref, are_ref, aim_ref,
               d_ref, wglu_ref, xo_ref, st_ref, bu_scr, state_scr, *, n_batch, n_time):
    i = pl.program_id(0)

    @pl.when(i == 0)
    def _():
        state_scr[...] = h0_ref[...]

    x3 = x_ref[...]
    a0, a1, _ = x3.shape
    rows = a0 * a1
    h3 = _modulate(x3, gain_ref[...], shift_ref[...], scale_ref[...])
    u = h3.reshape(rows, D_MODEL)
    ub = u.astype(BF16)
    for gb in range(SSM_GB):
        bu_scr[:, gb * 2 * SSM_GB_ST:(gb + 1) * 2 * SSM_GB_ST] = jnp.dot(
            ub[:, gb * SSM_GB_CH:(gb + 1) * SSM_GB_CH], bm_ref[gb], preferred_element_type=F32)

    for bb in range(n_batch // 8):
        for gb in range(SSM_GB):
            c_re = gb * 2 * SSM_GB_ST
            c_im = c_re + SSM_GB_ST
            ar = jnp.broadcast_to(are_ref[gb], (8, SSM_GB_ST))
            ai = jnp.broadcast_to(aim_ref[gb], (8, SSM_GB_ST))
            xr = state_scr[bb * 8:(bb + 1) * 8, c_re:c_re + SSM_GB_ST]
            xi = state_scr[bb * 8:(bb + 1) * 8, c_im:c_im + SSM_GB_ST]
            for t in range(n_time):
                rsel = pl.ds(t * n_batch + bb * 8, 8)
                br = bu_scr[rsel, c_re:c_re + SSM_GB_ST]
                bi = bu_scr[rsel, c_im:c_im + SSM_GB_ST]
                nr = ar * xr - ai * xi + br
                ni = ar * xi + ai * xr + bi
                bu_scr[rsel, c_re:c_re + SSM_GB_ST] = nr
                bu_scr[rsel, c_im:c_im + SSM_GB_ST] = ni
                xr, xi = nr, ni
            state_scr[bb * 8:(bb + 1) * 8, c_re:c_re + SSM_GB_ST] = xr
            state_scr[bb * 8:(bb + 1) * 8, c_im:c_im + SSM_GB_ST] = xi

    ys = []
    for gb in range(SSM_GB):
        st = bu_scr[:, gb * 2 * SSM_GB_ST:(gb + 1) * 2 * SSM_GB_ST].astype(BF16)
        ys.append(jnp.dot(st, cm_ref[gb], preferred_element_type=F32))
    y = jnp.concatenate(ys, axis=-1) + d_ref[...] * u
    z = jnp.dot(y.astype(BF16), wglu_ref[...], preferred_element_type=F32)
    out = z[:, :D_MODEL] * jax.nn.sigmoid(z[:, D_MODEL:])
    xo_ref[...] = x3 + gate_ref[...] * out.reshape(a0, a1, D_MODEL)

    @pl.when(i == pl.num_programs(0) - 1)
    def _():
        st_ref[...] = state_scr[...]


def _s5_weights(lam_re, lam_im, log_dt, b_re, b_im, c_re, c_im):
    dt = jnp.exp(log_dt)[:, None]
    mag = jnp.exp(lam_re * dt)
    a_re = mag * jnp.cos(lam_im * dt)
    a_im = mag * jnp.sin(lam_im * dt)
    den = lam_re * lam_re + lam_im * lam_im
    f_re = ((a_re - 1.0) * lam_re + a_im * lam_im) / den
    f_im = (a_im * lam_re - (a_re - 1.0) * lam_im) / den
    bb_re = f_re[..., None] * b_re - f_im[..., None] * b_im
    bb_im = f_re[..., None] * b_im + f_im[..., None] * b_re
    eye = jnp.eye(16, dtype=F32)

    def in_blk(b):
        b4 = b.reshape(SSM_GB, 16, SSM_N, SSM_P)
        return jnp.einsum('bgnp,gh->bgphn', b4, eye).reshape(SSM_GB, SSM_GB_CH, SSM_GB_ST)

    def out_blk(c):
        c4 = c.reshape(SSM_GB, 16, SSM_P, SSM_N)
        return jnp.einsum('bgpn,gh->bgnhp', c4, eye).reshape(SSM_GB, SSM_GB_ST, SSM_GB_CH)

    bm = jnp.concatenate([in_blk(bb_re), in_blk(bb_im)], axis=2).astype(BF16)
    cm = jnp.concatenate([out_blk(c_re), -out_blk(c_im)], axis=1).astype(BF16)
    return bm, cm, a_re.reshape(SSM_GB, 1, SSM_GB_ST), a_im.reshape(SSM_GB, 1, SSM_GB_ST)


def _state_to_rows(st):
    b = st.shape[0]
    return st.reshape(b, 2, SSM_GB, SSM_GB_ST).transpose(0, 2, 1, 3).reshape(b, SSM_STATE_W)


def _rows_to_state(rows):
    b = rows.shape[0]
    return rows.reshape(b, SSM_GB, 2, SSM_GB_ST).transpose(0, 2, 1, 3).reshape(b, 2, SSM_GROUPS, SSM_N)


def _s5_layer(x3, shift, scale, gate, gain, h0_rows, weights, d_skip, w_glu, *, t_chunk):
    bm, cm, a_re, a_im = weights
    seq, n_batch = x3.shape[0], x3.shape[1]
    n_time = t_chunk
    blk = (n_time, n_batch, D_MODEL)
    xmap = lambda i: (i, 0, 0)
    mblk = (1, n_batch, D_MODEL)
    grid = seq // n_time
    rows = n_batch * n_time
    const3 = lambda i: (0, 0, 0)
    const2 = lambda i: (0, 0)
    kern = functools.partial(_s5_kernel, n_batch=n_batch, n_time=n_time)
    return pl.pallas_call(
        kern,
        grid=(grid,),
        in_specs=[
            pl.BlockSpec(blk, xmap),
            pl.BlockSpec(mblk, const3), pl.BlockSpec(mblk, const3), pl.BlockSpec(mblk, const3),
            pl.BlockSpec((1, 1, D_MODEL), const3),
            pl.BlockSpec((n_batch, SSM_STATE_W), const2),
            pl.BlockSpec(bm.shape, const3), pl.BlockSpec(cm.shape, const3),
            pl.BlockSpec(a_re.shape, const3), pl.BlockSpec(a_im.shape, const3),
            pl.BlockSpec((1, D_MODEL), const2),
            pl.BlockSpec((D_MODEL, 2 * D_MODEL), const2),
        ],
        out_specs=[pl.BlockSpec(blk, xmap), pl.BlockSpec((n_batch, SSM_STATE_W), const2)],
        out_shape=[jax.ShapeDtypeStruct(x3.shape, F32), jax.ShapeDtypeStruct((n_batch, SSM_STATE_W), F32)],
        scratch_shapes=[pltpu.VMEM((rows, SSM_STATE_W), F32), pltpu.VMEM((n_batch, SSM_STATE_W), F32)],
        compiler_params=_cparams(("arbitrary",)),
        name="s5_layer",
    )(x3, shift, scale, gate, gain.reshape(1, 1, D_MODEL), h0_rows, bm, cm, a_re, a_im,
      d_skip.reshape(1, D_MODEL), w_glu)


def _qkv_kernel(x_ref, shift_ref, scale_ref, gain_ref, w_ref, qg_ref, kg_ref, rc_ref, rs1_ref, rs2_ref,
                ones_ref, q_ref, k_ref, v_ref):
    h = _modulate(x_ref[0], gain_ref[...], shift_ref[0], scale_ref[0])
    qkv = jnp.dot(h.astype(BF16), w_ref[...], preferred_element_type=F32)
    rc, rs1, rs2 = rc_ref[...], rs1_ref[...], rs2_ref[...]
    ones_bd = ones_ref[...]

    def norm_rope(t, g):
        tn = t * lax.rsqrt(_seg_mean(t * t, ones_bd) + EPS) * g
        return tn * rc + pltpu.roll(tn, ATT - ROPE_DIM // 2, 1) * rs1 + pltpu.roll(tn, ROPE_DIM // 2, 1) * rs2

    q_ref[0] = norm_rope(qkv[:, :ATT], qg_ref[...])
    k_ref[0] = norm_rope(qkv[:, ATT:2 * ATT], kg_ref[...])
    v_ref[0] = qkv[:, 2 * ATT:]


def _rope_tables(pos):
    half = ROPE_DIM // 2
    inv = ROPE_THETA ** (-jnp.arange(half, dtype=F32) / half)
    ang = pos.astype(F32)[:, None] * inv[None, :]
    cos, sin = jnp.cos(ang), jnp.sin(ang)
    lane = np.arange(ATT) % HEAD_DIM
    fidx = lane % half
    first = jnp.asarray(lane < half)
    second = jnp.asarray((lane >= half) & (lane < ROPE_DIM))
    cl, sl = cos[:, fidx], sin[:, fidx]
    rc = jnp.where(first | second, cl, 1.0)
    rs1 = jnp.where(first, -sl, 0.0)
    rs2 = jnp.where(second, sl, 0.0)
    return rc, rs1, rs2


def _qkv(x3, shift, scale, gain, w_qkv_bf, q_gain, k_gain, tables, ones_bd, *, t_tile):
    bq, sq, _ = x3.shape
    mrows = shift.shape[1]
    mt = t_tile if mrows == sq else 1
    mmap = (lambda b, i, g: (b, i, 0)) if mrows == sq else (lambda b, i, g: (b, 0, 0))
    rc, rs1, rs2 = tables
    qg = jnp.tile(q_gain, N_HEADS).reshape(1, ATT)
    kg = jnp.tile(k_gain, N_HEADS).reshape(1, ATT)
    tmap = lambda b, i, g: (i, 0)
    cmap = lambda b, i, g: (0, 0)
    omap = lambda b, i, g: (b, i, g)
    oshape = jax.ShapeDtypeStruct((bq, sq, N_GROUPS * ATT), F32)
    return pl.pallas_call(
        _qkv_kernel,
        grid=(bq, sq // t_tile, N_GROUPS),
        in_specs=[
            pl.BlockSpec((1, t_tile, D_MODEL), lambda b, i, g: (b, i, 0)),
            pl.BlockSpec((1, mt, D_MODEL), mmap), pl.BlockSpec((1, mt, D_MODEL), mmap),
            pl.BlockSpec((1, D_MODEL), cmap),
            pl.BlockSpec((D_MODEL, 3 * ATT), lambda b, i, g: (0, g)),
            pl.BlockSpec((1, ATT), cmap), pl.BlockSpec((1, ATT), cmap),
            pl.BlockSpec((t_tile, ATT), tmap), pl.BlockSpec((t_tile, ATT), tmap), pl.BlockSpec((t_tile, ATT), tmap),
            pl.BlockSpec((ATT, ATT), cmap),
        ],
        out_specs=[pl.BlockSpec((1, t_tile, ATT), omap)] * 3,
        out_shape=[oshape] * 3,
        compiler_params=_cparams(("parallel", "parallel", "arbitrary")),
        name="qkv_proj",
    )(x3, shift, scale, gain.reshape(1, D_MODEL), w_qkv_bf, qg, kg, rc, rs1, rs2, ones_bd)


def _band_kernel(q_ref, kp_ref, kc_ref, vp_ref, vc_ref, o_ref, l_ref):
    i = pl.program_id(2)
    q = q_ref[0].astype(BF16)
    k = jnp.concatenate([kp_ref[0], kc_ref[0]], axis=0).astype(BF16)
    v = jnp.concatenate([vp_ref[0], vc_ref[0]], axis=0).astype(BF16)
    iq = lax.broadcasted_iota(jnp.int32, (N_BACK, 2 * N_BACK), 0)
    ik = lax.broadcasted_iota(jnp.int32, (N_BACK, 2 * N_BACK), 1)
    dist = iq + N_BACK - ik
    mask = (dist >= 0) & (dist <= N_BACK) & ((ik >= N_BACK) | (i > 0))
    for h in range(N_HEADS):
        sl = slice(h * HEAD_DIM, (h + 1) * HEAD_DIM)
        s = lax.dot_general(q[:, sl], k[:, sl], (((1,), (1,)), ((), ())),
                            preferred_element_type=F32) * (HEAD_DIM ** -0.5)
        s = jnp.where(mask, s, NEG_INF)
        m = jnp.max(s, axis=-1, keepdims=True)
        p = jnp.exp(s - m)
        den = jnp.sum(p, axis=-1, keepdims=True)
        o = jnp.dot(p.astype(BF16), v[:, sl], preferred_element_type=F32) / den
        o_ref[0, :, sl] = o
        l_ref[0, :, sl] = jnp.broadcast_to(m + jnp.log(den), (N_BACK, HEAD_DIM))


def _band_attention(q, k, v, g):
    b, s, w = q.shape
    d = DILATIONS[g]
    sub = s // d
    nb = sub // N_BACK
    ncol = w // ATT
    qv, kv, vv = (t.reshape(b, sub, d * w) for t in (q, k, v))
    cur = lambda bi, r, i: (bi, i, r * ncol + g)
    prev = lambda bi, r, i: (bi, jnp.maximum(i - 1, 0), r * ncol + g)
    omap = lambda bi, r, i: (bi, i, r)
    blk = (1, N_BACK, ATT)
    oshape = jax.ShapeDtypeStruct((b, sub, d * ATT), F32)
    o, l = pl.pallas_call(
        _band_kernel,
        grid=(b, d, nb),
        in_specs=[pl.BlockSpec(blk, cur), pl.BlockSpec(blk, prev), pl.BlockSpec(blk, cur),
                  pl.BlockSpec(blk, prev), pl.BlockSpec(blk, cur)],
        out_specs=[pl.BlockSpec(blk, omap), pl.BlockSpec(blk, omap)],
        out_shape=[oshape, oshape],
        compiler_params=_cparams(("parallel", "parallel", "arbitrary")),
        name="band_attention",
    )(qv, kv, kv, vv, vv)
    return o.reshape(b, s, ATT), l.reshape(b, s, ATT)


def _cache_attn_kernel(q_ref, k_ref, v_ref, c0_ref, c1_ref, c2_ref, *out_refs):
    n_new = q_ref.shape[1]
    rows = n_new * N_HEADS
    lane_head = lax.broadcasted_iota(jnp.int32, (N_HEADS, ATT), 1) // HEAD_DIM
    head_mask = (lane_head == lax.broadcasted_iota(jnp.int32, (N_HEADS, ATT), 0)).astype(F32)
    row_t = lax.broadcasted_iota(jnp.int32, (rows, 1), 0) // N_HEADS
    caches = (c0_ref, c1_ref, c2_ref)
    for g in range(N_GROUPS):
        d, win = DILATIONS[g], WINDOWS[g]
        gs = slice(g * ATT, (g + 1) * ATT)
        q = q_ref[0][:, gs]
        kn = k_ref[0][:, gs]
        vn = v_ref[0][:, gs]
        kc = caches[g][0, 0, 0].astype(BF16)
        vc = caches[g][0, 0, 1].astype(BF16)
        qbd = jnp.concatenate([q[t:t + 1, :] * head_mask for t in range(n_new)], axis=0)
        s_c = lax.dot_general(qbd.astype(BF16), kc, (((1,), (1,)), ((), ())),
                              preferred_element_type=F32) * (HEAD_DIM ** -0.5)
        col = lax.broadcasted_iota(jnp.int32, (rows, win), 1)
        valid_c = (col >= row_t) & (((col - row_t) & (d - 1)) == 0)
        s_c = jnp.where(valid_c, s_c, NEG_INF)
        s_n = []
        for t2 in range(n_new):
            sn = jnp.sum(qbd * kn[t2:t2 + 1, :], axis=-1, keepdims=True) * (HEAD_DIM ** -0.5)
            valid_n = (row_t >= t2) & (((row_t - t2) & (d - 1)) == 0)
            s_n.append(jnp.where(valid_n, sn, NEG_INF))
        m = jnp.max(s_c, axis=-1, keepdims=True)
        for sn in s_n:
            m = jnp.maximum(m, sn)
        p_c = jnp.exp(s_c - m)
        den = jnp.sum(p_c, axis=-1, keepdims=True)
        o = jnp.dot(p_c.astype(BF16), vc, preferred_element_type=F32)
        for t2 in range(n_new):
            p_n = jnp.exp(s_n[t2] - m)
            den = den + p_n
            o = o + p_n * vn[t2:t2 + 1, :]
        o = o / den
        lse = m + jnp.log(den)
        for t in range(n_new):
            rs = slice(t * N_HEADS, (t + 1) * N_HEADS)
            out_refs[g][0, t:t + 1, :] = jnp.sum(o[rs] * head_mask, axis=0, keepdims=True)
            out_refs[N_GROUPS + g][0, t:t + 1, :] = jnp.sum(lse[rs] * head_mask, axis=0, keepdims=True)


def _cache_attention(q, k, v, caches, layer):
    bd, t, w = q.shape
    xmap = lambda b: (b, 0, 0)
    cmap = lambda b: (layer, b, 0, 0, 0)
    oshape = jax.ShapeDtypeStruct((bd, t, ATT), F32)
    return pl.pallas_call(
        _cache_attn_kernel,
        grid=(bd,),
        in_specs=[pl.BlockSpec((1, t, w), xmap)] * 3
        + [pl.BlockSpec((1, 1, 2, WINDOWS[g], ATT), cmap) for g in range(N_GROUPS)],
        out_specs=[pl.BlockSpec((1, t, ATT), xmap)] * (2 * N_GROUPS),
        out_shape=[oshape] * (2 * N_GROUPS),
        compiler_params=_cparams(("parallel",)),
        name="cache_attention",
    )(q, k, v, *caches)


def _merge_kernel(o0, o1, o2, l0, l1, l2, x_ref, gate_ref, wo_ref, xo_ref):
    la, lb, lc = l0[0], l1[0], l2[0]
    m = jnp.maximum(jnp.maximum(la, lb), lc)
    ea, eb, ec = jnp.exp(la - m), jnp.exp(lb - m), jnp.exp(lc - m)
    den = ea + eb + ec
    o = (ea / den) * o0[0] + (eb / den) * o1[0] + (ec / den) * o2[0]
    out = jnp.dot(o.astype(BF16), wo_ref[...], preferred_element_type=F32)
    xo_ref[0] = x_ref[0] + gate_ref[0] * out


def _merge(outs, lses, x3, gate, w_o_bf, *, t_tile):
    bq, sq, _ = x3.shape
    mrows = gate.shape[1]
    mt = t_tile if mrows == sq else 1
    mmap = (lambda b, i: (b, i, 0)) if mrows == sq else (lambda b, i: (b, 0, 0))
    tmap = lambda b, i: (b, i, 0)
    ablk = pl.BlockSpec((1, t_tile, ATT), tmap)
    return pl.pallas_call(
        _merge_kernel,
        grid=(bq, sq // t_tile),
        in_specs=[ablk] * 6 + [pl.BlockSpec((1, t_tile, D_MODEL), tmap), pl.BlockSpec((1, mt, D_MODEL), mmap),
                               pl.BlockSpec((ATT, D_MODEL), lambda b, i: (0, 0))],
        out_specs=pl.BlockSpec((1, t_tile, D_MODEL), tmap),
        out_shape=jax.ShapeDtypeStruct(x3.shape, F32),
        compiler_params=_cparams(("parallel", "parallel")),
        name="attn_merge",
    )(*outs, *lses, x3, gate, w_o_bf)


def _top16(s):
    vals = s
    rank = jnp.full(s.shape, float(PEER_N_KEYS), F32)
    tops = []
    for r in range(PEER_TOPK):
        m = jnp.max(vals, axis=0, keepdims=True)
        hit = vals == m
        rank = jnp.where(hit, float(r), rank)
        vals = jnp.where(hit, -jnp.inf, vals)
        tops.append(m)
    return tops, rank


def _peer_route_kernel(x_ref, shift_ref, scale_ref, gain_ref, wq_ref, sk_ref, ones_ref,
                       xt_ref, r1_ref, c1_ref, n2_ref, f2_ref, s_scr):
    hm = _modulate(x_ref[...], gain_ref[...], shift_ref[0], scale_ref[0])
    hb = hm.astype(BF16)
    xt_ref[...] = hm.T.astype(BF16)
    q = jnp.dot(hb, wq_ref[...], preferred_element_type=F32)
    qn = q * lax.rsqrt(_seg_mean(q * q, ones_ref[...]) + EPS)
    s_scr[...] = lax.dot_general(sk_ref[...], qn.astype(BF16), (((1,), (1,)), ((), ())),
                                 preferred_element_type=F32)

    def head(h, carry):
        base = pl.multiple_of(h * 2 * PEER_N_KEYS, 2 * PEER_N_KEYS)
        s1 = s_scr[pl.ds(base, PEER_N_KEYS), :]
        s2 = s_scr[pl.ds(base + PEER_N_KEYS, PEER_N_KEYS), :]
        top1, rank1 = _top16(s1)
        top2, rank2 = _top16(s2)
        row = lax.broadcasted_iota(jnp.int32, (PEER_TOPK, s1.shape[1]), 0)
        v1 = jnp.zeros((PEER_TOPK, s1.shape[1]), F32)
        for r in range(PEER_TOPK):
            v1 = jnp.where(row == r, top1[r], v1)
        cands = [v1[:8] + top2[b] for b in range(PEER_TOPK)] + [v1[8:] + top2[0]]
        best = top1[0] + top2[0]
        taken = [jnp.zeros_like(c) for c in cands]
        zsum = jnp.zeros_like(best)
        for _ in range(PEER_TOPK):
            m = cands[0]
            for c in cands[1:]:
                m = jnp.maximum(m, c)
            m = jnp.max(m, axis=0, keepdims=True)
            for j in range(len(cands)):
                hit = cands[j] == m
                taken[j] = jnp.where(hit, 1.0, taken[j])
                cands[j] = jnp.where(hit, -jnp.inf, cands[j])
            zsum = zsum + jnp.exp(m - best)
        cnt_lo = taken[0]
        for j in range(1, PEER_TOPK):
            cnt_lo = cnt_lo + taken[j]
        cnt = jnp.concatenate([cnt_lo, taken[PEER_TOPK]], axis=0)
        n2 = jnp.zeros_like(rank2)
        for a in range(PEER_TOPK):
            n2 = n2 + jnp.where(rank2 < cnt[a:a + 1], 1.0, 0.0)
        r1_ref[h] = rank1
        c1_ref[h] = jnp.exp(s1 - top1[0])
        n2_ref[h] = n2
        f2_ref[h] = jnp.exp(s2 - top2[0]) / zsum
        return carry

    lax.fori_loop(0, PEER_HEADS, head, 0)


def _peer_route(x2, shift, scale, gain, wq_bf, sk_bd, ones_bd):
    n = x2.shape[0]
    tt = PEER_SEL_TOK
    nt = n // tt
    per_mod = nt // shift.shape[0]
    mrows = shift.shape[1]
    if mrows == 1:
        mblk, mmap = (1, 1, D_MODEL), (lambda i: (i // per_mod, 0, 0))
    else:
        mblk, mmap = (1, tt, D_MODEL), (lambda i: (0, i, 0))
    cmap = lambda i: (0, 0)
    hmap = lambda i: (0, 0, i)
    hshape = jax.ShapeDtypeStruct((PEER_HEADS, PEER_N_KEYS, n), F32)
    hblk = pl.BlockSpec((PEER_HEADS, PEER_N_KEYS, tt), hmap)
    return pl.pallas_call(
        _peer_route_kernel,
        grid=(nt,),
        in_specs=[
            pl.BlockSpec((tt, D_MODEL), lambda i: (i, 0)),
            pl.BlockSpec(mblk, mmap), pl.BlockSpec(mblk, mmap),
            pl.BlockSpec((1, D_MODEL), cmap),
            pl.BlockSpec((D_MODEL, D_MODEL), cmap),
            pl.BlockSpec((2 * PEER_HEADS * PEER_N_KEYS, D_MODEL), cmap),
            pl.BlockSpec((D_MODEL, D_MODEL), cmap),
        ],
        out_specs=[pl.BlockSpec((D_MODEL, tt), lambda i: (0, i)), hblk, hblk, hblk, hblk],
        out_shape=[jax.ShapeDtypeStruct((D_MODEL, n), BF16), hshape, hshape, hshape, hshape],
        scratch_shapes=[pltpu.VMEM((2 * PEER_HEADS * PEER_N_KEYS, tt), F32)],
        compiler_params=_cparams(("parallel",)),
        name="peer_route",
    )(x2, shift, scale, gain.reshape(1, D_MODEL), wq_bf, sk_bd, ones_bd)


def _gelu(x):
    return 0.5 * x * (1.0 + lax.erf(x * np.float32(math.sqrt(0.5))))


def _peer_expert_kernel(xt_ref, u_ref, vt_ref, r1_ref, c1_ref, n2_ref, f2_ref, x_ref, gate_ref, xo_ref,
                        act_scr, w_scr, acc_scr, *, t_tile):
    e = pl.program_id(1)

    @pl.when(e == 0)
    def _():
        acc_scr[...] = jnp.zeros_like(acc_scr)

    act_scr[...] = jnp.dot(u_ref[...], xt_ref[...], preferred_element_type=F32)

    for lc in range(t_tile // 128):
        ls = slice(lc * 128, (lc + 1) * 128)

        def i1_body(i1, carry):
            r1_all = r1_ref[i1, :, ls]
            c1_all = c1_ref[i1, :, ls]
            r1 = [r1_all[h:h + 1] for h in range(PEER_HEADS)]
            c1 = [c1_all[h:h + 1] for h in range(PEER_HEADS)]
            row0 = pl.multiple_of(i1 * PEER_N_KEYS, PEER_N_KEYS)
            for j in range(PEER_N_KEYS // 16):
                js = slice(j * 16, (j + 1) * 16)
                g = jnp.zeros((16, 128), F32)
                for h in range(PEER_HEADS):
                    g = g + jnp.where(n2_ref[h, js, ls] > r1[h], f2_ref[h, js, ls], 0.0) * c1[h]
                a = act_scr[pl.ds(row0 + j * 16, 16), ls]
                w_scr[pl.ds(row0 + j * 16, 16), ls] = (g * _gelu(a)).astype(BF16)
            return carry

        lax.fori_loop(0, PEER_I1_TILE, i1_body, 0)

    acc_scr[...] += jnp.dot(vt_ref[...], w_scr[...], preferred_element_type=F32)

    @pl.when(e == pl.num_programs(1) - 1)
    def _():
        xo_ref[...] = x_ref[...] + gate_ref[0] * acc_scr[...].T


def _peer_expert(xt, u_bf, vt_bf, r1, c1, n2, f2, x2, gate, *, t_tile):
    n = x2.shape[0]
    nt = n // t_tile
    ne = u_bf.shape[0] // PEER_EXP_TILE
    per_mod = max(nt // gate.shape[0], 1)
    mrows = gate.shape[1]
    if mrows == 1:
        mblk, mmap = (1, 1, D_MODEL), (lambda i, e: (i // per_mod, 0, 0))
    else:
        mblk, mmap = (1, t_tile, D_MODEL), (lambda i, e: (0, i, 0))
    i1blk = pl.BlockSpec((PEER_I1_TILE, PEER_HEADS, t_tile), lambda i, e: (e, 0, i))
    i2blk = pl.BlockSpec((PEER_HEADS, PEER_N_KEYS, t_tile), lambda i, e: (0, 0, i))
    kern = functools.partial(_peer_expert_kernel, t_tile=t_tile)
    return pl.pallas_call(
        kern,
        grid=(nt, ne),
        in_specs=[
            pl.BlockSpec((D_MODEL, t_tile), lambda i, e: (0, i)),
            pl.BlockSpec((PEER_EXP_TILE, D_MODEL), lambda i, e: (e, 0)),
            pl.BlockSpec((D_MODEL, PEER_EXP_TILE), lambda i, e: (0, e)),
            i1blk, i1blk, i2blk, i2blk,
            pl.BlockSpec((t_tile, D_MODEL), lambda i, e: (i, 0)),
            pl.BlockSpec(mblk, mmap),
        ],
        out_specs=pl.BlockSpec((t_tile, D_MODEL), lambda i, e: (i, 0)),
        out_shape=jax.ShapeDtypeStruct(x2.shape, F32),
        scratch_shapes=[pltpu.VMEM((PEER_EXP_TILE, t_tile), F32), pltpu.VMEM((PEER_EXP_TILE, t_tile), BF16),
                        pltpu.VMEM((D_MODEL, t_tile), F32)],
        compiler_params=_cparams(("parallel", "arbitrary")),
        name="peer_expert",
    )(xt, u_bf, vt_bf, r1, c1, n2, f2, x2, gate)


def _peer(x2, shift, scale, gate, gain, pw, *, t_tile):
    wq_bf, sk_bd, u_bf, vt_bf, ones_bd = pw
    xt, r1, c1, n2, f2 = _peer_route(x2, shift, scale, gain, wq_bf, sk_bd, ones_bd)
    return _peer_expert(xt, u_bf, vt_bf, r1.swapaxes(0, 1), c1.swapaxes(0, 1), n2, f2, x2, gate, t_tile=t_tile)


def _peer_weights(w_q, sub_keys, u_tab, v_tab, ones_bd):
    hc = 2 * PEER_HEADS
    sk = sub_keys.reshape(hc, PEER_N_KEYS, PEER_HALF)
    sk_bd = jnp.einsum('akd,ab->akbd', sk, jnp.eye(hc, dtype=F32)).reshape(hc * PEER_N_KEYS, hc * PEER_HALF)
    return (w_q.astype(BF16), sk_bd.astype(BF16), u_tab.astype(BF16), v_tab.astype(BF16).T, ones_bd)


def kernel(x_prompt, x_sample, c_prompt, c_sample, state_ssm, cache_kv_w128, cache_kv_w512, cache_kv_w2048,
           norm_mix, norm_ffn, ada_w, ada_b, ssm_lam_re, ssm_lam_im, ssm_log_dt, ssm_b_re, ssm_b_im, ssm_c_re,
           ssm_c_im, ssm_d, ssm_w_glu, attn_w_qkv, attn_q_norm, attn_k_norm, attn_w_o, peer_w_q,
           peer_sub_keys, peer_u, peer_v):
    bp, sp, _ = x_prompt.shape
    bs, ts, _ = x_sample.shape
    ns = bs * ts
    ones_att = _ones_blockdiag(ATT)
    ones_d = _ones_blockdiag(D_MODEL)

    mods = _adaln(jnp.concatenate([c_prompt, c_sample], axis=0), ada_w, ada_b)
    caches = (cache_kv_w128, cache_kv_w512, cache_kv_w2048)
    tab_p = _rope_tables(jnp.arange(sp, dtype=jnp.int32))
    pos_s = PAST_LEN + jnp.arange(ts, dtype=jnp.int32)
    tab_s = tuple(jnp.tile(t, (bs, 1)) for t in _rope_tables(pos_s))

    xp, xs = x_prompt, x_sample
    ssm_p, ssm_s = [], []
    kv_p = [[] for _ in range(N_GROUPS)]
    kv_s = [[] for _ in range(N_GROUPS)]
    for i in range(DEPTH):
        j = i // 2
        mp = [m.reshape(bp, 1, D_MODEL) for m in jnp.split(mods[i, :bp], 6, axis=-1)]
        ms_b = jnp.split(mods[i, bp:], 6, axis=-1)
        ms_rows = [jnp.repeat(m, ts, axis=0).reshape(1, ns, D_MODEL) for m in ms_b]
        if i % 2 == 0:
            sw = _s5_weights(ssm_lam_re[j], ssm_lam_im[j], ssm_log_dt[j], ssm_b_re[j], ssm_b_im[j],
                             ssm_c_re[j], ssm_c_im[j])
            wglu = ssm_w_glu[j].astype(BF16)
            mp_t = [m.reshape(1, bp, D_MODEL) for m in mp[:3]]
            xp_t, st_p = _s5_layer(xp.transpose(1, 0, 2), mp_t[0], mp_t[1], mp_t[2], norm_mix[i],
                                   jnp.zeros((bp, SSM_STATE_W), F32), sw, ssm_d[j], wglu, t_chunk=32)
            xp = xp_t.transpose(1, 0, 2)
            ms_t = [m.reshape(1, bs, D_MODEL) for m in ms_b[:3]]
            xs_t, st_s = _s5_layer(xs.transpose(1, 0, 2), ms_t[0], ms_t[1], ms_t[2], norm_mix[i],
                                   _state_to_rows(state_ssm[j]), sw, ssm_d[j], wglu, t_chunk=ts)
            xs = xs_t.transpose(1, 0, 2)
            ssm_p.append(_rows_to_state(st_p))
            ssm_s.append(_rows_to_state(st_s))
        else:
            wqkv = attn_w_qkv[j].astype(BF16)
            wo = attn_w_o[j].astype(BF16)
            q, k, v = _qkv(xp, mp[0], mp[1], norm_mix[i], wqkv, attn_q_norm[j], attn_k_norm[j], tab_p,
                           ones_att, t_tile=512)
            outs, lses = zip(*[_band_attention(q, k, v, g) for g in range(N_GROUPS)])
            xp = _merge(outs, lses, xp, mp[2], wo, t_tile=512)
            for g in range(N_GROUPS):
                keep = min(WINDOWS[g], sp)
                gs = slice(g * ATT, (g + 1) * ATT)
                kv_p[g].append(jnp.stack([k[:, sp - keep:, gs], v[:, sp - keep:, gs]], axis=1)
                               .reshape(bp, 2, keep, N_HEADS, HEAD_DIM))
            xs_rows = xs.reshape(1, ns, D_MODEL)
            qs, ks, vs = _qkv(xs_rows, ms_rows[0], ms_rows[1], norm_mix[i], wqkv, attn_q_norm[j],
                              attn_k_norm[j], tab_s, ones_att, t_tile=ns)
            qs, ks, vs = (t.reshape(bs, ts, N_GROUPS * ATT) for t in (qs, ks, vs))
            res = _cache_attention(qs, ks, vs, [c.reshape(c.shape[:4] + (ATT,)) for c in caches], j)
            outs_s = [r.reshape(1, ns, ATT) for r in res[:N_GROUPS]]
            lses_s = [r.reshape(1, ns, ATT) for r in res[N_GROUPS:]]
            xs = _merge(outs_s, lses_s, xs_rows, ms_rows[2], wo, t_tile=ns).reshape(bs, ts, D_MODEL)
            for g in range(N_GROUPS):
                gs = slice(g * ATT, (g + 1) * ATT)
                kv_s[g].append(jnp.stack([ks[:, :, gs], vs[:, :, gs]], axis=1)
                               .reshape(bs, 2, ts, N_HEADS, HEAD_DIM))
        pw = _peer_weights(peer_w_q[i], peer_sub_keys[i], peer_u[i], peer_v[i], ones_d)
        xp = _peer(xp.reshape(bp * sp, D_MODEL), mp[3], mp[4], mp[5], norm_ffn[i], pw,
                   t_tile=512).reshape(bp, sp, D_MODEL)
        xs = _peer(xs.reshape(ns, D_MODEL), ms_rows[3], ms_rows[4], ms_rows[5], norm_ffn[i], pw,
                   t_tile=ns).reshape(bs, ts, D_MODEL)
    return (xp, xs,
            jnp.stack(ssm_p), jnp.stack(kv_p[0]), jnp.stack(kv_p[1]), jnp.stack(kv_p[2]),
            jnp.stack(ssm_s), jnp.stack(kv_s[0]), jnp.stack(kv_s[1]), jnp.stack(kv_s[2]))
```

```python
import functools
import math

import jax
import jax.numpy as jnp
import numpy as np
from jax import lax
from jax.experimental import pallas as pl
from jax.experimental.pallas import tpu as pltpu

F32 = jnp.float32
BF16 = jnp.bfloat16

D_MODEL = 1024
DEPTH = 4
PAST_LEN = 8192
EPS = 1e-6
NEG_INF = -1e30

SSM_P = 16
SSM_GROUPS = D_MODEL // SSM_P
SSM_N = 64
SSM_GB = 4
SSM_GB_CH = D_MODEL // SSM_GB
SSM_GB_ST = SSM_GROUPS * SSM_N // SSM_GB
SSM_STATE_W = 2 * SSM_GROUPS * SSM_N

N_HEADS = 8
HEAD_DIM = 64
ATT = N_HEADS * HEAD_DIM
WINDOWS = (128, 512, 2048)
DILATIONS = (1, 4, 16)
N_GROUPS = 3
N_BACK = 128
ROPE_DIM = HEAD_DIM // 4
ROPE_THETA = 500000.0

PEER_HEADS = 8
PEER_N_KEYS = 128
PEER_HALF = 64
PEER_TOPK = 16
PEER_SEL_TOK = 128
PEER_EXP_TILE = 1024
PEER_I1_TILE = PEER_EXP_TILE // PEER_N_KEYS
PEER_CHUNK = 256

VMEM_LIMIT = 56 * 1024 * 1024


def _cparams(sem):
    return pltpu.CompilerParams(dimension_semantics=sem, vmem_limit_bytes=VMEM_LIMIT)


def _modulate(x, gain, shift, scale):
    ms = jnp.mean(x * x, axis=-1, keepdims=True)
    return x * lax.rsqrt(ms + EPS) * gain * (1.0 + scale) + shift


def _seg_mean(x2, ones_bd):
    hi = x2.astype(BF16)
    lo = (x2 - hi.astype(F32)).astype(BF16)
    s = jnp.dot(hi, ones_bd, preferred_element_type=F32) + jnp.dot(lo, ones_bd, preferred_element_type=F32)
    return s * (1.0 / PEER_HALF)


def _ones_blockdiag(n):
    seg = np.arange(n) // 64
    return jnp.asarray((seg[:, None] == seg[None, :]).astype(np.float32), dtype=BF16)


def _adaln_kernel(c_ref, w_ref, b_ref, o_ref):
    s = jax.nn.silu(c_ref[...])
    o_ref[0] = jnp.dot(s.astype(BF16), w_ref[0].astype(BF16), preferred_element_type=F32) + b_ref[0]


def _adaln(c_all, ada_w, ada_b):
    nb = c_all.shape[0]
    nt = 6 * D_MODEL // 1024
    return pl.pallas_call(
        _adaln_kernel,
        grid=(DEPTH, nt),
        in_specs=[
            pl.BlockSpec((nb, D_MODEL), lambda l, j: (0, 0)),
            pl.BlockSpec((1, D_MODEL, 1024), lambda l, j: (l, 0, j)),
            pl.BlockSpec((1, 1, 1024), lambda l, j: (l, 0, j)),
        ],
        out_specs=pl.BlockSpec((1, nb, 1024), lambda l, j: (l, 0, j)),
        out_shape=jax.ShapeDtypeStruct((DEPTH, nb, 6 * D_MODEL), F32),
        compiler_params=_cparams(("parallel", "parallel")),
        name="adaln",
    )(c_all, ada_w, ada_b.reshape(DEPTH, 1, 6 * D_MODEL))


def _s5_kernel(x_ref, shift_ref, scale_ref, gate_ref, gain_ref, h0_ref, bm_ref, cm_ref, are_ref, aim_ref,
               d_ref, wglu_ref, xo_ref, st_ref, bu_scr, state_scr, *, n_batch, n_time):
    i = pl.program_id(0)

    @pl.when(i == 0)
    def _():
        state_scr[...] = h0_ref[...]

    x3 = x_ref[...]
    a0, a1, _ = x3.shape
    rows = a0 * a1
    h3 = _modulate(x3, gain_ref[...], shift_ref[...], scale_ref[...])
    u = h3.reshape(rows, D_MODEL)
    ub = u.astype(BF16)
    for gb in range(SSM_GB):
        bu_scr[:, gb * 2 * SSM_GB_ST:(gb + 1) * 2 * SSM_GB_ST] = jnp.dot(
            ub[:, gb * SSM_GB_CH:(gb + 1) * SSM_GB_CH], bm_ref[gb], preferred_element_type=F32)

    for bb in range(n_batch // 8):
        for gb in range(SSM_GB):
            c_re = gb * 2 * SSM_GB_ST
            c_im = c_re + SSM_GB_ST
            ar = jnp.broadcast_to(are_ref[gb], (8, SSM_GB_ST))
            ai = jnp.broadcast_to(aim_ref[gb], (8, SSM_GB_ST))
            xr = state_scr[bb * 8:(bb + 1) * 8, c_re:c_re + SSM_GB_ST]
            xi = state_scr[bb * 8:(bb + 1) * 8, c_im:c_im + SSM_GB_ST]
            for t in range(n_time):
                rsel = pl.ds(t * n_batch + bb * 8, 8)
                br = bu_scr[rsel, c_re:c_re + SSM_GB_ST]
                bi = bu_scr[rsel, c_im:c_im + SSM_GB_ST]
                nr = ar * xr - ai * xi + br
                ni = ar * xi + ai * xr + bi
                bu_scr[rsel, c_re:c_re + SSM_GB_ST] = nr
                bu_scr[rsel, c_im:c_im + SSM_GB_ST] = ni
                xr, xi = nr, ni
            state_scr[bb * 8:(bb + 1) * 8, c_re:c_re + SSM_GB_ST] = xr
            state_scr[bb * 8:(bb + 1) * 8, c_im:c_im + SSM_GB_ST] = xi

    ys = []
    for gb in range(SSM_GB):
        st = bu_scr[:, gb * 2 * SSM_GB_ST:(gb + 1) * 2 * SSM_GB_ST].astype(BF16)
        ys.append(jnp.dot(st, cm_ref[gb], preferred_element_type=F32))
    y = jnp.concatenate(ys, axis=-1) + d_ref[...] * u
    z = jnp.dot(y.astype(BF16), wglu_ref[...], preferred_element_type=F32)
    out = z[:, :D_MODEL] * jax.nn.sigmoid(z[:, D_MODEL:])
    xo_ref[...] = x3 + gate_ref[...] * out.reshape(a0, a1, D_MODEL)

    @pl.when(i == pl.num_programs(0) - 1)
    def _():
        st_ref[...] = state_scr[...]


def _s5_weights(lam_re, lam_im, log_dt, b_re, b_im, c_re, c_im):
    dt = jnp.exp(log_dt)[:, None]
    mag = jnp.exp(lam_re * dt)
    a_re = mag * jnp.cos(lam_im * dt)
    a_im = mag * jnp.sin(lam_im * dt)
    den = lam_re * lam_re + lam_im * lam_im
    f_re = ((a_re - 1.0) * lam_re + a_im * lam_im) / den
    f_im = (a_im * lam_re - (a_re - 1.0) * lam_im) / den
    bb_re = f_re[..., None] * b_re - f_im[..., None] * b_im
    bb_im = f_re[..., None] * b_im + f_im[..., None] * b_re
    eye = jnp.eye(16, dtype=F32)

    def in_blk(b):
        b4 = b.reshape(SSM_GB, 16, SSM_N, SSM_P)
        return jnp.einsum('bgnp,gh->bgphn', b4, eye).reshape(SSM_GB, SSM_GB_CH, SSM_GB_ST)

    def out_blk(c):
        c4 = c.reshape(SSM_GB, 16, SSM_P, SSM_N)
        return jnp.einsum('bgpn,gh->bgnhp', c4, eye).reshape(SSM_GB, SSM_GB_ST, SSM_GB_CH)

    bm = jnp.concatenate([in_blk(bb_re), in_blk(bb_im)], axis=2).astype(BF16)
    cm = jnp.concatenate([out_blk(c_re), -out_blk(c_im)], axis=1).astype(BF16)
    return bm, cm, a_re.reshape(SSM_GB, 1, SSM_GB_ST), a_im.reshape(SSM_GB, 1, SSM_GB_ST)


def _state_to_rows(st):
    b = st.shape[0]
    return st.reshape(b, 2, SSM_GB, SSM_GB_ST).transpose(0, 2, 1, 3).reshape(b, SSM_STATE_W)


def _rows_to_state(rows):
    b = rows.shape[0]
    return rows.reshape(b, SSM_GB, 2, SSM_GB_ST).transpose(0, 2, 1, 3).reshape(b, 2, SSM_GROUPS, SSM_N)


def _s5_layer(x3, shift, scale, gate, gain, h0_rows, weights, d_skip, w_glu, *, t_chunk):
    bm, cm, a_re, a_im = weights
    seq, n_batch = x3.shape[0], x3.shape[1]
    n_time = t_chunk
    blk = (n_time, n_batch, D_MODEL)
    xmap = lambda i: (i, 0, 0)
    mblk = (1, n_batch, D_MODEL)
    grid = seq // n_time
    rows = n_batch * n_time
    const3 = lambda i: (0, 0, 0)
    const2 = lambda i: (0, 0)
    kern = functools.partial(_s5_kernel, n_batch=n_batch, n_time=n_time)
    return pl.pallas_call(
        kern,
        grid=(grid,),
        in_specs=[
            pl.BlockSpec(blk, xmap),
            pl.BlockSpec(mblk, const3), pl.BlockSpec(mblk, const3), pl.BlockSpec(mblk, const3),
            pl.BlockSpec((1, 1, D_MODEL), const3),
            pl.BlockSpec((n_batch, SSM_STATE_W), const2),
            pl.BlockSpec(bm.shape, const3), pl.BlockSpec(cm.shape, const3),
            pl.BlockSpec(a_re.shape, const3), pl.BlockSpec(a_im.shape, const3),
            pl.BlockSpec((1, D_MODEL), const2),
            pl.BlockSpec((D_MODEL, 2 * D_MODEL), const2),
        ],
        out_specs=[pl.BlockSpec(blk, xmap), pl.BlockSpec((n_batch, SSM_STATE_W), const2)],
        out_shape=[jax.ShapeDtypeStruct(x3.shape, F32), jax.ShapeDtypeStruct((n_batch, SSM_STATE_W), F32)],
        scratch_shapes=[pltpu.VMEM((rows, SSM_STATE_W), F32), pltpu.VMEM((n_batch, SSM_STATE_W), F32)],
        compiler_params=_cparams(("arbitrary",)),
        name="s5_layer",
    )(x3, shift, scale, gate, gain.reshape(1, 1, D_MODEL), h0_rows, bm, cm, a_re, a_im,
      d_skip.reshape(1, D_MODEL), w_glu)


def _qkv_kernel(x_ref, shift_ref, scale_ref, gain_ref, w_ref, qg_ref, kg_ref, rc_ref, rs1_ref, rs2_ref,
                ones_ref, q_ref, k_ref, v_ref):
    h = _modulate(x_ref[0], gain_ref[...], shift_ref[0], scale_ref[0])
    qkv = jnp.dot(h.astype(BF16), w_ref[...], preferred_element_type=F32)
    rc, rs1, rs2 = rc_ref[...], rs1_ref[...], rs2_ref[...]
    ones_bd = ones_ref[...]

    def norm_rope(t, g):
        tn = t * lax.rsqrt(_seg_mean(t * t, ones_bd) + EPS) * g
        return tn * rc + pltpu.roll(tn, ATT - ROPE_DIM // 2, 1) * rs1 + pltpu.roll(tn, ROPE_DIM // 2, 1) * rs2

    q_ref[0] = norm_rope(qkv[:, :ATT], qg_ref[...])
    k_ref[0] = norm_rope(qkv[:, ATT:2 * ATT], kg_ref[...])
    v_ref[0] = qkv[:, 2 * ATT:]


def _rope_tables(pos):
    half = ROPE_DIM // 2
    inv = ROPE_THETA ** (-jnp.arange(half, dtype=F32) / half)
    ang = pos.astype(F32)[:, None] * inv[None, :]
    cos, sin = jnp.cos(ang), jnp.sin(ang)
    lane = np.arange(ATT) % HEAD_DIM
    fidx = lane % half
    first = jnp.asarray(lane < half)
    second = jnp.asarray((lane >= half) & (lane < ROPE_DIM))
    cl, sl = cos[:, fidx], sin[:, fidx]
    rc = jnp.where(first | second, cl, 1.0)
    rs1 = jnp.where(first, -sl, 0.0)
    rs2 = jnp.where(second, sl, 0.0)
    return rc, rs1, rs2


def _qkv(x3, shift, scale, gain, w_qkv_bf, q_gain, k_gain, tables, ones_bd, *, t_tile):
    bq, sq, _ = x3.shape
    mrows = shift.shape[1]
    mt = t_tile if mrows == sq else 1
    mmap = (lambda b, i, g: (b, i, 0)) if mrows == sq else (lambda b, i, g: (b, 0, 0))
    rc, rs1, rs2 = tables
    qg = jnp.tile(q_gain, N_HEADS).reshape(1, ATT)
    kg = jnp.tile(k_gain, N_HEADS).reshape(1, ATT)
    tmap = lambda b, i, g: (i, 0)
    cmap = lambda b, i, g: (0, 0)
    omap = lambda b, i, g: (b, i, g)
    oshape = jax.ShapeDtypeStruct((bq, sq, N_GROUPS * ATT), F32)
    return pl.pallas_call(
        _qkv_kernel,
        grid=(bq, sq // t_tile, N_GROUPS),
        in_specs=[
            pl.BlockSpec((1, t_tile, D_MODEL), lambda b, i, g: (b, i, 0)),
            pl.BlockSpec((1, mt, D_MODEL), mmap), pl.BlockSpec((1, mt, D_MODEL), mmap),
            pl.BlockSpec((1, D_MODEL), cmap),
            pl.BlockSpec((D_MODEL, 3 * ATT), lambda b, i, g: (0, g)),
            pl.BlockSpec((1, ATT), cmap), pl.BlockSpec((1, ATT), cmap),
            pl.BlockSpec((t_tile, ATT), tmap), pl.BlockSpec((t_tile, ATT), tmap), pl.BlockSpec((t_tile, ATT), tmap),
            pl.BlockSpec((ATT, ATT), cmap),
        ],
        out_specs=[pl.BlockSpec((1, t_tile, ATT), omap)] * 3,
        out_shape=[oshape] * 3,
        compiler_params=_cparams(("parallel", "parallel", "arbitrary")),
        name="qkv_proj",
    )(x3, shift, scale, gain.reshape(1, D_MODEL), w_qkv_bf, qg, kg, rc, rs1, rs2, ones_bd)


def _band_kernel(q_ref, kp_ref, kc_ref, vp_ref, vc_ref, o_ref, l_ref):
    i = pl.program_id(2)
    q = q_ref[0].astype(BF16)
    k = jnp.concatenate([kp_ref[0], kc_ref[0]], axis=0).astype(BF16)
    v = jnp.concatenate([vp_ref[0], vc_ref[0]], axis=0).astype(BF16)
    iq = lax.broadcasted_iota(jnp.int32, (N_BACK, 2 * N_BACK), 0)
    ik = lax.broadcasted_iota(jnp.int32, (N_BACK, 2 * N_BACK), 1)
    dist = iq + N_BACK - ik
    mask = (dist >= 0) & (dist <= N_BACK) & ((ik >= N_BACK) | (i > 0))
    for h in range(N_HEADS):
        sl = slice(h * HEAD_DIM, (h + 1) * HEAD_DIM)
        s = lax.dot_general(q[:, sl], k[:, sl], (((1,), (1,)), ((), ())),
                            preferred_element_type=F32) * (HEAD_DIM ** -0.5)
        s = jnp.where(mask, s, NEG_INF)
        m = jnp.max(s, axis=-1, keepdims=True)
        p = jnp.exp(s - m)
        den = jnp.sum(p, axis=-1, keepdims=True)
        o = jnp.dot(p.astype(BF16), v[:, sl], preferred_element_type=F32) / den
        o_ref[0, :, sl] = o
        l_ref[0, :, sl] = jnp.broadcast_to(m + jnp.log(den), (N_BACK, HEAD_DIM))


def _band_attention(q, k, v, g):
    b, s, w = q.shape
    d = DILATIONS[g]
    sub = s // d
    nb = sub // N_BACK
    ncol = w // ATT
    qv, kv, vv = (t.reshape(b, sub, d * w) for t in (q, k, v))
    cur = lambda bi, r, i: (bi, i, r * ncol + g)
    prev = lambda bi, r, i: (bi, jnp.maximum(i - 1, 0), r * ncol + g)
    omap = lambda bi, r, i: (bi, i, r)
    blk = (1, N_BACK, ATT)
    oshape = jax.ShapeDtypeStruct((b, sub, d * ATT), F32)
    o, l = pl.pallas_call(
        _band_kernel,
        grid=(b, d, nb),
        in_specs=[pl.BlockSpec(blk, cur), pl.BlockSpec(blk, prev), pl.BlockSpec(blk, cur),
                  pl.BlockSpec(blk, prev), pl.BlockSpec(blk, cur)],
        out_specs=[pl.BlockSpec(blk, omap), pl.BlockSpec(blk, omap)],
        out_shape=[oshape, oshape],
        compiler_params=_cparams(("parallel", "parallel", "arbitrary")),
        name="band_attention",
    )(qv, kv, kv, vv, vv)
    return o.reshape(b, s, ATT), l.reshape(b, s, ATT)


def _cache_attn_kernel(q_ref, k_ref, v_ref, c0_ref, c1_ref, c2_ref, *out_refs):
    n_new = q_ref.shape[1]
    rows = n_new * N_HEADS
    lane_head = lax.broadcasted_iota(jnp.int32, (N_HEADS, ATT), 1) // HEAD_DIM
    head_mask = (lane_head == lax.broadcasted_iota(jnp.int32, (N_HEADS, ATT), 0)).astype(F32)
    row_t = lax.broadcasted_iota(jnp.int32, (rows, 1), 0) // N_HEADS
    caches = (c0_ref, c1_ref, c2_ref)
    for g in range(N_GROUPS):
        d, win = DILATIONS[g], WINDOWS[g]
        gs = slice(g * ATT, (g + 1) * ATT)
        q = q_ref[0][:, gs]
        kn = k_ref[0][:, gs]
        vn = v_ref[0][:, gs]
        kc = caches[g][0, 0, 0].astype(BF16)
        vc = caches[g][0, 0, 1].astype(BF16)
        qbd = jnp.concatenate([q[t:t + 1, :] * head_mask for t in range(n_new)], axis=0)
        s_c = lax.dot_general(qbd.astype(BF16), kc, (((1,), (1,)), ((), ())),
                              preferred_element_type=F32) * (HEAD_DIM ** -0.5)
        col = lax.broadcasted_iota(jnp.int32, (rows, win), 1)
        valid_c = (col >= row_t) & (((col - row_t) & (d - 1)) == 0)
        s_c = jnp.where(valid_c, s_c, NEG_INF)
        s_n = []
        for t2 in range(n_new):
            sn = jnp.sum(qbd * kn[t2:t2 + 1, :], axis=-1, keepdims=True) * (HEAD_DIM ** -0.5)
            valid_n = (row_t >= t2) & (((row_t - t2) & (d - 1)) == 0)
            s_n.append(jnp.where(valid_n, sn, NEG_INF))
        m = jnp.max(s_c, axis=-1, keepdims=True)
        for sn in s_n:
            m = jnp.maximum(m, sn)
        p_c = jnp.exp(s_c - m)
        den = jnp.sum(p_c, axis=-1, keepdims=True)
        o = jnp.dot(p_c.astype(BF16), vc, preferred_element_type=F32)
        for t2 in range(n_new):
            p_n = jnp.exp(s_n[t2] - m)
            den = den + p_n
            o = o + p_n * vn[t2:t2 + 1, :]
        o = o / den
        lse = m + jnp.log(den)
        for t in range(n_new):
            rs = slice(t * N_HEADS, (t + 1) * N_HEADS)
            out_refs[g][0, t:t + 1, :] = jnp.sum(o[rs] * head_mask, axis=0, keepdims=True)
            out_refs[N_GROUPS + g][0, t:t + 1, :] = jnp.sum(lse[rs] * head_mask, axis=0, keepdims=True)


def _cache_attention(q, k, v, caches, layer):
    bd, t, w = q.shape
    xmap = lambda b: (b, 0, 0)
    cmap = lambda b: (layer, b, 0, 0, 0)
    oshape = jax.ShapeDtypeStruct((bd, t, ATT), F32)
    return pl.pallas_call(
        _cache_attn_kernel,
        grid=(bd,),
        in_specs=[pl.BlockSpec((1, t, w), xmap)] * 3
        + [pl.BlockSpec((1, 1, 2, WINDOWS[g], ATT), cmap) for g in range(N_GROUPS)],
        out_specs=[pl.BlockSpec((1, t, ATT), xmap)] * (2 * N_GROUPS),
        out_shape=[oshape] * (2 * N_GROUPS),
        compiler_params=_cparams(("parallel",)),
        name="cache_attention",
    )(q, k, v, *caches)


def _merge_kernel(o0, o1, o2, l0, l1, l2, x_ref, gate_ref, wo_ref, xo_ref):
    la, lb, lc = l0[0], l1[0], l2[0]
    m = jnp.maximum(jnp.maximum(la, lb), lc)
    ea, eb, ec = jnp.exp(la - m), jnp.exp(lb - m), jnp.exp(lc - m)
    den = ea + eb + ec
    o = (ea / den) * o0[0] + (eb / den) * o1[0] + (ec / den) * o2[0]
    out = jnp.dot(o.astype(BF16), wo_ref[...], preferred_element_type=F32)
    xo_ref[0] = x_ref[0] + gate_ref[0] * out


def _merge(outs, lses, x3, gate, w_o_bf, *, t_tile):
    bq, sq, _ = x3.shape
    mrows = gate.shape[1]
    mt = t_tile if mrows == sq else 1
    mmap = (lambda b, i: (b, i, 0)) if mrows == sq else (lambda b, i: (b, 0, 0))
    tmap = lambda b, i: (b, i, 0)
    ablk = pl.BlockSpec((1, t_tile, ATT), tmap)
    return pl.pallas_call(
        _merge_kernel,
        grid=(bq, sq // t_tile),
        in_specs=[ablk] * 6 + [pl.BlockSpec((1, t_tile, D_MODEL), tmap), pl.BlockSpec((1, mt, D_MODEL), mmap),
                               pl.BlockSpec((ATT, D_MODEL), lambda b, i: (0, 0))],
        out_specs=pl.BlockSpec((1, t_tile, D_MODEL), tmap),
        out_shape=jax.ShapeDtypeStruct(x3.shape, F32),
        compiler_params=_cparams(("parallel", "parallel")),
        name="attn_merge",
    )(*outs, *lses, x3, gate, w_o_bf)


def _top16(s):
    vals = s
    rank = jnp.full(s.shape, float(PEER_N_KEYS), F32)
    tops = []
    for r in range(PEER_TOPK):
        m = jnp.max(vals, axis=0, keepdims=True)
        hit = vals == m
        rank = jnp.where(hit, float(r), rank)
        vals = jnp.where(hit, -jnp.inf, vals)
        tops.append(m)
    return tops, rank


def _peer_route_kernel(x_ref, shift_ref, scale_ref, gain_ref, wq_ref, sk_ref, ones_ref,
                       xt_ref, r1_ref, c1_ref, nf_ref, s_scr):
    hm = _modulate(x_ref[...], gain_ref[...], shift_ref[0], scale_ref[0])
    hb = hm.astype(BF16)
    xt_ref[...] = hm.T.astype(BF16)
    q = jnp.dot(hb, wq_ref[...], preferred_element_type=F32)
    qn = q * lax.rsqrt(_seg_mean(q * q, ones_ref[...]) + EPS)
    s_scr[...] = lax.dot_general(sk_ref[...], qn.astype(BF16), (((1,), (1,)), ((), ())),
                                 preferred_element_type=F32)

    def head(h, carry):
        base = pl.multiple_of(h * 2 * PEER_N_KEYS, 2 * PEER_N_KEYS)
        s1 = s_scr[pl.ds(base, PEER_N_KEYS), :]
        s2 = s_scr[pl.ds(base + PEER_N_KEYS, PEER_N_KEYS), :]
        top1, rank1 = _top16(s1)
        top2, rank2 = _top16(s2)
        row = lax.broadcasted_iota(jnp.int32, (PEER_TOPK, s1.shape[1]), 0)
        v1 = jnp.zeros((PEER_TOPK, s1.shape[1]), F32)
        for r in range(PEER_TOPK):
            v1 = jnp.where(row == r, top1[r], v1)
        cands = [v1[:8] + top2[b] for b in range(PEER_TOPK)] + [v1[8:] + top2[0]]
        best = top1[0] + top2[0]
        taken = [jnp.zeros_like(c) for c in cands]
        zsum = jnp.zeros_like(best)
        for _ in range(PEER_TOPK):
            m = cands[0]
            for c in cands[1:]:
                m = jnp.maximum(m, c)
            m = jnp.max(m, axis=0, keepdims=True)
            for j in range(len(cands)):
                hit = cands[j] == m
                taken[j] = jnp.where(hit, 1.0, taken[j])
                cands[j] = jnp.where(hit, -jnp.inf, cands[j])
            zsum = zsum + jnp.exp(m - best)
        cnt_lo = taken[0]
        for j in range(1, PEER_TOPK):
            cnt_lo = cnt_lo + taken[j]
        cnt = jnp.concatenate([cnt_lo, taken[PEER_TOPK]], axis=0)
        n2 = jnp.zeros_like(rank2)
        for a in range(PEER_TOPK):
            n2 = n2 + jnp.where(rank2 < cnt[a:a + 1], 1.0, 0.0)
        r1_ref[0, h] = rank1
        c1_ref[0, h] = jnp.exp(s1 - top1[0])
        nf_ref[0, h, :, 0] = n2.reshape(PEER_N_KEYS // 8, 8, PEER_SEL_TOK)
        nf_ref[0, h, :, 1] = (jnp.exp(s2 - top2[0]) / zsum).reshape(PEER_N_KEYS // 8, 8, PEER_SEL_TOK)
        return carry

    lax.fori_loop(0, PEER_HEADS, head, 0)


def _peer_route(x2, shift, scale, gain, wq_bf, sk_bd, ones_bd):
    n = x2.shape[0]
    tt = PEER_SEL_TOK
    nt = n // tt
    per_mod = nt // shift.shape[0]
    mrows = shift.shape[1]
    if mrows == 1:
        mblk, mmap = (1, 1, D_MODEL), (lambda i: (i // per_mod, 0, 0))
    else:
        mblk, mmap = (1, tt, D_MODEL), (lambda i: (0, i, 0))
    cmap = lambda i: (0, 0)
    nrt = PEER_N_KEYS // 8
    hshape = jax.ShapeDtypeStruct((nt, PEER_HEADS, PEER_N_KEYS, tt), F32)
    hblk = pl.BlockSpec((1, PEER_HEADS, PEER_N_KEYS, tt), lambda i: (i, 0, 0, 0))
    nfshape = jax.ShapeDtypeStruct((nt, PEER_HEADS, nrt, 2, 8, tt), F32)
    nfblk = pl.BlockSpec((1, PEER_HEADS, nrt, 2, 8, tt), lambda i: (i, 0, 0, 0, 0, 0))
    return pl.pallas_call(
        _peer_route_kernel,
        grid=(nt,),
        in_specs=[
            pl.BlockSpec((tt, D_MODEL), lambda i: (i, 0)),
            pl.BlockSpec(mblk, mmap), pl.BlockSpec(mblk, mmap),
            pl.BlockSpec((1, D_MODEL), cmap),
            pl.BlockSpec((D_MODEL, D_MODEL), cmap),
            pl.BlockSpec((2 * PEER_HEADS * PEER_N_KEYS, D_MODEL), cmap),
            pl.BlockSpec((D_MODEL, D_MODEL), cmap),
        ],
        out_specs=[pl.BlockSpec((D_MODEL, tt), lambda i: (0, i)), hblk, hblk, nfblk],
        out_shape=[jax.ShapeDtypeStruct((D_MODEL, n), BF16), hshape, hshape, nfshape],
        scratch_shapes=[pltpu.VMEM((2 * PEER_HEADS * PEER_N_KEYS, tt), F32)],
        compiler_params=_cparams(("parallel",)),
        name="peer_route",
    )(x2, shift, scale, gain.reshape(1, D_MODEL), wq_bf, sk_bd, ones_bd)


def _gelu(x):
    return 0.5 * x * (1.0 + lax.erf(x * np.float32(math.sqrt(0.5))))


def _peer_expert_kernel(xt_ref, u_ref, vt_ref, r1_ref, c1_ref, nf_ref, x_ref, gate_ref, xo_ref,
                        act_scr, w_scr, acc_scr, *, t_tile):
    e = pl.program_id(1)

    @pl.when(e == 0)
    def _():
        acc_scr[...] = jnp.zeros_like(acc_scr)

    def stage_a(q):
        qs = slice(q * PEER_CHUNK, (q + 1) * PEER_CHUNK)
        act_scr[q] = jnp.dot(u_ref[qs, :], xt_ref[...], preferred_element_type=F32)

    def stage_b(q):
        for i1l in range(PEER_CHUNK // PEER_N_KEYS):
            i1 = q * (PEER_CHUNK // PEER_N_KEYS) + i1l
            for lc in range(t_tile // PEER_SEL_TOK):
                ls = slice(lc * PEER_SEL_TOK, (lc + 1) * PEER_SEL_TOK)
                r1 = [r1_ref[lc, h, i1:i1 + 1, :] for h in range(PEER_HEADS)]
                c1 = [c1_ref[lc, h, i1:i1 + 1, :] for h in range(PEER_HEADS)]
                for j in range(PEER_N_KEYS // 16):
                    g = jnp.zeros((16, PEER_SEL_TOK), F32)
                    for h in range(PEER_HEADS):
                        n2 = nf_ref[lc, h, 2 * j:2 * j + 2, 0].reshape(16, PEER_SEL_TOK)
                        f2 = nf_ref[lc, h, 2 * j:2 * j + 2, 1].reshape(16, PEER_SEL_TOK)
                        g = g + jnp.where(n2 > r1[h], f2, 0.0) * c1[h]
                    rs = slice(i1l * PEER_N_KEYS + j * 16, i1l * PEER_N_KEYS + (j + 1) * 16)
                    w_scr[q, rs, ls] = (g * _gelu(act_scr[q, rs, ls])).astype(BF16)

    def stage_c(q):
        qs = slice(q * PEER_CHUNK, (q + 1) * PEER_CHUNK)
        acc_scr[...] += jnp.dot(vt_ref[:, qs], w_scr[q], preferred_element_type=F32)

    nq = PEER_EXP_TILE // PEER_CHUNK
    stage_a(0)
    for q in range(nq):
        if q + 1 < nq:
            stage_a(q + 1)
        stage_b(q)
        if q > 0:
            stage_c(q - 1)
    stage_c(nq - 1)

    @pl.when(e == pl.num_programs(1) - 1)
    def _():
        xo_ref[...] = x_ref[...] + gate_ref[0] * acc_scr[...].T


def _peer_expert(xt, u_bf, vt_bf, r1, c1, nf, x2, gate, *, t_tile):
    n = x2.shape[0]
    nt = n // t_tile
    nlc = t_tile // PEER_SEL_TOK
    ne = u_bf.shape[0] // PEER_EXP_TILE
    per_mod = max(nt // gate.shape[0], 1)
    mrows = gate.shape[1]
    if mrows == 1:
        mblk, mmap = (1, 1, D_MODEL), (lambda i, e: (i // per_mod, 0, 0))
    else:
        mblk, mmap = (1, t_tile, D_MODEL), (lambda i, e: (0, i, 0))
    i1blk = pl.BlockSpec((nlc, PEER_HEADS, PEER_I1_TILE, PEER_SEL_TOK), lambda i, e: (i, 0, e, 0))
    nfblk = pl.BlockSpec((nlc, PEER_HEADS, PEER_N_KEYS // 8, 2, 8, PEER_SEL_TOK), lambda i, e: (i, 0, 0, 0, 0, 0))
    kern = functools.partial(_peer_expert_kernel, t_tile=t_tile)
    return pl.pallas_call(
        kern,
        grid=(nt, ne),
        in_specs=[
            pl.BlockSpec((D_MODEL, t_tile), lambda i, e: (0, i)),
            pl.BlockSpec((PEER_EXP_TILE, D_MODEL), lambda i, e: (e, 0)),
            pl.BlockSpec((D_MODEL, PEER_EXP_TILE), lambda i, e: (0, e)),
            i1blk, i1blk, nfblk,
            pl.BlockSpec((t_tile, D_MODEL), lambda i, e: (i, 0)),
            pl.BlockSpec(mblk, mmap),
        ],
        out_specs=pl.BlockSpec((t_tile, D_MODEL), lambda i, e: (i, 0)),
        out_shape=jax.ShapeDtypeStruct(x2.shape, F32),
        scratch_shapes=[pltpu.VMEM((PEER_EXP_TILE // PEER_CHUNK, PEER_CHUNK, t_tile), F32),
                        pltpu.VMEM((PEER_EXP_TILE // PEER_CHUNK, PEER_CHUNK, t_tile), BF16),
                        pltpu.VMEM((D_MODEL, t_tile), F32)],
        compiler_params=_cparams(("parallel", "arbitrary")),
        name="peer_expert",
    )(xt, u_bf, vt_bf, r1, c1, nf, x2, gate)


def _peer(x2, shift, scale, gate, gain, pw, *, t_tile):
    wq_bf, sk_bd, u_bf, vt_bf, ones_bd = pw
    xt, r1, c1, nf = _peer_route(x2, shift, scale, gain, wq_bf, sk_bd, ones_bd)
    return _peer_expert(xt, u_bf, vt_bf, r1, c1, nf, x2, gate, t_tile=t_tile)


def _peer_weights(w_q, sub_keys, u_tab, v_tab, ones_bd):
    hc = 2 * PEER_HEADS
    sk = sub_keys.reshape(hc, PEER_N_KEYS, PEER_HALF)
    sk_bd = jnp.einsum('akd,ab->akbd', sk, jnp.eye(hc, dtype=F32)).reshape(hc * PEER_N_KEYS, hc * PEER_HALF)
    return (w_q.astype(BF16), sk_bd.astype(BF16), u_tab.astype(BF16), v_tab.astype(BF16).T, ones_bd)


def kernel(x_prompt, x_sample, c_prompt, c_sample, state_ssm, cache_kv_w128, cache_kv_w512, cache_kv_w2048,
           norm_mix, norm_ffn, ada_w, ada_b, ssm_lam_re, ssm_lam_im, ssm_log_dt, ssm_b_re, ssm_b_im, ssm_c_re,
           ssm_c_im, ssm_d, ssm_w_glu, attn_w_qkv, attn_q_norm, attn_k_norm, attn_w_o, peer_w_q,
           peer_sub_keys, peer_u, peer_v):
    bp, sp, _ = x_prompt.shape
    bs, ts, _ = x_sample.shape
    ns = bs * ts
    ones_att = _ones_blockdiag(ATT)
    ones_d = _ones_blockdiag(D_MODEL)

    mods = _adaln(jnp.concatenate([c_prompt, c_sample], axis=0), ada_w, ada_b)
    caches = (cache_kv_w128, cache_kv_w512, cache_kv_w2048)
    tab_p = _rope_tables(jnp.arange(sp, dtype=jnp.int32))
    pos_s = PAST_LEN + jnp.arange(ts, dtype=jnp.int32)
    tab_s = tuple(jnp.tile(t, (bs, 1)) for t in _rope_tables(pos_s))

    xp, xs = x_prompt, x_sample
    ssm_p, ssm_s = [], []
    kv_p = [[] for _ in range(N_GROUPS)]
    kv_s = [[] for _ in range(N_GROUPS)]
    for i in range(DEPTH):
        j = i // 2
        mp = [m.reshape(bp, 1, D_MODEL) for m in jnp.split(mods[i, :bp], 6, axis=-1)]
        ms_b = jnp.split(mods[i, bp:], 6, axis=-1)
        ms_rows = [jnp.repeat(m, ts, axis=0).reshape(1, ns, D_MODEL) for m in ms_b]
        if i % 2 == 0:
            sw = _s5_weights(ssm_lam_re[j], ssm_lam_im[j], ssm_log_dt[j], ssm_b_re[j], ssm_b_im[j],
                             ssm_c_re[j], ssm_c_im[j])
            wglu = ssm_w_glu[j].astype(BF16)
            mp_t = [m.reshape(1, bp, D_MODEL) for m in mp[:3]]
            xp_t, st_p = _s5_layer(xp.transpose(1, 0, 2), mp_t[0], mp_t[1], mp_t[2], norm_mix[i],
                                   jnp.zeros((bp, SSM_STATE_W), F32), sw, ssm_d[j], wglu, t_chunk=32)
            xp = xp_t.transpose(1, 0, 2)
            ms_t = [m.reshape(1, bs, D_MODEL) for m in ms_b[:3]]
            xs_t, st_s = _s5_layer(xs.transpose(1, 0, 2), ms_t[0], ms_t[1], ms_t[2], norm_mix[i],
                                   _state_to_rows(state_ssm[j]), sw, ssm_d[j], wglu, t_chunk=ts)
            xs = xs_t.transpose(1, 0, 2)
            ssm_p.append(_rows_to_state(st_p))
            ssm_s.append(_rows_to_state(st_s))
        else:
            wqkv = attn_w_qkv[j].astype(BF16)
            wo = attn_w_o[j].astype(BF16)
            q, k, v = _qkv(xp, mp[0], mp[1], norm_mix[i], wqkv, attn_q_norm[j], attn_k_norm[j], tab_p,
                           ones_att, t_tile=512)
            outs, lses = zip(*[_band_attention(q, k, v, g) for g in range(N_GROUPS)])
            xp = _merge(outs, lses, xp, mp[2], wo, t_tile=512)
            for g in range(N_GROUPS):
                keep = min(WINDOWS[g], sp)
                gs = slice(g * ATT, (g + 1) * ATT)
                kv_p[g].append(jnp.stack([k[:, sp - keep:, gs], v[:, sp - keep:, gs]], axis=1)
                               .reshape(bp, 2, keep, N_HEADS, HEAD_DIM))
            xs_rows = xs.reshape(1, ns, D_MODEL)
            qs, ks, vs = _qkv(xs_rows, ms_rows[0], ms_rows[1], norm_mix[i], wqkv, attn_q_norm[j],
                              attn_k_norm[j], tab_s, ones_att, t_tile=ns)
            qs, ks, vs = (t.reshape(bs, ts, N_GROUPS * ATT) for t in (qs, ks, vs))
            res = _cache_attention(qs, ks, vs, [c.reshape(c.shape[:4] + (ATT,)) for c in caches], j)
            outs_s = [r.reshape(1, ns, ATT) for r in res[:N_GROUPS]]
            lses_s = [r.reshape(1, ns, ATT) for r in res[N_GROUPS:]]
            xs = _merge(outs_s, lses_s, xs_rows, ms_rows[2], wo, t_tile=ns).reshape(bs, ts, D_MODEL)
            for g in range(N_GROUPS):
                gs = slice(g * ATT, (g + 1) * ATT)
                kv_s[g].append(jnp.stack([ks[:, :, gs], vs[:, :, gs]], axis=1)
                               .reshape(bs, 2, ts, N_HEADS, HEAD_DIM))
        pw = _peer_weights(peer_w_q[i], peer_sub_keys[i], peer_u[i], peer_v[i], ones_d)
        xp = _peer(xp.reshape(bp * sp, D_MODEL), mp[3], mp[4], mp[5], norm_ffn[i], pw,
                   t_tile=512).reshape(bp, sp, D_MODEL)
        xs = _peer(xs.reshape(ns, D_MODEL), ms_rows[3], ms_rows[4], ms_rows[5], norm_ffn[i], pw,
                   t_tile=ns).reshape(bs, ts, D_MODEL)
    return (xp, xs,
            jnp.stack(ssm_p), jnp.stack(kv_p[0]), jnp.stack(kv_p[1]), jnp.stack(kv_p[2]),
            jnp.stack(ssm_s), jnp.stack(kv_s[0]), jnp.stack(kv_s[1]), jnp.stack(kv_s[2]))
```

```python
import functools
import math

import jax
import jax.numpy as jnp
import numpy as np
from jax import lax
from jax.experimental import pallas as pl
from jax.experimental.pallas import tpu as pltpu

F32 = jnp.float32
BF16 = jnp.bfloat16

D_MODEL = 1024
DEPTH = 4
PAST_LEN = 8192
EPS = 1e-6
NEG_INF = -1e30

SSM_P = 16
SSM_GROUPS = D_MODEL // SSM_P
SSM_N = 64
SSM_GB = 4
SSM_GB_CH = D_MODEL // SSM_GB
SSM_GB_ST = SSM_GROUPS * SSM_N // SSM_GB
SSM_STATE_W = 2 * SSM_GROUPS * SSM_N

N_HEADS = 8
HEAD_DIM = 64
ATT = N_HEADS * HEAD_DIM
WINDOWS = (128, 512, 2048)
DILATIONS = (1, 4, 16)
N_GROUPS = 3
N_BACK = 128
ROPE_DIM = HEAD_DIM // 4
ROPE_THETA = 500000.0

PEER_HEADS = 8
PEER_N_KEYS = 128
PEER_HALF = 64
PEER_TOPK = 16
PEER_SEL_TOK = 128
PEER_EXP_TILE = 1024
PEER_I1_TILE = PEER_EXP_TILE // PEER_N_KEYS
PEER_CHUNK = 256

VMEM_LIMIT = 56 * 1024 * 1024


def _cparams(sem):
    return pltpu.CompilerParams(dimension_semantics=sem, vmem_limit_bytes=VMEM_LIMIT)


def _modulate(x, gain, shift, scale):
    ms = jnp.mean(x * x, axis=-1, keepdims=True)
    return x * lax.rsqrt(ms + EPS) * gain * (1.0 + scale) + shift


def _seg_mean(x2, ones_bd):
    hi = x2.astype(BF16)
    lo = (x2 - hi.astype(F32)).astype(BF16)
    s = jnp.dot(hi, ones_bd, preferred_element_type=F32) + jnp.dot(lo, ones_bd, preferred_element_type=F32)
    return s * (1.0 / PEER_HALF)


def _ones_blockdiag(n):
    seg = np.arange(n) // 64
    return jnp.asarray((seg[:, None] == seg[None, :]).astype(np.float32), dtype=BF16)


def _adaln_kernel(c_ref, w_ref, b_ref, o_ref):
    s = jax.nn.silu(c_ref[...])
    o_ref[0] = jnp.dot(s.astype(BF16), w_ref[0].astype(BF16), preferred_element_type=F32) + b_ref[0]


def _adaln(c_all, ada_w, ada_b):
    nb = c_all.shape[0]
    nt = 6 * D_MODEL // 1024
    return pl.pallas_call(
        _adaln_kernel,
        grid=(DEPTH, nt),
        in_specs=[
            pl.BlockSpec((nb, D_MODEL), lambda l, j: (0, 0)),
            pl.BlockSpec((1, D_MODEL, 1024), lambda l, j: (l, 0, j)),
            pl.BlockSpec((1, 1, 1024), lambda l, j: (l, 0, j)),
        ],
        out_specs=pl.BlockSpec((1, nb, 1024), lambda l, j: (l, 0, j)),
        out_shape=jax.ShapeDtypeStruct((DEPTH, nb, 6 * D_MODEL), F32),
        compiler_params=_cparams(("parallel", "parallel")),
        name="adaln",
    )(c_all, ada_w, ada_b.reshape(DEPTH, 1, 6 * D_MODEL))


def _s5_kernel(x_ref, shift_ref, scale_ref, gate_ref, gain_ref, h0_ref, bm_ref, cm_ref, are_ref, aim_ref,
               d_ref, wglu_ref, xo_ref, st_ref, bu_scr, state_scr, *, n_batch, n_time):
    i = pl.program_id(0)

    @pl.when(i == 0)
    def _():
        state_scr[...] = h0_ref[...]

    x3 = x_ref[...]
    a0, a1, _ = x3.shape
    rows = a0 * a1
    h3 = _modulate(x3, gain_ref[...], shift_ref[...], scale_ref[...])
    u = h3.reshape(rows, D_MODEL)
    ub = u.astype(BF16)
    for gb in range(SSM_GB):
        bu_scr[:, gb * 2 * SSM_GB_ST:(gb + 1) * 2 * SSM_GB_ST] = jnp.dot(
            ub[:, gb * SSM_GB_CH:(gb + 1) * SSM_GB_CH], bm_ref[gb], preferred_element_type=F32)

    for bb in range(n_batch // 8):
        for gb in range(SSM_GB):
            c_re = gb * 2 * SSM_GB_ST
            c_im = c_re + SSM_GB_ST
            ar = jnp.broadcast_to(are_ref[gb], (8, SSM_GB_ST))
            ai = jnp.broadcast_to(aim_ref[gb], (8, SSM_GB_ST))
            xr = state_scr[bb * 8:(bb + 1) * 8, c_re:c_re + SSM_GB_ST]
            xi = state_scr[bb * 8:(bb + 1) * 8, c_im:c_im + SSM_GB_ST]
            for t in range(n_time):
                rsel = pl.ds(t * n_batch + bb * 8, 8)
                br = bu_scr[rsel, c_re:c_re + SSM_GB_ST]
                bi = bu_scr[rsel, c_im:c_im + SSM_GB_ST]
                nr = ar * xr - ai * xi + br
                ni = ar * xi + ai * xr + bi
                bu_scr[rsel, c_re:c_re + SSM_GB_ST] = nr
                bu_scr[rsel, c_im:c_im + SSM_GB_ST] = ni
                xr, xi = nr, ni
            state_scr[bb * 8:(bb + 1) * 8, c_re:c_re + SSM_GB_ST] = xr
            state_scr[bb * 8:(bb + 1) * 8, c_im:c_im + SSM_GB_ST] = xi

    ys = []
    for gb in range(SSM_GB):
        st = bu_scr[:, gb * 2 * SSM_GB_ST:(gb + 1) * 2 * SSM_GB_ST].astype(BF16)
        ys.append(jnp.dot(st, cm_ref[gb], preferred_element_type=F32))
    y = jnp.concatenate(ys, axis=-1) + d_ref[...] * u
    z = jnp.dot(y.astype(BF16), wglu_ref[...], preferred_element_type=F32)
    out = z[:, :D_MODEL] * jax.nn.sigmoid(z[:, D_MODEL:])
    xo_ref[...] = x3 + gate_ref[...] * out.reshape(a0, a1, D_MODEL)

    @pl.when(i == pl.num_programs(0) - 1)
    def _():
        st_ref[...] = state_scr[...]


def _s5_weights(lam_re, lam_im, log_dt, b_re, b_im, c_re, c_im):
    dt = jnp.exp(log_dt)[:, None]
    mag = jnp.exp(lam_re * dt)
    a_re = mag * jnp.cos(lam_im * dt)
    a_im = mag * jnp.sin(lam_im * dt)
    den = lam_re * lam_re + lam_im * lam_im
    f_re = ((a_re - 1.0) * lam_re + a_im * lam_im) / den
    f_im = (a_im * lam_re - (a_re - 1.0) * lam_im) / den
    bb_re = f_re[..., None] * b_re - f_im[..., None] * b_im
    bb_im = f_re[..., None] * b_im + f_im[..., None] * b_re
    eye = jnp.eye(16, dtype=F32)

    def in_blk(b):
        b4 = b.reshape(SSM_GB, 16, SSM_N, SSM_P)
        return jnp.einsum('bgnp,gh->bgphn', b4, eye).reshape(SSM_GB, SSM_GB_CH, SSM_GB_ST)

    def out_blk(c):
        c4 = c.reshape(SSM_GB, 16, SSM_P, SSM_N)
        return jnp.einsum('bgpn,gh->bgnhp', c4, eye).reshape(SSM_GB, SSM_GB_ST, SSM_GB_CH)

    bm = jnp.concatenate([in_blk(bb_re), in_blk(bb_im)], axis=2).astype(BF16)
    cm = jnp.concatenate([out_blk(c_re), -out_blk(c_im)], axis=1).astype(BF16)
    return bm, cm, a_re.reshape(SSM_GB, 1, SSM_GB_ST), a_im.reshape(SSM_GB, 1, SSM_GB_ST)


def _state_to_rows(st):
    b = st.shape[0]
    return st.reshape(b, 2, SSM_GB, SSM_GB_ST).transpose(0, 2, 1, 3).reshape(b, SSM_STATE_W)


def _rows_to_state(rows):
    b = rows.shape[0]
    return rows.reshape(b, SSM_GB, 2, SSM_GB_ST).transpose(0, 2, 1, 3).reshape(b, 2, SSM_GROUPS, SSM_N)


def _s5_layer(x3, shift, scale, gate, gain, h0_rows, weights, d_skip, w_glu, *, t_chunk):
    bm, cm, a_re, a_im = weights
    seq, n_batch = x3.shape[0], x3.shape[1]
    n_time = t_chunk
    blk = (n_time, n_batch, D_MODEL)
    xmap = lambda i: (i, 0, 0)
    mblk = (1, n_batch, D_MODEL)
    grid = seq // n_time
    rows = n_batch * n_time
    const3 = lambda i: (0, 0, 0)
    const2 = lambda i: (0, 0)
    kern = functools.partial(_s5_kernel, n_batch=n_batch, n_time=n_time)
    return pl.pallas_call(
        kern,
        grid=(grid,),
        in_specs=[
            pl.BlockSpec(blk, xmap),
            pl.BlockSpec(mblk, const3), pl.BlockSpec(mblk, const3), pl.BlockSpec(mblk, const3),
            pl.BlockSpec((1, 1, D_MODEL), const3),
            pl.BlockSpec((n_batch, SSM_STATE_W), const2),
            pl.BlockSpec(bm.shape, const3), pl.BlockSpec(cm.shape, const3),
            pl.BlockSpec(a_re.shape, const3), pl.BlockSpec(a_im.shape, const3),
            pl.BlockSpec((1, D_MODEL), const2),
            pl.BlockSpec((D_MODEL, 2 * D_MODEL), const2),
        ],
        out_specs=[pl.BlockSpec(blk, xmap), pl.BlockSpec((n_batch, SSM_STATE_W), const2)],
        out_shape=[jax.ShapeDtypeStruct(x3.shape, F32), jax.ShapeDtypeStruct((n_batch, SSM_STATE_W), F32)],
        scratch_shapes=[pltpu.VMEM((rows, SSM_STATE_W), F32), pltpu.VMEM((n_batch, SSM_STATE_W), F32)],
        compiler_params=_cparams(("arbitrary",)),
        name="s5_layer",
    )(x3, shift, scale, gate, gain.reshape(1, 1, D_MODEL), h0_rows, bm, cm, a_re, a_im,
      d_skip.reshape(1, D_MODEL), w_glu)


def _qkv_kernel(x_ref, shift_ref, scale_ref, gain_ref, w_ref, qg_ref, kg_ref, rc_ref, rs1_ref, rs2_ref,
                ones_ref, q_ref, k_ref, v_ref):
    h = _modulate(x_ref[0], gain_ref[...], shift_ref[0], scale_ref[0])
    qkv = jnp.dot(h.astype(BF16), w_ref[...], preferred_element_type=F32)
    rc, rs1, rs2 = rc_ref[...], rs1_ref[...], rs2_ref[...]
    ones_bd = ones_ref[...]

    def norm_rope(t, g):
        tn = t * lax.rsqrt(_seg_mean(t * t, ones_bd) + EPS) * g
        return tn * rc + pltpu.roll(tn, ATT - ROPE_DIM // 2, 1) * rs1 + pltpu.roll(tn, ROPE_DIM // 2, 1) * rs2

    q_ref[0] = norm_rope(qkv[:, :ATT], qg_ref[...])
    k_ref[0] = norm_rope(qkv[:, ATT:2 * ATT], kg_ref[...])
    v_ref[0] = qkv[:, 2 * ATT:]


def _rope_tables(pos):
    half = ROPE_DIM // 2
    inv = ROPE_THETA ** (-jnp.arange(half, dtype=F32) / half)
    ang = pos.astype(F32)[:, None] * inv[None, :]
    cos, sin = jnp.cos(ang), jnp.sin(ang)
    lane = np.arange(ATT) % HEAD_DIM
    fidx = lane % half
    first = jnp.asarray(lane < half)
    second = jnp.asarray((lane >= half) & (lane < ROPE_DIM))
    cl, sl = cos[:, fidx], sin[:, fidx]
    rc = jnp.where(first | second, cl, 1.0)
    rs1 = jnp.where(first, -sl, 0.0)
    rs2 = jnp.where(second, sl, 0.0)
    return rc, rs1, rs2


def _qkv(x3, shift, scale, gain, w_qkv_bf, q_gain, k_gain, tables, ones_bd, *, t_tile):
    bq, sq, _ = x3.shape
    mrows = shift.shape[1]
    mt = t_tile if mrows == sq else 1
    mmap = (lambda b, i, g: (b, i, 0)) if mrows == sq else (lambda b, i, g: (b, 0, 0))
    rc, rs1, rs2 = tables
    qg = jnp.tile(q_gain, N_HEADS).reshape(1, ATT)
    kg = jnp.tile(k_gain, N_HEADS).reshape(1, ATT)
    tmap = lambda b, i, g: (i, 0)
    cmap = lambda b, i, g: (0, 0)
    omap = lambda b, i, g: (b, i, g)
    oshape = jax.ShapeDtypeStruct((bq, sq, N_GROUPS * ATT), F32)
    return pl.pallas_call(
        _qkv_kernel,
        grid=(bq, sq // t_tile, N_GROUPS),
        in_specs=[
            pl.BlockSpec((1, t_tile, D_MODEL), lambda b, i, g: (b, i, 0)),
            pl.BlockSpec((1, mt, D_MODEL), mmap), pl.BlockSpec((1, mt, D_MODEL), mmap),
            pl.BlockSpec((1, D_MODEL), cmap),
            pl.BlockSpec((D_MODEL, 3 * ATT), lambda b, i, g: (0, g)),
            pl.BlockSpec((1, ATT), cmap), pl.BlockSpec((1, ATT), cmap),
            pl.BlockSpec((t_tile, ATT), tmap), pl.BlockSpec((t_tile, ATT), tmap), pl.BlockSpec((t_tile, ATT), tmap),
            pl.BlockSpec((ATT, ATT), cmap),
        ],
        out_specs=[pl.BlockSpec((1, t_tile, ATT), omap)] * 3,
        out_shape=[oshape] * 3,
        compiler_params=_cparams(("parallel", "parallel", "arbitrary")),
        name="qkv_proj",
    )(x3, shift, scale, gain.reshape(1, D_MODEL), w_qkv_bf, qg, kg, rc, rs1, rs2, ones_bd)


def _band_kernel(q_ref, kp_ref, kc_ref, vp_ref, vc_ref, o_ref, l_ref):
    i = pl.program_id(2)
    q = q_ref[0].astype(BF16)
    k = jnp.concatenate([kp_ref[0], kc_ref[0]], axis=0).astype(BF16)
    v = jnp.concatenate([vp_ref[0], vc_ref[0]], axis=0).astype(BF16)
    iq = lax.broadcasted_iota(jnp.int32, (N_BACK, 2 * N_BACK), 0)
    ik = lax.broadcasted_iota(jnp.int32, (N_BACK, 2 * N_BACK), 1)
    dist = iq + N_BACK - ik
    mask = (dist >= 0) & (dist <= N_BACK) & ((ik >= N_BACK) | (i > 0))
    for h in range(N_HEADS):
        sl = slice(h * HEAD_DIM, (h + 1) * HEAD_DIM)
        s = lax.dot_general(q[:, sl], k[:, sl], (((1,), (1,)), ((), ())),
                            preferred_element_type=F32) * (HEAD_DIM ** -0.5)
        s = jnp.where(mask, s, NEG_INF)
        m = jnp.max(s, axis=-1, keepdims=True)
        p = jnp.exp(s - m)
        den = jnp.sum(p, axis=-1, keepdims=True)
        o = jnp.dot(p.astype(BF16), v[:, sl], preferred_element_type=F32) / den
        o_ref[0, :, sl] = o
        l_ref[0, :, sl] = jnp.broadcast_to(m + jnp.log(den), (N_BACK, HEAD_DIM))


def _band_attention(q, k, v, g):
    b, s, w = q.shape
    d = DILATIONS[g]
    sub = s // d
    nb = sub // N_BACK
    ncol = w // ATT
    qv, kv, vv = (t.reshape(b, sub, d * w) for t in (q, k, v))
    cur = lambda bi, r, i: (bi, i, r * ncol + g)
    prev = lambda bi, r, i: (bi, jnp.maximum(i - 1, 0), r * ncol + g)
    omap = lambda bi, r, i: (bi, i, r)
    blk = (1, N_BACK, ATT)
    oshape = jax.ShapeDtypeStruct((b, sub, d * ATT), F32)
    o, l = pl.pallas_call(
        _band_kernel,
        grid=(b, d, nb),
        in_specs=[pl.BlockSpec(blk, cur), pl.BlockSpec(blk, prev), pl.BlockSpec(blk, cur),
                  pl.BlockSpec(blk, prev), pl.BlockSpec(blk, cur)],
        out_specs=[pl.BlockSpec(blk, omap), pl.BlockSpec(blk, omap)],
        out_shape=[oshape, oshape],
        compiler_params=_cparams(("parallel", "parallel", "arbitrary")),
        name="band_attention",
    )(qv, kv, kv, vv, vv)
    return o.reshape(b, s, ATT), l.reshape(b, s, ATT)


def _cache_attn_kernel(q_ref, k_ref, v_ref, c0_ref, c1_ref, c2_ref, *out_refs):
    n_new = q_ref.shape[1]
    rows = n_new * N_HEADS
    lane_head = lax.broadcasted_iota(jnp.int32, (N_HEADS, ATT), 1) // HEAD_DIM
    head_mask = (lane_head == lax.broadcasted_iota(jnp.int32, (N_HEADS, ATT), 0)).astype(F32)
    row_t = lax.broadcasted_iota(jnp.int32, (rows, 1), 0) // N_HEADS
    caches = (c0_ref, c1_ref, c2_ref)
    for g in range(N_GROUPS):
        d, win = DILATIONS[g], WINDOWS[g]
        gs = slice(g * ATT, (g + 1) * ATT)
        q = q_ref[0][:, gs]
        kn = k_ref[0][:, gs]
        vn = v_ref[0][:, gs]
        kc = caches[g][0, 0, 0].astype(BF16)
        vc = caches[g][0, 0, 1].astype(BF16)
        qbd = jnp.concatenate([q[t:t + 1, :] * head_mask for t in range(n_new)], axis=0)
        s_c = lax.dot_general(qbd.astype(BF16), kc, (((1,), (1,)), ((), ())),
                              preferred_element_type=F32) * (HEAD_DIM ** -0.5)
        col = lax.broadcasted_iota(jnp.int32, (rows, win), 1)
        valid_c = (col >= row_t) & (((col - row_t) & (d - 1)) == 0)
        s_c = jnp.where(valid_c, s_c, NEG_INF)
        s_n = []
        for t2 in range(n_new):
            sn = jnp.sum(qbd * kn[t2:t2 + 1, :], axis=-1, keepdims=True) * (HEAD_DIM ** -0.5)
            valid_n = (row_t >= t2) & (((row_t - t2) & (d - 1)) == 0)
            s_n.append(jnp.where(valid_n, sn, NEG_INF))
        m = jnp.max(s_c, axis=-1, keepdims=True)
        for sn in s_n:
            m = jnp.maximum(m, sn)
        p_c = jnp.exp(s_c - m)
        den = jnp.sum(p_c, axis=-1, keepdims=True)
        o = jnp.dot(p_c.astype(BF16), vc, preferred_element_type=F32)
        for t2 in range(n_new):
            p_n = jnp.exp(s_n[t2] - m)
            den = den + p_n
            o = o + p_n * vn[t2:t2 + 1, :]
        o = o / den
        lse = m + jnp.log(den)
        for t in range(n_new):
            rs = slice(t * N_HEADS, (t + 1) * N_HEADS)
            out_refs[g][0, t:t + 1, :] = jnp.sum(o[rs] * head_mask, axis=0, keepdims=True)
            out_refs[N_GROUPS + g][0, t:t + 1, :] = jnp.sum(lse[rs] * head_mask, axis=0, keepdims=True)


def _cache_attention(q, k, v, caches, layer):
    bd, t, w = q.shape
    xmap = lambda b: (b, 0, 0)
    cmap = lambda b: (layer, b, 0, 0, 0)
    oshape = jax.ShapeDtypeStruct((bd, t, ATT), F32)
    return pl.pallas_call(
        _cache_attn_kernel,
        grid=(bd,),
        in_specs=[pl.BlockSpec((1, t, w), xmap)] * 3
        + [pl.BlockSpec((1, 1, 2, WINDOWS[g], ATT), cmap) for g in range(N_GROUPS)],
        out_specs=[pl.BlockSpec((1, t, ATT), xmap)] * (2 * N_GROUPS),
        out_shape=[oshape] * (2 * N_GROUPS),
        compiler_params=_cparams(("parallel",)),
        name="cache_attention",
    )(q, k, v, *caches)


def _merge_kernel(o0, o1, o2, l0, l1, l2, x_ref, gate_ref, wo_ref, xo_ref):
    la, lb, lc = l0[0], l1[0], l2[0]
    m = jnp.maximum(jnp.maximum(la, lb), lc)
    ea, eb, ec = jnp.exp(la - m), jnp.exp(lb - m), jnp.exp(lc - m)
    den = ea + eb + ec
    o = (ea / den) * o0[0] + (eb / den) * o1[0] + (ec / den) * o2[0]
    out = jnp.dot(o.astype(BF16), wo_ref[...], preferred_element_type=F32)
    xo_ref[0] = x_ref[0] + gate_ref[0] * out


def _merge(outs, lses, x3, gate, w_o_bf, *, t_tile):
    bq, sq, _ = x3.shape
    mrows = gate.shape[1]
    mt = t_tile if mrows == sq else 1
    mmap = (lambda b, i: (b, i, 0)) if mrows == sq else (lambda b, i: (b, 0, 0))
    tmap = lambda b, i: (b, i, 0)
    ablk = pl.BlockSpec((1, t_tile, ATT), tmap)
    return pl.pallas_call(
        _merge_kernel,
        grid=(bq, sq // t_tile),
        in_specs=[ablk] * 6 + [pl.BlockSpec((1, t_tile, D_MODEL), tmap), pl.BlockSpec((1, mt, D_MODEL), mmap),
                               pl.BlockSpec((ATT, D_MODEL), lambda b, i: (0, 0))],
        out_specs=pl.BlockSpec((1, t_tile, D_MODEL), tmap),
        out_shape=jax.ShapeDtypeStruct(x3.shape, F32),
        compiler_params=_cparams(("parallel", "parallel")),
        name="attn_merge",
    )(*outs, *lses, x3, gate, w_o_bf)


def _top16(s):
    vals = s
    rank = jnp.full(s.shape, float(PEER_N_KEYS), F32)
    tops = []
    for r in range(PEER_TOPK):
        m = jnp.max(vals, axis=0, keepdims=True)
        hit = vals == m
        rank = jnp.where(hit, float(r), rank)
        vals = jnp.where(hit, -jnp.inf, vals)
        tops.append(m)
    return tops, rank


def _peer_route_kernel(x_ref, shift_ref, scale_ref, gain_ref, wq_ref, sk_ref, ones_ref,
                       xt_ref, r1_ref, c1_ref, nf_ref, s_scr):
    hm = _modulate(x_ref[...], gain_ref[...], shift_ref[0], scale_ref[0])
    hb = hm.astype(BF16)
    xt_ref[...] = hm.T.astype(BF16)
    q = jnp.dot(hb, wq_ref[...], preferred_element_type=F32)
    qn = q * lax.rsqrt(_seg_mean(q * q, ones_ref[...]) + EPS)
    s_scr[...] = lax.dot_general(sk_ref[...], qn.astype(BF16), (((1,), (1,)), ((), ())),
                                 preferred_element_type=F32)

    def head(h, carry):
        base = pl.multiple_of(h * 2 * PEER_N_KEYS, 2 * PEER_N_KEYS)
        s1 = s_scr[pl.ds(base, PEER_N_KEYS), :]
        s2 = s_scr[pl.ds(base + PEER_N_KEYS, PEER_N_KEYS), :]
        top1, rank1 = _top16(s1)
        top2, rank2 = _top16(s2)
        row = lax.broadcasted_iota(jnp.int32, (PEER_TOPK, s1.shape[1]), 0)
        v1 = jnp.zeros((PEER_TOPK, s1.shape[1]), F32)
        for r in range(PEER_TOPK):
            v1 = jnp.where(row == r, top1[r], v1)
        cands = [v1[:8] + top2[b] for b in range(PEER_TOPK)] + [v1[8:] + top2[0]]
        best = top1[0] + top2[0]
        taken = [jnp.zeros_like(c) for c in cands]
        zsum = jnp.zeros_like(best)
        for _ in range(PEER_TOPK):
            m = cands[0]
            for c in cands[1:]:
                m = jnp.maximum(m, c)
            m = jnp.max(m, axis=0, keepdims=True)
            for j in range(len(cands)):
                hit = cands[j] == m
                taken[j] = jnp.where(hit, 1.0, taken[j])
                cands[j] = jnp.where(hit, -jnp.inf, cands[j])
            zsum = zsum + jnp.exp(m - best)
        cnt_lo = taken[0]
        for j in range(1, PEER_TOPK):
            cnt_lo = cnt_lo + taken[j]
        cnt = jnp.concatenate([cnt_lo, taken[PEER_TOPK]], axis=0)
        n2 = jnp.zeros_like(rank2)
        for a in range(PEER_TOPK):
            n2 = n2 + jnp.where(rank2 < cnt[a:a + 1], 1.0, 0.0)
        r1_ref[0, h] = rank1
        c1_ref[0, h] = jnp.exp(s1 - top1[0])
        f2 = jnp.exp(s2 - top2[0]) / zsum
        nf_ref[0, h, :, 0] = pltpu.bitcast(n2.astype(BF16), jnp.int32).reshape(PEER_N_KEYS // 16, 8, PEER_SEL_TOK)
        nf_ref[0, h, :, 1] = pltpu.bitcast(f2.astype(BF16), jnp.int32).reshape(PEER_N_KEYS // 16, 8, PEER_SEL_TOK)
        return carry

    lax.fori_loop(0, PEER_HEADS, head, 0)


def _peer_route(x2, shift, scale, gain, wq_bf, sk_bd, ones_bd):
    n = x2.shape[0]
    tt = PEER_SEL_TOK
    nt = n // tt
    per_mod = nt // shift.shape[0]
    mrows = shift.shape[1]
    if mrows == 1:
        mblk, mmap = (1, 1, D_MODEL), (lambda i: (i // per_mod, 0, 0))
    else:
        mblk, mmap = (1, tt, D_MODEL), (lambda i: (0, i, 0))
    cmap = lambda i: (0, 0)
    nrt = PEER_N_KEYS // 16
    hshape = jax.ShapeDtypeStruct((nt, PEER_HEADS, PEER_N_KEYS, tt), F32)
    hblk = pl.BlockSpec((1, PEER_HEADS, PEER_N_KEYS, tt), lambda i: (i, 0, 0, 0))
    nfshape = jax.ShapeDtypeStruct((nt, PEER_HEADS, nrt, 2, 8, tt), jnp.int32)
    nfblk = pl.BlockSpec((1, PEER_HEADS, nrt, 2, 8, tt), lambda i: (i, 0, 0, 0, 0, 0))
    return pl.pallas_call(
        _peer_route_kernel,
        grid=(nt,),
        in_specs=[
            pl.BlockSpec((tt, D_MODEL), lambda i: (i, 0)),
            pl.BlockSpec(mblk, mmap), pl.BlockSpec(mblk, mmap),
            pl.BlockSpec((1, D_MODEL), cmap),
            pl.BlockSpec((D_MODEL, D_MODEL), cmap),
            pl.BlockSpec((2 * PEER_HEADS * PEER_N_KEYS, D_MODEL), cmap),
            pl.BlockSpec((D_MODEL, D_MODEL), cmap),
        ],
        out_specs=[pl.BlockSpec((D_MODEL, tt), lambda i: (0, i)), hblk, hblk, nfblk],
        out_shape=[jax.ShapeDtypeStruct((D_MODEL, n), BF16), hshape, hshape, nfshape],
        scratch_shapes=[pltpu.VMEM((2 * PEER_HEADS * PEER_N_KEYS, tt), F32)],
        compiler_params=_cparams(("parallel",)),
        name="peer_route",
    )(x2, shift, scale, gain.reshape(1, D_MODEL), wq_bf, sk_bd, ones_bd)


def _gelu(x):
    return 0.5 * x * (1.0 + lax.erf(x * np.float32(math.sqrt(0.5))))


def _peer_expert_kernel(xt_ref, u_ref, vt_ref, r1_ref, c1_ref, nf_ref, x_ref, gate_ref, xo_ref,
                        act_scr, w_scr, acc_scr, *, t_tile):
    e = pl.program_id(1)

    @pl.when(e == 0)
    def _():
        acc_scr[...] = jnp.zeros_like(acc_scr)

    def stage_a(q):
        qs = slice(q * PEER_CHUNK, (q + 1) * PEER_CHUNK)
        act_scr[q] = jnp.dot(u_ref[qs, :], xt_ref[...], preferred_element_type=F32)

    def stage_b(q):
        for i1l in range(PEER_CHUNK // PEER_N_KEYS):
            i1 = q * (PEER_CHUNK // PEER_N_KEYS) + i1l
            for lc in range(t_tile // PEER_SEL_TOK):
                ls = slice(lc * PEER_SEL_TOK, (lc + 1) * PEER_SEL_TOK)
                r1 = [jnp.broadcast_to(r1_ref[lc, h, i1:i1 + 1, :], (16, PEER_SEL_TOK)).astype(BF16)
                      for h in range(PEER_HEADS)]
                c1 = [jnp.broadcast_to(c1_ref[lc, h, i1:i1 + 1, :], (16, PEER_SEL_TOK)).astype(BF16)
                      for h in range(PEER_HEADS)]
                for j in range(PEER_N_KEYS // 16):
                    g = jnp.zeros((16, PEER_SEL_TOK), BF16)
                    for h in range(PEER_HEADS):
                        n2 = pltpu.bitcast(nf_ref[lc, h, j, 0], BF16)
                        f2 = pltpu.bitcast(nf_ref[lc, h, j, 1], BF16)
                        g = g + jnp.where(n2 > r1[h], f2, 0.0) * c1[h]
                    rs = slice(i1l * PEER_N_KEYS + j * 16, i1l * PEER_N_KEYS + (j + 1) * 16)
                    w_scr[q, rs, ls] = g * _gelu(act_scr[q, rs, ls]).astype(BF16)

    def stage_c(q):
        qs = slice(q * PEER_CHUNK, (q + 1) * PEER_CHUNK)
        acc_scr[...] += jnp.dot(vt_ref[:, qs], w_scr[q], preferred_element_type=F32)

    nq = PEER_EXP_TILE // PEER_CHUNK
    stage_a(0)
    for q in range(nq):
        if q + 1 < nq:
            stage_a(q + 1)
        stage_b(q)
        if q > 0:
            stage_c(q - 1)
    stage_c(nq - 1)

    @pl.when(e == pl.num_programs(1) - 1)
    def _():
        xo_ref[...] = x_ref[...] + gate_ref[0] * acc_scr[...].T


def _peer_expert(xt, u_bf, vt_bf, r1, c1, nf, x2, gate, *, t_tile):
    n = x2.shape[0]
    nt = n // t_tile
    nlc = t_tile // PEER_SEL_TOK
    ne = u_bf.shape[0] // PEER_EXP_TILE
    per_mod = max(nt // gate.shape[0], 1)
    mrows = gate.shape[1]
    if mrows == 1:
        mblk, mmap = (1, 1, D_MODEL), (lambda i, e: (i // per_mod, 0, 0))
    else:
        mblk, mmap = (1, t_tile, D_MODEL), (lambda i, e: (0, i, 0))
    i1blk = pl.BlockSpec((nlc, PEER_HEADS, PEER_I1_TILE, PEER_SEL_TOK), lambda i, e: (i, 0, e, 0))
    nfblk = pl.BlockSpec((nlc, PEER_HEADS, PEER_N_KEYS // 16, 2, 8, PEER_SEL_TOK), lambda i, e: (i, 0, 0, 0, 0, 0))
    kern = functools.partial(_peer_expert_kernel, t_tile=t_tile)
    return pl.pallas_call(
        kern,
        grid=(nt, ne),
        in_specs=[
            pl.BlockSpec((D_MODEL, t_tile), lambda i, e: (0, i)),
            pl.BlockSpec((PEER_EXP_TILE, D_MODEL), lambda i, e: (e, 0)),
            pl.BlockSpec((D_MODEL, PEER_EXP_TILE), lambda i, e: (0, e)),
            i1blk, i1blk, nfblk,
            pl.BlockSpec((t_tile, D_MODEL), lambda i, e: (i, 0)),
            pl.BlockSpec(mblk, mmap),
        ],
        out_specs=pl.BlockSpec((t_tile, D_MODEL), lambda i, e: (i, 0)),
        out_shape=jax.ShapeDtypeStruct(x2.shape, F32),
        scratch_shapes=[pltpu.VMEM((PEER_EXP_TILE // PEER_CHUNK, PEER_CHUNK, t_tile), F32),
                        pltpu.VMEM((PEER_EXP_TILE // PEER_CHUNK, PEER_CHUNK, t_tile), BF16),
                        pltpu.VMEM((D_MODEL, t_tile), F32)],
        compiler_params=_cparams(("parallel", "arbitrary")),
        name="peer_expert",
    )(xt, u_bf, vt_bf, r1, c1, nf, x2, gate)


def _peer(x2, shift, scale, gate, gain, pw, *, t_tile):
    wq_bf, sk_bd, u_bf, vt_bf, ones_bd = pw
    xt, r1, c1, nf = _peer_route(x2, shift, scale, gain, wq_bf, sk_bd, ones_bd)
    return _peer_expert(xt, u_bf, vt_bf, r1, c1, nf, x2, gate, t_tile=t_tile)


def _peer_weights(w_q, sub_keys, u_tab, v_tab, ones_bd):
    hc = 2 * PEER_HEADS
    sk = sub_keys.reshape(hc, PEER_N_KEYS, PEER_HALF)
    sk_bd = jnp.einsum('akd,ab->akbd', sk, jnp.eye(hc, dtype=F32)).reshape(hc * PEER_N_KEYS, hc * PEER_HALF)
    return (w_q.astype(BF16), sk_bd.astype(BF16), u_tab.astype(BF16), v_tab.astype(BF16).T, ones_bd)


def kernel(x_prompt, x_sample, c_prompt, c_sample, state_ssm, cache_kv_w128, cache_kv_w512, cache_kv_w2048,
           norm_mix, norm_ffn, ada_w, ada_b, ssm_lam_re, ssm_lam_im, ssm_log_dt, ssm_b_re, ssm_b_im, ssm_c_re,
           ssm_c_im, ssm_d, ssm_w_glu, attn_w_qkv, attn_q_norm, attn_k_norm, attn_w_o, peer_w_q,
           peer_sub_keys, peer_u, peer_v):
    bp, sp, _ = x_prompt.shape
    bs, ts, _ = x_sample.shape
    ns = bs * ts
    ones_att = _ones_blockdiag(ATT)
    ones_d = _ones_blockdiag(D_MODEL)

    mods = _adaln(jnp.concatenate([c_prompt, c_sample], axis=0), ada_w, ada_b)
    caches = (cache_kv_w128, cache_kv_w512, cache_kv_w2048)
    tab_p = _rope_tables(jnp.arange(sp, dtype=jnp.int32))
    pos_s = PAST_LEN + jnp.arange(ts, dtype=jnp.int32)
    tab_s = tuple(jnp.tile(t, (bs, 1)) for t in _rope_tables(pos_s))

    xp, xs = x_prompt, x_sample
    ssm_p, ssm_s = [], []
    kv_p = [[] for _ in range(N_GROUPS)]
    kv_s = [[] for _ in range(N_GROUPS)]
    for i in range(DEPTH):
        j = i // 2
        mp = [m.reshape(bp, 1, D_MODEL) for m in jnp.split(mods[i, :bp], 6, axis=-1)]
        ms_b = jnp.split(mods[i, bp:], 6, axis=-1)
        ms_rows = [jnp.repeat(m, ts, axis=0).reshape(1, ns, D_MODEL) for m in ms_b]
        if i % 2 == 0:
            sw = _s5_weights(ssm_lam_re[j], ssm_lam_im[j], ssm_log_dt[j], ssm_b_re[j], ssm_b_im[j],
                             ssm_c_re[j], ssm_c_im[j])
            wglu = ssm_w_glu[j].astype(BF16)
            mp_t = [m.reshape(1, bp, D_MODEL) for m in mp[:3]]
            xp_t, st_p = _s5_layer(xp.transpose(1, 0, 2), mp_t[0], mp_t[1], mp_t[2], norm_mix[i],
                                   jnp.zeros((bp, SSM_STATE_W), F32), sw, ssm_d[j], wglu, t_chunk=32)
            xp = xp_t.transpose(1, 0, 2)
            ms_t = [m.reshape(1, bs, D_MODEL) for m in ms_b[:3]]
            xs_t, st_s = _s5_layer(xs.transpose(1, 0, 2), ms_t[0], ms_t[1], ms_t[2], norm_mix[i],
                                   _state_to_rows(state_ssm[j]), sw, ssm_d[j], wglu, t_chunk=ts)
            xs = xs_t.transpose(1, 0, 2)
            ssm_p.append(_rows_to_state(st_p))
            ssm_s.append(_rows_to_state(st_s))
        else:
            wqkv = attn_w_qkv[j].astype(BF16)
            wo = attn_w_o[j].astype(BF16)
            q, k, v = _qkv(xp, mp[0], mp[1], norm_mix[i], wqkv, attn_q_norm[j], attn_k_norm[j], tab_p,
                           ones_att, t_tile=512)
            outs, lses = zip(*[_band_attention(q, k, v, g) for g in range(N_GROUPS)])
            xp = _merge(outs, lses, xp, mp[2], wo, t_tile=512)
            for g in range(N_GROUPS):
                keep = min(WINDOWS[g], sp)
                gs = slice(g * ATT, (g + 1) * ATT)
                kv_p[g].append(jnp.stack([k[:, sp - keep:, gs], v[:, sp - keep:, gs]], axis=1)
                               .reshape(bp, 2, keep, N_HEADS, HEAD_DIM))
            xs_rows = xs.reshape(1, ns, D_MODEL)
            qs, ks, vs = _qkv(xs_rows, ms_rows[0], ms_rows[1], norm_mix[i], wqkv, attn_q_norm[j],
                              attn_k_norm[j], tab_s, ones_att, t_tile=ns)
            qs, ks, vs = (t.reshape(bs, ts, N_GROUPS * ATT) for t in (qs, ks, vs))
            res = _cache_attention(qs, ks, vs, [c.reshape(c.shape[:4] + (ATT,)) for c in caches], j)
            outs_s = [r.reshape(1, ns, ATT) for r in res[:N_GROUPS]]
            lses_s = [r.reshape(1, ns, ATT) for r in res[N_GROUPS:]]
            xs = _merge(outs_s, lses_s, xs_rows, ms_rows[2], wo, t_tile=ns).reshape(bs, ts, D_MODEL)
            for g in range(N_GROUPS):
                gs = slice(g * ATT, (g + 1) * ATT)
                kv_s[g].append(jnp.stack([ks[:, :, gs], vs[:, :, gs]], axis=1)
                               .reshape(bs, 2, ts, N_HEADS, HEAD_DIM))
        pw = _peer_weights(peer_w_q[i], peer_sub_keys[i], peer_u[i], peer_v[i], ones_d)
        xp = _peer(xp.reshape(bp * sp, D_MODEL), mp[3], mp[4], mp[5], norm_ffn[i], pw,
                   t_tile=512).reshape(bp, sp, D_MODEL)
        xs = _peer(xs.reshape(ns, D_MODEL), ms_rows[3], ms_rows[4], ms_rows[5], norm_ffn[i], pw,
                   t_tile=ns).reshape(bs, ts, D_MODEL)
    return (xp, xs,
            jnp.stack(ssm_p), jnp.stack(kv_p[0]), jnp.stack(kv_p[1]), jnp.stack(kv_p[2]),
            jnp.stack(ssm_s), jnp.stack(kv_s[0]), jnp.stack(kv_s[1]), jnp.stack(kv_s[2]))
```

```python
import functools
import math

import jax
import jax.numpy as jnp
import numpy as np
from jax import lax
from jax.experimental import pallas as pl
from jax.experimental.pallas import tpu as pltpu

F32 = jnp.float32
BF16 = jnp.bfloat16

D_MODEL = 1024
DEPTH = 4
PAST_LEN = 8192
EPS = 1e-6
NEG_INF = -1e30

SSM_P = 16
SSM_GROUPS = D_MODEL // SSM_P
SSM_N = 64
SSM_GB = 4
SSM_GB_CH = D_MODEL // SSM_GB
SSM_GB_ST = SSM_GROUPS * SSM_N // SSM_GB
SSM_STATE_W = 2 * SSM_GROUPS * SSM_N

N_HEADS = 8
HEAD_DIM = 64
ATT = N_HEADS * HEAD_DIM
WINDOWS = (128, 512, 2048)
DILATIONS = (1, 4, 16)
N_GROUPS = 3
N_BACK = 128
ROPE_DIM = HEAD_DIM // 4
ROPE_THETA = 500000.0

PEER_HEADS = 8
PEER_N_KEYS = 128
PEER_HALF = 64
PEER_TOPK = 16
PEER_SEL_TOK = 128
PEER_EXP_TILE = 1024
PEER_I1_TILE = PEER_EXP_TILE // PEER_N_KEYS
PEER_CHUNK = 256

VMEM_LIMIT = 56 * 1024 * 1024


def _cparams(sem):
    return pltpu.CompilerParams(dimension_semantics=sem, vmem_limit_bytes=VMEM_LIMIT)


def _modulate(x, gain, shift, scale):
    ms = jnp.mean(x * x, axis=-1, keepdims=True)
    return x * lax.rsqrt(ms + EPS) * gain * (1.0 + scale) + shift


def _seg_mean(x2, ones_bd):
    hi = x2.astype(BF16)
    lo = (x2 - hi.astype(F32)).astype(BF16)
    s = jnp.dot(hi, ones_bd, preferred_element_type=F32) + jnp.dot(lo, ones_bd, preferred_element_type=F32)
    return s * (1.0 / PEER_HALF)


def _ones_blockdiag(n):
    seg = np.arange(n) // 64
    return jnp.asarray((seg[:, None] == seg[None, :]).astype(np.float32), dtype=BF16)


def _adaln_kernel(c_ref, w_ref, b_ref, o_ref):
    s = jax.nn.silu(c_ref[...])
    o_ref[0] = jnp.dot(s.astype(BF16), w_ref[0].astype(BF16), preferred_element_type=F32) + b_ref[0]


def _adaln(c_all, ada_w, ada_b):
    nb = c_all.shape[0]
    nt = 6 * D_MODEL // 1024
    return pl.pallas_call(
        _adaln_kernel,
        grid=(DEPTH, nt),
        in_specs=[
            pl.BlockSpec((nb, D_MODEL), lambda l, j: (0, 0)),
            pl.BlockSpec((1, D_MODEL, 1024), lambda l, j: (l, 0, j)),
            pl.BlockSpec((1, 1, 1024), lambda l, j: (l, 0, j)),
        ],
        out_specs=pl.BlockSpec((1, nb, 1024), lambda l, j: (l, 0, j)),
        out_shape=jax.ShapeDtypeStruct((DEPTH, nb, 6 * D_MODEL), F32),
        compiler_params=_cparams(("parallel", "parallel")),
        name="adaln",
    )(c_all, ada_w, ada_b.reshape(DEPTH, 1, 6 * D_MODEL))


def _s5_kernel(x_ref, shift_ref, scale_ref, gate_ref, gain_ref, h0_ref, bm_ref, cm_ref, are_ref, aim_ref,
               d_ref, wglu_ref, xo_ref, st_ref, bu_scr, state_scr, *, n_batch, n_time):
    i = pl.program_id(0)

    @pl.when(i == 0)
    def _():
        state_scr[...] = h0_ref[...]

    x3 = x_ref[...]
    a0, a1, _ = x3.shape
    rows = a0 * a1
    h3 = _modulate(x3, gain_ref[...], shift_ref[...], scale_ref[...])
    u = h3.reshape(rows, D_MODEL)
    ub = u.astype(BF16)
    for gb in range(SSM_GB):
        bu_scr[:, gb * 2 * SSM_GB_ST:(gb + 1) * 2 * SSM_GB_ST] = jnp.dot(
            ub[:, gb * SSM_GB_CH:(gb + 1) * SSM_GB_CH], bm_ref[gb], preferred_element_type=F32)

    for bb in range(n_batch // 8):
        for gb in range(SSM_GB):
            c_re = gb * 2 * SSM_GB_ST
            c_im = c_re + SSM_GB_ST
            ar = jnp.broadcast_to(are_ref[gb], (8, SSM_GB_ST))
            ai = jnp.broadcast_to(aim_ref[gb], (8, SSM_GB_ST))
            xr = state_scr[bb * 8:(bb + 1) * 8, c_re:c_re + SSM_GB_ST]
            xi = state_scr[bb * 8:(bb + 1) * 8, c_im:c_im + SSM_GB_ST]
            for t in range(n_time):
                rsel = pl.ds(t * n_batch + bb * 8, 8)
                br = bu_scr[rsel, c_re:c_re + SSM_GB_ST]
                bi = bu_scr[rsel, c_im:c_im + SSM_GB_ST]
                nr = ar * xr - ai * xi + br
                ni = ar * xi + ai * xr + bi
                bu_scr[rsel, c_re:c_re + SSM_GB_ST] = nr
                bu_scr[rsel, c_im:c_im + SSM_GB_ST] = ni
                xr, xi = nr, ni
            state_scr[bb * 8:(bb + 1) * 8, c_re:c_re + SSM_GB_ST] = xr
            state_scr[bb * 8:(bb + 1) * 8, c_im:c_im + SSM_GB_ST] = xi

    ys = []
    for gb in range(SSM_GB):
        st = bu_scr[:, gb * 2 * SSM_GB_ST:(gb + 1) * 2 * SSM_GB_ST].astype(BF16)
        ys.append(jnp.dot(st, cm_ref[gb], preferred_element_type=F32))
    y = jnp.concatenate(ys, axis=-1) + d_ref[...] * u
    z = jnp.dot(y.astype(BF16), wglu_ref[...], preferred_element_type=F32)
    out = z[:, :D_MODEL] * jax.nn.sigmoid(z[:, D_MODEL:])
    xo_ref[...] = x3 + gate_ref[...] * out.reshape(a0, a1, D_MODEL)

    @pl.when(i == pl.num_programs(0) - 1)
    def _():
        st_ref[...] = state_scr[...]


def _s5_weights(lam_re, lam_im, log_dt, b_re, b_im, c_re, c_im):
    dt = jnp.exp(log_dt)[:, None]
    mag = jnp.exp(lam_re * dt)
    a_re = mag * jnp.cos(lam_im * dt)
    a_im = mag * jnp.sin(lam_im * dt)
    den = lam_re * lam_re + lam_im * lam_im
    f_re = ((a_re - 1.0) * lam_re + a_im * lam_im) / den
    f_im = (a_im * lam_re - (a_re - 1.0) * lam_im) / den
    bb_re = f_re[..., None] * b_re - f_im[..., None] * b_im
    bb_im = f_re[..., None] * b_im + f_im[..., None] * b_re
    eye = jnp.eye(16, dtype=F32)

    def in_blk(b):
        b4 = b.reshape(SSM_GB, 16, SSM_N, SSM_P)
        return jnp.einsum('bgnp,gh->bgphn', b4, eye).reshape(SSM_GB, SSM_GB_CH, SSM_GB_ST)

    def out_blk(c):
        c4 = c.reshape(SSM_GB, 16, SSM_P, SSM_N)
        return jnp.einsum('bgpn,gh->bgnhp', c4, eye).reshape(SSM_GB, SSM_GB_ST, SSM_GB_CH)

    bm = jnp.concatenate([in_blk(bb_re), in_blk(bb_im)], axis=2).astype(BF16)
    cm = jnp.concatenate([out_blk(c_re), -out_blk(c_im)], axis=1).astype(BF16)
    return bm, cm, a_re.reshape(SSM_GB, 1, SSM_GB_ST), a_im.reshape(SSM_GB, 1, SSM_GB_ST)


def _state_to_rows(st):
    b = st.shape[0]
    return st.reshape(b, 2, SSM_GB, SSM_GB_ST).transpose(0, 2, 1, 3).reshape(b, SSM_STATE_W)


def _rows_to_state(rows):
    b = rows.shape[0]
    return rows.reshape(b, SSM_GB, 2, SSM_GB_ST).transpose(0, 2, 1, 3).reshape(b, 2, SSM_GROUPS, SSM_N)


def _s5_layer(x3, shift, scale, gate, gain, h0_rows, weights, d_skip, w_glu, *, t_chunk):
    bm, cm, a_re, a_im = weights
    seq, n_batch = x3.shape[0], x3.shape[1]
    n_time = t_chunk
    blk = (n_time, n_batch, D_MODEL)
    xmap = lambda i: (i, 0, 0)
    mblk = (1, n_batch, D_MODEL)
    grid = seq // n_time
    rows = n_batch * n_time
    const3 = lambda i: (0, 0, 0)
    const2 = lambda i: (0, 0)
    kern = functools.partial(_s5_kernel, n_batch=n_batch, n_time=n_time)
    return pl.pallas_call(
        kern,
        grid=(grid,),
        in_specs=[
            pl.BlockSpec(blk, xmap),
            pl.BlockSpec(mblk, const3), pl.BlockSpec(mblk, const3), pl.BlockSpec(mblk, const3),
            pl.BlockSpec((1, 1, D_MODEL), const3),
            pl.BlockSpec((n_batch, SSM_STATE_W), const2),
            pl.BlockSpec(bm.shape, const3), pl.BlockSpec(cm.shape, const3),
            pl.BlockSpec(a_re.shape, const3), pl.BlockSpec(a_im.shape, const3),
            pl.BlockSpec((1, D_MODEL), const2),
            pl.BlockSpec((D_MODEL, 2 * D_MODEL), const2),
        ],
        out_specs=[pl.BlockSpec(blk, xmap), pl.BlockSpec((n_batch, SSM_STATE_W), const2)],
        out_shape=[jax.ShapeDtypeStruct(x3.shape, F32), jax.ShapeDtypeStruct((n_batch, SSM_STATE_W), F32)],
        scratch_shapes=[pltpu.VMEM((rows, SSM_STATE_W), F32), pltpu.VMEM((n_batch, SSM_STATE_W), F32)],
        compiler_params=_cparams(("arbitrary",)),
        name="s5_layer",
    )(x3, shift, scale, gate, gain.reshape(1, 1, D_MODEL), h0_rows, bm, cm, a_re, a_im,
      d_skip.reshape(1, D_MODEL), w_glu)


def _qkv_kernel(x_ref, shift_ref, scale_ref, gain_ref, w_ref, qg_ref, kg_ref, rc_ref, rs1_ref, rs2_ref,
                ones_ref, q_ref, k_ref, v_ref):
    h = _modulate(x_ref[0], gain_ref[...], shift_ref[0], scale_ref[0])
    qkv = jnp.dot(h.astype(BF16), w_ref[...], preferred_element_type=F32)
    rc, rs1, rs2 = rc_ref[...], rs1_ref[...], rs2_ref[...]
    ones_bd = ones_ref[...]

    def norm_rope(t, g):
        tn = t * lax.rsqrt(_seg_mean(t * t, ones_bd) + EPS) * g
        return tn * rc + pltpu.roll(tn, ATT - ROPE_DIM // 2, 1) * rs1 + pltpu.roll(tn, ROPE_DIM // 2, 1) * rs2

    q_ref[0] = norm_rope(qkv[:, :ATT], qg_ref[...])
    k_ref[0] = norm_rope(qkv[:, ATT:2 * ATT], kg_ref[...])
    v_ref[0] = qkv[:, 2 * ATT:]


def _dilate(x, scr, out_ref, d):
    t = x.shape[0]
    for kc in range(ATT // 128):
        scr[kc] = x[:, kc * 128:(kc + 1) * 128]
    for r in range(d):
        for kc in range(ATT // 128):
            c0 = r * ATT + kc * 128
            out_ref[0, :, c0:c0 + 128] = scr[kc, pl.ds(r, t // d, stride=d), :]


def _qkv_dil_kernel(x_ref, shift_ref, scale_ref, gain_ref, w_ref, qg_ref, kg_ref, rc_ref, rs1_ref, rs2_ref,
                    ones_ref, kn_ref, vn_ref, q0_ref, q1_ref, k1_ref, v1_ref, q2_ref, k2_ref, v2_ref, scr):
    g = pl.program_id(2)
    h = _modulate(x_ref[0], gain_ref[...], shift_ref[0], scale_ref[0])
    qkv = jnp.dot(h.astype(BF16), w_ref[...], preferred_element_type=F32)
    rc, rs1, rs2 = rc_ref[...], rs1_ref[...], rs2_ref[...]
    ones_bd = ones_ref[...]

    def norm_rope(t, gn):
        tn = t * lax.rsqrt(_seg_mean(t * t, ones_bd) + EPS) * gn
        return tn * rc + pltpu.roll(tn, ATT - ROPE_DIM // 2, 1) * rs1 + pltpu.roll(tn, ROPE_DIM // 2, 1) * rs2

    q = norm_rope(qkv[:, :ATT], qg_ref[...])
    k = norm_rope(qkv[:, ATT:2 * ATT], kg_ref[...])
    v = qkv[:, 2 * ATT:]
    kn_ref[0] = k
    vn_ref[0] = v

    @pl.when(g == 0)
    def _():
        q0_ref[0] = q

    for gi, refs in ((1, (q1_ref, k1_ref, v1_ref)), (2, (q2_ref, k2_ref, v2_ref))):
        @pl.when(g == gi)
        def _():
            for val, ref in zip((q, k, v), refs):
                _dilate(val, scr, ref, DILATIONS[gi])


def _qkv_dilated(x3, shift, scale, gain, w_qkv_bf, q_gain, k_gain, tables, ones_bd, *, t_tile):
    bq, sq, _ = x3.shape
    rc, rs1, rs2 = tables
    qg = jnp.tile(q_gain, N_HEADS).reshape(1, ATT)
    kg = jnp.tile(k_gain, N_HEADS).reshape(1, ATT)
    mmap = lambda b, i, g: (b, 0, 0)
    tmap = lambda b, i, g: (i, 0)
    cmap = lambda b, i, g: (0, 0)
    gmap = lambda b, i, g: (b, i, g)
    bmap = lambda b, i, g: (b, i, 0)
    nat = jax.ShapeDtypeStruct((bq, sq, N_GROUPS * ATT), F32)
    out_shape = [nat, nat, jax.ShapeDtypeStruct((bq, sq, ATT), F32)]
    out_specs = [pl.BlockSpec((1, t_tile, ATT), gmap)] * 2 + [pl.BlockSpec((1, t_tile, ATT), bmap)]
    for gi in (1, 2):
        d = DILATIONS[gi]
        out_shape += [jax.ShapeDtypeStruct((bq, sq // d, d * ATT), F32)] * 3
        out_specs += [pl.BlockSpec((1, t_tile // d, d * ATT), bmap)] * 3
    return pl.pallas_call(
        _qkv_dil_kernel,
        grid=(bq, sq // t_tile, N_GROUPS),
        in_specs=[
            pl.BlockSpec((1, t_tile, D_MODEL), bmap),
            pl.BlockSpec((1, 1, D_MODEL), mmap), pl.BlockSpec((1, 1, D_MODEL), mmap),
            pl.BlockSpec((1, D_MODEL), cmap),
            pl.BlockSpec((D_MODEL, 3 * ATT), lambda b, i, g: (0, g)),
            pl.BlockSpec((1, ATT), cmap), pl.BlockSpec((1, ATT), cmap),
            pl.BlockSpec((t_tile, ATT), tmap), pl.BlockSpec((t_tile, ATT), tmap), pl.BlockSpec((t_tile, ATT), tmap),
            pl.BlockSpec((ATT, ATT), cmap),
        ],
        out_specs=out_specs,
        out_shape=out_shape,
        scratch_shapes=[pltpu.VMEM((ATT // 128, t_tile, 128), F32)],
        compiler_params=_cparams(("parallel", "parallel", "arbitrary")),
        name="qkv_proj_dilated",
    )(x3, shift, scale, gain.reshape(1, D_MODEL), w_qkv_bf, qg, kg, rc, rs1, rs2, ones_bd)


def _rope_tables(pos):
    half = ROPE_DIM // 2
    inv = ROPE_THETA ** (-jnp.arange(half, dtype=F32) / half)
    ang = pos.astype(F32)[:, None] * inv[None, :]
    cos, sin = jnp.cos(ang), jnp.sin(ang)
    lane = np.arange(ATT) % HEAD_DIM
    fidx = lane % half
    first = jnp.asarray(lane < half)
    second = jnp.asarray((lane >= half) & (lane < ROPE_DIM))
    cl, sl = cos[:, fidx], sin[:, fidx]
    rc = jnp.where(first | second, cl, 1.0)
    rs1 = jnp.where(first, -sl, 0.0)
    rs2 = jnp.where(second, sl, 0.0)
    return rc, rs1, rs2


def _qkv(x3, shift, scale, gain, w_qkv_bf, q_gain, k_gain, tables, ones_bd, *, t_tile):
    bq, sq, _ = x3.shape
    mrows = shift.shape[1]
    mt = t_tile if mrows == sq else 1
    mmap = (lambda b, i, g: (b, i, 0)) if mrows == sq else (lambda b, i, g: (b, 0, 0))
    rc, rs1, rs2 = tables
    qg = jnp.tile(q_gain, N_HEADS).reshape(1, ATT)
    kg = jnp.tile(k_gain, N_HEADS).reshape(1, ATT)
    tmap = lambda b, i, g: (i, 0)
    cmap = lambda b, i, g: (0, 0)
    omap = lambda b, i, g: (b, i, g)
    oshape = jax.ShapeDtypeStruct((bq, sq, N_GROUPS * ATT), F32)
    return pl.pallas_call(
        _qkv_kernel,
        grid=(bq, sq // t_tile, N_GROUPS),
        in_specs=[
            pl.BlockSpec((1, t_tile, D_MODEL), lambda b, i, g: (b, i, 0)),
            pl.BlockSpec((1, mt, D_MODEL), mmap), pl.BlockSpec((1, mt, D_MODEL), mmap),
            pl.BlockSpec((1, D_MODEL), cmap),
            pl.BlockSpec((D_MODEL, 3 * ATT), lambda b, i, g: (0, g)),
            pl.BlockSpec((1, ATT), cmap), pl.BlockSpec((1, ATT), cmap),
            pl.BlockSpec((t_tile, ATT), tmap), pl.BlockSpec((t_tile, ATT), tmap), pl.BlockSpec((t_tile, ATT), tmap),
            pl.BlockSpec((ATT, ATT), cmap),
        ],
        out_specs=[pl.BlockSpec((1, t_tile, ATT), omap)] * 3,
        out_shape=[oshape] * 3,
        compiler_params=_cparams(("parallel", "parallel", "arbitrary")),
        name="qkv_proj",
    )(x3, shift, scale, gain.reshape(1, D_MODEL), w_qkv_bf, qg, kg, rc, rs1, rs2, ones_bd)


def _band_kernel(q_ref, kp_ref, kc_ref, vp_ref, vc_ref, o_ref, l_ref):
    i = pl.program_id(2)
    q = q_ref[0].astype(BF16)
    k = jnp.concatenate([kp_ref[0], kc_ref[0]], axis=0).astype(BF16)
    v = jnp.concatenate([vp_ref[0], vc_ref[0]], axis=0).astype(BF16)
    iq = lax.broadcasted_iota(jnp.int32, (N_BACK, 2 * N_BACK), 0)
    ik = lax.broadcasted_iota(jnp.int32, (N_BACK, 2 * N_BACK), 1)
    dist = iq + N_BACK - ik
    mask = (dist >= 0) & (dist <= N_BACK) & ((ik >= N_BACK) | (i > 0))
    for h in range(N_HEADS):
        sl = slice(h * HEAD_DIM, (h + 1) * HEAD_DIM)
        s = lax.dot_general(q[:, sl], k[:, sl], (((1,), (1,)), ((), ())),
                            preferred_element_type=F32) * (HEAD_DIM ** -0.5)
        s = jnp.where(mask, s, NEG_INF)
        m = jnp.max(s, axis=-1, keepdims=True)
        p = jnp.exp(s - m)
        den = jnp.sum(p, axis=-1, keepdims=True)
        o = jnp.dot(p.astype(BF16), v[:, sl], preferred_element_type=F32) / den
        o_ref[0, :, sl] = o
        l_ref[0, :, sl] = jnp.broadcast_to(m + jnp.log(den), (N_BACK, HEAD_DIM))


def _band_attention(qv, kv, vv, d, qcols, qcol0, kcols, kcol0):
    b, sub, _ = qv.shape
    nb = sub // N_BACK
    qcur = lambda bi, r, i: (bi, i, r * qcols + qcol0)
    cur = lambda bi, r, i: (bi, i, r * kcols + kcol0)
    prev = lambda bi, r, i: (bi, jnp.maximum(i - 1, 0), r * kcols + kcol0)
    omap = lambda bi, r, i: (bi, i, r)
    blk = (1, N_BACK, ATT)
    oshape = jax.ShapeDtypeStruct((b, sub, d * ATT), F32)
    return pl.pallas_call(
        _band_kernel,
        grid=(b, d, nb),
        in_specs=[pl.BlockSpec(blk, qcur), pl.BlockSpec(blk, prev), pl.BlockSpec(blk, cur),
                  pl.BlockSpec(blk, prev), pl.BlockSpec(blk, cur)],
        out_specs=[pl.BlockSpec(blk, omap), pl.BlockSpec(blk, omap)],
        out_shape=[oshape, oshape],
        compiler_params=_cparams(("parallel", "parallel", "arbitrary")),
        name="band_attention",
    )(qv, kv, kv, vv, vv)


def _cache_attn_kernel(q_ref, k_ref, v_ref, c0_ref, c1_ref, c2_ref, *out_refs):
    n_new = q_ref.shape[1]
    rows = n_new * N_HEADS
    lane_head = lax.broadcasted_iota(jnp.int32, (N_HEADS, ATT), 1) // HEAD_DIM
    head_mask = (lane_head == lax.broadcasted_iota(jnp.int32, (N_HEADS, ATT), 0)).astype(F32)
    row_t = lax.broadcasted_iota(jnp.int32, (rows, 1), 0) // N_HEADS
    caches = (c0_ref, c1_ref, c2_ref)
    for g in range(N_GROUPS):
        d, win = DILATIONS[g], WINDOWS[g]
        gs = slice(g * ATT, (g + 1) * ATT)
        q = q_ref[0][:, gs]
        kn = k_ref[0][:, gs]
        vn = v_ref[0][:, gs]
        kc = caches[g][0, 0, 0].astype(BF16)
        vc = caches[g][0, 0, 1].astype(BF16)
        qbd = jnp.concatenate([q[t:t + 1, :] * head_mask for t in range(n_new)], axis=0)
        s_c = lax.dot_general(qbd.astype(BF16), kc, (((1,), (1,)), ((), ())),
                              preferred_element_type=F32) * (HEAD_DIM ** -0.5)
        col = lax.broadcasted_iota(jnp.int32, (rows, win), 1)
        valid_c = (col >= row_t) & (((col - row_t) & (d - 1)) == 0)
        s_c = jnp.where(valid_c, s_c, NEG_INF)
        s_n = []
        for t2 in range(n_new):
            sn = jnp.sum(qbd * kn[t2:t2 + 1, :], axis=-1, keepdims=True) * (HEAD_DIM ** -0.5)
            valid_n = (row_t >= t2) & (((row_t - t2) & (d - 1)) == 0)
            s_n.append(jnp.where(valid_n, sn, NEG_INF))
        m = jnp.max(s_c, axis=-1, keepdims=True)
        for sn in s_n:
            m = jnp.maximum(m, sn)
        p_c = jnp.exp(s_c - m)
        den = jnp.sum(p_c, axis=-1, keepdims=True)
        o = jnp.dot(p_c.astype(BF16), vc, preferred_element_type=F32)
        for t2 in range(n_new):
            p_n = jnp.exp(s_n[t2] - m)
            den = den + p_n
            o = o + p_n * vn[t2:t2 + 1, :]
        o = o / den
        lse = m + jnp.log(den)
        for t in range(n_new):
            rs = slice(t * N_HEADS, (t + 1) * N_HEADS)
            out_refs[g][0, t:t + 1, :] = jnp.sum(o[rs] * head_mask, axis=0, keepdims=True)
            out_refs[N_GROUPS + g][0, t:t + 1, :] = jnp.sum(lse[rs] * head_mask, axis=0, keepdims=True)


def _cache_attention(q, k, v, caches, layer):
    bd, t, w = q.shape
    xmap = lambda b: (b, 0, 0)
    cmap = lambda b: (layer, b, 0, 0, 0)
    oshape = jax.ShapeDtypeStruct((bd, t, ATT), F32)
    return pl.pallas_call(
        _cache_attn_kernel,
        grid=(bd,),
        in_specs=[pl.BlockSpec((1, t, w), xmap)] * 3
        + [pl.BlockSpec((1, 1, 2, WINDOWS[g], ATT), cmap) for g in range(N_GROUPS)],
        out_specs=[pl.BlockSpec((1, t, ATT), xmap)] * (2 * N_GROUPS),
        out_shape=[oshape] * (2 * N_GROUPS),
        compiler_params=_cparams(("parallel",)),
        name="cache_attention",
    )(q, k, v, *caches)


def _undilate(ref, scr, d):
    if d == 1:
        return ref[0]
    rows = ref.shape[1]
    for r in range(d):
        for kc in range(ATT // 128):
            c0 = r * ATT + kc * 128
            scr[kc, pl.ds(r, rows, stride=d), :] = ref[0, :, c0:c0 + 128]
    return jnp.concatenate([scr[kc] for kc in range(ATT // 128)], axis=1)


def _merge_kernel(o0, o1, o2, l0, l1, l2, x_ref, gate_ref, wo_ref, xo_ref, scr, *, dils):
    la, lb, lc = [_undilate(r, scr, d) for r, d in zip((l0, l1, l2), dils)]
    m = jnp.maximum(jnp.maximum(la, lb), lc)
    ea, eb, ec = jnp.exp(la - m), jnp.exp(lb - m), jnp.exp(lc - m)
    den = ea + eb + ec
    o = (ea / den) * _undilate(o0, scr, dils[0])
    o = o + (eb / den) * _undilate(o1, scr, dils[1])
    o = o + (ec / den) * _undilate(o2, scr, dils[2])
    out = jnp.dot(o.astype(BF16), wo_ref[...], preferred_element_type=F32)
    xo_ref[0] = x_ref[0] + gate_ref[0] * out


def _merge(outs, lses, x3, gate, w_o_bf, *, t_tile, dils):
    bq, sq, _ = x3.shape
    mrows = gate.shape[1]
    mt = t_tile if mrows == sq else 1
    mmap = (lambda b, i: (b, i, 0)) if mrows == sq else (lambda b, i: (b, 0, 0))
    tmap = lambda b, i: (b, i, 0)
    ablk = [pl.BlockSpec((1, t_tile // d, d * ATT), tmap) for d in dils]
    return pl.pallas_call(
        functools.partial(_merge_kernel, dils=dils),
        grid=(bq, sq // t_tile),
        in_specs=ablk + ablk + [pl.BlockSpec((1, t_tile, D_MODEL), tmap), pl.BlockSpec((1, mt, D_MODEL), mmap),
                                pl.BlockSpec((ATT, D_MODEL), lambda b, i: (0, 0))],
        out_specs=pl.BlockSpec((1, t_tile, D_MODEL), tmap),
        out_shape=jax.ShapeDtypeStruct(x3.shape, F32),
        scratch_shapes=[pltpu.VMEM((ATT // 128, t_tile, 128), F32)],
        compiler_params=_cparams(("parallel", "parallel")),
        name="attn_merge",
    )(*outs, *lses, x3, gate, w_o_bf)


def _top16(s):
    vals = s
    rank = jnp.full(s.shape, float(PEER_N_KEYS), F32)
    tops = []
    for r in range(PEER_TOPK):
        m = jnp.max(vals, axis=0, keepdims=True)
        hit = vals == m
        rank = jnp.where(hit, float(r), rank)
        vals = jnp.where(hit, -jnp.inf, vals)
        tops.append(m)
    return tops, rank


def _peer_route_kernel(x_ref, shift_ref, scale_ref, gain_ref, wq_ref, sk_ref, ones_ref,
                       xt_ref, r1_ref, c1_ref, nf_ref, s_scr):
    hm = _modulate(x_ref[...], gain_ref[...], shift_ref[0], scale_ref[0])
    hb = hm.astype(BF16)
    xt_ref[...] = hm.T.astype(BF16)
    q = jnp.dot(hb, wq_ref[...], preferred_element_type=F32)
    qn = q * lax.rsqrt(_seg_mean(q * q, ones_ref[...]) + EPS)
    s_scr[...] = lax.dot_general(sk_ref[...], qn.astype(BF16), (((1,), (1,)), ((), ())),
                                 preferred_element_type=F32)

    def head(h, carry):
        base = pl.multiple_of(h * 2 * PEER_N_KEYS, 2 * PEER_N_KEYS)
        s1 = s_scr[pl.ds(base, PEER_N_KEYS), :]
        s2 = s_scr[pl.ds(base + PEER_N_KEYS, PEER_N_KEYS), :]
        top1, rank1 = _top16(s1)
        top2, rank2 = _top16(s2)
        row = lax.broadcasted_iota(jnp.int32, (PEER_TOPK, s1.shape[1]), 0)
        v1 = jnp.zeros((PEER_TOPK, s1.shape[1]), F32)
        for r in range(PEER_TOPK):
            v1 = jnp.where(row == r, top1[r], v1)
        cands = [v1[:8] + top2[b] for b in range(PEER_TOPK)] + [v1[8:] + top2[0]]
        best = top1[0] + top2[0]
        taken = [jnp.zeros_like(c) for c in cands]
        zsum = jnp.zeros_like(best)
        for _ in range(PEER_TOPK):
            m = cands[0]
            for c in cands[1:]:
                m = jnp.maximum(m, c)
            m = jnp.max(m, axis=0, keepdims=True)
            for j in range(len(cands)):
                hit = cands[j] == m
                taken[j] = jnp.where(hit, 1.0, taken[j])
                cands[j] = jnp.where(hit, -jnp.inf, cands[j])
            zsum = zsum + jnp.exp(m - best)
        cnt_lo = taken[0]
        for j in range(1, PEER_TOPK):
            cnt_lo = cnt_lo + taken[j]
        cnt = jnp.concatenate([cnt_lo, taken[PEER_TOPK]], axis=0)
        n2 = jnp.zeros_like(rank2)
        for a in range(PEER_TOPK):
            n2 = n2 + jnp.where(rank2 < cnt[a:a + 1], 1.0, 0.0)
        r1_ref[0, h] = rank1
        c1_ref[0, h] = jnp.exp(s1 - top1[0])
        f2 = jnp.exp(s2 - top2[0]) / zsum
        nf_ref[0, h, :, 0] = pltpu.bitcast(n2.astype(BF16), jnp.int32).reshape(PEER_N_KEYS // 16, 8, PEER_SEL_TOK)
        nf_ref[0, h, :, 1] = pltpu.bitcast(f2.astype(BF16), jnp.int32).reshape(PEER_N_KEYS // 16, 8, PEER_SEL_TOK)
        return carry

    lax.fori_loop(0, PEER_HEADS, head, 0)


def _peer_route(x2, shift, scale, gain, wq_bf, sk_bd, ones_bd):
    n = x2.shape[0]
    tt = PEER_SEL_TOK
    nt = n // tt
    per_mod = nt // shift.shape[0]
    mrows = shift.shape[1]
    if mrows == 1:
        mblk, mmap = (1, 1, D_MODEL), (lambda i: (i // per_mod, 0, 0))
    else:
        mblk, mmap = (1, tt, D_MODEL), (lambda i: (0, i, 0))
    cmap = lambda i: (0, 0)
    nrt = PEER_N_KEYS // 16
    hshape = jax.ShapeDtypeStruct((nt, PEER_HEADS, PEER_N_KEYS, tt), F32)
    hblk = pl.BlockSpec((1, PEER_HEADS, PEER_N_KEYS, tt), lambda i: (i, 0, 0, 0))
    nfshape = jax.ShapeDtypeStruct((nt, PEER_HEADS, nrt, 2, 8, tt), jnp.int32)
    nfblk = pl.BlockSpec((1, PEER_HEADS, nrt, 2, 8, tt), lambda i: (i, 0, 0, 0, 0, 0))
    return pl.pallas_call(
        _peer_route_kernel,
        grid=(nt,),
        in_specs=[
            pl.BlockSpec((tt, D_MODEL), lambda i: (i, 0)),
            pl.BlockSpec(mblk, mmap), pl.BlockSpec(mblk, mmap),
            pl.BlockSpec((1, D_MODEL), cmap),
            pl.BlockSpec((D_MODEL, D_MODEL), cmap),
            pl.BlockSpec((2 * PEER_HEADS * PEER_N_KEYS, D_MODEL), cmap),
            pl.BlockSpec((D_MODEL, D_MODEL), cmap),
        ],
        out_specs=[pl.BlockSpec((D_MODEL, tt), lambda i: (0, i)), hblk, hblk, nfblk],
        out_shape=[jax.ShapeDtypeStruct((D_MODEL, n), BF16), hshape, hshape, nfshape],
        scratch_shapes=[pltpu.VMEM((2 * PEER_HEADS * PEER_N_KEYS, tt), F32)],
        compiler_params=_cparams(("parallel",)),
        name="peer_route",
    )(x2, shift, scale, gain.reshape(1, D_MODEL), wq_bf, sk_bd, ones_bd)


def _gelu(x):
    return 0.5 * x * (1.0 + lax.erf(x * np.float32(math.sqrt(0.5))))


def _peer_expert_kernel(xt_ref, u_ref, vt_ref, r1_ref, c1_ref, nf_ref, x_ref, gate_ref, xo_ref,
                        act_scr, w_scr, acc_scr, *, t_tile):
    e = pl.program_id(1)

    @pl.when(e == 0)
    def _():
        acc_scr[...] = jnp.zeros_like(acc_scr)

    def stage_a(q):
        qs = slice(q * PEER_CHUNK, (q + 1) * PEER_CHUNK)
        act_scr[q] = jnp.dot(u_ref[0, qs, :], xt_ref[...], preferred_element_type=F32)

    def stage_b(q):
        for i1l in range(PEER_CHUNK // PEER_N_KEYS):
            i1 = q * (PEER_CHUNK // PEER_N_KEYS) + i1l
            for lc in range(t_tile // PEER_SEL_TOK):
                ls = slice(lc * PEER_SEL_TOK, (lc + 1) * PEER_SEL_TOK)
                r1 = [jnp.broadcast_to(r1_ref[lc, h, i1:i1 + 1, :], (16, PEER_SEL_TOK)).astype(BF16)
                      for h in range(PEER_HEADS)]
                c1 = [jnp.broadcast_to(c1_ref[lc, h, i1:i1 + 1, :], (16, PEER_SEL_TOK)).astype(BF16)
                      for h in range(PEER_HEADS)]
                for j in range(PEER_N_KEYS // 16):
                    g = jnp.zeros((16, PEER_SEL_TOK), BF16)
                    for h in range(PEER_HEADS):
                        n2 = pltpu.bitcast(nf_ref[lc, h, j, 0], BF16)
                        f2 = pltpu.bitcast(nf_ref[lc, h, j, 1], BF16)
                        g = g + jnp.where(n2 > r1[h], f2, 0.0) * c1[h]
                    rs = slice(i1l * PEER_N_KEYS + j * 16, i1l * PEER_N_KEYS + (j + 1) * 16)
                    w_scr[q, rs, ls] = g * _gelu(act_scr[q, rs, ls]).astype(BF16)

    def stage_c(q):
        qs = slice(q * PEER_CHUNK, (q + 1) * PEER_CHUNK)
        acc_scr[...] += jnp.dot(vt_ref[0, :, qs], w_scr[q], preferred_element_type=F32)

    nq = PEER_EXP_TILE // PEER_CHUNK
    stage_a(0)
    for q in range(nq):
        if q + 1 < nq:
            stage_a(q + 1)
        stage_b(q)
        if q > 0:
            stage_c(q - 1)
    stage_c(nq - 1)

    @pl.when(e == pl.num_programs(1) - 1)
    def _():
        xo_ref[...] = x_ref[...] + gate_ref[0] * acc_scr[...].T


def _peer_expert(xt, u_bf, vt_bf, layer, r1, c1, nf, x2, gate, *, t_tile):
    n = x2.shape[0]
    nt = n // t_tile
    nlc = t_tile // PEER_SEL_TOK
    ne = u_bf.shape[1] // PEER_EXP_TILE
    per_mod = max(nt // gate.shape[0], 1)
    mrows = gate.shape[1]
    if mrows == 1:
        mblk, mmap = (1, 1, D_MODEL), (lambda i, e: (i // per_mod, 0, 0))
    else:
        mblk, mmap = (1, t_tile, D_MODEL), (lambda i, e: (0, i, 0))
    i1blk = pl.BlockSpec((nlc, PEER_HEADS, PEER_I1_TILE, PEER_SEL_TOK), lambda i, e: (i, 0, e, 0))
    nfblk = pl.BlockSpec((nlc, PEER_HEADS, PEER_N_KEYS // 16, 2, 8, PEER_SEL_TOK), lambda i, e: (i, 0, 0, 0, 0, 0))
    kern = functools.partial(_peer_expert_kernel, t_tile=t_tile)
    return pl.pallas_call(
        kern,
        grid=(nt, ne),
        in_specs=[
            pl.BlockSpec((D_MODEL, t_tile), lambda i, e: (0, i)),
            pl.BlockSpec((1, PEER_EXP_TILE, D_MODEL), lambda i, e: (layer, e, 0)),
            pl.BlockSpec((1, D_MODEL, PEER_EXP_TILE), lambda i, e: (layer, 0, e)),
            i1blk, i1blk, nfblk,
            pl.BlockSpec((t_tile, D_MODEL), lambda i, e: (i, 0)),
            pl.BlockSpec(mblk, mmap),
        ],
        out_specs=pl.BlockSpec((t_tile, D_MODEL), lambda i, e: (i, 0)),
        out_shape=jax.ShapeDtypeStruct(x2.shape, F32),
        scratch_shapes=[pltpu.VMEM((PEER_EXP_TILE // PEER_CHUNK, PEER_CHUNK, t_tile), F32),
                        pltpu.VMEM((PEER_EXP_TILE // PEER_CHUNK, PEER_CHUNK, t_tile), BF16),
                        pltpu.VMEM((D_MODEL, t_tile), F32)],
        compiler_params=_cparams(("parallel", "arbitrary")),
        name="peer_expert",
    )(xt, u_bf, vt_bf, r1, c1, nf, x2, gate)


def _peer(x2, shift, scale, gate, gain, pw, tables, layer, *, t_tile):
    wq_bf, sk_bd, ones_bd = pw
    u_bf, vt_bf = tables
    xt, r1, c1, nf = _peer_route(x2, shift, scale, gain, wq_bf, sk_bd, ones_bd)
    return _peer_expert(xt, u_bf, vt_bf, layer, r1, c1, nf, x2, gate, t_tile=t_tile)


def _peer_weights(w_q, sub_keys, ones_bd):
    hc = 2 * PEER_HEADS
    sk = sub_keys.reshape(hc, PEER_N_KEYS, PEER_HALF)
    sk_bd = jnp.einsum('akd,ab->akbd', sk, jnp.eye(hc, dtype=F32)).reshape(hc * PEER_N_KEYS, hc * PEER_HALF)
    return (w_q.astype(BF16), sk_bd.astype(BF16), ones_bd)


def kernel(x_prompt, x_sample, c_prompt, c_sample, state_ssm, cache_kv_w128, cache_kv_w512, cache_kv_w2048,
           norm_mix, norm_ffn, ada_w, ada_b, ssm_lam_re, ssm_lam_im, ssm_log_dt, ssm_b_re, ssm_b_im, ssm_c_re,
           ssm_c_im, ssm_d, ssm_w_glu, attn_w_qkv, attn_q_norm, attn_k_norm, attn_w_o, peer_w_q,
           peer_sub_keys, peer_u, peer_v):
    bp, sp, _ = x_prompt.shape
    bs, ts, _ = x_sample.shape
    ns = bs * ts
    ones_att = _ones_blockdiag(ATT)
    ones_d = _ones_blockdiag(D_MODEL)

    mods = _adaln(jnp.concatenate([c_prompt, c_sample], axis=0), ada_w, ada_b)
    caches = (cache_kv_w128, cache_kv_w512, cache_kv_w2048)
    tab_p = _rope_tables(jnp.arange(sp, dtype=jnp.int32))
    pos_s = PAST_LEN + jnp.arange(ts, dtype=jnp.int32)
    tab_s = tuple(jnp.tile(t, (bs, 1)) for t in _rope_tables(pos_s))

    peer_tabs = (peer_u.astype(BF16), peer_v.astype(BF16).transpose(0, 2, 1))
    xp, xs = x_prompt, x_sample
    ssm_p, ssm_s = [], []
    kv_p = [[] for _ in range(N_GROUPS)]
    kv_s = [[] for _ in range(N_GROUPS)]
    for i in range(DEPTH):
        j = i // 2
        mp = [m.reshape(bp, 1, D_MODEL) for m in jnp.split(mods[i, :bp], 6, axis=-1)]
        ms_b = jnp.split(mods[i, bp:], 6, axis=-1)
        ms_rows = [jnp.repeat(m, ts, axis=0).reshape(1, ns, D_MODEL) for m in ms_b]
        if i % 2 == 0:
            sw = _s5_weights(ssm_lam_re[j], ssm_lam_im[j], ssm_log_dt[j], ssm_b_re[j], ssm_b_im[j],
                             ssm_c_re[j], ssm_c_im[j])
            wglu = ssm_w_glu[j].astype(BF16)
            mp_t = [m.reshape(1, bp, D_MODEL) for m in mp[:3]]
            xp_t, st_p = _s5_layer(xp.transpose(1, 0, 2), mp_t[0], mp_t[1], mp_t[2], norm_mix[i],
                                   jnp.zeros((bp, SSM_STATE_W), F32), sw, ssm_d[j], wglu, t_chunk=32)
            xp = xp_t.transpose(1, 0, 2)
            ms_t = [m.reshape(1, bs, D_MODEL) for m in ms_b[:3]]
            xs_t, st_s = _s5_layer(xs.transpose(1, 0, 2), ms_t[0], ms_t[1], ms_t[2], norm_mix[i],
                                   _state_to_rows(state_ssm[j]), sw, ssm_d[j], wglu, t_chunk=ts)
            xs = xs_t.transpose(1, 0, 2)
            ssm_p.append(_rows_to_state(st_p))
            ssm_s.append(_rows_to_state(st_s))
        else:
            wqkv = attn_w_qkv[j].astype(BF16)
            wo = attn_w_o[j].astype(BF16)
            k, v, q0, q1, k1, v1, q2, k2, v2 = _qkv_dilated(xp, mp[0], mp[1], norm_mix[i], wqkv, attn_q_norm[j],
                                                            attn_k_norm[j], tab_p, ones_att, t_tile=512)
            res = [_band_attention(q0, k, v, 1, 1, 0, N_GROUPS, 0),
                   _band_attention(q1, k1, v1, DILATIONS[1], 1, 0, 1, 0),
                   _band_attention(q2, k2, v2, DILATIONS[2], 1, 0, 1, 0)]
            outs, lses = zip(*res)
            xp = _merge(outs, lses, xp, mp[2], wo, t_tile=512, dils=DILATIONS)
            for g in range(N_GROUPS):
                keep = min(WINDOWS[g], sp)
                gs = slice(g * ATT, (g + 1) * ATT)
                kv_p[g].append(jnp.stack([k[:, sp - keep:, gs], v[:, sp - keep:, gs]], axis=1)
                               .reshape(bp, 2, keep, N_HEADS, HEAD_DIM))
            xs_rows = xs.reshape(1, ns, D_MODEL)
            qs, ks, vs = _qkv(xs_rows, ms_rows[0], ms_rows[1], norm_mix[i], wqkv, attn_q_norm[j],
                              attn_k_norm[j], tab_s, ones_att, t_tile=ns)
            qs, ks, vs = (t.reshape(bs, ts, N_GROUPS * ATT) for t in (qs, ks, vs))
            res = _cache_attention(qs, ks, vs, [c.reshape(c.shape[:4] + (ATT,)) for c in caches], j)
            outs_s = [r.reshape(1, ns, ATT) for r in res[:N_GROUPS]]
            lses_s = [r.reshape(1, ns, ATT) for r in res[N_GROUPS:]]
            xs = _merge(outs_s, lses_s, xs_rows, ms_rows[2], wo, t_tile=ns,
                        dils=(1, 1, 1)).reshape(bs, ts, D_MODEL)
            for g in range(N_GROUPS):
                gs = slice(g * ATT, (g + 1) * ATT)
                kv_s[g].append(jnp.stack([ks[:, :, gs], vs[:, :, gs]], axis=1)
                               .reshape(bs, 2, ts, N_HEADS, HEAD_DIM))
        pw = _peer_weights(peer_w_q[i], peer_sub_keys[i], ones_d)
        xp = _peer(xp.reshape(bp * sp, D_MODEL), mp[3], mp[4], mp[5], norm_ffn[i], pw, peer_tabs, i,
                   t_tile=512).reshape(bp, sp, D_MODEL)
        xs = _peer(xs.reshape(ns, D_MODEL), ms_rows[3], ms_rows[4], ms_rows[5], norm_ffn[i], pw, peer_tabs, i,
                   t_tile=ns).reshape(bs, ts, D_MODEL)
    return (xp, xs,
            jnp.stack(ssm_p), jnp.stack(kv_p[0]), jnp.stack(kv_p[1]), jnp.stack(kv_p[2]),
            jnp.stack(ssm_s), jnp.stack(kv_s[0]), jnp.stack(kv_s[1]), jnp.stack(kv_s[2]))
```

```python
import functools
import math

import jax
import jax.numpy as jnp
import numpy as np
from jax import lax
from jax.experimental import pallas as pl
from jax.experimental.pallas import tpu as pltpu

F32 = jnp.float32
BF16 = jnp.bfloat16

D_MODEL = 1024
DEPTH = 4
PAST_LEN = 8192
EPS = 1e-6
NEG_INF = -1e30

SSM_P = 16
SSM_GROUPS = D_MODEL // SSM_P
SSM_N = 64
SSM_GB = 4
SSM_GB_CH = D_MODEL // SSM_GB
SSM_GB_ST = SSM_GROUPS * SSM_N // SSM_GB
SSM_STATE_W = 2 * SSM_GROUPS * SSM_N

N_HEADS = 8
HEAD_DIM = 64
ATT = N_HEADS * HEAD_DIM
WINDOWS = (128, 512, 2048)
DILATIONS = (1, 4, 16)
N_GROUPS = 3
N_BACK = 128
ROPE_DIM = HEAD_DIM // 4
ROPE_THETA = 500000.0

PEER_HEADS = 8
PEER_N_KEYS = 128
PEER_HALF = 64
PEER_TOPK = 16
PEER_SEL_TOK = 128
PEER_EXP_TILE = 1024
PEER_I1_TILE = PEER_EXP_TILE // PEER_N_KEYS
PEER_CHUNK = 256

VMEM_LIMIT = 56 * 1024 * 1024


def _cparams(sem):
    return pltpu.CompilerParams(dimension_semantics=sem, vmem_limit_bytes=VMEM_LIMIT)


def _modulate(x, gain, shift, scale):
    ms = jnp.mean(x * x, axis=-1, keepdims=True)
    return x * lax.rsqrt(ms + EPS) * gain * (1.0 + scale) + shift


def _seg_mean(x2, ones_bd):
    hi = x2.astype(BF16)
    lo = (x2 - hi.astype(F32)).astype(BF16)
    s = jnp.dot(hi, ones_bd, preferred_element_type=F32) + jnp.dot(lo, ones_bd, preferred_element_type=F32)
    return s * (1.0 / PEER_HALF)


def _ones_blockdiag(n):
    seg = np.arange(n) // 64
    return jnp.asarray((seg[:, None] == seg[None, :]).astype(np.float32), dtype=BF16)


def _adaln_kernel(c_ref, w_ref, b_ref, o_ref):
    s = jax.nn.silu(c_ref[...])
    o_ref[0] = jnp.dot(s.astype(BF16), w_ref[0].astype(BF16), preferred_element_type=F32) + b_ref[0]


def _adaln(c_all, ada_w, ada_b):
    nb = c_all.shape[0]
    nt = 6 * D_MODEL // 1024
    return pl.pallas_call(
        _adaln_kernel,
        grid=(DEPTH, nt),
        in_specs=[
            pl.BlockSpec((nb, D_MODEL), lambda l, j: (0, 0)),
            pl.BlockSpec((1, D_MODEL, 1024), lambda l, j: (l, 0, j)),
            pl.BlockSpec((1, 1, 1024), lambda l, j: (l, 0, j)),
        ],
        out_specs=pl.BlockSpec((1, nb, 1024), lambda l, j: (l, 0, j)),
        out_shape=jax.ShapeDtypeStruct((DEPTH, nb, 6 * D_MODEL), F32),
        compiler_params=_cparams(("parallel", "parallel")),
        name="adaln",
    )(c_all, ada_w, ada_b.reshape(DEPTH, 1, 6 * D_MODEL))


def _s5_kernel(x_ref, shift_ref, scale_ref, gate_ref, gain_ref, h0_ref, bm_ref, cm_ref, are_ref, aim_ref,
               d_ref, wglu_ref, xo_ref, st_ref, bu_scr, state_scr, *, n_batch, n_time):
    i = pl.program_id(0)

    @pl.when(i == 0)
    def _():
        state_scr[...] = h0_ref[...]

    x3 = x_ref[...]
    a0, a1, _ = x3.shape
    rows = a0 * a1
    h3 = _modulate(x3, gain_ref[...], shift_ref[...], scale_ref[...])
    u = h3.reshape(rows, D_MODEL)
    ub = u.astype(BF16)
    for gb in range(SSM_GB):
        bu_scr[:, gb * 2 * SSM_GB_ST:(gb + 1) * 2 * SSM_GB_ST] = jnp.dot(
            ub[:, gb * SSM_GB_CH:(gb + 1) * SSM_GB_CH], bm_ref[gb], preferred_element_type=F32)

    for bb in range(n_batch // 8):
        for gb in range(SSM_GB):
            c_re = gb * 2 * SSM_GB_ST
            c_im = c_re + SSM_GB_ST
            ar = jnp.broadcast_to(are_ref[gb], (8, SSM_GB_ST))
            ai = jnp.broadcast_to(aim_ref[gb], (8, SSM_GB_ST))
            xr = state_scr[bb * 8:(bb + 1) * 8, c_re:c_re + SSM_GB_ST]
            xi = state_scr[bb * 8:(bb + 1) * 8, c_im:c_im + SSM_GB_ST]
            for t in range(n_time):
                rsel = pl.ds(t * n_batch + bb * 8, 8)
                br = bu_scr[rsel, c_re:c_re + SSM_GB_ST]
                bi = bu_scr[rsel, c_im:c_im + SSM_GB_ST]
                nr = ar * xr - ai * xi + br
                ni = ar * xi + ai * xr + bi
                bu_scr[rsel, c_re:c_re + SSM_GB_ST] = nr
                bu_scr[rsel, c_im:c_im + SSM_GB_ST] = ni
                xr, xi = nr, ni
            state_scr[bb * 8:(bb + 1) * 8, c_re:c_re + SSM_GB_ST] = xr
            state_scr[bb * 8:(bb + 1) * 8, c_im:c_im + SSM_GB_ST] = xi

    ys = []
    for gb in range(SSM_GB):
        st = bu_scr[:, gb * 2 * SSM_GB_ST:(gb + 1) * 2 * SSM_GB_ST].astype(BF16)
        ys.append(jnp.dot(st, cm_ref[gb], preferred_element_type=F32))
    y = jnp.concatenate(ys, axis=-1) + d_ref[...] * u
    z = jnp.dot(y.astype(BF16), wglu_ref[...], preferred_element_type=F32)
    out = z[:, :D_MODEL] * jax.nn.sigmoid(z[:, D_MODEL:])
    xo_ref[...] = x3 + gate_ref[...] * out.reshape(a0, a1, D_MODEL)

    @pl.when(i == pl.num_programs(0) - 1)
    def _():
        st_ref[...] = state_scr[...]


def _s5_weights(lam_re, lam_im, log_dt, b_re, b_im, c_re, c_im):
    dt = jnp.exp(log_dt)[:, None]
    mag = jnp.exp(lam_re * dt)
    a_re = mag * jnp.cos(lam_im * dt)
    a_im = mag * jnp.sin(lam_im * dt)
    den = lam_re * lam_re + lam_im * lam_im
    f_re = ((a_re - 1.0) * lam_re + a_im * lam_im) / den
    f_im = (a_im * lam_re - (a_re - 1.0) * lam_im) / den
    bb_re = f_re[..., None] * b_re - f_im[..., None] * b_im
    bb_im = f_re[..., None] * b_im + f_im[..., None] * b_re
    eye = jnp.eye(16, dtype=F32)

    def in_blk(b):
        b4 = b.reshape(SSM_GB, 16, SSM_N, SSM_P)
        return jnp.einsum('bgnp,gh->bgphn', b4, eye).reshape(SSM_GB, SSM_GB_CH, SSM_GB_ST)

    def out_blk(c):
        c4 = c.reshape(SSM_GB, 16, SSM_P, SSM_N)
        return jnp.einsum('bgpn,gh->bgnhp', c4, eye).reshape(SSM_GB, SSM_GB_ST, SSM_GB_CH)

    bm = jnp.concatenate([in_blk(bb_re), in_blk(bb_im)], axis=2).astype(BF16)
    cm = jnp.concatenate([out_blk(c_re), -out_blk(c_im)], axis=1).astype(BF16)
    return bm, cm, a_re.reshape(SSM_GB, 1, SSM_GB_ST), a_im.reshape(SSM_GB, 1, SSM_GB_ST)


def _state_to_rows(st):
    b = st.shape[0]
    return st.reshape(b, 2, SSM_GB, SSM_GB_ST).transpose(0, 2, 1, 3).reshape(b, SSM_STATE_W)


def _rows_to_state(rows):
    b = rows.shape[0]
    return rows.reshape(b, SSM_GB, 2, SSM_GB_ST).transpose(0, 2, 1, 3).reshape(b, 2, SSM_GROUPS, SSM_N)


def _s5_layer(x3, shift, scale, gate, gain, h0_rows, weights, d_skip, w_glu, *, t_chunk):
    bm, cm, a_re, a_im = weights
    seq, n_batch = x3.shape[0], x3.shape[1]
    n_time = t_chunk
    blk = (n_time, n_batch, D_MODEL)
    xmap = lambda i: (i, 0, 0)
    mblk = (1, n_batch, D_MODEL)
    grid = seq // n_time
    rows = n_batch * n_time
    const3 = lambda i: (0, 0, 0)
    const2 = lambda i: (0, 0)
    kern = functools.partial(_s5_kernel, n_batch=n_batch, n_time=n_time)
    return pl.pallas_call(
        kern,
        grid=(grid,),
        in_specs=[
            pl.BlockSpec(blk, xmap),
            pl.BlockSpec(mblk, const3), pl.BlockSpec(mblk, const3), pl.BlockSpec(mblk, const3),
            pl.BlockSpec((1, 1, D_MODEL), const3),
            pl.BlockSpec((n_batch, SSM_STATE_W), const2),
            pl.BlockSpec(bm.shape, const3), pl.BlockSpec(cm.shape, const3),
            pl.BlockSpec(a_re.shape, const3), pl.BlockSpec(a_im.shape, const3),
            pl.BlockSpec((1, D_MODEL), const2),
            pl.BlockSpec((D_MODEL, 2 * D_MODEL), const2),
        ],
        out_specs=[pl.BlockSpec(blk, xmap), pl.BlockSpec((n_batch, SSM_STATE_W), const2)],
        out_shape=[jax.ShapeDtypeStruct(x3.shape, F32), jax.ShapeDtypeStruct((n_batch, SSM_STATE_W), F32)],
        scratch_shapes=[pltpu.VMEM((rows, SSM_STATE_W), F32), pltpu.VMEM((n_batch, SSM_STATE_W), F32)],
        compiler_params=_cparams(("arbitrary",)),
        name="s5_layer",
    )(x3, shift, scale, gate, gain.reshape(1, 1, D_MODEL), h0_rows, bm, cm, a_re, a_im,
      d_skip.reshape(1, D_MODEL), w_glu)


def _qkv_kernel(x_ref, shift_ref, scale_ref, gain_ref, w_ref, qg_ref, kg_ref, rc_ref, rs1_ref, rs2_ref,
                ones_ref, q_ref, k_ref, v_ref):
    h = _modulate(x_ref[0], gain_ref[...], shift_ref[0], scale_ref[0])
    qkv = jnp.dot(h.astype(BF16), w_ref[...], preferred_element_type=F32)
    rc, rs1, rs2 = rc_ref[...], rs1_ref[...], rs2_ref[...]
    ones_bd = ones_ref[...]

    def norm_rope(t, g):
        tn = t * lax.rsqrt(_seg_mean(t * t, ones_bd) + EPS) * g
        return tn * rc + pltpu.roll(tn, ATT - ROPE_DIM // 2, 1) * rs1 + pltpu.roll(tn, ROPE_DIM // 2, 1) * rs2

    q_ref[0] = norm_rope(qkv[:, :ATT], qg_ref[...])
    k_ref[0] = norm_rope(qkv[:, ATT:2 * ATT], kg_ref[...])
    v_ref[0] = qkv[:, 2 * ATT:]


def _dilate(x, scr, out_ref, d):
    t = x.shape[0]
    for kc in range(ATT // 128):
        scr[kc] = x[:, kc * 128:(kc + 1) * 128]
    for r in range(d):
        for kc in range(ATT // 128):
            c0 = r * ATT + kc * 128
            out_ref[0, :, c0:c0 + 128] = scr[kc, pl.ds(r, t // d, stride=d), :]


def _qkv_dil_kernel(x_ref, shift_ref, scale_ref, gain_ref, w_ref, qg_ref, kg_ref, rc_ref, rs1_ref, rs2_ref,
                    ones_ref, kn_ref, vn_ref, q0_ref, q1_ref, k1_ref, v1_ref, q2_ref, k2_ref, v2_ref, scr):
    g = pl.program_id(2)
    h = _modulate(x_ref[0], gain_ref[...], shift_ref[0], scale_ref[0])
    qkv = jnp.dot(h.astype(BF16), w_ref[...], preferred_element_type=F32)
    rc, rs1, rs2 = rc_ref[...], rs1_ref[...], rs2_ref[...]
    ones_bd = ones_ref[...]

    def norm_rope(t, gn):
        tn = t * lax.rsqrt(_seg_mean(t * t, ones_bd) + EPS) * gn
        return tn * rc + pltpu.roll(tn, ATT - ROPE_DIM // 2, 1) * rs1 + pltpu.roll(tn, ROPE_DIM // 2, 1) * rs2

    q = norm_rope(qkv[:, :ATT], qg_ref[...])
    k = norm_rope(qkv[:, ATT:2 * ATT], kg_ref[...])
    v = qkv[:, 2 * ATT:]
    kn_ref[0] = k
    vn_ref[0] = v

    @pl.when(g == 0)
    def _():
        q0_ref[0] = q

    for gi, refs in ((1, (q1_ref, k1_ref, v1_ref)), (2, (q2_ref, k2_ref, v2_ref))):
        @pl.when(g == gi)
        def _():
            for val, ref in zip((q, k, v), refs):
                _dilate(val, scr, ref, DILATIONS[gi])


def _qkv_dilated(x3, shift, scale, gain, w_qkv_bf, q_gain, k_gain, tables, ones_bd, *, t_tile):
    bq, sq, _ = x3.shape
    rc, rs1, rs2 = tables
    qg = jnp.tile(q_gain, N_HEADS).reshape(1, ATT)
    kg = jnp.tile(k_gain, N_HEADS).reshape(1, ATT)
    mmap = lambda b, i, g: (b, 0, 0)
    tmap = lambda b, i, g: (i, 0)
    cmap = lambda b, i, g: (0, 0)
    gmap = lambda b, i, g: (b, i, g)
    bmap = lambda b, i, g: (b, i, 0)
    nat = jax.ShapeDtypeStruct((bq, sq, N_GROUPS * ATT), F32)
    out_shape = [nat, nat, jax.ShapeDtypeStruct((bq, sq, ATT), F32)]
    out_specs = [pl.BlockSpec((1, t_tile, ATT), gmap)] * 2 + [pl.BlockSpec((1, t_tile, ATT), bmap)]
    for gi in (1, 2):
        d = DILATIONS[gi]
        out_shape += [jax.ShapeDtypeStruct((bq, sq // d, d * ATT), F32)] * 3
        out_specs += [pl.BlockSpec((1, t_tile // d, d * ATT), bmap)] * 3
    return pl.pallas_call(
        _qkv_dil_kernel,
        grid=(bq, sq // t_tile, N_GROUPS),
        in_specs=[
            pl.BlockSpec((1, t_tile, D_MODEL), bmap),
            pl.BlockSpec((1, 1, D_MODEL), mmap), pl.BlockSpec((1, 1, D_MODEL), mmap),
            pl.BlockSpec((1, D_MODEL), cmap),
            pl.BlockSpec((D_MODEL, 3 * ATT), lambda b, i, g: (0, g)),
            pl.BlockSpec((1, ATT), cmap), pl.BlockSpec((1, ATT), cmap),
            pl.BlockSpec((t_tile, ATT), tmap), pl.BlockSpec((t_tile, ATT), tmap), pl.BlockSpec((t_tile, ATT), tmap),
            pl.BlockSpec((ATT, ATT), cmap),
        ],
        out_specs=out_specs,
        out_shape=out_shape,
        scratch_shapes=[pltpu.VMEM((ATT // 128, t_tile, 128), F32)],
        compiler_params=_cparams(("parallel", "parallel", "arbitrary")),
        name="qkv_proj_dilated",
    )(x3, shift, scale, gain.reshape(1, D_MODEL), w_qkv_bf, qg, kg, rc, rs1, rs2, ones_bd)


def _rope_tables(pos):
    half = ROPE_DIM // 2
    inv = ROPE_THETA ** (-jnp.arange(half, dtype=F32) / half)
    ang = pos.astype(F32)[:, None] * inv[None, :]
    cos, sin = jnp.cos(ang), jnp.sin(ang)
    lane = np.arange(ATT) % HEAD_DIM
    fidx = lane % half
    first = jnp.asarray(lane < half)
    second = jnp.asarray((lane >= half) & (lane < ROPE_DIM))
    cl, sl = cos[:, fidx], sin[:, fidx]
    rc = jnp.where(first | second, cl, 1.0)
    rs1 = jnp.where(first, -sl, 0.0)
    rs2 = jnp.where(second, sl, 0.0)
    return rc, rs1, rs2


def _qkv(x3, shift, scale, gain, w_qkv_bf, q_gain, k_gain, tables, ones_bd, *, t_tile):
    bq, sq, _ = x3.shape
    mrows = shift.shape[1]
    mt = t_tile if mrows == sq else 1
    mmap = (lambda b, i, g: (b, i, 0)) if mrows == sq else (lambda b, i, g: (b, 0, 0))
    rc, rs1, rs2 = tables
    qg = jnp.tile(q_gain, N_HEADS).reshape(1, ATT)
    kg = jnp.tile(k_gain, N_HEADS).reshape(1, ATT)
    tmap = lambda b, i, g: (i, 0)
    cmap = lambda b, i, g: (0, 0)
    omap = lambda b, i, g: (b, i, g)
    oshape = jax.ShapeDtypeStruct((bq, sq, N_GROUPS * ATT), F32)
    return pl.pallas_call(
        _qkv_kernel,
        grid=(bq, sq // t_tile, N_GROUPS),
        in_specs=[
            pl.BlockSpec((1, t_tile, D_MODEL), lambda b, i, g: (b, i, 0)),
            pl.BlockSpec((1, mt, D_MODEL), mmap), pl.BlockSpec((1, mt, D_MODEL), mmap),
            pl.BlockSpec((1, D_MODEL), cmap),
            pl.BlockSpec((D_MODEL, 3 * ATT), lambda b, i, g: (0, g)),
            pl.BlockSpec((1, ATT), cmap), pl.BlockSpec((1, ATT), cmap),
            pl.BlockSpec((t_tile, ATT), tmap), pl.BlockSpec((t_tile, ATT), tmap), pl.BlockSpec((t_tile, ATT), tmap),
            pl.BlockSpec((ATT, ATT), cmap),
        ],
        out_specs=[pl.BlockSpec((1, t_tile, ATT), omap)] * 3,
        out_shape=[oshape] * 3,
        compiler_params=_cparams(("parallel", "parallel", "arbitrary")),
        name="qkv_proj",
    )(x3, shift, scale, gain.reshape(1, D_MODEL), w_qkv_bf, qg, kg, rc, rs1, rs2, ones_bd)


def _band_kernel(q_ref, kp_ref, kc_ref, vp_ref, vc_ref, o_ref, l_ref):
    i = pl.program_id(2)
    q = q_ref[0].astype(BF16)
    k = jnp.concatenate([kp_ref[0], kc_ref[0]], axis=0).astype(BF16)
    v = jnp.concatenate([vp_ref[0], vc_ref[0]], axis=0).astype(BF16)
    iq = lax.broadcasted_iota(jnp.int32, (N_BACK, 2 * N_BACK), 0)
    ik = lax.broadcasted_iota(jnp.int32, (N_BACK, 2 * N_BACK), 1)
    dist = iq + N_BACK - ik
    mask = (dist >= 0) & (dist <= N_BACK) & ((ik >= N_BACK) | (i > 0))
    for h in range(N_HEADS):
        sl = slice(h * HEAD_DIM, (h + 1) * HEAD_DIM)
        s = lax.dot_general(q[:, sl], k[:, sl], (((1,), (1,)), ((), ())),
                            preferred_element_type=F32) * (HEAD_DIM ** -0.5)
        s = jnp.where(mask, s, NEG_INF)
        m = jnp.max(s, axis=-1, keepdims=True)
        p = jnp.exp(s - m)
        den = jnp.sum(p, axis=-1, keepdims=True)
        o = jnp.dot(p.astype(BF16), v[:, sl], preferred_element_type=F32) / den
        o_ref[0, :, sl] = o
        l_ref[0, :, sl] = jnp.broadcast_to(m + jnp.log(den), (N_BACK, HEAD_DIM))


def _band_attention(qv, kv, vv, d, qcols, qcol0, kcols, kcol0):
    b, sub, _ = qv.shape
    nb = sub // N_BACK
    qcur = lambda bi, r, i: (bi, i, r * qcols + qcol0)
    cur = lambda bi, r, i: (bi, i, r * kcols + kcol0)
    prev = lambda bi, r, i: (bi, jnp.maximum(i - 1, 0), r * kcols + kcol0)
    omap = lambda bi, r, i: (bi, i, r)
    blk = (1, N_BACK, ATT)
    oshape = jax.ShapeDtypeStruct((b, sub, d * ATT), F32)
    return pl.pallas_call(
        _band_kernel,
        grid=(b, d, nb),
        in_specs=[pl.BlockSpec(blk, qcur), pl.BlockSpec(blk, prev), pl.BlockSpec(blk, cur),
                  pl.BlockSpec(blk, prev), pl.BlockSpec(blk, cur)],
        out_specs=[pl.BlockSpec(blk, omap), pl.BlockSpec(blk, omap)],
        out_shape=[oshape, oshape],
        compiler_params=_cparams(("parallel", "parallel", "arbitrary")),
        name="band_attention",
    )(qv, kv, kv, vv, vv)


def _cache_attn_kernel(q_ref, k_ref, v_ref, c0_ref, c1_ref, c2_ref, *out_refs):
    n_new = q_ref.shape[1]
    rows = n_new * N_HEADS
    lane_head = lax.broadcasted_iota(jnp.int32, (N_HEADS, ATT), 1) // HEAD_DIM
    head_mask = (lane_head == lax.broadcasted_iota(jnp.int32, (N_HEADS, ATT), 0)).astype(F32)
    row_t = lax.broadcasted_iota(jnp.int32, (rows, 1), 0) // N_HEADS
    caches = (c0_ref, c1_ref, c2_ref)
    for g in range(N_GROUPS):
        d, win = DILATIONS[g], WINDOWS[g]
        gs = slice(g * ATT, (g + 1) * ATT)
        q = q_ref[0][:, gs]
        kn = k_ref[0][:, gs]
        vn = v_ref[0][:, gs]
        kc = caches[g][0, 0, 0].astype(BF16)
        vc = caches[g][0, 0, 1].astype(BF16)
        qbd = jnp.concatenate([q[t:t + 1, :] * head_mask for t in range(n_new)], axis=0)
        s_c = lax.dot_general(qbd.astype(BF16), kc, (((1,), (1,)), ((), ())),
                              preferred_element_type=F32) * (HEAD_DIM ** -0.5)
        col = lax.broadcasted_iota(jnp.int32, (rows, win), 1)
        valid_c = (col >= row_t) & (((col - row_t) & (d - 1)) == 0)
        s_c = jnp.where(valid_c, s_c, NEG_INF)
        s_n = []
        for t2 in range(n_new):
            sn = jnp.sum(qbd * kn[t2:t2 + 1, :], axis=-1, keepdims=True) * (HEAD_DIM ** -0.5)
            valid_n = (row_t >= t2) & (((row_t - t2) & (d - 1)) == 0)
            s_n.append(jnp.where(valid_n, sn, NEG_INF))
        m = jnp.max(s_c, axis=-1, keepdims=True)
        for sn in s_n:
            m = jnp.maximum(m, sn)
        p_c = jnp.exp(s_c - m)
        den = jnp.sum(p_c, axis=-1, keepdims=True)
        o = jnp.dot(p_c.astype(BF16), vc, preferred_element_type=F32)
        for t2 in range(n_new):
            p_n = jnp.exp(s_n[t2] - m)
            den = den + p_n
            o = o + p_n * vn[t2:t2 + 1, :]
        o = o / den
        lse = m + jnp.log(den)
        for t in range(n_new):
            rs = slice(t * N_HEADS, (t + 1) * N_HEADS)
            out_refs[g][0, t:t + 1, :] = jnp.sum(o[rs] * head_mask, axis=0, keepdims=True)
            out_refs[N_GROUPS + g][0, t:t + 1, :] = jnp.sum(lse[rs] * head_mask, axis=0, keepdims=True)


def _cache_attention(q, k, v, caches, layer):
    bd, t, w = q.shape
    xmap = lambda b: (b, 0, 0)
    cmap = lambda b: (layer, b, 0, 0, 0)
    oshape = jax.ShapeDtypeStruct((bd, t, ATT), F32)
    return pl.pallas_call(
        _cache_attn_kernel,
        grid=(bd,),
        in_specs=[pl.BlockSpec((1, t, w), xmap)] * 3
        + [pl.BlockSpec((1, 1, 2, WINDOWS[g], ATT), cmap) for g in range(N_GROUPS)],
        out_specs=[pl.BlockSpec((1, t, ATT), xmap)] * (2 * N_GROUPS),
        out_shape=[oshape] * (2 * N_GROUPS),
        compiler_params=_cparams(("parallel",)),
        name="cache_attention",
    )(q, k, v, *caches)


def _undilate(ref, scr, d):
    if d == 1:
        return ref[0]
    rows = ref.shape[1]
    for r in range(d):
        for kc in range(ATT // 128):
            c0 = r * ATT + kc * 128
            scr[kc, pl.ds(r, rows, stride=d), :] = ref[0, :, c0:c0 + 128]
    return jnp.concatenate([scr[kc] for kc in range(ATT // 128)], axis=1)


def _merge_kernel(o0, o1, o2, l0, l1, l2, x_ref, gate_ref, wo_ref, xo_ref, scr, *, dils):
    la, lb, lc = [_undilate(r, scr, d) for r, d in zip((l0, l1, l2), dils)]
    m = jnp.maximum(jnp.maximum(la, lb), lc)
    ea, eb, ec = jnp.exp(la - m), jnp.exp(lb - m), jnp.exp(lc - m)
    den = ea + eb + ec
    o = (ea / den) * _undilate(o0, scr, dils[0])
    o = o + (eb / den) * _undilate(o1, scr, dils[1])
    o = o + (ec / den) * _undilate(o2, scr, dils[2])
    out = jnp.dot(o.astype(BF16), wo_ref[...], preferred_element_type=F32)
    xo_ref[0] = x_ref[0] + gate_ref[0] * out


def _merge(outs, lses, x3, gate, w_o_bf, *, t_tile, dils):
    bq, sq, _ = x3.shape
    mrows = gate.shape[1]
    mt = t_tile if mrows == sq else 1
    mmap = (lambda b, i: (b, i, 0)) if mrows == sq else (lambda b, i: (b, 0, 0))
    tmap = lambda b, i: (b, i, 0)
    ablk = [pl.BlockSpec((1, t_tile // d, d * ATT), tmap) for d in dils]
    return pl.pallas_call(
        functools.partial(_merge_kernel, dils=dils),
        grid=(bq, sq // t_tile),
        in_specs=ablk + ablk + [pl.BlockSpec((1, t_tile, D_MODEL), tmap), pl.BlockSpec((1, mt, D_MODEL), mmap),
                                pl.BlockSpec((ATT, D_MODEL), lambda b, i: (0, 0))],
        out_specs=pl.BlockSpec((1, t_tile, D_MODEL), tmap),
        out_shape=jax.ShapeDtypeStruct(x3.shape, F32),
        scratch_shapes=[pltpu.VMEM((ATT // 128, t_tile, 128), F32)],
        compiler_params=_cparams(("parallel", "parallel")),
        name="attn_merge",
    )(*outs, *lses, x3, gate, w_o_bf)


def _top16(s):
    vals = s
    rank = jnp.full(s.shape, float(PEER_N_KEYS), F32)
    tops = []
    for r in range(PEER_TOPK):
        m = jnp.max(vals, axis=0, keepdims=True)
        hit = vals == m
        rank = jnp.where(hit, float(r), rank)
        vals = jnp.where(hit, -jnp.inf, vals)
        tops.append(m)
    return tops, rank


def _peer_route_kernel(x_ref, shift_ref, scale_ref, gain_ref, wq_ref, sk_ref, ones_ref,
                       xt_ref, r1_ref, c1_ref, nf_ref, s_scr):
    hm = _modulate(x_ref[...], gain_ref[...], shift_ref[0], scale_ref[0])
    hb = hm.astype(BF16)
    xt_ref[...] = hm.T.astype(BF16)
    q = jnp.dot(hb, wq_ref[...], preferred_element_type=F32)
    qn = q * lax.rsqrt(_seg_mean(q * q, ones_ref[...]) + EPS)
    s_scr[...] = lax.dot_general(sk_ref[...], qn.astype(BF16), (((1,), (1,)), ((), ())),
                                 preferred_element_type=F32)

    def head(h, carry):
        base = pl.multiple_of(h * 2 * PEER_N_KEYS, 2 * PEER_N_KEYS)
        s1 = s_scr[pl.ds(base, PEER_N_KEYS), :]
        s2 = s_scr[pl.ds(base + PEER_N_KEYS, PEER_N_KEYS), :]
        top1, rank1 = _top16(s1)
        top2, rank2 = _top16(s2)
        row = lax.broadcasted_iota(jnp.int32, (PEER_TOPK, s1.shape[1]), 0)
        v1 = jnp.zeros((PEER_TOPK, s1.shape[1]), F32)
        for r in range(PEER_TOPK):
            v1 = jnp.where(row == r, top1[r], v1)
        cands = [v1[:8] + top2[b] for b in range(PEER_TOPK)] + [v1[8:] + top2[0]]
        best = top1[0] + top2[0]
        taken = [jnp.zeros_like(c) for c in cands]
        zsum = jnp.zeros_like(best)
        for _ in range(PEER_TOPK):
            m = cands[0]
            for c in cands[1:]:
                m = jnp.maximum(m, c)
            m = jnp.max(m, axis=0, keepdims=True)
            for j in range(len(cands)):
                hit = cands[j] == m
                taken[j] = jnp.where(hit, 1.0, taken[j])
                cands[j] = jnp.where(hit, -jnp.inf, cands[j])
            zsum = zsum + jnp.exp(m - best)
        cnt_lo = taken[0]
        for j in range(1, PEER_TOPK):
            cnt_lo = cnt_lo + taken[j]
        cnt = jnp.concatenate([cnt_lo, taken[PEER_TOPK]], axis=0)
        n2 = jnp.zeros_like(rank2)
        for a in range(PEER_TOPK):
            n2 = n2 + jnp.where(rank2 < cnt[a:a + 1], 1.0, 0.0)
        r1_ref[0, h] = rank1
        c1_ref[0, h] = jnp.exp(s1 - top1[0])
        f2 = jnp.exp(s2 - top2[0]) / zsum
        nf_ref[0, h, :, 0] = pltpu.bitcast(n2.astype(BF16), jnp.int32).reshape(PEER_N_KEYS // 16, 8, PEER_SEL_TOK)
        nf_ref[0, h, :, 1] = pltpu.bitcast(f2.astype(BF16), jnp.int32).reshape(PEER_N_KEYS // 16, 8, PEER_SEL_TOK)
        return carry

    lax.fori_loop(0, PEER_HEADS, head, 0)


def _peer_route(x2, shift, scale, gain, wq_bf, sk_bd, ones_bd):
    n = x2.shape[0]
    tt = PEER_SEL_TOK
    nt = n // tt
    per_mod = nt // shift.shape[0]
    mrows = shift.shape[1]
    if mrows == 1:
        mblk, mmap = (1, 1, D_MODEL), (lambda i: (i // per_mod, 0, 0))
    else:
        mblk, mmap = (1, tt, D_MODEL), (lambda i: (0, i, 0))
    cmap = lambda i: (0, 0)
    nrt = PEER_N_KEYS // 16
    hshape = jax.ShapeDtypeStruct((nt, PEER_HEADS, PEER_N_KEYS, tt), F32)
    hblk = pl.BlockSpec((1, PEER_HEADS, PEER_N_KEYS, tt), lambda i: (i, 0, 0, 0))
    nfshape = jax.ShapeDtypeStruct((nt, PEER_HEADS, nrt, 2, 8, tt), jnp.int32)
    nfblk = pl.BlockSpec((1, PEER_HEADS, nrt, 2, 8, tt), lambda i: (i, 0, 0, 0, 0, 0))
    return pl.pallas_call(
        _peer_route_kernel,
        grid=(nt,),
        in_specs=[
            pl.BlockSpec((tt, D_MODEL), lambda i: (i, 0)),
            pl.BlockSpec(mblk, mmap), pl.BlockSpec(mblk, mmap),
            pl.BlockSpec((1, D_MODEL), cmap),
            pl.BlockSpec((D_MODEL, D_MODEL), cmap),
            pl.BlockSpec((2 * PEER_HEADS * PEER_N_KEYS, D_MODEL), cmap),
            pl.BlockSpec((D_MODEL, D_MODEL), cmap),
        ],
        out_specs=[pl.BlockSpec((D_MODEL, tt), lambda i: (0, i)), hblk, hblk, nfblk],
        out_shape=[jax.ShapeDtypeStruct((D_MODEL, n), BF16), hshape, hshape, nfshape],
        scratch_shapes=[pltpu.VMEM((2 * PEER_HEADS * PEER_N_KEYS, tt), F32)],
        compiler_params=_cparams(("parallel",)),
        name="peer_route",
    )(x2, shift, scale, gain.reshape(1, D_MODEL), wq_bf, sk_bd, ones_bd)


def _gelu(x):
    return 0.5 * x * (1.0 + lax.erf(x * np.float32(math.sqrt(0.5))))


def _peer_expert_kernel(xt_ref, u_ref, un_ref, vt_ref, vp_ref, r1_ref, c1_ref, nf_ref, x_ref, gate_ref, xo_ref,
                        act_scr, w_scr, acc_scr, *, t_tile):
    e = pl.program_id(1)
    nq = PEER_EXP_TILE // PEER_CHUNK
    last = slice((nq - 1) * PEER_CHUNK, nq * PEER_CHUNK)

    @pl.when(e == 0)
    def _():
        acc_scr[...] = jnp.zeros_like(acc_scr)
        w_scr[nq - 1] = jnp.zeros((PEER_CHUNK, t_tile), BF16)
        act_scr[0] = jnp.dot(u_ref[0, :PEER_CHUNK, :], xt_ref[...], preferred_element_type=F32)

    def stage_a(q):
        qs = slice(q * PEER_CHUNK, (q + 1) * PEER_CHUNK)
        act_scr[q] = jnp.dot(u_ref[0, qs, :], xt_ref[...], preferred_element_type=F32)

    def stage_a_next():
        act_scr[0] = jnp.dot(un_ref[0], xt_ref[...], preferred_element_type=F32)

    def stage_b(q):
        for i1l in range(PEER_CHUNK // PEER_N_KEYS):
            i1 = q * (PEER_CHUNK // PEER_N_KEYS) + i1l
            for lc in range(t_tile // PEER_SEL_TOK):
                ls = slice(lc * PEER_SEL_TOK, (lc + 1) * PEER_SEL_TOK)
                r1 = [jnp.broadcast_to(r1_ref[lc, h, i1:i1 + 1, :], (16, PEER_SEL_TOK)).astype(BF16)
                      for h in range(PEER_HEADS)]
                c1 = [jnp.broadcast_to(c1_ref[lc, h, i1:i1 + 1, :], (16, PEER_SEL_TOK)).astype(BF16)
                      for h in range(PEER_HEADS)]
                for j in range(PEER_N_KEYS // 16):
                    g = jnp.zeros((16, PEER_SEL_TOK), BF16)
                    for h in range(PEER_HEADS):
                        n2 = pltpu.bitcast(nf_ref[lc, h, j, 0], BF16)
                        f2 = pltpu.bitcast(nf_ref[lc, h, j, 1], BF16)
                        g = g + jnp.where(n2 > r1[h], f2, 0.0) * c1[h]
                    rs = slice(i1l * PEER_N_KEYS + j * 16, i1l * PEER_N_KEYS + (j + 1) * 16)
                    w_scr[q, rs, ls] = g * _gelu(act_scr[q, rs, ls]).astype(BF16)

    def stage_c(q):
        qs = slice(q * PEER_CHUNK, (q + 1) * PEER_CHUNK)
        acc_scr[...] += jnp.dot(vt_ref[0, :, qs], w_scr[q], preferred_element_type=F32)

    def stage_c_prev():
        acc_scr[...] += jnp.dot(vp_ref[0], w_scr[nq - 1], preferred_element_type=F32)

    stage_c_prev()
    for q in range(nq):
        if q + 1 < nq:
            stage_a(q + 1)
        else:
            stage_a_next()
        stage_b(q)
        if q > 0:
            stage_c(q - 1)

    @pl.when(e == pl.num_programs(1) - 1)
    def _():
        acc = acc_scr[...] + jnp.dot(vt_ref[0, :, last], w_scr[nq - 1], preferred_element_type=F32)
        xo_ref[...] = x_ref[...] + gate_ref[0] * acc.T


def _peer_expert(xt, u_bf, vt_bf, layer, r1, c1, nf, x2, gate, *, t_tile):
    n = x2.shape[0]
    nt = n // t_tile
    nlc = t_tile // PEER_SEL_TOK
    ne = u_bf.shape[1] // PEER_EXP_TILE
    nq = PEER_EXP_TILE // PEER_CHUNK
    per_mod = max(nt // gate.shape[0], 1)
    mrows = gate.shape[1]
    if mrows == 1:
        mblk, mmap = (1, 1, D_MODEL), (lambda i, e: (i // per_mod, 0, 0))
    else:
        mblk, mmap = (1, t_tile, D_MODEL), (lambda i, e: (0, i, 0))
    i1blk = pl.BlockSpec((nlc, PEER_HEADS, PEER_I1_TILE, PEER_SEL_TOK), lambda i, e: (i, 0, e, 0))
    nfblk = pl.BlockSpec((nlc, PEER_HEADS, PEER_N_KEYS // 16, 2, 8, PEER_SEL_TOK), lambda i, e: (i, 0, 0, 0, 0, 0))
    kern = functools.partial(_peer_expert_kernel, t_tile=t_tile)
    return pl.pallas_call(
        kern,
        grid=(nt, ne),
        in_specs=[
            pl.BlockSpec((D_MODEL, t_tile), lambda i, e: (0, i)),
            pl.BlockSpec((1, PEER_EXP_TILE, D_MODEL), lambda i, e: (layer, e, 0)),
            pl.BlockSpec((1, PEER_CHUNK, D_MODEL), lambda i, e: (layer, jnp.minimum(e + 1, ne - 1) * nq, 0)),
            pl.BlockSpec((1, D_MODEL, PEER_EXP_TILE), lambda i, e: (layer, 0, e)),
            pl.BlockSpec((1, D_MODEL, PEER_CHUNK), lambda i, e: (layer, 0, jnp.maximum(e, 1) * nq - 1)),
            i1blk, i1blk, nfblk,
            pl.BlockSpec((t_tile, D_MODEL), lambda i, e: (i, 0)),
            pl.BlockSpec(mblk, mmap),
        ],
        out_specs=pl.BlockSpec((t_tile, D_MODEL), lambda i, e: (i, 0)),
        out_shape=jax.ShapeDtypeStruct(x2.shape, F32),
        scratch_shapes=[pltpu.VMEM((PEER_EXP_TILE // PEER_CHUNK, PEER_CHUNK, t_tile), F32),
                        pltpu.VMEM((PEER_EXP_TILE // PEER_CHUNK, PEER_CHUNK, t_tile), BF16),
                        pltpu.VMEM((D_MODEL, t_tile), F32)],
        compiler_params=_cparams(("parallel", "arbitrary")),
        name="peer_expert",
    )(xt, u_bf, u_bf, vt_bf, vt_bf, r1, c1, nf, x2, gate)


def _peer(x2, shift, scale, gate, gain, pw, tables, layer, *, t_tile):
    wq_bf, sk_bd, ones_bd = pw
    u_bf, vt_bf = tables
    xt, r1, c1, nf = _peer_route(x2, shift, scale, gain, wq_bf, sk_bd, ones_bd)
    return _peer_expert(xt, u_bf, vt_bf, layer, r1, c1, nf, x2, gate, t_tile=t_tile)


def _peer_weights(w_q, sub_keys, ones_bd):
    hc = 2 * PEER_HEADS
    sk = sub_keys.reshape(hc, PEER_N_KEYS, PEER_HALF)
    sk_bd = jnp.einsum('akd,ab->akbd', sk, jnp.eye(hc, dtype=F32)).reshape(hc * PEER_N_KEYS, hc * PEER_HALF)
    return (w_q.astype(BF16), sk_bd.astype(BF16), ones_bd)


def kernel(x_prompt, x_sample, c_prompt, c_sample, state_ssm, cache_kv_w128, cache_kv_w512, cache_kv_w2048,
           norm_mix, norm_ffn, ada_w, ada_b, ssm_lam_re, ssm_lam_im, ssm_log_dt, ssm_b_re, ssm_b_im, ssm_c_re,
           ssm_c_im, ssm_d, ssm_w_glu, attn_w_qkv, attn_q_norm, attn_k_norm, attn_w_o, peer_w_q,
           peer_sub_keys, peer_u, peer_v):
    bp, sp, _ = x_prompt.shape
    bs, ts, _ = x_sample.shape
    ns = bs * ts
    ones_att = _ones_blockdiag(ATT)
    ones_d = _ones_blockdiag(D_MODEL)

    mods = _adaln(jnp.concatenate([c_prompt, c_sample], axis=0), ada_w, ada_b)
    caches = (cache_kv_w128, cache_kv_w512, cache_kv_w2048)
    tab_p = _rope_tables(jnp.arange(sp, dtype=jnp.int32))
    pos_s = PAST_LEN + jnp.arange(ts, dtype=jnp.int32)
    tab_s = tuple(jnp.tile(t, (bs, 1)) for t in _rope_tables(pos_s))

    peer_tabs = (peer_u.astype(BF16), peer_v.astype(BF16).transpose(0, 2, 1))
    xp, xs = x_prompt, x_sample
    ssm_p, ssm_s = [], []
    kv_p = [[] for _ in range(N_GROUPS)]
    kv_s = [[] for _ in range(N_GROUPS)]
    for i in range(DEPTH):
        j = i // 2
        mp = [m.reshape(bp, 1, D_MODEL) for m in jnp.split(mods[i, :bp], 6, axis=-1)]
        ms_b = jnp.split(mods[i, bp:], 6, axis=-1)
        ms_rows = [jnp.repeat(m, ts, axis=0).reshape(1, ns, D_MODEL) for m in ms_b]
        if i % 2 == 0:
            sw = _s5_weights(ssm_lam_re[j], ssm_lam_im[j], ssm_log_dt[j], ssm_b_re[j], ssm_b_im[j],
                             ssm_c_re[j], ssm_c_im[j])
            wglu = ssm_w_glu[j].astype(BF16)
            mp_t = [m.reshape(1, bp, D_MODEL) for m in mp[:3]]
            xp_t, st_p = _s5_layer(xp.transpose(1, 0, 2), mp_t[0], mp_t[1], mp_t[2], norm_mix[i],
                                   jnp.zeros((bp, SSM_STATE_W), F32), sw, ssm_d[j], wglu, t_chunk=32)
            xp = xp_t.transpose(1, 0, 2)
            ms_t = [m.reshape(1, bs, D_MODEL) for m in ms_b[:3]]
            xs_t, st_s = _s5_layer(xs.transpose(1, 0, 2), ms_t[0], ms_t[1], ms_t[2], norm_mix[i],
                                   _state_to_rows(state_ssm[j]), sw, ssm_d[j], wglu, t_chunk=ts)
            xs = xs_t.transpose(1, 0, 2)
            ssm_p.append(_rows_to_state(st_p))
            ssm_s.append(_rows_to_state(st_s))
        else:
            wqkv = attn_w_qkv[j].astype(BF16)
            wo = attn_w_o[j].astype(BF16)
            k, v, q0, q1, k1, v1, q2, k2, v2 = _qkv_dilated(xp, mp[0], mp[1], norm_mix[i], wqkv, attn_q_norm[j],
                                                            attn_k_norm[j], tab_p, ones_att, t_tile=512)
            res = [_band_attention(q0, k, v, 1, 1, 0, N_GROUPS, 0),
                   _band_attention(q1, k1, v1, DILATIONS[1], 1, 0, 1, 0),
                   _band_attention(q2, k2, v2, DILATIONS[2], 1, 0, 1, 0)]
            outs, lses = zip(*res)
            xp = _merge(outs, lses, xp, mp[2], wo, t_tile=512, dils=DILATIONS)
            for g in range(N_GROUPS):
                keep = min(WINDOWS[g], sp)
                gs = slice(g * ATT, (g + 1) * ATT)
                kv_p[g].append(jnp.stack([k[:, sp - keep:, gs], v[:, sp - keep:, gs]], axis=1)
                               .reshape(bp, 2, keep, N_HEADS, HEAD_DIM))
            xs_rows = xs.reshape(1, ns, D_MODEL)
            qs, ks, vs = _qkv(xs_rows, ms_rows[0], ms_rows[1], norm_mix[i], wqkv, attn_q_norm[j],
                              attn_k_norm[j], tab_s, ones_att, t_tile=ns)
            qs, ks, vs = (t.reshape(bs, ts, N_GROUPS * ATT) for t in (qs, ks, vs))
            res = _cache_attention(qs, ks, vs, [c.reshape(c.shape[:4] + (ATT,)) for c in caches], j)
            outs_s = [r.reshape(1, ns, ATT) for r in res[:N_GROUPS]]
            lses_s = [r.reshape(1, ns, ATT) for r in res[N_GROUPS:]]
            xs = _merge(outs_s, lses_s, xs_rows, ms_rows[2], wo, t_tile=ns,
                        dils=(1, 1, 1)).reshape(bs, ts, D_MODEL)
            for g in range(N_GROUPS):
                gs = slice(g * ATT, (g + 1) * ATT)
                kv_s[g].append(jnp.stack([ks[:, :, gs], vs[:, :, gs]], axis=1)
                               .reshape(bs, 2, ts, N_HEADS, HEAD_DIM))
        pw = _peer_weights(peer_w_q[i], peer_sub_keys[i], ones_d)
        xp = _peer(xp.reshape(bp * sp, D_MODEL), mp[3], mp[4], mp[5], norm_ffn[i], pw, peer_tabs, i,
                   t_tile=512).reshape(bp, sp, D_MODEL)
        xs = _peer(xs.reshape(ns, D_MODEL), ms_rows[3], ms_rows[4], ms_rows[5], norm_ffn[i], pw, peer_tabs, i,
                   t_tile=ns).reshape(bs, ts, D_MODEL)
    return (xp, xs,
            jnp.stack(ssm_p), jnp.stack(kv_p[0]), jnp.stack(kv_p[1]), jnp.stack(kv_p[2]),
            jnp.stack(ssm_s), jnp.stack(kv_s[0]), jnp.stack(kv_s[1]), jnp.stack(kv_s[2]))
```

```python
import functools
import math

import jax
import jax.numpy as jnp
import numpy as np
from jax import lax
from jax.experimental import pallas as pl
from jax.experimental.pallas import tpu as pltpu

F32 = jnp.float32
BF16 = jnp.bfloat16

D_MODEL = 1024
DEPTH = 4
PAST_LEN = 8192
EPS = 1e-6
NEG_INF = -1e30

SSM_P = 16
SSM_GROUPS = D_MODEL // SSM_P
SSM_N = 64
SSM_GB = 4
SSM_GB_CH = D_MODEL // SSM_GB
SSM_GB_ST = SSM_GROUPS * SSM_N // SSM_GB
SSM_STATE_W = 2 * SSM_GROUPS * SSM_N

N_HEADS = 8
HEAD_DIM = 64
ATT = N_HEADS * HEAD_DIM
WINDOWS = (128, 512, 2048)
DILATIONS = (1, 4, 16)
N_GROUPS = 3
N_BACK = 128
ROPE_DIM = HEAD_DIM // 4
ROPE_THETA = 500000.0

PEER_HEADS = 8
PEER_N_KEYS = 128
PEER_HALF = 64
PEER_TOPK = 16
PEER_SEL_TOK = 128
PEER_EXP_TILE = 1024
PEER_I1_TILE = PEER_EXP_TILE // PEER_N_KEYS
PEER_CHUNK = 256

VMEM_LIMIT = 56 * 1024 * 1024


def _cparams(sem):
    return pltpu.CompilerParams(dimension_semantics=sem, vmem_limit_bytes=VMEM_LIMIT)


def _modulate(x, gain, shift, scale):
    ms = jnp.mean(x * x, axis=-1, keepdims=True)
    return x * lax.rsqrt(ms + EPS) * gain * (1.0 + scale) + shift


def _seg_mean(x2, ones_bd):
    hi = x2.astype(BF16)
    lo = (x2 - hi.astype(F32)).astype(BF16)
    s = jnp.dot(hi, ones_bd, preferred_element_type=F32) + jnp.dot(lo, ones_bd, preferred_element_type=F32)
    return s * (1.0 / PEER_HALF)


def _ones_blockdiag(n):
    seg = np.arange(n) // 64
    return jnp.asarray((seg[:, None] == seg[None, :]).astype(np.float32), dtype=BF16)


def _adaln_kernel(c_ref, w_ref, b_ref, o_ref):
    s = jax.nn.silu(c_ref[...])
    o_ref[0] = jnp.dot(s.astype(BF16), w_ref[0].astype(BF16), preferred_element_type=F32) + b_ref[0]


def _adaln(c_all, ada_w, ada_b):
    nb = c_all.shape[0]
    nt = 6 * D_MODEL // 1024
    return pl.pallas_call(
        _adaln_kernel,
        grid=(DEPTH, nt),
        in_specs=[
            pl.BlockSpec((nb, D_MODEL), lambda l, j: (0, 0)),
            pl.BlockSpec((1, D_MODEL, 1024), lambda l, j: (l, 0, j)),
            pl.BlockSpec((1, 1, 1024), lambda l, j: (l, 0, j)),
        ],
        out_specs=pl.BlockSpec((1, nb, 1024), lambda l, j: (l, 0, j)),
        out_shape=jax.ShapeDtypeStruct((DEPTH, nb, 6 * D_MODEL), F32),
        compiler_params=_cparams(("parallel", "parallel")),
        name="adaln",
    )(c_all, ada_w, ada_b.reshape(DEPTH, 1, 6 * D_MODEL))


def _s5_kernel(x_ref, shift_ref, scale_ref, gate_ref, gain_ref, h0_ref, bm_ref, cm_ref, are_ref, aim_ref,
               d_ref, wglu_ref, xo_ref, st_ref, bu_scr, state_scr, *, n_batch, n_time):
    i = pl.program_id(0)

    @pl.when(i == 0)
    def _():
        state_scr[...] = h0_ref[...]

    x3 = x_ref[...]
    a0, a1, _ = x3.shape
    rows = a0 * a1
    h3 = _modulate(x3, gain_ref[...], shift_ref[...], scale_ref[...])
    u = h3.reshape(rows, D_MODEL)
    ub = u.astype(BF16)
    for gb in range(SSM_GB):
        bu_scr[:, gb * 2 * SSM_GB_ST:(gb + 1) * 2 * SSM_GB_ST] = jnp.dot(
            ub[:, gb * SSM_GB_CH:(gb + 1) * SSM_GB_CH], bm_ref[gb], preferred_element_type=F32)

    for bb in range(n_batch // 8):
        for gb in range(SSM_GB):
            c_re = gb * 2 * SSM_GB_ST
            c_im = c_re + SSM_GB_ST
            ar = jnp.broadcast_to(are_ref[gb], (8, SSM_GB_ST))
            ai = jnp.broadcast_to(aim_ref[gb], (8, SSM_GB_ST))
            xr = state_scr[bb * 8:(bb + 1) * 8, c_re:c_re + SSM_GB_ST]
            xi = state_scr[bb * 8:(bb + 1) * 8, c_im:c_im + SSM_GB_ST]
            for t in range(n_time):
                rsel = pl.ds(t * n_batch + bb * 8, 8)
                br = bu_scr[rsel, c_re:c_re + SSM_GB_ST]
                bi = bu_scr[rsel, c_im:c_im + SSM_GB_ST]
                nr = ar * xr - ai * xi + br
                ni = ar * xi + ai * xr + bi
                bu_scr[rsel, c_re:c_re + SSM_GB_ST] = nr
                bu_scr[rsel, c_im:c_im + SSM_GB_ST] = ni
                xr, xi = nr, ni
            state_scr[bb * 8:(bb + 1) * 8, c_re:c_re + SSM_GB_ST] = xr
            state_scr[bb * 8:(bb + 1) * 8, c_im:c_im + SSM_GB_ST] = xi

    ys = []
    for gb in range(SSM_GB):
        st = bu_scr[:, gb * 2 * SSM_GB_ST:(gb + 1) * 2 * SSM_GB_ST].astype(BF16)
        ys.append(jnp.dot(st, cm_ref[gb], preferred_element_type=F32))
    y = jnp.concatenate(ys, axis=-1) + d_ref[...] * u
    z = jnp.dot(y.astype(BF16), wglu_ref[...], preferred_element_type=F32)
    out = z[:, :D_MODEL] * jax.nn.sigmoid(z[:, D_MODEL:])
    xo_ref[...] = x3 + gate_ref[...] * out.reshape(a0, a1, D_MODEL)

    @pl.when(i == pl.num_programs(0) - 1)
    def _():
        st_ref[...] = state_scr[...]


def _s5_weights(lam_re, lam_im, log_dt, b_re, b_im, c_re, c_im):
    dt = jnp.exp(log_dt)[:, None]
    mag = jnp.exp(lam_re * dt)
    a_re = mag * jnp.cos(lam_im * dt)
    a_im = mag * jnp.sin(lam_im * dt)
    den = lam_re * lam_re + lam_im * lam_im
    f_re = ((a_re - 1.0) * lam_re + a_im * lam_im) / den
    f_im = (a_im * lam_re - (a_re - 1.0) * lam_im) / den
    bb_re = f_re[..., None] * b_re - f_im[..., None] * b_im
    bb_im = f_re[..., None] * b_im + f_im[..., None] * b_re
    eye = jnp.eye(16, dtype=F32)

    def in_blk(b):
        b4 = b.reshape(SSM_GB, 16, SSM_N, SSM_P)
        return jnp.einsum('bgnp,gh->bgphn', b4, eye).reshape(SSM_GB, SSM_GB_CH, SSM_GB_ST)

    def out_blk(c):
        c4 = c.reshape(SSM_GB, 16, SSM_P, SSM_N)
        return jnp.einsum('bgpn,gh->bgnhp', c4, eye).reshape(SSM_GB, SSM_GB_ST, SSM_GB_CH)

    bm = jnp.concatenate([in_blk(bb_re), in_blk(bb_im)], axis=2).astype(BF16)
    cm = jnp.concatenate([out_blk(c_re), -out_blk(c_im)], axis=1).astype(BF16)
    return bm, cm, a_re.reshape(SSM_GB, 1, SSM_GB_ST), a_im.reshape(SSM_GB, 1, SSM_GB_ST)


def _state_to_rows(st):
    b = st.shape[0]
    return st.reshape(b, 2, SSM_GB, SSM_GB_ST).transpose(0, 2, 1, 3).reshape(b, SSM_STATE_W)


def _rows_to_state(rows):
    b = rows.shape[0]
    return rows.reshape(b, SSM_GB, 2, SSM_GB_ST).transpose(0, 2, 1, 3).reshape(b, 2, SSM_GROUPS, SSM_N)


def _s5_layer(x3, shift, scale, gate, gain, h0_rows, weights, d_skip, w_glu, *, t_chunk):
    bm, cm, a_re, a_im = weights
    seq, n_batch = x3.shape[0], x3.shape[1]
    n_time = t_chunk
    blk = (n_time, n_batch, D_MODEL)
    xmap = lambda i: (i, 0, 0)
    mblk = (1, n_batch, D_MODEL)
    grid = seq // n_time
    rows = n_batch * n_time
    const3 = lambda i: (0, 0, 0)
    const2 = lambda i: (0, 0)
    kern = functools.partial(_s5_kernel, n_batch=n_batch, n_time=n_time)
    return pl.pallas_call(
        kern,
        grid=(grid,),
        in_specs=[
            pl.BlockSpec(blk, xmap),
            pl.BlockSpec(mblk, const3), pl.BlockSpec(mblk, const3), pl.BlockSpec(mblk, const3),
            pl.BlockSpec((1, 1, D_MODEL), const3),
            pl.BlockSpec((n_batch, SSM_STATE_W), const2),
            pl.BlockSpec(bm.shape, const3), pl.BlockSpec(cm.shape, const3),
            pl.BlockSpec(a_re.shape, const3), pl.BlockSpec(a_im.shape, const3),
            pl.BlockSpec((1, D_MODEL), const2),
            pl.BlockSpec((D_MODEL, 2 * D_MODEL), const2),
        ],
        out_specs=[pl.BlockSpec(blk, xmap), pl.BlockSpec((n_batch, SSM_STATE_W), const2)],
        out_shape=[jax.ShapeDtypeStruct(x3.shape, F32), jax.ShapeDtypeStruct((n_batch, SSM_STATE_W), F32)],
        scratch_shapes=[pltpu.VMEM((rows, SSM_STATE_W), F32), pltpu.VMEM((n_batch, SSM_STATE_W), F32)],
        compiler_params=_cparams(("arbitrary",)),
        name="s5_layer",
    )(x3, shift, scale, gate, gain.reshape(1, 1, D_MODEL), h0_rows, bm, cm, a_re, a_im,
      d_skip.reshape(1, D_MODEL), w_glu)


def _qkv_kernel(x_ref, shift_ref, scale_ref, gain_ref, w_ref, qg_ref, kg_ref, rc_ref, rs1_ref, rs2_ref,
                ones_ref, q_ref, k_ref, v_ref):
    h = _modulate(x_ref[0], gain_ref[...], shift_ref[0], scale_ref[0])
    qkv = jnp.dot(h.astype(BF16), w_ref[...], preferred_element_type=F32)
    rc, rs1, rs2 = rc_ref[...], rs1_ref[...], rs2_ref[...]
    ones_bd = ones_ref[...]

    def norm_rope(t, g):
        tn = t * lax.rsqrt(_seg_mean(t * t, ones_bd) + EPS) * g
        return tn * rc + pltpu.roll(tn, ATT - ROPE_DIM // 2, 1) * rs1 + pltpu.roll(tn, ROPE_DIM // 2, 1) * rs2

    q_ref[0] = norm_rope(qkv[:, :ATT], qg_ref[...])
    k_ref[0] = norm_rope(qkv[:, ATT:2 * ATT], kg_ref[...])
    v_ref[0] = qkv[:, 2 * ATT:]


def _dilate(x, scr, out_ref, d):
    t = x.shape[0]
    for kc in range(ATT // 128):
        scr[kc] = x[:, kc * 128:(kc + 1) * 128]
    for r in range(d):
        for kc in range(ATT // 128):
            c0 = r * ATT + kc * 128
            out_ref[0, :, c0:c0 + 128] = scr[kc, pl.ds(r, t // d, stride=d), :]


def _qkv_dil_kernel(x_ref, shift_ref, scale_ref, gain_ref, w_ref, qg_ref, kg_ref, rc_ref, rs1_ref, rs2_ref,
                    ones_ref, kn_ref, vn_ref, q0_ref, q1_ref, k1_ref, v1_ref, q2_ref, k2_ref, v2_ref, scr):
    g = pl.program_id(2)
    h = _modulate(x_ref[0], gain_ref[...], shift_ref[0], scale_ref[0])
    qkv = jnp.dot(h.astype(BF16), w_ref[...], preferred_element_type=F32)
    rc, rs1, rs2 = rc_ref[...], rs1_ref[...], rs2_ref[...]
    ones_bd = ones_ref[...]

    def norm_rope(t, gn):
        tn = t * lax.rsqrt(_seg_mean(t * t, ones_bd) + EPS) * gn
        return tn * rc + pltpu.roll(tn, ATT - ROPE_DIM // 2, 1) * rs1 + pltpu.roll(tn, ROPE_DIM // 2, 1) * rs2

    q = norm_rope(qkv[:, :ATT], qg_ref[...])
    k = norm_rope(qkv[:, ATT:2 * ATT], kg_ref[...])
    v = qkv[:, 2 * ATT:]
    kn_ref[0] = k
    vn_ref[0] = v

    @pl.when(g == 0)
    def _():
        q0_ref[0] = q

    for gi, refs in ((1, (q1_ref, k1_ref, v1_ref)), (2, (q2_ref, k2_ref, v2_ref))):
        @pl.when(g == gi)
        def _():
            for val, ref in zip((q, k, v), refs):
                _dilate(val, scr, ref, DILATIONS[gi])


def _qkv_dilated(x3, shift, scale, gain, w_qkv_bf, q_gain, k_gain, tables, ones_bd, *, t_tile):
    bq, sq, _ = x3.shape
    rc, rs1, rs2 = tables
    qg = jnp.tile(q_gain, N_HEADS).reshape(1, ATT)
    kg = jnp.tile(k_gain, N_HEADS).reshape(1, ATT)
    mmap = lambda b, i, g: (b, 0, 0)
    tmap = lambda b, i, g: (i, 0)
    cmap = lambda b, i, g: (0, 0)
    gmap = lambda b, i, g: (b, i, g)
    bmap = lambda b, i, g: (b, i, 0)
    nat = jax.ShapeDtypeStruct((bq, sq, N_GROUPS * ATT), F32)
    out_shape = [nat, nat, jax.ShapeDtypeStruct((bq, sq, ATT), F32)]
    out_specs = [pl.BlockSpec((1, t_tile, ATT), gmap)] * 2 + [pl.BlockSpec((1, t_tile, ATT), bmap)]
    for gi in (1, 2):
        d = DILATIONS[gi]
        out_shape += [jax.ShapeDtypeStruct((bq, sq // d, d * ATT), F32)] * 3
        out_specs += [pl.BlockSpec((1, t_tile // d, d * ATT), bmap)] * 3
    return pl.pallas_call(
        _qkv_dil_kernel,
        grid=(bq, sq // t_tile, N_GROUPS),
        in_specs=[
            pl.BlockSpec((1, t_tile, D_MODEL), bmap),
            pl.BlockSpec((1, 1, D_MODEL), mmap), pl.BlockSpec((1, 1, D_MODEL), mmap),
            pl.BlockSpec((1, D_MODEL), cmap),
            pl.BlockSpec((D_MODEL, 3 * ATT), lambda b, i, g: (0, g)),
            pl.BlockSpec((1, ATT), cmap), pl.BlockSpec((1, ATT), cmap),
            pl.BlockSpec((t_tile, ATT), tmap), pl.BlockSpec((t_tile, ATT), tmap), pl.BlockSpec((t_tile, ATT), tmap),
            pl.BlockSpec((ATT, ATT), cmap),
        ],
        out_specs=out_specs,
        out_shape=out_shape,
        scratch_shapes=[pltpu.VMEM((ATT // 128, t_tile, 128), F32)],
        compiler_params=_cparams(("parallel", "parallel", "arbitrary")),
        name="qkv_proj_dilated",
    )(x3, shift, scale, gain.reshape(1, D_MODEL), w_qkv_bf, qg, kg, rc, rs1, rs2, ones_bd)


def _rope_tables(pos):
    half = ROPE_DIM // 2
    inv = ROPE_THETA ** (-jnp.arange(half, dtype=F32) / half)
    ang = pos.astype(F32)[:, None] * inv[None, :]
    cos, sin = jnp.cos(ang), jnp.sin(ang)
    lane = np.arange(ATT) % HEAD_DIM
    fidx = lane % half
    first = jnp.asarray(lane < half)
    second = jnp.asarray((lane >= half) & (lane < ROPE_DIM))
    cl, sl = cos[:, fidx], sin[:, fidx]
    rc = jnp.where(first | second, cl, 1.0)
    rs1 = jnp.where(first, -sl, 0.0)
    rs2 = jnp.where(second, sl, 0.0)
    return rc, rs1, rs2


def _qkv(x3, shift, scale, gain, w_qkv_bf, q_gain, k_gain, tables, ones_bd, *, t_tile):
    bq, sq, _ = x3.shape
    mrows = shift.shape[1]
    mt = t_tile if mrows == sq else 1
    mmap = (lambda b, i, g: (b, i, 0)) if mrows == sq else (lambda b, i, g: (b, 0, 0))
    rc, rs1, rs2 = tables
    qg = jnp.tile(q_gain, N_HEADS).reshape(1, ATT)
    kg = jnp.tile(k_gain, N_HEADS).reshape(1, ATT)
    tmap = lambda b, i, g: (i, 0)
    cmap = lambda b, i, g: (0, 0)
    omap = lambda b, i, g: (b, i, g)
    oshape = jax.ShapeDtypeStruct((bq, sq, N_GROUPS * ATT), F32)
    return pl.pallas_call(
        _qkv_kernel,
        grid=(bq, sq // t_tile, N_GROUPS),
        in_specs=[
            pl.BlockSpec((1, t_tile, D_MODEL), lambda b, i, g: (b, i, 0)),
            pl.BlockSpec((1, mt, D_MODEL), mmap), pl.BlockSpec((1, mt, D_MODEL), mmap),
            pl.BlockSpec((1, D_MODEL), cmap),
            pl.BlockSpec((D_MODEL, 3 * ATT), lambda b, i, g: (0, g)),
            pl.BlockSpec((1, ATT), cmap), pl.BlockSpec((1, ATT), cmap),
            pl.BlockSpec((t_tile, ATT), tmap), pl.BlockSpec((t_tile, ATT), tmap), pl.BlockSpec((t_tile, ATT), tmap),
            pl.BlockSpec((ATT, ATT), cmap),
        ],
        out_specs=[pl.BlockSpec((1, t_tile, ATT), omap)] * 3,
        out_shape=[oshape] * 3,
        compiler_params=_cparams(("parallel", "parallel", "arbitrary")),
        name="qkv_proj",
    )(x3, shift, scale, gain.reshape(1, D_MODEL), w_qkv_bf, qg, kg, rc, rs1, rs2, ones_bd)


def _band_kernel(q_ref, kp_ref, kc_ref, vp_ref, vc_ref, o_ref, l_ref):
    i = pl.program_id(2)
    q = q_ref[0].astype(BF16)
    k = jnp.concatenate([kp_ref[0], kc_ref[0]], axis=0).astype(BF16)
    v = jnp.concatenate([vp_ref[0], vc_ref[0]], axis=0).astype(BF16)
    iq = lax.broadcasted_iota(jnp.int32, (N_BACK, 2 * N_BACK), 0)
    ik = lax.broadcasted_iota(jnp.int32, (N_BACK, 2 * N_BACK), 1)
    dist = iq + N_BACK - ik
    mask = (dist >= 0) & (dist <= N_BACK) & ((ik >= N_BACK) | (i > 0))
    for h in range(N_HEADS):
        sl = slice(h * HEAD_DIM, (h + 1) * HEAD_DIM)
        s = lax.dot_general(q[:, sl], k[:, sl], (((1,), (1,)), ((), ())),
                            preferred_element_type=F32) * (HEAD_DIM ** -0.5)
        s = jnp.where(mask, s, NEG_INF)
        m = jnp.max(s, axis=-1, keepdims=True)
        p = jnp.exp(s - m)
        den = jnp.sum(p, axis=-1, keepdims=True)
        o = jnp.dot(p.astype(BF16), v[:, sl], preferred_element_type=F32) / den
        o_ref[0, :, sl] = o
        l_ref[0, :, sl] = jnp.broadcast_to(m + jnp.log(den), (N_BACK, HEAD_DIM))


def _band_attention(qv, kv, vv, d, qcols, qcol0, kcols, kcol0):
    b, sub, _ = qv.shape
    nb = sub // N_BACK
    qcur = lambda bi, r, i: (bi, i, r * qcols + qcol0)
    cur = lambda bi, r, i: (bi, i, r * kcols + kcol0)
    prev = lambda bi, r, i: (bi, jnp.maximum(i - 1, 0), r * kcols + kcol0)
    omap = lambda bi, r, i: (bi, i, r)
    blk = (1, N_BACK, ATT)
    oshape = jax.ShapeDtypeStruct((b, sub, d * ATT), F32)
    return pl.pallas_call(
        _band_kernel,
        grid=(b, d, nb),
        in_specs=[pl.BlockSpec(blk, qcur), pl.BlockSpec(blk, prev), pl.BlockSpec(blk, cur),
                  pl.BlockSpec(blk, prev), pl.BlockSpec(blk, cur)],
        out_specs=[pl.BlockSpec(blk, omap), pl.BlockSpec(blk, omap)],
        out_shape=[oshape, oshape],
        compiler_params=_cparams(("parallel", "parallel", "arbitrary")),
        name="band_attention",
    )(qv, kv, kv, vv, vv)


def _cache_attn_kernel(q_ref, k_ref, v_ref, c0_ref, c1_ref, c2_ref, *out_refs):
    n_new = q_ref.shape[1]
    rows = n_new * N_HEADS
    lane_head = lax.broadcasted_iota(jnp.int32, (N_HEADS, ATT), 1) // HEAD_DIM
    head_mask = (lane_head == lax.broadcasted_iota(jnp.int32, (N_HEADS, ATT), 0)).astype(F32)
    row_t = lax.broadcasted_iota(jnp.int32, (rows, 1), 0) // N_HEADS
    caches = (c0_ref, c1_ref, c2_ref)
    for g in range(N_GROUPS):
        d, win = DILATIONS[g], WINDOWS[g]
        gs = slice(g * ATT, (g + 1) * ATT)
        q = q_ref[0][:, gs]
        kn = k_ref[0][:, gs]
        vn = v_ref[0][:, gs]
        kc = caches[g][0, 0, 0].astype(BF16)
        vc = caches[g][0, 0, 1].astype(BF16)
        qbd = jnp.concatenate([q[t:t + 1, :] * head_mask for t in range(n_new)], axis=0)
        s_c = lax.dot_general(qbd.astype(BF16), kc, (((1,), (1,)), ((), ())),
                              preferred_element_type=F32) * (HEAD_DIM ** -0.5)
        col = lax.broadcasted_iota(jnp.int32, (rows, win), 1)
        valid_c = (col >= row_t) & (((col - row_t) & (d - 1)) == 0)
        s_c = jnp.where(valid_c, s_c, NEG_INF)
        s_n = []
        for t2 in range(n_new):
            sn = jnp.sum(qbd * kn[t2:t2 + 1, :], axis=-1, keepdims=True) * (HEAD_DIM ** -0.5)
            valid_n = (row_t >= t2) & (((row_t - t2) & (d - 1)) == 0)
            s_n.append(jnp.where(valid_n, sn, NEG_INF))
        m = jnp.max(s_c, axis=-1, keepdims=True)
        for sn in s_n:
            m = jnp.maximum(m, sn)
        p_c = jnp.exp(s_c - m)
        den = jnp.sum(p_c, axis=-1, keepdims=True)
        o = jnp.dot(p_c.astype(BF16), vc, preferred_element_type=F32)
        for t2 in range(n_new):
            p_n = jnp.exp(s_n[t2] - m)
            den = den + p_n
            o = o + p_n * vn[t2:t2 + 1, :]
        o = o / den
        lse = m + jnp.log(den)
        for t in range(n_new):
            rs = slice(t * N_HEADS, (t + 1) * N_HEADS)
            out_refs[g][0, t:t + 1, :] = jnp.sum(o[rs] * head_mask, axis=0, keepdims=True)
            out_refs[N_GROUPS + g][0, t:t + 1, :] = jnp.sum(lse[rs] * head_mask, axis=0, keepdims=True)


def _cache_attention(q, k, v, caches, layer):
    bd, t, w = q.shape
    xmap = lambda b: (b, 0, 0)
    cmap = lambda b: (layer, b, 0, 0, 0)
    oshape = jax.ShapeDtypeStruct((bd, t, ATT), F32)
    return pl.pallas_call(
        _cache_attn_kernel,
        grid=(bd,),
        in_specs=[pl.BlockSpec((1, t, w), xmap)] * 3
        + [pl.BlockSpec((1, 1, 2, WINDOWS[g], ATT), cmap) for g in range(N_GROUPS)],
        out_specs=[pl.BlockSpec((1, t, ATT), xmap)] * (2 * N_GROUPS),
        out_shape=[oshape] * (2 * N_GROUPS),
        compiler_params=_cparams(("parallel",)),
        name="cache_attention",
    )(q, k, v, *caches)


def _undilate(ref, scr, d):
    if d == 1:
        return ref[0]
    rows = ref.shape[1]
    for r in range(d):
        for kc in range(ATT // 128):
            c0 = r * ATT + kc * 128
            scr[kc, pl.ds(r, rows, stride=d), :] = ref[0, :, c0:c0 + 128]
    return jnp.concatenate([scr[kc] for kc in range(ATT // 128)], axis=1)


def _merge_kernel(o0, o1, o2, l0, l1, l2, x_ref, gate_ref, wo_ref, xo_ref, scr, *, dils):
    la, lb, lc = [_undilate(r, scr, d) for r, d in zip((l0, l1, l2), dils)]
    m = jnp.maximum(jnp.maximum(la, lb), lc)
    ea, eb, ec = jnp.exp(la - m), jnp.exp(lb - m), jnp.exp(lc - m)
    den = ea + eb + ec
    o = (ea / den) * _undilate(o0, scr, dils[0])
    o = o + (eb / den) * _undilate(o1, scr, dils[1])
    o = o + (ec / den) * _undilate(o2, scr, dils[2])
    out = jnp.dot(o.astype(BF16), wo_ref[...], preferred_element_type=F32)
    xo_ref[0] = x_ref[0] + gate_ref[0] * out


def _merge(outs, lses, x3, gate, w_o_bf, *, t_tile, dils):
    bq, sq, _ = x3.shape
    mrows = gate.shape[1]
    mt = t_tile if mrows == sq else 1
    mmap = (lambda b, i: (b, i, 0)) if mrows == sq else (lambda b, i: (b, 0, 0))
    tmap = lambda b, i: (b, i, 0)
    ablk = [pl.BlockSpec((1, t_tile // d, d * ATT), tmap) for d in dils]
    return pl.pallas_call(
        functools.partial(_merge_kernel, dils=dils),
        grid=(bq, sq // t_tile),
        in_specs=ablk + ablk + [pl.BlockSpec((1, t_tile, D_MODEL), tmap), pl.BlockSpec((1, mt, D_MODEL), mmap),
                                pl.BlockSpec((ATT, D_MODEL), lambda b, i: (0, 0))],
        out_specs=pl.BlockSpec((1, t_tile, D_MODEL), tmap),
        out_shape=jax.ShapeDtypeStruct(x3.shape, F32),
        scratch_shapes=[pltpu.VMEM((ATT // 128, t_tile, 128), F32)],
        compiler_params=_cparams(("parallel", "parallel")),
        name="attn_merge",
    )(*outs, *lses, x3, gate, w_o_bf)


def _top16(s):
    vals = s
    rank = jnp.full(s.shape, float(PEER_N_KEYS), F32)
    tops = []
    for r in range(PEER_TOPK):
        m = jnp.max(vals, axis=0, keepdims=True)
        hit = vals == m
        rank = jnp.where(hit, float(r), rank)
        vals = jnp.where(hit, -jnp.inf, vals)
        tops.append(m)
    return tops, rank


def _peer_route_kernel(x_ref, shift_ref, scale_ref, gain_ref, wq_ref, sk_ref, ones_ref,
                       xt_ref, r1_ref, c1_ref, nf_ref, s_scr):
    hm = _modulate(x_ref[...], gain_ref[...], shift_ref[0], scale_ref[0])
    hb = hm.astype(BF16)
    xt_ref[...] = hm.T.astype(BF16)
    q = jnp.dot(hb, wq_ref[...], preferred_element_type=F32)
    qn = q * lax.rsqrt(_seg_mean(q * q, ones_ref[...]) + EPS)
    s_scr[...] = lax.dot_general(sk_ref[...], qn.astype(BF16), (((1,), (1,)), ((), ())),
                                 preferred_element_type=F32)

    def head(h, carry):
        base = pl.multiple_of(h * 2 * PEER_N_KEYS, 2 * PEER_N_KEYS)
        s1 = s_scr[pl.ds(base, PEER_N_KEYS), :]
        s2 = s_scr[pl.ds(base + PEER_N_KEYS, PEER_N_KEYS), :]
        top1, rank1 = _top16(s1)
        top2, rank2 = _top16(s2)
        row = lax.broadcasted_iota(jnp.int32, (PEER_TOPK, s1.shape[1]), 0)
        v1 = jnp.zeros((PEER_TOPK, s1.shape[1]), F32)
        for r in range(PEER_TOPK):
            v1 = jnp.where(row == r, top1[r], v1)
        cands = [v1[:8] + top2[b] for b in range(PEER_TOPK)] + [v1[8:] + top2[0]]
        best = top1[0] + top2[0]
        taken = [jnp.zeros_like(c) for c in cands]
        zsum = jnp.zeros_like(best)
        for _ in range(PEER_TOPK):
            m = cands[0]
            for c in cands[1:]:
                m = jnp.maximum(m, c)
            m = jnp.max(m, axis=0, keepdims=True)
            for j in range(len(cands)):
                hit = cands[j] == m
                taken[j] = jnp.where(hit, 1.0, taken[j])
                cands[j] = jnp.where(hit, -jnp.inf, cands[j])
            zsum = zsum + jnp.exp(m - best)
        cnt_lo = taken[0]
        for j in range(1, PEER_TOPK):
            cnt_lo = cnt_lo + taken[j]
        cnt = jnp.concatenate([cnt_lo, taken[PEER_TOPK]], axis=0)
        n2 = jnp.zeros_like(rank2)
        for a in range(PEER_TOPK):
            n2 = n2 + jnp.where(rank2 < cnt[a:a + 1], 1.0, 0.0)
        r1_ref[0, h] = rank1
        c1_ref[0, h] = jnp.exp(s1 - top1[0])
        f2 = jnp.exp(s2 - top2[0]) / zsum
        nf_ref[0, h, :, 0] = pltpu.bitcast(n2.astype(BF16), jnp.int32).reshape(PEER_N_KEYS // 16, 8, PEER_SEL_TOK)
        nf_ref[0, h, :, 1] = pltpu.bitcast(f2.astype(BF16), jnp.int32).reshape(PEER_N_KEYS // 16, 8, PEER_SEL_TOK)
        return carry

    lax.fori_loop(0, PEER_HEADS, head, 0)


def _peer_route(x2, shift, scale, gain, wq_bf, sk_bd, ones_bd):
    n = x2.shape[0]
    tt = PEER_SEL_TOK
    nt = n // tt
    per_mod = nt // shift.shape[0]
    mrows = shift.shape[1]
    if mrows == 1:
        mblk, mmap = (1, 1, D_MODEL), (lambda i: (i // per_mod, 0, 0))
    else:
        mblk, mmap = (1, tt, D_MODEL), (lambda i: (0, i, 0))
    cmap = lambda i: (0, 0)
    nrt = PEER_N_KEYS // 16
    hshape = jax.ShapeDtypeStruct((nt, PEER_HEADS, PEER_N_KEYS, tt), F32)
    hblk = pl.BlockSpec((1, PEER_HEADS, PEER_N_KEYS, tt), lambda i: (i, 0, 0, 0))
    nfshape = jax.ShapeDtypeStruct((nt, PEER_HEADS, nrt, 2, 8, tt), jnp.int32)
    nfblk = pl.BlockSpec((1, PEER_HEADS, nrt, 2, 8, tt), lambda i: (i, 0, 0, 0, 0, 0))
    return pl.pallas_call(
        _peer_route_kernel,
        grid=(nt,),
        in_specs=[
            pl.BlockSpec((tt, D_MODEL), lambda i: (i, 0)),
            pl.BlockSpec(mblk, mmap), pl.BlockSpec(mblk, mmap),
            pl.BlockSpec((1, D_MODEL), cmap),
            pl.BlockSpec((D_MODEL, D_MODEL), cmap),
            pl.BlockSpec((2 * PEER_HEADS * PEER_N_KEYS, D_MODEL), cmap),
            pl.BlockSpec((D_MODEL, D_MODEL), cmap),
        ],
        out_specs=[pl.BlockSpec((D_MODEL, tt), lambda i: (0, i)), hblk, hblk, nfblk],
        out_shape=[jax.ShapeDtypeStruct((D_MODEL, n), BF16), hshape, hshape, nfshape],
        scratch_shapes=[pltpu.VMEM((2 * PEER_HEADS * PEER_N_KEYS, tt), F32)],
        compiler_params=_cparams(("parallel",)),
        name="peer_route",
    )(x2, shift, scale, gain.reshape(1, D_MODEL), wq_bf, sk_bd, ones_bd)


def _gelu(x):
    return 0.5 * x * (1.0 + lax.erf(x * np.float32(math.sqrt(0.5))))


def _peer_expert_kernel(xt_ref, u_ref, un_ref, vt_ref, vp_ref, r1_ref, c1_ref, nf_ref, x_ref, gate_ref, xo_ref,
                        act_scr, w_scr, acc_scr, *, t_tile):
    e = pl.program_id(1)
    nq = PEER_EXP_TILE // PEER_CHUNK
    last = slice((nq - 1) * PEER_CHUNK, nq * PEER_CHUNK)

    @pl.when(e == 0)
    def _():
        acc_scr[...] = jnp.zeros_like(acc_scr)
        w_scr[nq - 1] = jnp.zeros((PEER_CHUNK, t_tile), BF16)
        act_scr[0] = jnp.dot(u_ref[0, :PEER_CHUNK, :], xt_ref[...], preferred_element_type=F32)

    def stage_a(q):
        qs = slice(q * PEER_CHUNK, (q + 1) * PEER_CHUNK)
        act_scr[q] = jnp.dot(u_ref[0, qs, :], xt_ref[...], preferred_element_type=F32)

    def stage_a_next():
        act_scr[0] = jnp.dot(un_ref[0], xt_ref[...], preferred_element_type=F32)

    def stage_b(q):
        for i1l in range(PEER_CHUNK // PEER_N_KEYS):
            i1 = q * (PEER_CHUNK // PEER_N_KEYS) + i1l
            for lc in range(t_tile // PEER_SEL_TOK):
                ls = slice(lc * PEER_SEL_TOK, (lc + 1) * PEER_SEL_TOK)
                r1 = [jnp.broadcast_to(r1_ref[lc, h, i1:i1 + 1, :], (16, PEER_SEL_TOK)).astype(BF16)
                      for h in range(PEER_HEADS)]
                c1 = [jnp.broadcast_to(c1_ref[lc, h, i1:i1 + 1, :], (16, PEER_SEL_TOK)).astype(BF16)
                      for h in range(PEER_HEADS)]
                for j in range(PEER_N_KEYS // 16):
                    g = jnp.zeros((16, PEER_SEL_TOK), BF16)
                    for h in range(PEER_HEADS):
                        n2 = pltpu.bitcast(nf_ref[lc, h, j, 0], BF16)
                        f2 = pltpu.bitcast(nf_ref[lc, h, j, 1], BF16)
                        g = g + jnp.where(n2 > r1[h], f2, 0.0) * c1[h]
                    rs = slice(i1l * PEER_N_KEYS + j * 16, i1l * PEER_N_KEYS + (j + 1) * 16)
                    w_scr[q, rs, ls] = g * _gelu(act_scr[q, rs, ls]).astype(BF16)

    def stage_c(q):
        qs = slice(q * PEER_CHUNK, (q + 1) * PEER_CHUNK)
        acc_scr[...] += jnp.dot(vt_ref[0, :, qs], w_scr[q], preferred_element_type=F32)

    def stage_c_prev():
        acc_scr[...] += jnp.dot(vp_ref[0], w_scr[nq - 1], preferred_element_type=F32)

    stage_c_prev()
    for q in range(nq):
        if q + 1 < nq:
            stage_a(q + 1)
        else:
            stage_a_next()
        stage_b(q)
        if q > 0:
            stage_c(q - 1)

    @pl.when(e == pl.num_programs(1) - 1)
    def _():
        acc = acc_scr[...] + jnp.dot(vt_ref[0, :, last], w_scr[nq - 1], preferred_element_type=F32)
        xo_ref[...] = x_ref[...] + gate_ref[0] * acc.T


def _peer_expert(xt, u_bf, vt_bf, layer, r1, c1, nf, x2, gate, *, t_tile):
    n = x2.shape[0]
    nt = n // t_tile
    nlc = t_tile // PEER_SEL_TOK
    ne = u_bf.shape[1] // PEER_EXP_TILE
    nq = PEER_EXP_TILE // PEER_CHUNK
    per_mod = max(nt // gate.shape[0], 1)
    mrows = gate.shape[1]
    if mrows == 1:
        mblk, mmap = (1, 1, D_MODEL), (lambda i, e: (i // per_mod, 0, 0))
    else:
        mblk, mmap = (1, t_tile, D_MODEL), (lambda i, e: (0, i, 0))
    i1blk = pl.BlockSpec((nlc, PEER_HEADS, PEER_I1_TILE, PEER_SEL_TOK), lambda i, e: (i, 0, e, 0))
    nfblk = pl.BlockSpec((nlc, PEER_HEADS, PEER_N_KEYS // 16, 2, 8, PEER_SEL_TOK), lambda i, e: (i, 0, 0, 0, 0, 0))
    kern = functools.partial(_peer_expert_kernel, t_tile=t_tile)
    return pl.pallas_call(
        kern,
        grid=(nt, ne),
        in_specs=[
            pl.BlockSpec((D_MODEL, t_tile), lambda i, e: (0, i)),
            pl.BlockSpec((1, PEER_EXP_TILE, D_MODEL), lambda i, e: (layer, e, 0)),
            pl.BlockSpec((1, PEER_CHUNK, D_MODEL), lambda i, e: (layer, jnp.minimum(e + 1, ne - 1) * nq, 0)),
            pl.BlockSpec((1, D_MODEL, PEER_EXP_TILE), lambda i, e: (layer, 0, e)),
            pl.BlockSpec((1, D_MODEL, PEER_CHUNK), lambda i, e: (layer, 0, jnp.maximum(e, 1) * nq - 1)),
            i1blk, i1blk, nfblk,
            pl.BlockSpec((t_tile, D_MODEL), lambda i, e: (i, 0)),
            pl.BlockSpec(mblk, mmap),
        ],
        out_specs=pl.BlockSpec((t_tile, D_MODEL), lambda i, e: (i, 0)),
        out_shape=jax.ShapeDtypeStruct(x2.shape, F32),
        scratch_shapes=[pltpu.VMEM((PEER_EXP_TILE // PEER_CHUNK, PEER_CHUNK, t_tile), F32),
                        pltpu.VMEM((PEER_EXP_TILE // PEER_CHUNK, PEER_CHUNK, t_tile), BF16),
                        pltpu.VMEM((D_MODEL, t_tile), F32)],
        compiler_params=_cparams(("parallel", "arbitrary")),
        name="peer_expert",
    )(xt, u_bf, u_bf, vt_bf, vt_bf, r1, c1, nf, x2, gate)


def _peer(x2, shift, scale, gate, gain, pw, tables, layer, *, t_tile):
    wq_bf, sk_bd, ones_bd = pw
    u_bf, vt_bf = tables
    xt, r1, c1, nf = _peer_route(x2, shift, scale, gain, wq_bf, sk_bd, ones_bd)
    return _peer_expert(xt, u_bf, vt_bf, layer, r1, c1, nf, x2, gate, t_tile=t_tile)


def _peer_weights(w_q, sub_keys, ones_bd):
    hc = 2 * PEER_HEADS
    sk = sub_keys.reshape(hc, PEER_N_KEYS, PEER_HALF)
    sk_bd = jnp.einsum('akd,ab->akbd', sk, jnp.eye(hc, dtype=F32)).reshape(hc * PEER_N_KEYS, hc * PEER_HALF)
    return (w_q.astype(BF16), sk_bd.astype(BF16), ones_bd)


def kernel(x_prompt, x_sample, c_prompt, c_sample, state_ssm, cache_kv_w128, cache_kv_w512, cache_kv_w2048,
           norm_mix, norm_ffn, ada_w, ada_b, ssm_lam_re, ssm_lam_im, ssm_log_dt, ssm_b_re, ssm_b_im, ssm_c_re,
           ssm_c_im, ssm_d, ssm_w_glu, attn_w_qkv, attn_q_norm, attn_k_norm, attn_w_o, peer_w_q,
           peer_sub_keys, peer_u, peer_v):
    bp, sp, _ = x_prompt.shape
    bs, ts, _ = x_sample.shape
    ns = bs * ts
    ones_att = _ones_blockdiag(ATT)
    ones_d = _ones_blockdiag(D_MODEL)

    mods = _adaln(jnp.concatenate([c_prompt, c_sample], axis=0), ada_w, ada_b)
    caches = (cache_kv_w128, cache_kv_w512, cache_kv_w2048)
    tab_p = _rope_tables(jnp.arange(sp, dtype=jnp.int32))
    pos_s = PAST_LEN + jnp.arange(ts, dtype=jnp.int32)
    tab_s = tuple(jnp.tile(t, (bs, 1)) for t in _rope_tables(pos_s))

    peer_tabs = (peer_u.astype(BF16), peer_v.astype(BF16).transpose(0, 2, 1))
    xp, xs = x_prompt, x_sample
    ssm_p, ssm_s = [], []
    kv_p = [[] for _ in range(N_GROUPS)]
    kv_s = [[] for _ in range(N_GROUPS)]
    for i in range(DEPTH):
        j = i // 2
        mp = [m.reshape(bp, 1, D_MODEL) for m in jnp.split(mods[i, :bp], 6, axis=-1)]
        ms_b = jnp.split(mods[i, bp:], 6, axis=-1)
        ms_rows = [jnp.repeat(m, ts, axis=0).reshape(1, ns, D_MODEL) for m in ms_b]
        if i % 2 == 0:
            sw = _s5_weights(ssm_lam_re[j], ssm_lam_im[j], ssm_log_dt[j], ssm_b_re[j], ssm_b_im[j],
                             ssm_c_re[j], ssm_c_im[j])
            wglu = ssm_w_glu[j].astype(BF16)
            mp_t = [m.reshape(1, bp, D_MODEL) for m in mp[:3]]
            xp_t, st_p = _s5_layer(xp.transpose(1, 0, 2), mp_t[0], mp_t[1], mp_t[2], norm_mix[i],
                                   jnp.zeros((bp, SSM_STATE_W), F32), sw, ssm_d[j], wglu, t_chunk=32)
            xp = xp_t.transpose(1, 0, 2)
            ms_t = [m.reshape(1, bs, D_MODEL) for m in ms_b[:3]]
            xs_t, st_s = _s5_layer(xs.transpose(1, 0, 2), ms_t[0], ms_t[1], ms_t[2], norm_mix[i],
                                   _state_to_rows(state_ssm[j]), sw, ssm_d[j], wglu, t_chunk=ts)
            xs = xs_t.transpose(1, 0, 2)
            ssm_p.append(_rows_to_state(st_p))
            ssm_s.append(_rows_to_state(st_s))
        else:
            wqkv = attn_w_qkv[j].astype(BF16)
            wo = attn_w_o[j].astype(BF16)
            k, v, q0, q1, k1, v1, q2, k2, v2 = _qkv_dilated(xp, mp[0], mp[1], norm_mix[i], wqkv, attn_q_norm[j],
                                                            attn_k_norm[j], tab_p, ones_att, t_tile=512)
            res = [_band_attention(q0, k, v, 1, 1, 0, N_GROUPS, 0),
                   _band_attention(q1, k1, v1, DILATIONS[1], 1, 0, 1, 0),
                   _band_attention(q2, k2, v2, DILATIONS[2], 1, 0, 1, 0)]
            outs, lses = zip(*res)
            xp = _merge(outs, lses, xp, mp[2], wo, t_tile=512, dils=DILATIONS)
            for g in range(N_GROUPS):
                keep = min(WINDOWS[g], sp)
                gs = slice(g * ATT, (g + 1) * ATT)
                kv_p[g].append(jnp.stack([k[:, sp - keep:, gs], v[:, sp - keep:, gs]], axis=1)
                               .reshape(bp, 2, keep, N_HEADS, HEAD_DIM))
            xs_rows = xs.reshape(1, ns, D_MODEL)
            qs, ks, vs = _qkv(xs_rows, ms_rows[0], ms_rows[1], norm_mix[i], wqkv, attn_q_norm[j],
                              attn_k_norm[j], tab_s, ones_att, t_tile=ns)
            qs, ks, vs = (t.reshape(bs, ts, N_GROUPS * ATT) for t in (qs, ks, vs))
            res = _cache_attention(qs, ks, vs, [c.reshape(c.shape[:4] + (ATT,)) for c in caches], j)
            outs_s = [r.reshape(1, ns, ATT) for r in res[:N_GROUPS]]
            lses_s = [r.reshape(1, ns, ATT) for r in res[N_GROUPS:]]
            xs = _merge(outs_s, lses_s, xs_rows, ms_rows[2], wo, t_tile=ns,
                        dils=(1, 1, 1)).reshape(bs, ts, D_MODEL)
            for g in range(N_GROUPS):
                gs = slice(g * ATT, (g + 1) * ATT)
                kv_s[g].append(jnp.stack([ks[:, :, gs], vs[:, :, gs]], axis=1)
                               .reshape(bs, 2, ts, N_HEADS, HEAD_DIM))
        pw = _peer_weights(peer_w_q[i], peer_sub_keys[i], ones_d)
        xp = _peer(xp.reshape(bp * sp, D_MODEL), mp[3], mp[4], mp[5], norm_ffn[i], pw, peer_tabs, i,
                   t_tile=1024).reshape(bp, sp, D_MODEL)
        xs = _peer(xs.reshape(ns, D_MODEL), ms_rows[3], ms_rows[4], ms_rows[5], norm_ffn[i], pw, peer_tabs, i,
                   t_tile=ns).reshape(bs, ts, D_MODEL)
    return (xp, xs,
            jnp.stack(ssm_p), jnp.stack(kv_p[0]), jnp.stack(kv_p[1]), jnp.stack(kv_p[2]),
            jnp.stack(ssm_s), jnp.stack(kv_s[0]), jnp.stack(kv_s[1]), jnp.stack(kv_s[2]))
```

```python
import functools
import math

import jax
import jax.numpy as jnp
import numpy as np
from jax import lax
from jax.experimental import pallas as pl
from jax.experimental.pallas import tpu as pltpu

F32 = jnp.float32
BF16 = jnp.bfloat16

D_MODEL = 1024
DEPTH = 4
PAST_LEN = 8192
EPS = 1e-6
NEG_INF = -1e30

SSM_P = 16
SSM_GROUPS = D_MODEL // SSM_P
SSM_N = 64
SSM_GB = 4
SSM_GB_CH = D_MODEL // SSM_GB
SSM_GB_ST = SSM_GROUPS * SSM_N // SSM_GB
SSM_STATE_W = 2 * SSM_GROUPS * SSM_N

N_HEADS = 8
HEAD_DIM = 64
ATT = N_HEADS * HEAD_DIM
WINDOWS = (128, 512, 2048)
DILATIONS = (1, 4, 16)
N_GROUPS = 3
N_BACK = 128
ROPE_DIM = HEAD_DIM // 4
ROPE_THETA = 500000.0

PEER_HEADS = 8
PEER_N_KEYS = 128
PEER_HALF = 64
PEER_TOPK = 16
PEER_SEL_TOK = 128
PEER_EXP_TILE = 1024
PEER_I1_TILE = PEER_EXP_TILE // PEER_N_KEYS
PEER_CHUNK = 256

VMEM_LIMIT = 56 * 1024 * 1024


def _cparams(sem):
    return pltpu.CompilerParams(dimension_semantics=sem, vmem_limit_bytes=VMEM_LIMIT)


def _modulate(x, gain, shift, scale):
    ms = jnp.mean(x * x, axis=-1, keepdims=True)
    return x * lax.rsqrt(ms + EPS) * gain * (1.0 + scale) + shift


def _seg_mean(x2, ones_bd):
    hi = x2.astype(BF16)
    lo = (x2 - hi.astype(F32)).astype(BF16)
    s = jnp.dot(hi, ones_bd, preferred_element_type=F32) + jnp.dot(lo, ones_bd, preferred_element_type=F32)
    return s * (1.0 / PEER_HALF)


def _ones_blockdiag(n):
    seg = np.arange(n) // 64
    return jnp.asarray((seg[:, None] == seg[None, :]).astype(np.float32), dtype=BF16)


def _adaln_kernel(c_ref, w_ref, b_ref, o_ref):
    s = jax.nn.silu(c_ref[...])
    o_ref[0] = jnp.dot(s.astype(BF16), w_ref[0].astype(BF16), preferred_element_type=F32) + b_ref[0]


def _adaln(c_all, ada_w, ada_b):
    nb = c_all.shape[0]
    nt = 6 * D_MODEL // 1024
    return pl.pallas_call(
        _adaln_kernel,
        grid=(DEPTH, nt),
        in_specs=[
            pl.BlockSpec((nb, D_MODEL), lambda l, j: (0, 0)),
            pl.BlockSpec((1, D_MODEL, 1024), lambda l, j: (l, 0, j)),
            pl.BlockSpec((1, 1, 1024), lambda l, j: (l, 0, j)),
        ],
        out_specs=pl.BlockSpec((1, nb, 1024), lambda l, j: (l, 0, j)),
        out_shape=jax.ShapeDtypeStruct((DEPTH, nb, 6 * D_MODEL), F32),
        compiler_params=_cparams(("parallel", "parallel")),
        name="adaln",
    )(c_all, ada_w, ada_b.reshape(DEPTH, 1, 6 * D_MODEL))


def _s5_kernel(x_ref, shift_ref, scale_ref, gate_ref, gain_ref, h0_ref, bm_ref, cm_ref, are_ref, aim_ref,
               d_ref, wglu_ref, xo_ref, st_ref, bu_scr, state_scr, *, n_batch, n_time):
    i = pl.program_id(0)

    @pl.when(i == 0)
    def _():
        state_scr[...] = h0_ref[...]

    x3 = x_ref[...]
    a0, a1, _ = x3.shape
    rows = a0 * a1
    h3 = _modulate(x3, gain_ref[...], shift_ref[...], scale_ref[...])
    u = h3.reshape(rows, D_MODEL)
    ub = u.astype(BF16)
    for gb in range(SSM_GB):
        bu_scr[:, gb * 2 * SSM_GB_ST:(gb + 1) * 2 * SSM_GB_ST] = jnp.dot(
            ub[:, gb * SSM_GB_CH:(gb + 1) * SSM_GB_CH], bm_ref[gb], preferred_element_type=F32)

    for bb in range(n_batch // 8):
        for gb in range(SSM_GB):
            c_re = gb * 2 * SSM_GB_ST
            c_im = c_re + SSM_GB_ST
            ar = jnp.broadcast_to(are_ref[gb], (8, SSM_GB_ST))
            ai = jnp.broadcast_to(aim_ref[gb], (8, SSM_GB_ST))
            xr = state_scr[bb * 8:(bb + 1) * 8, c_re:c_re + SSM_GB_ST]
            xi = state_scr[bb * 8:(bb + 1) * 8, c_im:c_im + SSM_GB_ST]
            for t in range(n_time):
                rsel = pl.ds(t * n_batch + bb * 8, 8)
                br = bu_scr[rsel, c_re:c_re + SSM_GB_ST]
                bi = bu_scr[rsel, c_im:c_im + SSM_GB_ST]
                nr = ar * xr - ai * xi + br
                ni = ar * xi + ai * xr + bi
                bu_scr[rsel, c_re:c_re + SSM_GB_ST] = nr
                bu_scr[rsel, c_im:c_im + SSM_GB_ST] = ni
                xr, xi = nr, ni
            state_scr[bb * 8:(bb + 1) * 8, c_re:c_re + SSM_GB_ST] = xr
            state_scr[bb * 8:(bb + 1) * 8, c_im:c_im + SSM_GB_ST] = xi

    ys = []
    for gb in range(SSM_GB):
        st = bu_scr[:, gb * 2 * SSM_GB_ST:(gb + 1) * 2 * SSM_GB_ST].astype(BF16)
        ys.append(jnp.dot(st, cm_ref[gb], preferred_element_type=F32))
    y = jnp.concatenate(ys, axis=-1) + d_ref[...] * u
    z = jnp.dot(y.astype(BF16), wglu_ref[...], preferred_element_type=F32)
    out = z[:, :D_MODEL] * jax.nn.sigmoid(z[:, D_MODEL:])
    xo_ref[...] = x3 + gate_ref[...] * out.reshape(a0, a1, D_MODEL)

    @pl.when(i == pl.num_programs(0) - 1)
    def _():
        st_ref[...] = state_scr[...]


def _s5_weights(lam_re, lam_im, log_dt, b_re, b_im, c_re, c_im):
    dt = jnp.exp(log_dt)[:, None]
    mag = jnp.exp(lam_re * dt)
    a_re = mag * jnp.cos(lam_im * dt)
    a_im = mag * jnp.sin(lam_im * dt)
    den = lam_re * lam_re + lam_im * lam_im
    f_re = ((a_re - 1.0) * lam_re + a_im * lam_im) / den
    f_im = (a_im * lam_re - (a_re - 1.0) * lam_im) / den
    bb_re = f_re[..., None] * b_re - f_im[..., None] * b_im
    bb_im = f_re[..., None] * b_im + f_im[..., None] * b_re
    eye = jnp.eye(16, dtype=F32)

    def in_blk(b):
        b4 = b.reshape(SSM_GB, 16, SSM_N, SSM_P)
        return jnp.einsum('bgnp,gh->bgphn', b4, eye).reshape(SSM_GB, SSM_GB_CH, SSM_GB_ST)

    def out_blk(c):
        c4 = c.reshape(SSM_GB, 16, SSM_P, SSM_N)
        return jnp.einsum('bgpn,gh->bgnhp', c4, eye).reshape(SSM_GB, SSM_GB_ST, SSM_GB_CH)

    bm = jnp.concatenate([in_blk(bb_re), in_blk(bb_im)], axis=2).astype(BF16)
    cm = jnp.concatenate([out_blk(c_re), -out_blk(c_im)], axis=1).astype(BF16)
    return bm, cm, a_re.reshape(SSM_GB, 1, SSM_GB_ST), a_im.reshape(SSM_GB, 1, SSM_GB_ST)


def _state_to_rows(st):
    b = st.shape[0]
    return st.reshape(b, 2, SSM_GB, SSM_GB_ST).transpose(0, 2, 1, 3).reshape(b, SSM_STATE_W)


def _rows_to_state(rows):
    b = rows.shape[0]
    return rows.reshape(b, SSM_GB, 2, SSM_GB_ST).transpose(0, 2, 1, 3).reshape(b, 2, SSM_GROUPS, SSM_N)


def _s5_layer(x3, shift, scale, gate, gain, h0_rows, weights, d_skip, w_glu, *, t_chunk):
    bm, cm, a_re, a_im = weights
    seq, n_batch = x3.shape[0], x3.shape[1]
    n_time = t_chunk
    blk = (n_time, n_batch, D_MODEL)
    xmap = lambda i: (i, 0, 0)
    mblk = (1, n_batch, D_MODEL)
    grid = seq // n_time
    rows = n_batch * n_time
    const3 = lambda i: (0, 0, 0)
    const2 = lambda i: (0, 0)
    kern = functools.partial(_s5_kernel, n_batch=n_batch, n_time=n_time)
    return pl.pallas_call(
        kern,
        grid=(grid,),
        in_specs=[
            pl.BlockSpec(blk, xmap),
            pl.BlockSpec(mblk, const3), pl.BlockSpec(mblk, const3), pl.BlockSpec(mblk, const3),
            pl.BlockSpec((1, 1, D_MODEL), const3),
            pl.BlockSpec((n_batch, SSM_STATE_W), const2),
            pl.BlockSpec(bm.shape, const3), pl.BlockSpec(cm.shape, const3),
            pl.BlockSpec(a_re.shape, const3), pl.BlockSpec(a_im.shape, const3),
            pl.BlockSpec((1, D_MODEL), const2),
            pl.BlockSpec((D_MODEL, 2 * D_MODEL), const2),
        ],
        out_specs=[pl.BlockSpec(blk, xmap), pl.BlockSpec((n_batch, SSM_STATE_W), const2)],
        out_shape=[jax.ShapeDtypeStruct(x3.shape, F32), jax.ShapeDtypeStruct((n_batch, SSM_STATE_W), F32)],
        scratch_shapes=[pltpu.VMEM((rows, SSM_STATE_W), F32), pltpu.VMEM((n_batch, SSM_STATE_W), F32)],
        compiler_params=_cparams(("arbitrary",)),
        name="s5_layer",
    )(x3, shift, scale, gate, gain.reshape(1, 1, D_MODEL), h0_rows, bm, cm, a_re, a_im,
      d_skip.reshape(1, D_MODEL), w_glu)


def _qkv_kernel(x_ref, shift_ref, scale_ref, gain_ref, w_ref, qg_ref, kg_ref, rc_ref, rs1_ref, rs2_ref,
                ones_ref, q_ref, k_ref, v_ref):
    h = _modulate(x_ref[0], gain_ref[...], shift_ref[0], scale_ref[0])
    qkv = jnp.dot(h.astype(BF16), w_ref[...], preferred_element_type=F32)
    rc, rs1, rs2 = rc_ref[...], rs1_ref[...], rs2_ref[...]
    ones_bd = ones_ref[...]

    def norm_rope(t, g):
        tn = t * lax.rsqrt(_seg_mean(t * t, ones_bd) + EPS) * g
        return tn * rc + pltpu.roll(tn, ATT - ROPE_DIM // 2, 1) * rs1 + pltpu.roll(tn, ROPE_DIM // 2, 1) * rs2

    q_ref[0] = norm_rope(qkv[:, :ATT], qg_ref[...])
    k_ref[0] = norm_rope(qkv[:, ATT:2 * ATT], kg_ref[...])
    v_ref[0] = qkv[:, 2 * ATT:]


def _dilate(x, scr, out_ref, d):
    t = x.shape[0]
    for kc in range(ATT // 128):
        scr[kc] = x[:, kc * 128:(kc + 1) * 128]
    for r in range(d):
        for kc in range(ATT // 128):
            c0 = r * ATT + kc * 128
            out_ref[0, :, c0:c0 + 128] = scr[kc, pl.ds(r, t // d, stride=d), :]


def _qkv_dil_kernel(x_ref, shift_ref, scale_ref, gain_ref, w_ref, qg_ref, kg_ref, rc_ref, rs1_ref, rs2_ref,
                    ones_ref, kn_ref, vn_ref, q0_ref, q1_ref, k1_ref, v1_ref, q2_ref, k2_ref, v2_ref, scr):
    g = pl.program_id(2)
    h = _modulate(x_ref[0], gain_ref[...], shift_ref[0], scale_ref[0])
    qkv = jnp.dot(h.astype(BF16), w_ref[...], preferred_element_type=F32)
    rc, rs1, rs2 = rc_ref[...], rs1_ref[...], rs2_ref[...]
    ones_bd = ones_ref[...]

    def norm_rope(t, gn):
        tn = t * lax.rsqrt(_seg_mean(t * t, ones_bd) + EPS) * gn
        return tn * rc + pltpu.roll(tn, ATT - ROPE_DIM // 2, 1) * rs1 + pltpu.roll(tn, ROPE_DIM // 2, 1) * rs2

    q = norm_rope(qkv[:, :ATT], qg_ref[...])
    k = norm_rope(qkv[:, ATT:2 * ATT], kg_ref[...])
    v = qkv[:, 2 * ATT:]
    kn_ref[0] = k
    vn_ref[0] = v

    @pl.when(g == 0)
    def _():
        q0_ref[0] = q

    for gi, refs in ((1, (q1_ref, k1_ref, v1_ref)), (2, (q2_ref, k2_ref, v2_ref))):
        @pl.when(g == gi)
        def _():
            for val, ref in zip((q, k, v), refs):
                _dilate(val, scr, ref, DILATIONS[gi])


def _qkv_dilated(x3, shift, scale, gain, w_qkv_bf, q_gain, k_gain, tables, ones_bd, *, t_tile):
    bq, sq, _ = x3.shape
    rc, rs1, rs2 = tables
    qg = jnp.tile(q_gain, N_HEADS).reshape(1, ATT)
    kg = jnp.tile(k_gain, N_HEADS).reshape(1, ATT)
    mmap = lambda b, i, g: (b, 0, 0)
    tmap = lambda b, i, g: (i, 0)
    cmap = lambda b, i, g: (0, 0)
    gmap = lambda b, i, g: (b, i, g)
    bmap = lambda b, i, g: (b, i, 0)
    nat = jax.ShapeDtypeStruct((bq, sq, N_GROUPS * ATT), F32)
    out_shape = [nat, nat, jax.ShapeDtypeStruct((bq, sq, ATT), F32)]
    out_specs = [pl.BlockSpec((1, t_tile, ATT), gmap)] * 2 + [pl.BlockSpec((1, t_tile, ATT), bmap)]
    for gi in (1, 2):
        d = DILATIONS[gi]
        out_shape += [jax.ShapeDtypeStruct((bq, sq // d, d * ATT), F32)] * 3
        out_specs += [pl.BlockSpec((1, t_tile // d, d * ATT), bmap)] * 3
    return pl.pallas_call(
        _qkv_dil_kernel,
        grid=(bq, sq // t_tile, N_GROUPS),
        in_specs=[
            pl.BlockSpec((1, t_tile, D_MODEL), bmap),
            pl.BlockSpec((1, 1, D_MODEL), mmap), pl.BlockSpec((1, 1, D_MODEL), mmap),
            pl.BlockSpec((1, D_MODEL), cmap),
            pl.BlockSpec((D_MODEL, 3 * ATT), lambda b, i, g: (0, g)),
            pl.BlockSpec((1, ATT), cmap), pl.BlockSpec((1, ATT), cmap),
            pl.BlockSpec((t_tile, ATT), tmap), pl.BlockSpec((t_tile, ATT), tmap), pl.BlockSpec((t_tile, ATT), tmap),
            pl.BlockSpec((ATT, ATT), cmap),
        ],
        out_specs=out_specs,
        out_shape=out_shape,
        scratch_shapes=[pltpu.VMEM((ATT // 128, t_tile, 128), F32)],
        compiler_params=_cparams(("parallel", "parallel", "arbitrary")),
        name="qkv_proj_dilated",
    )(x3, shift, scale, gain.reshape(1, D_MODEL), w_qkv_bf, qg, kg, rc, rs1, rs2, ones_bd)


def _rope_tables(pos):
    half = ROPE_DIM // 2
    inv = ROPE_THETA ** (-jnp.arange(half, dtype=F32) / half)
    ang = pos.astype(F32)[:, None] * inv[None, :]
    cos, sin = jnp.cos(ang), jnp.sin(ang)
    lane = np.arange(ATT) % HEAD_DIM
    fidx = lane % half
    first = jnp.asarray(lane < half)
    second = jnp.asarray((lane >= half) & (lane < ROPE_DIM))
    cl, sl = cos[:, fidx], sin[:, fidx]
    rc = jnp.where(first | second, cl, 1.0)
    rs1 = jnp.where(first, -sl, 0.0)
    rs2 = jnp.where(second, sl, 0.0)
    return rc, rs1, rs2


def _qkv(x3, shift, scale, gain, w_qkv_bf, q_gain, k_gain, tables, ones_bd, *, t_tile):
    bq, sq, _ = x3.shape
    mrows = shift.shape[1]
    mt = t_tile if mrows == sq else 1
    mmap = (lambda b, i, g: (b, i, 0)) if mrows == sq else (lambda b, i, g: (b, 0, 0))
    rc, rs1, rs2 = tables
    qg = jnp.tile(q_gain, N_HEADS).reshape(1, ATT)
    kg = jnp.tile(k_gain, N_HEADS).reshape(1, ATT)
    tmap = lambda b, i, g: (i, 0)
    cmap = lambda b, i, g: (0, 0)
    omap = lambda b, i, g: (b, i, g)
    oshape = jax.ShapeDtypeStruct((bq, sq, N_GROUPS * ATT), F32)
    return pl.pallas_call(
        _qkv_kernel,
        grid=(bq, sq // t_tile, N_GROUPS),
        in_specs=[
            pl.BlockSpec((1, t_tile, D_MODEL), lambda b, i, g: (b, i, 0)),
            pl.BlockSpec((1, mt, D_MODEL), mmap), pl.BlockSpec((1, mt, D_MODEL), mmap),
            pl.BlockSpec((1, D_MODEL), cmap),
            pl.BlockSpec((D_MODEL, 3 * ATT), lambda b, i, g: (0, g)),
            pl.BlockSpec((1, ATT), cmap), pl.BlockSpec((1, ATT), cmap),
            pl.BlockSpec((t_tile, ATT), tmap), pl.BlockSpec((t_tile, ATT), tmap), pl.BlockSpec((t_tile, ATT), tmap),
            pl.BlockSpec((ATT, ATT), cmap),
        ],
        out_specs=[pl.BlockSpec((1, t_tile, ATT), omap)] * 3,
        out_shape=[oshape] * 3,
        compiler_params=_cparams(("parallel", "parallel", "arbitrary")),
        name="qkv_proj",
    )(x3, shift, scale, gain.reshape(1, D_MODEL), w_qkv_bf, qg, kg, rc, rs1, rs2, ones_bd)


def _band_kernel(q_ref, kp_ref, kc_ref, vp_ref, vc_ref, o_ref, l_ref):
    i = pl.program_id(2)
    q = q_ref[0].astype(BF16)
    k = jnp.concatenate([kp_ref[0], kc_ref[0]], axis=0).astype(BF16)
    v = jnp.concatenate([vp_ref[0], vc_ref[0]], axis=0).astype(BF16)
    iq = lax.broadcasted_iota(jnp.int32, (N_BACK, 2 * N_BACK), 0)
    ik = lax.broadcasted_iota(jnp.int32, (N_BACK, 2 * N_BACK), 1)
    dist = iq + N_BACK - ik
    mask = (dist >= 0) & (dist <= N_BACK) & ((ik >= N_BACK) | (i > 0))
    for h in range(N_HEADS):
        sl = slice(h * HEAD_DIM, (h + 1) * HEAD_DIM)
        s = lax.dot_general(q[:, sl], k[:, sl], (((1,), (1,)), ((), ())),
                            preferred_element_type=F32) * (HEAD_DIM ** -0.5)
        s = jnp.where(mask, s, NEG_INF)
        m = jnp.max(s, axis=-1, keepdims=True)
        p = jnp.exp(s - m)
        den = jnp.sum(p, axis=-1, keepdims=True)
        o = jnp.dot(p.astype(BF16), v[:, sl], preferred_element_type=F32) / den
        o_ref[0, :, sl] = o
        l_ref[0, :, sl] = jnp.broadcast_to(m + jnp.log(den), (N_BACK, HEAD_DIM))


def _band_attention(qv, kv, vv, d, qcols, qcol0, kcols, kcol0):
    b, sub, _ = qv.shape
    nb = sub // N_BACK
    qcur = lambda bi, r, i: (bi, i, r * qcols + qcol0)
    cur = lambda bi, r, i: (bi, i, r * kcols + kcol0)
    prev = lambda bi, r, i: (bi, jnp.maximum(i - 1, 0), r * kcols + kcol0)
    omap = lambda bi, r, i: (bi, i, r)
    blk = (1, N_BACK, ATT)
    oshape = jax.ShapeDtypeStruct((b, sub, d * ATT), F32)
    return pl.pallas_call(
        _band_kernel,
        grid=(b, d, nb),
        in_specs=[pl.BlockSpec(blk, qcur), pl.BlockSpec(blk, prev), pl.BlockSpec(blk, cur),
                  pl.BlockSpec(blk, prev), pl.BlockSpec(blk, cur)],
        out_specs=[pl.BlockSpec(blk, omap), pl.BlockSpec(blk, omap)],
        out_shape=[oshape, oshape],
        compiler_params=_cparams(("parallel", "parallel", "arbitrary")),
        name="band_attention",
    )(qv, kv, kv, vv, vv)


def _cache_attn_kernel(q_ref, k_ref, v_ref, c0_ref, c1_ref, c2_ref, *out_refs):
    n_new = q_ref.shape[1]
    rows = n_new * N_HEADS
    lane_head = lax.broadcasted_iota(jnp.int32, (N_HEADS, ATT), 1) // HEAD_DIM
    head_mask = (lane_head == lax.broadcasted_iota(jnp.int32, (N_HEADS, ATT), 0)).astype(F32)
    row_t = lax.broadcasted_iota(jnp.int32, (rows, 1), 0) // N_HEADS
    caches = (c0_ref, c1_ref, c2_ref)
    for g in range(N_GROUPS):
        d, win = DILATIONS[g], WINDOWS[g]
        gs = slice(g * ATT, (g + 1) * ATT)
        q = q_ref[0][:, gs]
        kn = k_ref[0][:, gs]
        vn = v_ref[0][:, gs]
        kc = caches[g][0, 0, 0].astype(BF16)
        vc = caches[g][0, 0, 1].astype(BF16)
        qbd = jnp.concatenate([q[t:t + 1, :] * head_mask for t in range(n_new)], axis=0)
        s_c = lax.dot_general(qbd.astype(BF16), kc, (((1,), (1,)), ((), ())),
                              preferred_element_type=F32) * (HEAD_DIM ** -0.5)
        col = lax.broadcasted_iota(jnp.int32, (rows, win), 1)
        valid_c = (col >= row_t) & (((col - row_t) & (d - 1)) == 0)
        s_c = jnp.where(valid_c, s_c, NEG_INF)
        s_n = []
        for t2 in range(n_new):
            sn = jnp.sum(qbd * kn[t2:t2 + 1, :], axis=-1, keepdims=True) * (HEAD_DIM ** -0.5)
            valid_n = (row_t >= t2) & (((row_t - t2) & (d - 1)) == 0)
            s_n.append(jnp.where(valid_n, sn, NEG_INF))
        m = jnp.max(s_c, axis=-1, keepdims=True)
        for sn in s_n:
            m = jnp.maximum(m, sn)
        p_c = jnp.exp(s_c - m)
        den = jnp.sum(p_c, axis=-1, keepdims=True)
        o = jnp.dot(p_c.astype(BF16), vc, preferred_element_type=F32)
        for t2 in range(n_new):
            p_n = jnp.exp(s_n[t2] - m)
            den = den + p_n
            o = o + p_n * vn[t2:t2 + 1, :]
        o = o / den
        lse = m + jnp.log(den)
        for t in range(n_new):
            rs = slice(t * N_HEADS, (t + 1) * N_HEADS)
            out_refs[g][0, t:t + 1, :] = jnp.sum(o[rs] * head_mask, axis=0, keepdims=True)
            out_refs[N_GROUPS + g][0, t:t + 1, :] = jnp.sum(lse[rs] * head_mask, axis=0, keepdims=True)


def _cache_attention(q, k, v, caches, layer):
    bd, t, w = q.shape
    xmap = lambda b: (b, 0, 0)
    cmap = lambda b: (layer, b, 0, 0, 0)
    oshape = jax.ShapeDtypeStruct((bd, t, ATT), F32)
    return pl.pallas_call(
        _cache_attn_kernel,
        grid=(bd,),
        in_specs=[pl.BlockSpec((1, t, w), xmap)] * 3
        + [pl.BlockSpec((1, 1, 2, WINDOWS[g], ATT), cmap) for g in range(N_GROUPS)],
        out_specs=[pl.BlockSpec((1, t, ATT), xmap)] * (2 * N_GROUPS),
        out_shape=[oshape] * (2 * N_GROUPS),
        compiler_params=_cparams(("parallel",)),
        name="cache_attention",
    )(q, k, v, *caches)


def _undilate(ref, scr, d):
    if d == 1:
        return ref[0]
    rows = ref.shape[1]
    for r in range(d):
        for kc in range(ATT // 128):
            c0 = r * ATT + kc * 128
            scr[kc, pl.ds(r, rows, stride=d), :] = ref[0, :, c0:c0 + 128]
    return jnp.concatenate([scr[kc] for kc in range(ATT // 128)], axis=1)


def _merge_kernel(o0, o1, o2, l0, l1, l2, x_ref, gate_ref, wo_ref, xo_ref, scr, *, dils):
    la, lb, lc = [_undilate(r, scr, d) for r, d in zip((l0, l1, l2), dils)]
    m = jnp.maximum(jnp.maximum(la, lb), lc)
    ea, eb, ec = jnp.exp(la - m), jnp.exp(lb - m), jnp.exp(lc - m)
    den = ea + eb + ec
    o = (ea / den) * _undilate(o0, scr, dils[0])
    o = o + (eb / den) * _undilate(o1, scr, dils[1])
    o = o + (ec / den) * _undilate(o2, scr, dils[2])
    out = jnp.dot(o.astype(BF16), wo_ref[...], preferred_element_type=F32)
    xo_ref[0] = x_ref[0] + gate_ref[0] * out


def _merge(outs, lses, x3, gate, w_o_bf, *, t_tile, dils):
    bq, sq, _ = x3.shape
    mrows = gate.shape[1]
    mt = t_tile if mrows == sq else 1
    mmap = (lambda b, i: (b, i, 0)) if mrows == sq else (lambda b, i: (b, 0, 0))
    tmap = lambda b, i: (b, i, 0)
    ablk = [pl.BlockSpec((1, t_tile // d, d * ATT), tmap) for d in dils]
    return pl.pallas_call(
        functools.partial(_merge_kernel, dils=dils),
        grid=(bq, sq // t_tile),
        in_specs=ablk + ablk + [pl.BlockSpec((1, t_tile, D_MODEL), tmap), pl.BlockSpec((1, mt, D_MODEL), mmap),
                                pl.BlockSpec((ATT, D_MODEL), lambda b, i: (0, 0))],
        out_specs=pl.BlockSpec((1, t_tile, D_MODEL), tmap),
        out_shape=jax.ShapeDtypeStruct(x3.shape, F32),
        scratch_shapes=[pltpu.VMEM((ATT // 128, t_tile, 128), F32)],
        compiler_params=_cparams(("parallel", "parallel")),
        name="attn_merge",
    )(*outs, *lses, x3, gate, w_o_bf)


def _top16(s, exact):
    vals = s
    rank = jnp.full(s.shape, float(PEER_N_KEYS), F32)
    key = lax.broadcasted_iota(jnp.int32, s.shape, 0).astype(F32) if exact else None
    tops = []
    for r in range(PEER_TOPK):
        m = jnp.max(vals, axis=0, keepdims=True)
        hit = vals == m
        if exact:
            first = jnp.min(jnp.where(hit, key, float(PEER_N_KEYS)), axis=0, keepdims=True)
            hit = key == first
        rank = jnp.where(hit, float(r), rank)
        vals = jnp.where(hit, -jnp.inf, vals)
        tops.append(m)
    return tops, rank


def _count_taken(rank):
    return jnp.sum(jnp.where(rank < float(PEER_N_KEYS), 1.0, 0.0), axis=0, keepdims=True)


def _route_head(s1, s2, exact):
    n_tok = s1.shape[1]
    top1, rank1 = _top16(s1, exact)
    top2, rank2 = _top16(s2, exact)
    row = lax.broadcasted_iota(jnp.int32, (PEER_TOPK, n_tok), 0)
    v1 = jnp.zeros((PEER_TOPK, n_tok), F32)
    for r in range(PEER_TOPK):
        v1 = jnp.where(row == r, top1[r], v1)
    best = top1[0] + top2[0]
    zsum = jnp.zeros_like(best)
    if exact:
        cands = [v1 + top2[b] for b in range(PEER_TOPK)]
        flat = [(row * PEER_TOPK + b).astype(F32) for b in range(PEER_TOPK)]
    else:
        cands = [v1[:8] + top2[b] for b in range(PEER_TOPK)] + [v1[8:] + top2[0]]
    taken = [jnp.zeros_like(c) for c in cands]
    for _ in range(PEER_TOPK):
        m = cands[0]
        for c in cands[1:]:
            m = jnp.maximum(m, c)
        m = jnp.max(m, axis=0, keepdims=True)
        if exact:
            fmin = jnp.full_like(cands[0], float(PEER_TOPK * PEER_TOPK))
            for c, f in zip(cands, flat):
                fmin = jnp.minimum(fmin, jnp.where(c == m, f, float(PEER_TOPK * PEER_TOPK)))
            fmin = jnp.min(fmin, axis=0, keepdims=True)
        for j in range(len(cands)):
            hit = (flat[j] == fmin) if exact else (cands[j] == m)
            taken[j] = jnp.where(hit, 1.0, taken[j])
            cands[j] = jnp.where(hit, -jnp.inf, cands[j])
        zsum = zsum + jnp.exp(m - best)
    cnt = taken[0]
    for j in range(1, PEER_TOPK):
        cnt = cnt + taken[j]
    tie = jnp.zeros_like(best)
    if not exact:
        n_pairs = jnp.sum(cnt + taken[PEER_TOPK], axis=0, keepdims=True)
        cnt = jnp.concatenate([cnt, taken[PEER_TOPK]], axis=0)
        bad = ((_count_taken(rank1) != float(PEER_TOPK)) | (_count_taken(rank2) != float(PEER_TOPK))
               | (n_pairs != float(PEER_TOPK)))
        tie = jnp.where(bad, 1.0, 0.0)
    n2 = jnp.zeros_like(rank2)
    for a in range(PEER_TOPK):
        n2 = n2 + jnp.where(rank2 < cnt[a:a + 1], 1.0, 0.0)
    return rank1, jnp.exp(s1 - top1[0]), n2, jnp.exp(s2 - top2[0]) / zsum, tie


def _peer_route_kernel(x_ref, shift_ref, scale_ref, gain_ref, wq_ref, sk_ref, ones_ref,
                       xt_ref, r1_ref, c1_ref, nf_ref, s_scr):
    hm = _modulate(x_ref[...], gain_ref[...], shift_ref[0], scale_ref[0])
    hb = hm.astype(BF16)
    xt_ref[...] = hm.T.astype(BF16)
    q = jnp.dot(hb, wq_ref[...], preferred_element_type=F32)
    qn = q * lax.rsqrt(_seg_mean(q * q, ones_ref[...]) + EPS)
    s_scr[...] = lax.dot_general(sk_ref[...], qn.astype(BF16), (((1,), (1,)), ((), ())),
                                 preferred_element_type=F32)

    def head(h, tie, exact):
        base = pl.multiple_of(h * 2 * PEER_N_KEYS, 2 * PEER_N_KEYS)
        s1 = s_scr[pl.ds(base, PEER_N_KEYS), :]
        s2 = s_scr[pl.ds(base + PEER_N_KEYS, PEER_N_KEYS), :]
        rank1, c1, n2, f2, tie_h = _route_head(s1, s2, exact)
        r1_ref[0, h] = rank1
        c1_ref[0, h] = c1
        nf_ref[0, h, :, 0] = pltpu.bitcast(n2.astype(BF16), jnp.int32).reshape(PEER_N_KEYS // 16, 8, PEER_SEL_TOK)
        nf_ref[0, h, :, 1] = pltpu.bitcast(f2.astype(BF16), jnp.int32).reshape(PEER_N_KEYS // 16, 8, PEER_SEL_TOK)
        return jnp.maximum(tie, tie_h)

    no_tie = jnp.zeros((1, s_scr.shape[1]), F32)
    tie = lax.fori_loop(0, PEER_HEADS, functools.partial(head, exact=False), no_tie)

    @pl.when(jnp.max(tie) > 0.0)
    def _():
        lax.fori_loop(0, PEER_HEADS, functools.partial(head, exact=True), no_tie)


def _peer_route(x2, shift, scale, gain, wq_bf, sk_bd, ones_bd):
    n = x2.shape[0]
    tt = PEER_SEL_TOK
    nt = n // tt
    per_mod = nt // shift.shape[0]
    mrows = shift.shape[1]
    if mrows == 1:
        mblk, mmap = (1, 1, D_MODEL), (lambda i: (i // per_mod, 0, 0))
    else:
        mblk, mmap = (1, tt, D_MODEL), (lambda i: (0, i, 0))
    cmap = lambda i: (0, 0)
    nrt = PEER_N_KEYS // 16
    hshape = jax.ShapeDtypeStruct((nt, PEER_HEADS, PEER_N_KEYS, tt), F32)
    hblk = pl.BlockSpec((1, PEER_HEADS, PEER_N_KEYS, tt), lambda i: (i, 0, 0, 0))
    nfshape = jax.ShapeDtypeStruct((nt, PEER_HEADS, nrt, 2, 8, tt), jnp.int32)
    nfblk = pl.BlockSpec((1, PEER_HEADS, nrt, 2, 8, tt), lambda i: (i, 0, 0, 0, 0, 0))
    return pl.pallas_call(
        _peer_route_kernel,
        grid=(nt,),
        in_specs=[
            pl.BlockSpec((tt, D_MODEL), lambda i: (i, 0)),
            pl.BlockSpec(mblk, mmap), pl.BlockSpec(mblk, mmap),
            pl.BlockSpec((1, D_MODEL), cmap),
            pl.BlockSpec((D_MODEL, D_MODEL), cmap),
            pl.BlockSpec((2 * PEER_HEADS * PEER_N_KEYS, D_MODEL), cmap),
            pl.BlockSpec((D_MODEL, D_MODEL), cmap),
        ],
        out_specs=[pl.BlockSpec((D_MODEL, tt), lambda i: (0, i)), hblk, hblk, nfblk],
        out_shape=[jax.ShapeDtypeStruct((D_MODEL, n), BF16), hshape, hshape, nfshape],
        scratch_shapes=[pltpu.VMEM((2 * PEER_HEADS * PEER_N_KEYS, tt), F32)],
        compiler_params=_cparams(("parallel",)),
        name="peer_route",
    )(x2, shift, scale, gain.reshape(1, D_MODEL), wq_bf, sk_bd, ones_bd)


def _gelu(x):
    return 0.5 * x * (1.0 + lax.erf(x * np.float32(math.sqrt(0.5))))


def _peer_expert_kernel(xt_ref, u_ref, un_ref, vt_ref, vp_ref, r1_ref, c1_ref, nf_ref, x_ref, gate_ref, xo_ref,
                        act_scr, w_scr, acc_scr, *, t_tile):
    e = pl.program_id(1)
    nq = PEER_EXP_TILE // PEER_CHUNK
    last = slice((nq - 1) * PEER_CHUNK, nq * PEER_CHUNK)

    @pl.when(e == 0)
    def _():
        acc_scr[...] = jnp.zeros_like(acc_scr)
        w_scr[nq - 1] = jnp.zeros((PEER_CHUNK, t_tile), BF16)
        act_scr[0] = jnp.dot(u_ref[0, :PEER_CHUNK, :], xt_ref[...], preferred_element_type=F32)

    def stage_a(q):
        qs = slice(q * PEER_CHUNK, (q + 1) * PEER_CHUNK)
        act_scr[q] = jnp.dot(u_ref[0, qs, :], xt_ref[...], preferred_element_type=F32)

    def stage_a_next():
        act_scr[0] = jnp.dot(un_ref[0], xt_ref[...], preferred_element_type=F32)

    def stage_b(q):
        for i1l in range(PEER_CHUNK // PEER_N_KEYS):
            i1 = q * (PEER_CHUNK // PEER_N_KEYS) + i1l
            for lc in range(t_tile // PEER_SEL_TOK):
                ls = slice(lc * PEER_SEL_TOK, (lc + 1) * PEER_SEL_TOK)
                r1 = [jnp.broadcast_to(r1_ref[lc, h, i1:i1 + 1, :], (16, PEER_SEL_TOK)).astype(BF16)
                      for h in range(PEER_HEADS)]
                c1 = [jnp.broadcast_to(c1_ref[lc, h, i1:i1 + 1, :], (16, PEER_SEL_TOK)).astype(BF16)
                      for h in range(PEER_HEADS)]
                for j in range(PEER_N_KEYS // 16):
                    g = jnp.zeros((16, PEER_SEL_TOK), BF16)
                    for h in range(PEER_HEADS):
                        n2 = pltpu.bitcast(nf_ref[lc, h, j, 0], BF16)
                        f2 = pltpu.bitcast(nf_ref[lc, h, j, 1], BF16)
                        g = g + jnp.where(n2 > r1[h], f2, 0.0) * c1[h]
                    rs = slice(i1l * PEER_N_KEYS + j * 16, i1l * PEER_N_KEYS + (j + 1) * 16)
                    w_scr[q, rs, ls] = g * _gelu(act_scr[q, rs, ls]).astype(BF16)

    def stage_c(q):
        qs = slice(q * PEER_CHUNK, (q + 1) * PEER_CHUNK)
        acc_scr[...] += jnp.dot(vt_ref[0, :, qs], w_scr[q], preferred_element_type=F32)

    def stage_c_prev():
        acc_scr[...] += jnp.dot(vp_ref[0], w_scr[nq - 1], preferred_element_type=F32)

    stage_c_prev()
    for q in range(nq):
        if q + 1 < nq:
            stage_a(q + 1)
        else:
            stage_a_next()
        stage_b(q)
        if q > 0:
            stage_c(q - 1)

    @pl.when(e == pl.num_programs(1) - 1)
    def _():
        acc = acc_scr[...] + jnp.dot(vt_ref[0, :, last], w_scr[nq - 1], preferred_element_type=F32)
        xo_ref[...] = x_ref[...] + gate_ref[0] * acc.T


def _peer_expert(xt, u_bf, vt_bf, layer, r1, c1, nf, x2, gate, *, t_tile):
    n = x2.shape[0]
    nt = n // t_tile
    nlc = t_tile // PEER_SEL_TOK
    ne = u_bf.shape[1] // PEER_EXP_TILE
    nq = PEER_EXP_TILE // PEER_CHUNK
    per_mod = max(nt // gate.shape[0], 1)
    mrows = gate.shape[1]
    if mrows == 1:
        mblk, mmap = (1, 1, D_MODEL), (lambda i, e: (i // per_mod, 0, 0))
    else:
        mblk, mmap = (1, t_tile, D_MODEL), (lambda i, e: (0, i, 0))
    i1blk = pl.BlockSpec((nlc, PEER_HEADS, PEER_I1_TILE, PEER_SEL_TOK), lambda i, e: (i, 0, e, 0))
    nfblk = pl.BlockSpec((nlc, PEER_HEADS, PEER_N_KEYS // 16, 2, 8, PEER_SEL_TOK), lambda i, e: (i, 0, 0, 0, 0, 0))
    kern = functools.partial(_peer_expert_kernel, t_tile=t_tile)
    return pl.pallas_call(
        kern,
        grid=(nt, ne),
        in_specs=[
            pl.BlockSpec((D_MODEL, t_tile), lambda i, e: (0, i)),
            pl.BlockSpec((1, PEER_EXP_TILE, D_MODEL), lambda i, e: (layer, e, 0)),
            pl.BlockSpec((1, PEER_CHUNK, D_MODEL), lambda i, e: (layer, jnp.minimum(e + 1, ne - 1) * nq, 0)),
            pl.BlockSpec((1, D_MODEL, PEER_EXP_TILE), lambda i, e: (layer, 0, e)),
            pl.BlockSpec((1, D_MODEL, PEER_CHUNK), lambda i, e: (layer, 0, jnp.maximum(e, 1) * nq - 1)),
            i1blk, i1blk, nfblk,
            pl.BlockSpec((t_tile, D_MODEL), lambda i, e: (i, 0)),
            pl.BlockSpec(mblk, mmap),
        ],
        out_specs=pl.BlockSpec((t_tile, D_MODEL), lambda i, e: (i, 0)),
        out_shape=jax.ShapeDtypeStruct(x2.shape, F32),
        scratch_shapes=[pltpu.VMEM((PEER_EXP_TILE // PEER_CHUNK, PEER_CHUNK, t_tile), F32),
                        pltpu.VMEM((PEER_EXP_TILE // PEER_CHUNK, PEER_CHUNK, t_tile), BF16),
                        pltpu.VMEM((D_MODEL, t_tile), F32)],
        compiler_params=_cparams(("parallel", "arbitrary")),
        name="peer_expert",
    )(xt, u_bf, u_bf, vt_bf, vt_bf, r1, c1, nf, x2, gate)


def _peer(x2, shift, scale, gate, gain, pw, tables, layer, *, t_tile):
    wq_bf, sk_bd, ones_bd = pw
    u_bf, vt_bf = tables
    xt, r1, c1, nf = _peer_route(x2, shift, scale, gain, wq_bf, sk_bd, ones_bd)
    return _peer_expert(xt, u_bf, vt_bf, layer, r1, c1, nf, x2, gate, t_tile=t_tile)


def _peer_weights(w_q, sub_keys, ones_bd):
    hc = 2 * PEER_HEADS
    sk = sub_keys.reshape(hc, PEER_N_KEYS, PEER_HALF)
    sk_bd = jnp.einsum('akd,ab->akbd', sk, jnp.eye(hc, dtype=F32)).reshape(hc * PEER_N_KEYS, hc * PEER_HALF)
    return (w_q.astype(BF16), sk_bd.astype(BF16), ones_bd)


def kernel(x_prompt, x_sample, c_prompt, c_sample, state_ssm, cache_kv_w128, cache_kv_w512, cache_kv_w2048,
           norm_mix, norm_ffn, ada_w, ada_b, ssm_lam_re, ssm_lam_im, ssm_log_dt, ssm_b_re, ssm_b_im, ssm_c_re,
           ssm_c_im, ssm_d, ssm_w_glu, attn_w_qkv, attn_q_norm, attn_k_norm, attn_w_o, peer_w_q,
           peer_sub_keys, peer_u, peer_v):
    bp, sp, _ = x_prompt.shape
    bs, ts, _ = x_sample.shape
    ns = bs * ts
    ones_att = _ones_blockdiag(ATT)
    ones_d = _ones_blockdiag(D_MODEL)

    mods = _adaln(jnp.concatenate([c_prompt, c_sample], axis=0), ada_w, ada_b)
    caches = (cache_kv_w128, cache_kv_w512, cache_kv_w2048)
    tab_p = _rope_tables(jnp.arange(sp, dtype=jnp.int32))
    pos_s = PAST_LEN + jnp.arange(ts, dtype=jnp.int32)
    tab_s = tuple(jnp.tile(t, (bs, 1)) for t in _rope_tables(pos_s))

    peer_tabs = (peer_u.astype(BF16), peer_v.astype(BF16).transpose(0, 2, 1))
    xp, xs = x_prompt, x_sample
    ssm_p, ssm_s = [], []
    kv_p = [[] for _ in range(N_GROUPS)]
    kv_s = [[] for _ in range(N_GROUPS)]
    for i in range(DEPTH):
        j = i // 2
        mp = [m.reshape(bp, 1, D_MODEL) for m in jnp.split(mods[i, :bp], 6, axis=-1)]
        ms_b = jnp.split(mods[i, bp:], 6, axis=-1)
        ms_rows = [jnp.repeat(m, ts, axis=0).reshape(1, ns, D_MODEL) for m in ms_b]
        if i % 2 == 0:
            sw = _s5_weights(ssm_lam_re[j], ssm_lam_im[j], ssm_log_dt[j], ssm_b_re[j], ssm_b_im[j],
                             ssm_c_re[j], ssm_c_im[j])
            wglu = ssm_w_glu[j].astype(BF16)
            mp_t = [m.reshape(1, bp, D_MODEL) for m in mp[:3]]
            xp_t, st_p = _s5_layer(xp.transpose(1, 0, 2), mp_t[0], mp_t[1], mp_t[2], norm_mix[i],
                                   jnp.zeros((bp, SSM_STATE_W), F32), sw, ssm_d[j], wglu, t_chunk=32)
            xp = xp_t.transpose(1, 0, 2)
            ms_t = [m.reshape(1, bs, D_MODEL) for m in ms_b[:3]]
            xs_t, st_s = _s5_layer(xs.transpose(1, 0, 2), ms_t[0], ms_t[1], ms_t[2], norm_mix[i],
                                   _state_to_rows(state_ssm[j]), sw, ssm_d[j], wglu, t_chunk=ts)
            xs = xs_t.transpose(1, 0, 2)
            ssm_p.append(_rows_to_state(st_p))
            ssm_s.append(_rows_to_state(st_s))
        else:
            wqkv = attn_w_qkv[j].astype(BF16)
            wo = attn_w_o[j].astype(BF16)
            k, v, q0, q1, k1, v1, q2, k2, v2 = _qkv_dilated(xp, mp[0], mp[1], norm_mix[i], wqkv, attn_q_norm[j],
                                                            attn_k_norm[j], tab_p, ones_att, t_tile=512)
            res = [_band_attention(q0, k, v, 1, 1, 0, N_GROUPS, 0),
                   _band_attention(q1, k1, v1, DILATIONS[1], 1, 0, 1, 0),
                   _band_attention(q2, k2, v2, DILATIONS[2], 1, 0, 1, 0)]
            outs, lses = zip(*res)
            xp = _merge(outs, lses, xp, mp[2], wo, t_tile=512, dils=DILATIONS)
            for g in range(N_GROUPS):
                keep = min(WINDOWS[g], sp)
                gs = slice(g * ATT, (g + 1) * ATT)
                kv_p[g].append(jnp.stack([k[:, sp - keep:, gs], v[:, sp - keep:, gs]], axis=1)
                               .reshape(bp, 2, keep, N_HEADS, HEAD_DIM))
            xs_rows = xs.reshape(1, ns, D_MODEL)
            qs, ks, vs = _qkv(xs_rows, ms_rows[0], ms_rows[1], norm_mix[i], wqkv, attn_q_norm[j],
                              attn_k_norm[j], tab_s, ones_att, t_tile=ns)
            qs, ks, vs = (t.reshape(bs, ts, N_GROUPS * ATT) for t in (qs, ks, vs))
            res = _cache_attention(qs, ks, vs, [c.reshape(c.shape[:4] + (ATT,)) for c in caches], j)
            outs_s = [r.reshape(1, ns, ATT) for r in res[:N_GROUPS]]
            lses_s = [r.reshape(1, ns, ATT) for r in res[N_GROUPS:]]
            xs = _merge(outs_s, lses_s, xs_rows, ms_rows[2], wo, t_tile=ns,
                        dils=(1, 1, 1)).reshape(bs, ts, D_MODEL)
            for g in range(N_GROUPS):
                gs = slice(g * ATT, (g + 1) * ATT)
                kv_s[g].append(jnp.stack([ks[:, :, gs], vs[:, :, gs]], axis=1)
                               .reshape(bs, 2, ts, N_HEADS, HEAD_DIM))
        pw = _peer_weights(peer_w_q[i], peer_sub_keys[i], ones_d)
        xp = _peer(xp.reshape(bp * sp, D_MODEL), mp[3], mp[4], mp[5], norm_ffn[i], pw, peer_tabs, i,
                   t_tile=512).reshape(bp, sp, D_MODEL)
        xs = _peer(xs.reshape(ns, D_MODEL), ms_rows[3], ms_rows[4], ms_rows[5], norm_ffn[i], pw, peer_tabs, i,
                   t_tile=ns).reshape(bs, ts, D_MODEL)
    return (xp, xs,
            jnp.stack(ssm_p), jnp.stack(kv_p[0]), jnp.stack(kv_p[1]), jnp.stack(kv_p[2]),
            jnp.stack(ssm_s), jnp.stack(kv_s[0]), jnp.stack(kv_s[1]), jnp.stack(kv_s[2]))
```

```python
import functools
import math

import jax
import jax.numpy as jnp
import numpy as np
from jax import lax
from jax.experimental import pallas as pl
from jax.experimental.pallas import tpu as pltpu

F32 = jnp.float32
BF16 = jnp.bfloat16

D_MODEL = 1024
DEPTH = 4
PAST_LEN = 8192
EPS = 1e-6
NEG_INF = -1e30

SSM_P = 16
SSM_GROUPS = D_MODEL // SSM_P
SSM_N = 64
SSM_GB = 4
SSM_GB_CH = D_MODEL // SSM_GB
SSM_GB_ST = SSM_GROUPS * SSM_N // SSM_GB
SSM_STATE_W = 2 * SSM_GROUPS * SSM_N

N_HEADS = 8
HEAD_DIM = 64
ATT = N_HEADS * HEAD_DIM
WINDOWS = (128, 512, 2048)
DILATIONS = (1, 4, 16)
N_GROUPS = 3
N_BACK = 128
ROPE_DIM = HEAD_DIM // 4
ROPE_THETA = 500000.0

PEER_HEADS = 8
PEER_N_KEYS = 128
PEER_HALF = 64
PEER_TOPK = 16
PEER_SEL_TOK = 128
PEER_EXP_TILE = 1024
PEER_I1_TILE = PEER_EXP_TILE // PEER_N_KEYS
PEER_CHUNK = 256

VMEM_LIMIT = 56 * 1024 * 1024


def _cparams(sem):
    return pltpu.CompilerParams(dimension_semantics=sem, vmem_limit_bytes=VMEM_LIMIT)


def _modulate(x, gain, shift, scale):
    ms = jnp.mean(x * x, axis=-1, keepdims=True)
    return x * lax.rsqrt(ms + EPS) * gain * (1.0 + scale) + shift


def _seg_mean(x2, ones_bd):
    hi = x2.astype(BF16)
    lo = (x2 - hi.astype(F32)).astype(BF16)
    s = jnp.dot(hi, ones_bd, preferred_element_type=F32) + jnp.dot(lo, ones_bd, preferred_element_type=F32)
    return s * (1.0 / PEER_HALF)


def _ones_blockdiag(n):
    seg = np.arange(n) // 64
    return jnp.asarray((seg[:, None] == seg[None, :]).astype(np.float32), dtype=BF16)


def _adaln_kernel(c_ref, w_ref, b_ref, o_ref):
    s = jax.nn.silu(c_ref[...])
    o_ref[0] = jnp.dot(s.astype(BF16), w_ref[0].astype(BF16), preferred_element_type=F32) + b_ref[0]


def _adaln(c_all, ada_w, ada_b):
    nb = c_all.shape[0]
    nt = 6 * D_MODEL // 1024
    return pl.pallas_call(
        _adaln_kernel,
        grid=(DEPTH, nt),
        in_specs=[
            pl.BlockSpec((nb, D_MODEL), lambda l, j: (0, 0)),
            pl.BlockSpec((1, D_MODEL, 1024), lambda l, j: (l, 0, j)),
            pl.BlockSpec((1, 1, 1024), lambda l, j: (l, 0, j)),
        ],
        out_specs=pl.BlockSpec((1, nb, 1024), lambda l, j: (l, 0, j)),
        out_shape=jax.ShapeDtypeStruct((DEPTH, nb, 6 * D_MODEL), F32),
        compiler_params=_cparams(("parallel", "parallel")),
        name="adaln",
    )(c_all, ada_w, ada_b.reshape(DEPTH, 1, 6 * D_MODEL))


def _s5_kernel(x_ref, shift_ref, scale_ref, gate_ref, gain_ref, h0_ref, bm_ref, cm_ref, are_ref, aim_ref,
               d_ref, wglu_ref, xo_ref, st_ref, bu_scr, state_scr, *, n_batch, n_time):
    i = pl.program_id(0)

    @pl.when(i == 0)
    def _():
        state_scr[...] = h0_ref[...]

    x3 = x_ref[...]
    a0, a1, _ = x3.shape
    rows = a0 * a1
    h3 = _modulate(x3, gain_ref[...], shift_ref[...], scale_ref[...])
    u = h3.reshape(rows, D_MODEL)
    ub = u.astype(BF16)
    for gb in range(SSM_GB):
        bu_scr[:, gb * 2 * SSM_GB_ST:(gb + 1) * 2 * SSM_GB_ST] = jnp.dot(
            ub[:, gb * SSM_GB_CH:(gb + 1) * SSM_GB_CH], bm_ref[gb], preferred_element_type=F32)

    for bb in range(n_batch // 8):
        for gb in range(SSM_GB):
            c_re = gb * 2 * SSM_GB_ST
            c_im = c_re + SSM_GB_ST
            ar = jnp.broadcast_to(are_ref[gb], (8, SSM_GB_ST))
            ai = jnp.broadcast_to(aim_ref[gb], (8, SSM_GB_ST))
            xr = state_scr[bb * 8:(bb + 1) * 8, c_re:c_re + SSM_GB_ST]
            xi = state_scr[bb * 8:(bb + 1) * 8, c_im:c_im + SSM_GB_ST]
            for t in range(n_time):
                rsel = pl.ds(t * n_batch + bb * 8, 8)
                br = bu_scr[rsel, c_re:c_re + SSM_GB_ST]
                bi = bu_scr[rsel, c_im:c_im + SSM_GB_ST]
                nr = ar * xr - ai * xi + br
                ni = ar * xi + ai * xr + bi
                bu_scr[rsel, c_re:c_re + SSM_GB_ST] = nr
                bu_scr[rsel, c_im:c_im + SSM_GB_ST] = ni
                xr, xi = nr, ni
            state_scr[bb * 8:(bb + 1) * 8, c_re:c_re + SSM_GB_ST] = xr
            state_scr[bb * 8:(bb + 1) * 8, c_im:c_im + SSM_GB_ST] = xi

    ys = []
    for gb in range(SSM_GB):
        st = bu_scr[:, gb * 2 * SSM_GB_ST:(gb + 1) * 2 * SSM_GB_ST].astype(BF16)
        ys.append(jnp.dot(st, cm_ref[gb], preferred_element_type=F32))
    y = jnp.concatenate(ys, axis=-1) + d_ref[...] * u
    z = jnp.dot(y.astype(BF16), wglu_ref[...], preferred_element_type=F32)
    out = z[:, :D_MODEL] * jax.nn.sigmoid(z[:, D_MODEL:])
    xo_ref[...] = x3 + gate_ref[...] * out.reshape(a0, a1, D_MODEL)

    @pl.when(i == pl.num_programs(0) - 1)
    def _():
        st_ref[...] = state_scr[...]


def _s5_weights(lam_re, lam_im, log_dt, b_re, b_im, c_re, c_im):
    dt = jnp.exp(log_dt)[:, None]
    mag = jnp.exp(lam_re * dt)
    a_re = mag * jnp.cos(lam_im * dt)
    a_im = mag * jnp.sin(lam_im * dt)
    den = lam_re * lam_re + lam_im * lam_im
    f_re = ((a_re - 1.0) * lam_re + a_im * lam_im) / den
    f_im = (a_im * lam_re - (a_re - 1.0) * lam_im) / den
    bb_re = f_re[..., None] * b_re - f_im[..., None] * b_im
    bb_im = f_re[..., None] * b_im + f_im[..., None] * b_re
    eye = jnp.eye(16, dtype=F32)

    def in_blk(b):
        b4 = b.reshape(SSM_GB, 16, SSM_N, SSM_P)
        return jnp.einsum('bgnp,gh->bgphn', b4, eye).reshape(SSM_GB, SSM_GB_CH, SSM_GB_ST)

    def out_blk(c):
        c4 = c.reshape(SSM_GB, 16, SSM_P, SSM_N)
        return jnp.einsum('bgpn,gh->bgnhp', c4, eye).reshape(SSM_GB, SSM_GB_ST, SSM_GB_CH)

    bm = jnp.concatenate([in_blk(bb_re), in_blk(bb_im)], axis=2).astype(BF16)
    cm = jnp.concatenate([out_blk(c_re), -out_blk(c_im)], axis=1).astype(BF16)
    return bm, cm, a_re.reshape(SSM_GB, 1, SSM_GB_ST), a_im.reshape(SSM_GB, 1, SSM_GB_ST)


def _state_to_rows(st):
    b = st.shape[0]
    return st.reshape(b, 2, SSM_GB, SSM_GB_ST).transpose(0, 2, 1, 3).reshape(b, SSM_STATE_W)


def _rows_to_state(rows):
    b = rows.shape[0]
    return rows.reshape(b, SSM_GB, 2, SSM_GB_ST).transpose(0, 2, 1, 3).reshape(b, 2, SSM_GROUPS, SSM_N)


def _s5_layer(x3, shift, scale, gate, gain, h0_rows, weights, d_skip, w_glu, *, t_chunk):
    bm, cm, a_re, a_im = weights
    seq, n_batch = x3.shape[0], x3.shape[1]
    n_time = t_chunk
    blk = (n_time, n_batch, D_MODEL)
    xmap = lambda i: (i, 0, 0)
    mblk = (1, n_batch, D_MODEL)
    grid = seq // n_time
    rows = n_batch * n_time
    const3 = lambda i: (0, 0, 0)
    const2 = lambda i: (0, 0)
    kern = functools.partial(_s5_kernel, n_batch=n_batch, n_time=n_time)
    return pl.pallas_call(
        kern,
        grid=(grid,),
        in_specs=[
            pl.BlockSpec(blk, xmap),
            pl.BlockSpec(mblk, const3), pl.BlockSpec(mblk, const3), pl.BlockSpec(mblk, const3),
            pl.BlockSpec((1, 1, D_MODEL), const3),
            pl.BlockSpec((n_batch, SSM_STATE_W), const2),
            pl.BlockSpec(bm.shape, const3), pl.BlockSpec(cm.shape, const3),
            pl.BlockSpec(a_re.shape, const3), pl.BlockSpec(a_im.shape, const3),
            pl.BlockSpec((1, D_MODEL), const2),
            pl.BlockSpec((D_MODEL, 2 * D_MODEL), const2),
        ],
        out_specs=[pl.BlockSpec(blk, xmap), pl.BlockSpec((n_batch, SSM_STATE_W), const2)],
        out_shape=[jax.ShapeDtypeStruct(x3.shape, F32), jax.ShapeDtypeStruct((n_batch, SSM_STATE_W), F32)],
        scratch_shapes=[pltpu.VMEM((rows, SSM_STATE_W), F32), pltpu.VMEM((n_batch, SSM_STATE_W), F32)],
        compiler_params=_cparams(("arbitrary",)),
        name="s5_layer",
    )(x3, shift, scale, gate, gain.reshape(1, 1, D_MODEL), h0_rows, bm, cm, a_re, a_im,
      d_skip.reshape(1, D_MODEL), w_glu)


def _qkv_kernel(x_ref, shift_ref, scale_ref, gain_ref, w_ref, qg_ref, kg_ref, rc_ref, rs1_ref, rs2_ref,
                ones_ref, q_ref, k_ref, v_ref):
    h = _modulate(x_ref[0], gain_ref[...], shift_ref[0], scale_ref[0])
    qkv = jnp.dot(h.astype(BF16), w_ref[...], preferred_element_type=F32)
    rc, rs1, rs2 = rc_ref[...], rs1_ref[...], rs2_ref[...]
    ones_bd = ones_ref[...]

    def norm_rope(t, g):
        tn = t * lax.rsqrt(_seg_mean(t * t, ones_bd) + EPS) * g
        return tn * rc + pltpu.roll(tn, ATT - ROPE_DIM // 2, 1) * rs1 + pltpu.roll(tn, ROPE_DIM // 2, 1) * rs2

    q_ref[0] = norm_rope(qkv[:, :ATT], qg_ref[...])
    k_ref[0] = norm_rope(qkv[:, ATT:2 * ATT], kg_ref[...])
    v_ref[0] = qkv[:, 2 * ATT:]


def _dilate(x, scr, out_ref, d):
    t = x.shape[0]
    for kc in range(ATT // 128):
        scr[kc] = x[:, kc * 128:(kc + 1) * 128]
    for r in range(d):
        for kc in range(ATT // 128):
            c0 = r * ATT + kc * 128
            out_ref[0, :, c0:c0 + 128] = scr[kc, pl.ds(r, t // d, stride=d), :]


def _qkv_dil_kernel(x_ref, shift_ref, scale_ref, gain_ref, w_ref, qg_ref, kg_ref, rc_ref, rs1_ref, rs2_ref,
                    ones_ref, kn_ref, vn_ref, q0_ref, q1_ref, k1_ref, v1_ref, q2_ref, k2_ref, v2_ref, scr):
    g = pl.program_id(2)
    h = _modulate(x_ref[0], gain_ref[...], shift_ref[0], scale_ref[0])
    qkv = jnp.dot(h.astype(BF16), w_ref[...], preferred_element_type=F32)
    rc, rs1, rs2 = rc_ref[...], rs1_ref[...], rs2_ref[...]
    ones_bd = ones_ref[...]

    def norm_rope(t, gn):
        tn = t * lax.rsqrt(_seg_mean(t * t, ones_bd) + EPS) * gn
        return tn * rc + pltpu.roll(tn, ATT - ROPE_DIM // 2, 1) * rs1 + pltpu.roll(tn, ROPE_DIM // 2, 1) * rs2

    q = norm_rope(qkv[:, :ATT], qg_ref[...])
    k = norm_rope(qkv[:, ATT:2 * ATT], kg_ref[...])
    v = qkv[:, 2 * ATT:]
    kn_ref[0] = k
    vn_ref[0] = v

    @pl.when(g == 0)
    def _():
        q0_ref[0] = q

    for gi, refs in ((1, (q1_ref, k1_ref, v1_ref)), (2, (q2_ref, k2_ref, v2_ref))):
        @pl.when(g == gi)
        def _():
            for val, ref in zip((q, k, v), refs):
                _dilate(val, scr, ref, DILATIONS[gi])


def _qkv_dilated(x3, shift, scale, gain, w_qkv_bf, q_gain, k_gain, tables, ones_bd, *, t_tile):
    bq, sq, _ = x3.shape
    rc, rs1, rs2 = tables
    qg = jnp.tile(q_gain, N_HEADS).reshape(1, ATT)
    kg = jnp.tile(k_gain, N_HEADS).reshape(1, ATT)
    mmap = lambda b, i, g: (b, 0, 0)
    tmap = lambda b, i, g: (i, 0)
    cmap = lambda b, i, g: (0, 0)
    gmap = lambda b, i, g: (b, i, g)
    bmap = lambda b, i, g: (b, i, 0)
    nat = jax.ShapeDtypeStruct((bq, sq, N_GROUPS * ATT), F32)
    out_shape = [nat, nat, jax.ShapeDtypeStruct((bq, sq, ATT), F32)]
    out_specs = [pl.BlockSpec((1, t_tile, ATT), gmap)] * 2 + [pl.BlockSpec((1, t_tile, ATT), bmap)]
    for gi in (1, 2):
        d = DILATIONS[gi]
        out_shape += [jax.ShapeDtypeStruct((bq, sq // d, d * ATT), F32)] * 3
        out_specs += [pl.BlockSpec((1, t_tile // d, d * ATT), bmap)] * 3
    return pl.pallas_call(
        _qkv_dil_kernel,
        grid=(bq, sq // t_tile, N_GROUPS),
        in_specs=[
            pl.BlockSpec((1, t_tile, D_MODEL), bmap),
            pl.BlockSpec((1, 1, D_MODEL), mmap), pl.BlockSpec((1, 1, D_MODEL), mmap),
            pl.BlockSpec((1, D_MODEL), cmap),
            pl.BlockSpec((D_MODEL, 3 * ATT), lambda b, i, g: (0, g)),
            pl.BlockSpec((1, ATT), cmap), pl.BlockSpec((1, ATT), cmap),
            pl.BlockSpec((t_tile, ATT), tmap), pl.BlockSpec((t_tile, ATT), tmap), pl.BlockSpec((t_tile, ATT), tmap),
            pl.BlockSpec((ATT, ATT), cmap),
        ],
        out_specs=out_specs,
        out_shape=out_shape,
        scratch_shapes=[pltpu.VMEM((ATT // 128, t_tile, 128), F32)],
        compiler_params=_cparams(("parallel", "parallel", "arbitrary")),
        name="qkv_proj_dilated",
    )(x3, shift, scale, gain.reshape(1, D_MODEL), w_qkv_bf, qg, kg, rc, rs1, rs2, ones_bd)


def _rope_tables(pos):
    half = ROPE_DIM // 2
    inv = ROPE_THETA ** (-jnp.arange(half, dtype=F32) / half)
    ang = pos.astype(F32)[:, None] * inv[None, :]
    cos, sin = jnp.cos(ang), jnp.sin(ang)
    lane = np.arange(ATT) % HEAD_DIM
    fidx = lane % half
    first = jnp.asarray(lane < half)
    second = jnp.asarray((lane >= half) & (lane < ROPE_DIM))
    cl, sl = cos[:, fidx], sin[:, fidx]
    rc = jnp.where(first | second, cl, 1.0)
    rs1 = jnp.where(first, -sl, 0.0)
    rs2 = jnp.where(second, sl, 0.0)
    return rc, rs1, rs2


def _qkv(x3, shift, scale, gain, w_qkv_bf, q_gain, k_gain, tables, ones_bd, *, t_tile):
    bq, sq, _ = x3.shape
    mrows = shift.shape[1]
    mt = t_tile if mrows == sq else 1
    mmap = (lambda b, i, g: (b, i, 0)) if mrows == sq else (lambda b, i, g: (b, 0, 0))
    rc, rs1, rs2 = tables
    qg = jnp.tile(q_gain, N_HEADS).reshape(1, ATT)
    kg = jnp.tile(k_gain, N_HEADS).reshape(1, ATT)
    tmap = lambda b, i, g: (i, 0)
    cmap = lambda b, i, g: (0, 0)
    omap = lambda b, i, g: (b, i, g)
    oshape = jax.ShapeDtypeStruct((bq, sq, N_GROUPS * ATT), F32)
    return pl.pallas_call(
        _qkv_kernel,
        grid=(bq, sq // t_tile, N_GROUPS),
        in_specs=[
            pl.BlockSpec((1, t_tile, D_MODEL), lambda b, i, g: (b, i, 0)),
            pl.BlockSpec((1, mt, D_MODEL), mmap), pl.BlockSpec((1, mt, D_MODEL), mmap),
            pl.BlockSpec((1, D_MODEL), cmap),
            pl.BlockSpec((D_MODEL, 3 * ATT), lambda b, i, g: (0, g)),
            pl.BlockSpec((1, ATT), cmap), pl.BlockSpec((1, ATT), cmap),
            pl.BlockSpec((t_tile, ATT), tmap), pl.BlockSpec((t_tile, ATT), tmap), pl.BlockSpec((t_tile, ATT), tmap),
            pl.BlockSpec((ATT, ATT), cmap),
        ],
        out_specs=[pl.BlockSpec((1, t_tile, ATT), omap)] * 3,
        out_shape=[oshape] * 3,
        compiler_params=_cparams(("parallel", "parallel", "arbitrary")),
        name="qkv_proj",
    )(x3, shift, scale, gain.reshape(1, D_MODEL), w_qkv_bf, qg, kg, rc, rs1, rs2, ones_bd)


def _band_kernel(q_ref, kp_ref, kc_ref, vp_ref, vc_ref, o_ref, l_ref):
    i = pl.program_id(2)
    q = q_ref[0].astype(BF16)
    k = jnp.concatenate([kp_ref[0], kc_ref[0]], axis=0).astype(BF16)
    v = jnp.concatenate([vp_ref[0], vc_ref[0]], axis=0).astype(BF16)
    iq = lax.broadcasted_iota(jnp.int32, (N_BACK, 2 * N_BACK), 0)
    ik = lax.broadcasted_iota(jnp.int32, (N_BACK, 2 * N_BACK), 1)
    dist = iq + N_BACK - ik
    mask = (dist >= 0) & (dist <= N_BACK) & ((ik >= N_BACK) | (i > 0))
    for h in range(N_HEADS):
        sl = slice(h * HEAD_DIM, (h + 1) * HEAD_DIM)
        s = lax.dot_general(q[:, sl], k[:, sl], (((1,), (1,)), ((), ())),
                            preferred_element_type=F32) * (HEAD_DIM ** -0.5)
        s = jnp.where(mask, s, NEG_INF)
        m = jnp.max(s, axis=-1, keepdims=True)
        p = jnp.exp(s - m)
        den = jnp.sum(p, axis=-1, keepdims=True)
        o = jnp.dot(p.astype(BF16), v[:, sl], preferred_element_type=F32) / den
        o_ref[0, :, sl] = o
        l_ref[0, :, sl] = jnp.broadcast_to(m + jnp.log(den), (N_BACK, HEAD_DIM))


def _band_attention(qv, kv, vv, d, qcols, qcol0, kcols, kcol0):
    b, sub, _ = qv.shape
    nb = sub // N_BACK
    qcur = lambda bi, r, i: (bi, i, r * qcols + qcol0)
    cur = lambda bi, r, i: (bi, i, r * kcols + kcol0)
    prev = lambda bi, r, i: (bi, jnp.maximum(i - 1, 0), r * kcols + kcol0)
    omap = lambda bi, r, i: (bi, i, r)
    blk = (1, N_BACK, ATT)
    oshape = jax.ShapeDtypeStruct((b, sub, d * ATT), F32)
    return pl.pallas_call(
        _band_kernel,
        grid=(b, d, nb),
        in_specs=[pl.BlockSpec(blk, qcur), pl.BlockSpec(blk, prev), pl.BlockSpec(blk, cur),
                  pl.BlockSpec(blk, prev), pl.BlockSpec(blk, cur)],
        out_specs=[pl.BlockSpec(blk, omap), pl.BlockSpec(blk, omap)],
        out_shape=[oshape, oshape],
        compiler_params=_cparams(("parallel", "parallel", "arbitrary")),
        name="band_attention",
    )(qv, kv, kv, vv, vv)


def _cache_attn_kernel(q_ref, k_ref, v_ref, c0_ref, c1_ref, c2_ref, *out_refs):
    n_new = q_ref.shape[1]
    rows = n_new * N_HEADS
    lane_head = lax.broadcasted_iota(jnp.int32, (N_HEADS, ATT), 1) // HEAD_DIM
    head_mask = (lane_head == lax.broadcasted_iota(jnp.int32, (N_HEADS, ATT), 0)).astype(F32)
    row_t = lax.broadcasted_iota(jnp.int32, (rows, 1), 0) // N_HEADS
    caches = (c0_ref, c1_ref, c2_ref)
    for g in range(N_GROUPS):
        d, win = DILATIONS[g], WINDOWS[g]
        gs = slice(g * ATT, (g + 1) * ATT)
        q = q_ref[0][:, gs]
        kn = k_ref[0][:, gs]
        vn = v_ref[0][:, gs]
        kc = caches[g][0, 0, 0].astype(BF16)
        vc = caches[g][0, 0, 1].astype(BF16)
        qbd = jnp.concatenate([q[t:t + 1, :] * head_mask for t in range(n_new)], axis=0)
        s_c = lax.dot_general(qbd.astype(BF16), kc, (((1,), (1,)), ((), ())),
                              preferred_element_type=F32) * (HEAD_DIM ** -0.5)
        col = lax.broadcasted_iota(jnp.int32, (rows, win), 1)
        valid_c = (col >= row_t) & (((col - row_t) & (d - 1)) == 0)
        s_c = jnp.where(valid_c, s_c, NEG_INF)
        s_n = []
        for t2 in range(n_new):
            sn = jnp.sum(qbd * kn[t2:t2 + 1, :], axis=-1, keepdims=True) * (HEAD_DIM ** -0.5)
            valid_n = (row_t >= t2) & (((row_t - t2) & (d - 1)) == 0)
            s_n.append(jnp.where(valid_n, sn, NEG_INF))
        m = jnp.max(s_c, axis=-1, keepdims=True)
        for sn in s_n:
            m = jnp.maximum(m, sn)
        p_c = jnp.exp(s_c - m)
        den = jnp.sum(p_c, axis=-1, keepdims=True)
        o = jnp.dot(p_c.astype(BF16), vc, preferred_element_type=F32)
        for t2 in range(n_new):
            p_n = jnp.exp(s_n[t2] - m)
            den = den + p_n
            o = o + p_n * vn[t2:t2 + 1, :]
        o = o / den
        lse = m + jnp.log(den)
        for t in range(n_new):
            rs = slice(t * N_HEADS, (t + 1) * N_HEADS)
            out_refs[g][0, t:t + 1, :] = jnp.sum(o[rs] * head_mask, axis=0, keepdims=True)
            out_refs[N_GROUPS + g][0, t:t + 1, :] = jnp.sum(lse[rs] * head_mask, axis=0, keepdims=True)


def _cache_attention(q, k, v, caches, layer):
    bd, t, w = q.shape
    xmap = lambda b: (b, 0, 0)
    cmap = lambda b: (layer, b, 0, 0, 0)
    oshape = jax.ShapeDtypeStruct((bd, t, ATT), F32)
    return pl.pallas_call(
        _cache_attn_kernel,
        grid=(bd,),
        in_specs=[pl.BlockSpec((1, t, w), xmap)] * 3
        + [pl.BlockSpec((1, 1, 2, WINDOWS[g], ATT), cmap) for g in range(N_GROUPS)],
        out_specs=[pl.BlockSpec((1, t, ATT), xmap)] * (2 * N_GROUPS),
        out_shape=[oshape] * (2 * N_GROUPS),
        compiler_params=_cparams(("parallel",)),
        name="cache_attention",
    )(q, k, v, *caches)


def _undilate(ref, scr, d):
    if d == 1:
        return ref[0]
    rows = ref.shape[1]
    for r in range(d):
        for kc in range(ATT // 128):
            c0 = r * ATT + kc * 128
            scr[kc, pl.ds(r, rows, stride=d), :] = ref[0, :, c0:c0 + 128]
    return jnp.concatenate([scr[kc] for kc in range(ATT // 128)], axis=1)


def _merge_kernel(o0, o1, o2, l0, l1, l2, x_ref, gate_ref, wo_ref, xo_ref, scr, *, dils):
    la, lb, lc = [_undilate(r, scr, d) for r, d in zip((l0, l1, l2), dils)]
    m = jnp.maximum(jnp.maximum(la, lb), lc)
    ea, eb, ec = jnp.exp(la - m), jnp.exp(lb - m), jnp.exp(lc - m)
    den = ea + eb + ec
    o = (ea / den) * _undilate(o0, scr, dils[0])
    o = o + (eb / den) * _undilate(o1, scr, dils[1])
    o = o + (ec / den) * _undilate(o2, scr, dils[2])
    out = jnp.dot(o.astype(BF16), wo_ref[...], preferred_element_type=F32)
    xo_ref[0] = x_ref[0] + gate_ref[0] * out


def _merge(outs, lses, x3, gate, w_o_bf, *, t_tile, dils):
    bq, sq, _ = x3.shape
    mrows = gate.shape[1]
    mt = t_tile if mrows == sq else 1
    mmap = (lambda b, i: (b, i, 0)) if mrows == sq else (lambda b, i: (b, 0, 0))
    tmap = lambda b, i: (b, i, 0)
    ablk = [pl.BlockSpec((1, t_tile // d, d * ATT), tmap) for d in dils]
    return pl.pallas_call(
        functools.partial(_merge_kernel, dils=dils),
        grid=(bq, sq // t_tile),
        in_specs=ablk + ablk + [pl.BlockSpec((1, t_tile, D_MODEL), tmap), pl.BlockSpec((1, mt, D_MODEL), mmap),
                                pl.BlockSpec((ATT, D_MODEL), lambda b, i: (0, 0))],
        out_specs=pl.BlockSpec((1, t_tile, D_MODEL), tmap),
        out_shape=jax.ShapeDtypeStruct(x3.shape, F32),
        scratch_shapes=[pltpu.VMEM((ATT // 128, t_tile, 128), F32)],
        compiler_params=_cparams(("parallel", "parallel")),
        name="attn_merge",
    )(*outs, *lses, x3, gate, w_o_bf)


def _top16(s, exact):
    vals = s
    rank = jnp.full(s.shape, float(PEER_N_KEYS), F32)
    key = lax.broadcasted_iota(jnp.int32, s.shape, 0).astype(F32) if exact else None
    tops = []
    for r in range(PEER_TOPK):
        m = jnp.max(vals, axis=0, keepdims=True)
        hit = vals == m
        if exact:
            first = jnp.min(jnp.where(hit, key, float(PEER_N_KEYS)), axis=0, keepdims=True)
            hit = key == first
        rank = jnp.where(hit, float(r), rank)
        vals = jnp.where(hit, -jnp.inf, vals)
        tops.append(m)
    return tops, rank


def _count_taken(rank):
    return jnp.sum(jnp.where(rank < float(PEER_N_KEYS), 1.0, 0.0), axis=0, keepdims=True)


def _route_head(s1, s2, exact):
    n_tok = s1.shape[1]
    top1, rank1 = _top16(s1, exact)
    top2, rank2 = _top16(s2, exact)
    row = lax.broadcasted_iota(jnp.int32, (PEER_TOPK, n_tok), 0)
    v1 = jnp.zeros((PEER_TOPK, n_tok), F32)
    for r in range(PEER_TOPK):
        v1 = jnp.where(row == r, top1[r], v1)
    best = top1[0] + top2[0]
    zsum = jnp.zeros_like(best)
    if exact:
        cands = [v1 + top2[b] for b in range(PEER_TOPK)]
        flat = [(row * PEER_TOPK + b).astype(F32) for b in range(PEER_TOPK)]
    else:
        v2 = jnp.zeros((PEER_TOPK, n_tok), F32)
        for r in range(PEER_TOPK):
            v2 = jnp.where(row == r, top2[r], v2)
        cands = [v1[:8] + top2[b] for b in range(8)] + [v1[8:] + top2[0], top1[0] + v2[8:]]
    taken = [jnp.zeros_like(c) for c in cands]
    for _ in range(PEER_TOPK):
        m = cands[0]
        for c in cands[1:]:
            m = jnp.maximum(m, c)
        m = jnp.max(m, axis=0, keepdims=True)
        if exact:
            fmin = jnp.full_like(cands[0], float(PEER_TOPK * PEER_TOPK))
            for c, f in zip(cands, flat):
                fmin = jnp.minimum(fmin, jnp.where(c == m, f, float(PEER_TOPK * PEER_TOPK)))
            fmin = jnp.min(fmin, axis=0, keepdims=True)
        for j in range(len(cands)):
            hit = (flat[j] == fmin) if exact else (cands[j] == m)
            taken[j] = jnp.where(hit, 1.0, taken[j])
            cands[j] = jnp.where(hit, -jnp.inf, cands[j])
        zsum = zsum + jnp.exp(m - best)
    tie = jnp.zeros_like(best)
    if exact:
        cnt = taken[0]
        for j in range(1, PEER_TOPK):
            cnt = cnt + taken[j]
    else:
        cnt = taken[0]
        for j in range(1, 8):
            cnt = cnt + taken[j]
        cnt = jnp.concatenate([cnt, taken[8]], axis=0)
        cnt = cnt + jnp.where(row == 0, jnp.sum(taken[9], axis=0, keepdims=True), 0.0)
        n_pairs = jnp.sum(cnt, axis=0, keepdims=True)
        bad = ((_count_taken(rank1) != float(PEER_TOPK)) | (_count_taken(rank2) != float(PEER_TOPK))
               | (n_pairs != float(PEER_TOPK)))
        tie = jnp.where(bad, 1.0, 0.0)
    n2 = jnp.where(rank2 < 1.0, jnp.sum(cnt[8:], axis=0, keepdims=True), 0.0)
    for a in range(8):
        n2 = n2 + jnp.where(rank2 < cnt[a:a + 1], 1.0, 0.0)
    return rank1, jnp.exp(s1 - top1[0]), n2, jnp.exp(s2 - top2[0]) / zsum, tie


def _peer_route_kernel(x_ref, shift_ref, scale_ref, gain_ref, wq_ref, sk_ref, ones_ref,
                       xt_ref, r1_ref, c1_ref, nf_ref, s_scr, tie_scr):
    hm = _modulate(x_ref[...], gain_ref[...], shift_ref[0], scale_ref[0])
    hb = hm.astype(BF16)
    xt_ref[...] = hm.T.astype(BF16)
    q = jnp.dot(hb, wq_ref[...], preferred_element_type=F32)
    nt_dims = (((1,), (1,)), ((), ()))
    q2 = q * q
    q2_hi = q2.astype(BF16)
    q2_lo = (q2 - q2_hi.astype(F32)).astype(BF16)
    ms_t = (lax.dot_general(ones_ref[...], q2_hi, nt_dims, preferred_element_type=F32)
            + lax.dot_general(ones_ref[...], q2_lo, nt_dims, preferred_element_type=F32)) * (1.0 / PEER_HALF)
    inv_t = lax.rsqrt(ms_t + EPS)
    qb = q.astype(BF16)
    for hc in range(2 * PEER_HEADS):
        raw = lax.dot_general(sk_ref[hc], qb[:, hc * PEER_HALF:(hc + 1) * PEER_HALF], nt_dims,
                              preferred_element_type=F32)
        s_scr[hc * PEER_N_KEYS:(hc + 1) * PEER_N_KEYS, :] = raw * inv_t[hc:hc + 1, :]

    def head(h, tie, exact):
        base = pl.multiple_of(h * 2 * PEER_N_KEYS, 2 * PEER_N_KEYS)
        s1 = s_scr[pl.ds(base, PEER_N_KEYS), :]
        s2 = s_scr[pl.ds(base + PEER_N_KEYS, PEER_N_KEYS), :]
        rank1, c1, n2, f2, tie_h = _route_head(s1, s2, exact)
        r1_ref[0, h] = rank1
        c1_ref[0, h] = c1
        nf_ref[0, h, :, 0] = pltpu.bitcast(n2.astype(BF16), jnp.int32).reshape(PEER_N_KEYS // 16, 8, PEER_SEL_TOK)
        nf_ref[0, h, :, 1] = pltpu.bitcast(f2.astype(BF16), jnp.int32).reshape(PEER_N_KEYS // 16, 8, PEER_SEL_TOK)
        if not exact:
            tie_scr[h] = tie_h
        return jnp.maximum(tie, tie_h)

    no_tie = jnp.zeros((1, s_scr.shape[1]), F32)
    tie = lax.fori_loop(0, PEER_HEADS, functools.partial(head, exact=False), no_tie)

    @pl.when(jnp.max(tie) > 0.0)
    def _():
        def redo(h, carry):
            @pl.when(jnp.max(tie_scr[h]) > 0.0)
            def _():
                head(h, no_tie, True)
            return carry

        lax.fori_loop(0, PEER_HEADS, redo, 0)


def _peer_route(x2, shift, scale, gain, wq_bf, sk_bf, ones_t):
    n = x2.shape[0]
    tt = PEER_SEL_TOK
    nt = n // tt
    per_mod = nt // shift.shape[0]
    mrows = shift.shape[1]
    if mrows == 1:
        mblk, mmap = (1, 1, D_MODEL), (lambda i: (i // per_mod, 0, 0))
    else:
        mblk, mmap = (1, tt, D_MODEL), (lambda i: (0, i, 0))
    cmap = lambda i: (0, 0)
    nrt = PEER_N_KEYS // 16
    hshape = jax.ShapeDtypeStruct((nt, PEER_HEADS, PEER_N_KEYS, tt), F32)
    hblk = pl.BlockSpec((1, PEER_HEADS, PEER_N_KEYS, tt), lambda i: (i, 0, 0, 0))
    nfshape = jax.ShapeDtypeStruct((nt, PEER_HEADS, nrt, 2, 8, tt), jnp.int32)
    nfblk = pl.BlockSpec((1, PEER_HEADS, nrt, 2, 8, tt), lambda i: (i, 0, 0, 0, 0, 0))
    return pl.pallas_call(
        _peer_route_kernel,
        grid=(nt,),
        in_specs=[
            pl.BlockSpec((tt, D_MODEL), lambda i: (i, 0)),
            pl.BlockSpec(mblk, mmap), pl.BlockSpec(mblk, mmap),
            pl.BlockSpec((1, D_MODEL), cmap),
            pl.BlockSpec((D_MODEL, D_MODEL), cmap),
            pl.BlockSpec((2 * PEER_HEADS, PEER_N_KEYS, PEER_HALF), lambda i: (0, 0, 0)),
            pl.BlockSpec((2 * PEER_HEADS, D_MODEL), cmap),
        ],
        out_specs=[pl.BlockSpec((D_MODEL, tt), lambda i: (0, i)), hblk, hblk, nfblk],
        out_shape=[jax.ShapeDtypeStruct((D_MODEL, n), BF16), hshape, hshape, nfshape],
        scratch_shapes=[pltpu.VMEM((2 * PEER_HEADS * PEER_N_KEYS, tt), F32), pltpu.VMEM((PEER_HEADS, 1, tt), F32)],
        compiler_params=_cparams(("parallel",)),
        name="peer_route",
    )(x2, shift, scale, gain.reshape(1, D_MODEL), wq_bf, sk_bf, ones_t)


def _gelu(x):
    return 0.5 * x * (1.0 + lax.erf(x * np.float32(math.sqrt(0.5))))


def _peer_expert_kernel(xt_ref, u_ref, un_ref, vt_ref, vp_ref, r1_ref, c1_ref, nf_ref, x_ref, gate_ref, xo_ref,
                        act_scr, w_scr, acc_scr, *, t_tile):
    e = pl.program_id(1)
    nq = PEER_EXP_TILE // PEER_CHUNK
    last = slice((nq - 1) * PEER_CHUNK, nq * PEER_CHUNK)

    @pl.when(e == 0)
    def _():
        acc_scr[...] = jnp.zeros_like(acc_scr)
        w_scr[nq - 1] = jnp.zeros((PEER_CHUNK, t_tile), BF16)
        act_scr[0] = jnp.dot(u_ref[0, :PEER_CHUNK, :], xt_ref[...], preferred_element_type=F32)

    def stage_a(q):
        qs = slice(q * PEER_CHUNK, (q + 1) * PEER_CHUNK)
        act_scr[q] = jnp.dot(u_ref[0, qs, :], xt_ref[...], preferred_element_type=F32)

    def stage_a_next():
        act_scr[0] = jnp.dot(un_ref[0], xt_ref[...], preferred_element_type=F32)

    def stage_b(q):
        for i1l in range(PEER_CHUNK // PEER_N_KEYS):
            i1 = q * (PEER_CHUNK // PEER_N_KEYS) + i1l
            for lc in range(t_tile // PEER_SEL_TOK):
                ls = slice(lc * PEER_SEL_TOK, (lc + 1) * PEER_SEL_TOK)
                r1 = [jnp.broadcast_to(r1_ref[lc, h, i1:i1 + 1, :], (16, PEER_SEL_TOK)).astype(BF16)
                      for h in range(PEER_HEADS)]
                c1 = [jnp.broadcast_to(c1_ref[lc, h, i1:i1 + 1, :], (16, PEER_SEL_TOK)).astype(BF16)
                      for h in range(PEER_HEADS)]
                for j in range(PEER_N_KEYS // 16):
                    g = jnp.zeros((16, PEER_SEL_TOK), BF16)
                    for h in range(PEER_HEADS):
                        n2 = pltpu.bitcast(nf_ref[lc, h, j, 0], BF16)
                        f2 = pltpu.bitcast(nf_ref[lc, h, j, 1], BF16)
                        g = g + jnp.where(n2 > r1[h], f2, 0.0) * c1[h]
                    rs = slice(i1l * PEER_N_KEYS + j * 16, i1l * PEER_N_KEYS + (j + 1) * 16)
                    w_scr[q, rs, ls] = g * _gelu(act_scr[q, rs, ls]).astype(BF16)

    def stage_c(q):
        qs = slice(q * PEER_CHUNK, (q + 1) * PEER_CHUNK)
        acc_scr[...] += jnp.dot(vt_ref[0, :, qs], w_scr[q], preferred_element_type=F32)

    def stage_c_prev():
        acc_scr[...] += jnp.dot(vp_ref[0], w_scr[nq - 1], preferred_element_type=F32)

    stage_c_prev()
    for q in range(nq):
        if q + 1 < nq:
            stage_a(q + 1)
        else:
            stage_a_next()
        stage_b(q)
        if q > 0:
            stage_c(q - 1)

    @pl.when(e == pl.num_programs(1) - 1)
    def _():
        acc = acc_scr[...] + jnp.dot(vt_ref[0, :, last], w_scr[nq - 1], preferred_element_type=F32)
        xo_ref[...] = x_ref[...] + gate_ref[0] * acc.T


def _peer_expert(xt, u_bf, vt_bf, layer, r1, c1, nf, x2, gate, *, t_tile):
    n = x2.shape[0]
    nt = n // t_tile
    nlc = t_tile // PEER_SEL_TOK
    ne = u_bf.shape[1] // PEER_EXP_TILE
    nq = PEER_EXP_TILE // PEER_CHUNK
    per_mod = max(nt // gate.shape[0], 1)
    mrows = gate.shape[1]
    if mrows == 1:
        mblk, mmap = (1, 1, D_MODEL), (lambda i, e: (i // per_mod, 0, 0))
    else:
        mblk, mmap = (1, t_tile, D_MODEL), (lambda i, e: (0, i, 0))
    i1blk = pl.BlockSpec((nlc, PEER_HEADS, PEER_I1_TILE, PEER_SEL_TOK), lambda i, e: (i, 0, e, 0))
    nfblk = pl.BlockSpec((nlc, PEER_HEADS, PEER_N_KEYS // 16, 2, 8, PEER_SEL_TOK), lambda i, e: (i, 0, 0, 0, 0, 0))
    kern = functools.partial(_peer_expert_kernel, t_tile=t_tile)
    return pl.pallas_call(
        kern,
        grid=(nt, ne),
        in_specs=[
            pl.BlockSpec((D_MODEL, t_tile), lambda i, e: (0, i)),
            pl.BlockSpec((1, PEER_EXP_TILE, D_MODEL), lambda i, e: (layer, e, 0)),
            pl.BlockSpec((1, PEER_CHUNK, D_MODEL), lambda i, e: (layer, jnp.minimum(e + 1, ne - 1) * nq, 0)),
            pl.BlockSpec((1, D_MODEL, PEER_EXP_TILE), lambda i, e: (layer, 0, e)),
            pl.BlockSpec((1, D_MODEL, PEER_CHUNK), lambda i, e: (layer, 0, jnp.maximum(e, 1) * nq - 1)),
            i1blk, i1blk, nfblk,
            pl.BlockSpec((t_tile, D_MODEL), lambda i, e: (i, 0)),
            pl.BlockSpec(mblk, mmap),
        ],
        out_specs=pl.BlockSpec((t_tile, D_MODEL), lambda i, e: (i, 0)),
        out_shape=jax.ShapeDtypeStruct(x2.shape, F32),
        scratch_shapes=[pltpu.VMEM((PEER_EXP_TILE // PEER_CHUNK, PEER_CHUNK, t_tile), F32),
                        pltpu.VMEM((PEER_EXP_TILE // PEER_CHUNK, PEER_CHUNK, t_tile), BF16),
                        pltpu.VMEM((D_MODEL, t_tile), F32)],
        compiler_params=_cparams(("parallel", "arbitrary")),
        name="peer_expert",
    )(xt, u_bf, u_bf, vt_bf, vt_bf, r1, c1, nf, x2, gate)


def _peer(x2, shift, scale, gate, gain, pw, tables, layer, *, t_tile):
    wq_bf, sk_bf, ones_t = pw
    u_bf, vt_bf = tables
    xt, r1, c1, nf = _peer_route(x2, shift, scale, gain, wq_bf, sk_bf, ones_t)
    return _peer_expert(xt, u_bf, vt_bf, layer, r1, c1, nf, x2, gate, t_tile=t_tile)


def _peer_weights(w_q, sub_keys):
    hc = 2 * PEER_HEADS
    seg = np.arange(D_MODEL) // PEER_HALF
    ones_t = jnp.asarray((seg[None, :] == np.arange(hc)[:, None]).astype(np.float32), dtype=BF16)
    return (w_q.astype(BF16), sub_keys.reshape(hc, PEER_N_KEYS, PEER_HALF).astype(BF16), ones_t)


def kernel(x_prompt, x_sample, c_prompt, c_sample, state_ssm, cache_kv_w128, cache_kv_w512, cache_kv_w2048,
           norm_mix, norm_ffn, ada_w, ada_b, ssm_lam_re, ssm_lam_im, ssm_log_dt, ssm_b_re, ssm_b_im, ssm_c_re,
           ssm_c_im, ssm_d, ssm_w_glu, attn_w_qkv, attn_q_norm, attn_k_norm, attn_w_o, peer_w_q,
           peer_sub_keys, peer_u, peer_v):
    bp, sp, _ = x_prompt.shape
    bs, ts, _ = x_sample.shape
    ns = bs * ts
    ones_att = _ones_blockdiag(ATT)

    mods = _adaln(jnp.concatenate([c_prompt, c_sample], axis=0), ada_w, ada_b)
    caches = (cache_kv_w128, cache_kv_w512, cache_kv_w2048)
    tab_p = _rope_tables(jnp.arange(sp, dtype=jnp.int32))
    pos_s = PAST_LEN + jnp.arange(ts, dtype=jnp.int32)
    tab_s = tuple(jnp.tile(t, (bs, 1)) for t in _rope_tables(pos_s))

    peer_tabs = (peer_u.astype(BF16), peer_v.astype(BF16).transpose(0, 2, 1))
    xp, xs = x_prompt, x_sample
    ssm_p, ssm_s = [], []
    kv_p = [[] for _ in range(N_GROUPS)]
    kv_s = [[] for _ in range(N_GROUPS)]
    for i in range(DEPTH):
        j = i // 2
        mp = [m.reshape(bp, 1, D_MODEL) for m in jnp.split(mods[i, :bp], 6, axis=-1)]
        ms_b = jnp.split(mods[i, bp:], 6, axis=-1)
        ms_rows = [jnp.repeat(m, ts, axis=0).reshape(1, ns, D_MODEL) for m in ms_b]
        if i % 2 == 0:
            sw = _s5_weights(ssm_lam_re[j], ssm_lam_im[j], ssm_log_dt[j], ssm_b_re[j], ssm_b_im[j],
                             ssm_c_re[j], ssm_c_im[j])
            wglu = ssm_w_glu[j].astype(BF16)
            mp_t = [m.reshape(1, bp, D_MODEL) for m in mp[:3]]
            xp_t, st_p = _s5_layer(xp.transpose(1, 0, 2), mp_t[0], mp_t[1], mp_t[2], norm_mix[i],
                                   jnp.zeros((bp, SSM_STATE_W), F32), sw, ssm_d[j], wglu, t_chunk=32)
            xp = xp_t.transpose(1, 0, 2)
            ms_t = [m.reshape(1, bs, D_MODEL) for m in ms_b[:3]]
            xs_t, st_s = _s5_layer(xs.transpose(1, 0, 2), ms_t[0], ms_t[1], ms_t[2], norm_mix[i],
                                   _state_to_rows(state_ssm[j]), sw, ssm_d[j], wglu, t_chunk=ts)
            xs = xs_t.transpose(1, 0, 2)
            ssm_p.append(_rows_to_state(st_p))
            ssm_s.append(_rows_to_state(st_s))
        else:
            wqkv = attn_w_qkv[j].astype(BF16)
            wo = attn_w_o[j].astype(BF16)
            k, v, q0, q1, k1, v1, q2, k2, v2 = _qkv_dilated(xp, mp[0], mp[1], norm_mix[i], wqkv, attn_q_norm[j],
                                                            attn_k_norm[j], tab_p, ones_att, t_tile=512)
            res = [_band_attention(q0, k, v, 1, 1, 0, N_GROUPS, 0),
                   _band_attention(q1, k1, v1, DILATIONS[1], 1, 0, 1, 0),
                   _band_attention(q2, k2, v2, DILATIONS[2], 1, 0, 1, 0)]
            outs, lses = zip(*res)
            xp = _merge(outs, lses, xp, mp[2], wo, t_tile=512, dils=DILATIONS)
            for g in range(N_GROUPS):
                keep = min(WINDOWS[g], sp)
                gs = slice(g * ATT, (g + 1) * ATT)
                kv_p[g].append(jnp.stack([k[:, sp - keep:, gs], v[:, sp - keep:, gs]], axis=1)
                               .reshape(bp, 2, keep, N_HEADS, HEAD_DIM))
            xs_rows = xs.reshape(1, ns, D_MODEL)
            qs, ks, vs = _qkv(xs_rows, ms_rows[0], ms_rows[1], norm_mix[i], wqkv, attn_q_norm[j],
                              attn_k_norm[j], tab_s, ones_att, t_tile=ns)
            qs, ks, vs = (t.reshape(bs, ts, N_GROUPS * ATT) for t in (qs, ks, vs))
            res = _cache_attention(qs, ks, vs, [c.reshape(c.shape[:4] + (ATT,)) for c in caches], j)
            outs_s = [r.reshape(1, ns, ATT) for r in res[:N_GROUPS]]
            lses_s = [r.reshape(1, ns, ATT) for r in res[N_GROUPS:]]
            xs = _merge(outs_s, lses_s, xs_rows, ms_rows[2], wo, t_tile=ns,
                        dils=(1, 1, 1)).reshape(bs, ts, D_MODEL)
            for g in range(N_GROUPS):
                gs = slice(g * ATT, (g + 1) * ATT)
                kv_s[g].append(jnp.stack([ks[:, :, gs], vs[:, :, gs]], axis=1)
                               .reshape(bs, 2, ts, N_HEADS, HEAD_DIM))
        pw = _peer_weights(peer_w_q[i], peer_sub_keys[i])
        xp = _peer(xp.reshape(bp * sp, D_MODEL), mp[3], mp[4], mp[5], norm_ffn[i], pw, peer_tabs, i,
                   t_tile=512).reshape(bp, sp, D_MODEL)
        xs = _peer(xs.reshape(ns, D_MODEL), ms_rows[3], ms_rows[4], ms_rows[5], norm_ffn[i], pw, peer_tabs, i,
                   t_tile=ns).reshape(bs, ts, D_MODEL)
    return (xp, xs,
            jnp.stack(ssm_p), jnp.stack(kv_p[0]), jnp.stack(kv_p[1]), jnp.stack(kv_p[2]),
            jnp.stack(ssm_s), jnp.stack(kv_s[0]), jnp.stack(kv_s[1]), jnp.stack(kv_s[2]))
```

```python
import functools
import math

import jax
import jax.numpy as jnp
import numpy as np
from jax import lax
from jax.experimental import pallas as pl
from jax.experimental.pallas import tpu as pltpu

F32 = jnp.float32
BF16 = jnp.bfloat16

D_MODEL = 1024
DEPTH = 4
PAST_LEN = 8192
EPS = 1e-6
NEG_INF = -1e30

SSM_P = 16
SSM_GROUPS = D_MODEL // SSM_P
SSM_N = 64
SSM_GB = 4
SSM_GB_CH = D_MODEL // SSM_GB
SSM_GB_ST = SSM_GROUPS * SSM_N // SSM_GB
SSM_STATE_W = 2 * SSM_GROUPS * SSM_N

N_HEADS = 8
HEAD_DIM = 64
ATT = N_HEADS * HEAD_DIM
WINDOWS = (128, 512, 2048)
DILATIONS = (1, 4, 16)
N_GROUPS = 3
N_BACK = 128
ROPE_DIM = HEAD_DIM // 4
ROPE_THETA = 500000.0

PEER_HEADS = 8
PEER_N_KEYS = 128
PEER_HALF = 64
PEER_TOPK = 16
PEER_SEL_TOK = 128
PEER_EXP_TILE = 1024
PEER_I1_TILE = PEER_EXP_TILE // PEER_N_KEYS
PEER_CHUNK = 256

VMEM_LIMIT = 56 * 1024 * 1024


def _cparams(sem):
    return pltpu.CompilerParams(dimension_semantics=sem, vmem_limit_bytes=VMEM_LIMIT)


def _modulate(x, gain, shift, scale):
    ms = jnp.mean(x * x, axis=-1, keepdims=True)
    return x * lax.rsqrt(ms + EPS) * gain * (1.0 + scale) + shift


def _seg_mean(x2, ones_bd):
    hi = x2.astype(BF16)
    lo = (x2 - hi.astype(F32)).astype(BF16)
    s = jnp.dot(hi, ones_bd, preferred_element_type=F32) + jnp.dot(lo, ones_bd, preferred_element_type=F32)
    return s * (1.0 / PEER_HALF)


def _ones_blockdiag(n):
    seg = np.arange(n) // 64
    return jnp.asarray((seg[:, None] == seg[None, :]).astype(np.float32), dtype=BF16)


def _adaln_kernel(c_ref, w_ref, b_ref, o_ref):
    s = jax.nn.silu(c_ref[...])
    o_ref[0] = jnp.dot(s.astype(BF16), w_ref[0].astype(BF16), preferred_element_type=F32) + b_ref[0]


def _adaln(c_all, ada_w, ada_b):
    nb = c_all.shape[0]
    nt = 6 * D_MODEL // 1024
    return pl.pallas_call(
        _adaln_kernel,
        grid=(DEPTH, nt),
        in_specs=[
            pl.BlockSpec((nb, D_MODEL), lambda l, j: (0, 0)),
            pl.BlockSpec((1, D_MODEL, 1024), lambda l, j: (l, 0, j)),
            pl.BlockSpec((1, 1, 1024), lambda l, j: (l, 0, j)),
        ],
        out_specs=pl.BlockSpec((1, nb, 1024), lambda l, j: (l, 0, j)),
        out_shape=jax.ShapeDtypeStruct((DEPTH, nb, 6 * D_MODEL), F32),
        compiler_params=_cparams(("parallel", "parallel")),
        name="adaln",
    )(c_all, ada_w, ada_b.reshape(DEPTH, 1, 6 * D_MODEL))


def _s5_kernel(x_ref, shift_ref, scale_ref, gate_ref, gain_ref, h0_ref, bm_ref, cm_ref, are_ref, aim_ref,
               d_ref, wglu_ref, xo_ref, st_ref, bu_scr, state_scr, *, n_batch, n_time):
    i = pl.program_id(0)

    @pl.when(i == 0)
    def _():
        state_scr[...] = h0_ref[...]

    x3 = x_ref[...]
    a0, a1, _ = x3.shape
    rows = a0 * a1
    h3 = _modulate(x3, gain_ref[...], shift_ref[...], scale_ref[...])
    u = h3.reshape(rows, D_MODEL)
    ub = u.astype(BF16)
    for gb in range(SSM_GB):
        bu_scr[:, gb * 2 * SSM_GB_ST:(gb + 1) * 2 * SSM_GB_ST] = jnp.dot(
            ub[:, gb * SSM_GB_CH:(gb + 1) * SSM_GB_CH], bm_ref[gb], preferred_element_type=F32)

    for bb in range(n_batch // 8):
        for gb in range(SSM_GB):
            c_re = gb * 2 * SSM_GB_ST
            c_im = c_re + SSM_GB_ST
            ar = jnp.broadcast_to(are_ref[gb], (8, SSM_GB_ST))
            ai = jnp.broadcast_to(aim_ref[gb], (8, SSM_GB_ST))
            xr = state_scr[bb * 8:(bb + 1) * 8, c_re:c_re + SSM_GB_ST]
            xi = state_scr[bb * 8:(bb + 1) * 8, c_im:c_im + SSM_GB_ST]
            for t in range(n_time):
                rsel = pl.ds(t * n_batch + bb * 8, 8)
                br = bu_scr[rsel, c_re:c_re + SSM_GB_ST]
                bi = bu_scr[rsel, c_im:c_im + SSM_GB_ST]
                nr = ar * xr - ai * xi + br
                ni = ar * xi + ai * xr + bi
                bu_scr[rsel, c_re:c_re + SSM_GB_ST] = nr
                bu_scr[rsel, c_im:c_im + SSM_GB_ST] = ni
                xr, xi = nr, ni
            state_scr[bb * 8:(bb + 1) * 8, c_re:c_re + SSM_GB_ST] = xr
            state_scr[bb * 8:(bb + 1) * 8, c_im:c_im + SSM_GB_ST] = xi

    ys = []
    for gb in range(SSM_GB):
        st = bu_scr[:, gb * 2 * SSM_GB_ST:(gb + 1) * 2 * SSM_GB_ST].astype(BF16)
        ys.append(jnp.dot(st, cm_ref[gb], preferred_element_type=F32))
    y = jnp.concatenate(ys, axis=-1) + d_ref[...] * u
    z = jnp.dot(y.astype(BF16), wglu_ref[...], preferred_element_type=F32)
    out = z[:, :D_MODEL] * jax.nn.sigmoid(z[:, D_MODEL:])
    xo_ref[...] = x3 + gate_ref[...] * out.reshape(a0, a1, D_MODEL)

    @pl.when(i == pl.num_programs(0) - 1)
    def _():
        st_ref[...] = state_scr[...]


def _s5_weights(lam_re, lam_im, log_dt, b_re, b_im, c_re, c_im):
    dt = jnp.exp(log_dt)[:, None]
    mag = jnp.exp(lam_re * dt)
    a_re = mag * jnp.cos(lam_im * dt)
    a_im = mag * jnp.sin(lam_im * dt)
    den = lam_re * lam_re + lam_im * lam_im
    f_re = ((a_re - 1.0) * lam_re + a_im * lam_im) / den
    f_im = (a_im * lam_re - (a_re - 1.0) * lam_im) / den
    bb_re = f_re[..., None] * b_re - f_im[..., None] * b_im
    bb_im = f_re[..., None] * b_im + f_im[..., None] * b_re
    eye = jnp.eye(16, dtype=F32)

    def in_blk(b):
        b4 = b.reshape(SSM_GB, 16, SSM_N, SSM_P)
        return jnp.einsum('bgnp,gh->bgphn', b4, eye).reshape(SSM_GB, SSM_GB_CH, SSM_GB_ST)

    def out_blk(c):
        c4 = c.reshape(SSM_GB, 16, SSM_P, SSM_N)
        return jnp.einsum('bgpn,gh->bgnhp', c4, eye).reshape(SSM_GB, SSM_GB_ST, SSM_GB_CH)

    bm = jnp.concatenate([in_blk(bb_re), in_blk(bb_im)], axis=2).astype(BF16)
    cm = jnp.concatenate([out_blk(c_re), -out_blk(c_im)], axis=1).astype(BF16)
    return bm, cm, a_re.reshape(SSM_GB, 1, SSM_GB_ST), a_im.reshape(SSM_GB, 1, SSM_GB_ST)


def _state_to_rows(st):
    b = st.shape[0]
    return st.reshape(b, 2, SSM_GB, SSM_GB_ST).transpose(0, 2, 1, 3).reshape(b, SSM_STATE_W)


def _rows_to_state(rows):
    b = rows.shape[0]
    return rows.reshape(b, SSM_GB, 2, SSM_GB_ST).transpose(0, 2, 1, 3).reshape(b, 2, SSM_GROUPS, SSM_N)


def _s5_layer(x3, shift, scale, gate, gain, h0_rows, weights, d_skip, w_glu, *, t_chunk):
    bm, cm, a_re, a_im = weights
    seq, n_batch = x3.shape[0], x3.shape[1]
    n_time = t_chunk
    blk = (n_time, n_batch, D_MODEL)
    xmap = lambda i: (i, 0, 0)
    mblk = (1, n_batch, D_MODEL)
    grid = seq // n_time
    rows = n_batch * n_time
    const3 = lambda i: (0, 0, 0)
    const2 = lambda i: (0, 0)
    kern = functools.partial(_s5_kernel, n_batch=n_batch, n_time=n_time)
    return pl.pallas_call(
        kern,
        grid=(grid,),
        in_specs=[
            pl.BlockSpec(blk, xmap),
            pl.BlockSpec(mblk, const3), pl.BlockSpec(mblk, const3), pl.BlockSpec(mblk, const3),
            pl.BlockSpec((1, 1, D_MODEL), const3),
            pl.BlockSpec((n_batch, SSM_STATE_W), const2),
            pl.BlockSpec(bm.shape, const3), pl.BlockSpec(cm.shape, const3),
            pl.BlockSpec(a_re.shape, const3), pl.BlockSpec(a_im.shape, const3),
            pl.BlockSpec((1, D_MODEL), const2),
            pl.BlockSpec((D_MODEL, 2 * D_MODEL), const2),
        ],
        out_specs=[pl.BlockSpec(blk, xmap), pl.BlockSpec((n_batch, SSM_STATE_W), const2)],
        out_shape=[jax.ShapeDtypeStruct(x3.shape, F32), jax.ShapeDtypeStruct((n_batch, SSM_STATE_W), F32)],
        scratch_shapes=[pltpu.VMEM((rows, SSM_STATE_W), F32), pltpu.VMEM((n_batch, SSM_STATE_W), F32)],
        compiler_params=_cparams(("arbitrary",)),
        name="s5_layer",
    )(x3, shift, scale, gate, gain.reshape(1, 1, D_MODEL), h0_rows, bm, cm, a_re, a_im,
      d_skip.reshape(1, D_MODEL), w_glu)


def _qkv_kernel(x_ref, shift_ref, scale_ref, gain_ref, w_ref, qg_ref, kg_ref, rc_ref, rs1_ref, rs2_ref,
                ones_ref, q_ref, k_ref, v_ref):
    h = _modulate(x_ref[0], gain_ref[...], shift_ref[0], scale_ref[0])
    qkv = jnp.dot(h.astype(BF16), w_ref[...], preferred_element_type=F32)
    rc, rs1, rs2 = rc_ref[...], rs1_ref[...], rs2_ref[...]
    ones_bd = ones_ref[...]

    def norm_rope(t, g):
        tn = t * lax.rsqrt(_seg_mean(t * t, ones_bd) + EPS) * g
        return tn * rc + pltpu.roll(tn, ATT - ROPE_DIM // 2, 1) * rs1 + pltpu.roll(tn, ROPE_DIM // 2, 1) * rs2

    q_ref[0] = norm_rope(qkv[:, :ATT], qg_ref[...])
    k_ref[0] = norm_rope(qkv[:, ATT:2 * ATT], kg_ref[...])
    v_ref[0] = qkv[:, 2 * ATT:]


def _dilate(x, scr, out_ref, d):
    t = x.shape[0]
    for kc in range(ATT // 128):
        scr[kc] = x[:, kc * 128:(kc + 1) * 128]
    for r in range(d):
        for kc in range(ATT // 128):
            c0 = r * ATT + kc * 128
            out_ref[0, :, c0:c0 + 128] = scr[kc, pl.ds(r, t // d, stride=d), :]


def _qkv_dil_kernel(x_ref, shift_ref, scale_ref, gain_ref, w_ref, qg_ref, kg_ref, rc_ref, rs1_ref, rs2_ref,
                    ones_ref, kn_ref, vn_ref, q0_ref, q1_ref, k1_ref, v1_ref, q2_ref, k2_ref, v2_ref, scr):
    g = pl.program_id(2)
    h = _modulate(x_ref[0], gain_ref[...], shift_ref[0], scale_ref[0])
    qkv = jnp.dot(h.astype(BF16), w_ref[...], preferred_element_type=F32)
    rc, rs1, rs2 = rc_ref[...], rs1_ref[...], rs2_ref[...]
    ones_bd = ones_ref[...]

    def norm_rope(t, gn):
        tn = t * lax.rsqrt(_seg_mean(t * t, ones_bd) + EPS) * gn
        return tn * rc + pltpu.roll(tn, ATT - ROPE_DIM // 2, 1) * rs1 + pltpu.roll(tn, ROPE_DIM // 2, 1) * rs2

    q = norm_rope(qkv[:, :ATT], qg_ref[...])
    k = norm_rope(qkv[:, ATT:2 * ATT], kg_ref[...])
    v = qkv[:, 2 * ATT:]
    kn_ref[0] = k
    vn_ref[0] = v

    @pl.when(g == 0)
    def _():
        q0_ref[0] = q

    for gi, refs in ((1, (q1_ref, k1_ref, v1_ref)), (2, (q2_ref, k2_ref, v2_ref))):
        @pl.when(g == gi)
        def _():
            for val, ref in zip((q, k, v), refs):
                _dilate(val, scr, ref, DILATIONS[gi])


def _qkv_dilated(x3, shift, scale, gain, w_qkv_bf, q_gain, k_gain, tables, ones_bd, *, t_tile):
    bq, sq, _ = x3.shape
    rc, rs1, rs2 = tables
    qg = jnp.tile(q_gain, N_HEADS).reshape(1, ATT)
    kg = jnp.tile(k_gain, N_HEADS).reshape(1, ATT)
    mmap = lambda b, i, g: (b, 0, 0)
    tmap = lambda b, i, g: (i, 0)
    cmap = lambda b, i, g: (0, 0)
    gmap = lambda b, i, g: (b, i, g)
    bmap = lambda b, i, g: (b, i, 0)
    nat = jax.ShapeDtypeStruct((bq, sq, N_GROUPS * ATT), F32)
    out_shape = [nat, nat, jax.ShapeDtypeStruct((bq, sq, ATT), F32)]
    out_specs = [pl.BlockSpec((1, t_tile, ATT), gmap)] * 2 + [pl.BlockSpec((1, t_tile, ATT), bmap)]
    for gi in (1, 2):
        d = DILATIONS[gi]
        out_shape += [jax.ShapeDtypeStruct((bq, sq // d, d * ATT), F32)] * 3
        out_specs += [pl.BlockSpec((1, t_tile // d, d * ATT), bmap)] * 3
    return pl.pallas_call(
        _qkv_dil_kernel,
        grid=(bq, sq // t_tile, N_GROUPS),
        in_specs=[
            pl.BlockSpec((1, t_tile, D_MODEL), bmap),
            pl.BlockSpec((1, 1, D_MODEL), mmap), pl.BlockSpec((1, 1, D_MODEL), mmap),
            pl.BlockSpec((1, D_MODEL), cmap),
            pl.BlockSpec((D_MODEL, 3 * ATT), lambda b, i, g: (0, g)),
            pl.BlockSpec((1, ATT), cmap), pl.BlockSpec((1, ATT), cmap),
            pl.BlockSpec((t_tile, ATT), tmap), pl.BlockSpec((t_tile, ATT), tmap), pl.BlockSpec((t_tile, ATT), tmap),
            pl.BlockSpec((ATT, ATT), cmap),
        ],
        out_specs=out_specs,
        out_shape=out_shape,
        scratch_shapes=[pltpu.VMEM((ATT // 128, t_tile, 128), F32)],
        compiler_params=_cparams(("parallel", "parallel", "arbitrary")),
        name="qkv_proj_dilated",
    )(x3, shift, scale, gain.reshape(1, D_MODEL), w_qkv_bf, qg, kg, rc, rs1, rs2, ones_bd)


def _rope_tables(pos):
    half = ROPE_DIM // 2
    inv = ROPE_THETA ** (-jnp.arange(half, dtype=F32) / half)
    ang = pos.astype(F32)[:, None] * inv[None, :]
    cos, sin = jnp.cos(ang), jnp.sin(ang)
    lane = np.arange(ATT) % HEAD_DIM
    fidx = lane % half
    first = jnp.asarray(lane < half)
    second = jnp.asarray((lane >= half) & (lane < ROPE_DIM))
    cl, sl = cos[:, fidx], sin[:, fidx]
    rc = jnp.where(first | second, cl, 1.0)
    rs1 = jnp.where(first, -sl, 0.0)
    rs2 = jnp.where(second, sl, 0.0)
    return rc, rs1, rs2


def _qkv(x3, shift, scale, gain, w_qkv_bf, q_gain, k_gain, tables, ones_bd, *, t_tile):
    bq, sq, _ = x3.shape
    mrows = shift.shape[1]
    mt = t_tile if mrows == sq else 1
    mmap = (lambda b, i, g: (b, i, 0)) if mrows == sq else (lambda b, i, g: (b, 0, 0))
    rc, rs1, rs2 = tables
    qg = jnp.tile(q_gain, N_HEADS).reshape(1, ATT)
    kg = jnp.tile(k_gain, N_HEADS).reshape(1, ATT)
    tmap = lambda b, i, g: (i, 0)
    cmap = lambda b, i, g: (0, 0)
    omap = lambda b, i, g: (b, i, g)
    oshape = jax.ShapeDtypeStruct((bq, sq, N_GROUPS * ATT), F32)
    return pl.pallas_call(
        _qkv_kernel,
        grid=(bq, sq // t_tile, N_GROUPS),
        in_specs=[
            pl.BlockSpec((1, t_tile, D_MODEL), lambda b, i, g: (b, i, 0)),
            pl.BlockSpec((1, mt, D_MODEL), mmap), pl.BlockSpec((1, mt, D_MODEL), mmap),
            pl.BlockSpec((1, D_MODEL), cmap),
            pl.BlockSpec((D_MODEL, 3 * ATT), lambda b, i, g: (0, g)),
            pl.BlockSpec((1, ATT), cmap), pl.BlockSpec((1, ATT), cmap),
            pl.BlockSpec((t_tile, ATT), tmap), pl.BlockSpec((t_tile, ATT), tmap), pl.BlockSpec((t_tile, ATT), tmap),
            pl.BlockSpec((ATT, ATT), cmap),
        ],
        out_specs=[pl.BlockSpec((1, t_tile, ATT), omap)] * 3,
        out_shape=[oshape] * 3,
        compiler_params=_cparams(("parallel", "parallel", "arbitrary")),
        name="qkv_proj",
    )(x3, shift, scale, gain.reshape(1, D_MODEL), w_qkv_bf, qg, kg, rc, rs1, rs2, ones_bd)


def _band_kernel(q_ref, kp_ref, kc_ref, vp_ref, vc_ref, o_ref, l_ref):
    i = pl.program_id(2)
    q = q_ref[0].astype(BF16)
    k = jnp.concatenate([kp_ref[0], kc_ref[0]], axis=0).astype(BF16)
    v = jnp.concatenate([vp_ref[0], vc_ref[0]], axis=0).astype(BF16)
    iq = lax.broadcasted_iota(jnp.int32, (N_BACK, 2 * N_BACK), 0)
    ik = lax.broadcasted_iota(jnp.int32, (N_BACK, 2 * N_BACK), 1)
    dist = iq + N_BACK - ik
    mask = (dist >= 0) & (dist <= N_BACK) & ((ik >= N_BACK) | (i > 0))
    for h in range(N_HEADS):
        sl = slice(h * HEAD_DIM, (h + 1) * HEAD_DIM)
        s = lax.dot_general(q[:, sl], k[:, sl], (((1,), (1,)), ((), ())),
                            preferred_element_type=F32) * (HEAD_DIM ** -0.5)
        s = jnp.where(mask, s, NEG_INF)
        m = jnp.max(s, axis=-1, keepdims=True)
        p = jnp.exp(s - m)
        den = jnp.sum(p, axis=-1, keepdims=True)
        o = jnp.dot(p.astype(BF16), v[:, sl], preferred_element_type=F32) / den
        o_ref[0, :, sl] = o
        l_ref[0, :, sl] = jnp.broadcast_to(m + jnp.log(den), (N_BACK, HEAD_DIM))


def _band_attention(qv, kv, vv, d, qcols, qcol0, kcols, kcol0):
    b, sub, _ = qv.shape
    nb = sub // N_BACK
    qcur = lambda bi, r, i: (bi, i, r * qcols + qcol0)
    cur = lambda bi, r, i: (bi, i, r * kcols + kcol0)
    prev = lambda bi, r, i: (bi, jnp.maximum(i - 1, 0), r * kcols + kcol0)
    omap = lambda bi, r, i: (bi, i, r)
    blk = (1, N_BACK, ATT)
    oshape = jax.ShapeDtypeStruct((b, sub, d * ATT), F32)
    return pl.pallas_call(
        _band_kernel,
        grid=(b, d, nb),
        in_specs=[pl.BlockSpec(blk, qcur), pl.BlockSpec(blk, prev), pl.BlockSpec(blk, cur),
                  pl.BlockSpec(blk, prev), pl.BlockSpec(blk, cur)],
        out_specs=[pl.BlockSpec(blk, omap), pl.BlockSpec(blk, omap)],
        out_shape=[oshape, oshape],
        compiler_params=_cparams(("parallel", "parallel", "arbitrary")),
        name="band_attention",
    )(qv, kv, kv, vv, vv)


def _cache_attn_kernel(q_ref, k_ref, v_ref, c0_ref, c1_ref, c2_ref, *out_refs):
    n_new = q_ref.shape[1]
    rows = n_new * N_HEADS
    lane_head = lax.broadcasted_iota(jnp.int32, (N_HEADS, ATT), 1) // HEAD_DIM
    head_mask = (lane_head == lax.broadcasted_iota(jnp.int32, (N_HEADS, ATT), 0)).astype(F32)
    row_t = lax.broadcasted_iota(jnp.int32, (rows, 1), 0) // N_HEADS
    caches = (c0_ref, c1_ref, c2_ref)
    for g in range(N_GROUPS):
        d, win = DILATIONS[g], WINDOWS[g]
        gs = slice(g * ATT, (g + 1) * ATT)
        q = q_ref[0][:, gs]
        kn = k_ref[0][:, gs]
        vn = v_ref[0][:, gs]
        kc = caches[g][0, 0, 0].astype(BF16)
        vc = caches[g][0, 0, 1].astype(BF16)
        qbd = jnp.concatenate([q[t:t + 1, :] * head_mask for t in range(n_new)], axis=0)
        s_c = lax.dot_general(qbd.astype(BF16), kc, (((1,), (1,)), ((), ())),
                              preferred_element_type=F32) * (HEAD_DIM ** -0.5)
        col = lax.broadcasted_iota(jnp.int32, (rows, win), 1)
        valid_c = (col >= row_t) & (((col - row_t) & (d - 1)) == 0)
        s_c = jnp.where(valid_c, s_c, NEG_INF)
        s_n = []
        for t2 in range(n_new):
            sn = jnp.sum(qbd * kn[t2:t2 + 1, :], axis=-1, keepdims=True) * (HEAD_DIM ** -0.5)
            valid_n = (row_t >= t2) & (((row_t - t2) & (d - 1)) == 0)
            s_n.append(jnp.where(valid_n, sn, NEG_INF))
        m = jnp.max(s_c, axis=-1, keepdims=True)
        for sn in s_n:
            m = jnp.maximum(m, sn)
        p_c = jnp.exp(s_c - m)
        den = jnp.sum(p_c, axis=-1, keepdims=True)
        o = jnp.dot(p_c.astype(BF16), vc, preferred_element_type=F32)
        for t2 in range(n_new):
            p_n = jnp.exp(s_n[t2] - m)
            den = den + p_n
            o = o + p_n * vn[t2:t2 + 1, :]
        o = o / den
        lse = m + jnp.log(den)
        for t in range(n_new):
            rs = slice(t * N_HEADS, (t + 1) * N_HEADS)
            out_refs[g][0, t:t + 1, :] = jnp.sum(o[rs] * head_mask, axis=0, keepdims=True)
            out_refs[N_GROUPS + g][0, t:t + 1, :] = jnp.sum(lse[rs] * head_mask, axis=0, keepdims=True)


def _cache_attention(q, k, v, caches, layer):
    bd, t, w = q.shape
    xmap = lambda b: (b, 0, 0)
    cmap = lambda b: (layer, b, 0, 0, 0)
    oshape = jax.ShapeDtypeStruct((bd, t, ATT), F32)
    return pl.pallas_call(
        _cache_attn_kernel,
        grid=(bd,),
        in_specs=[pl.BlockSpec((1, t, w), xmap)] * 3
        + [pl.BlockSpec((1, 1, 2, WINDOWS[g], ATT), cmap) for g in range(N_GROUPS)],
        out_specs=[pl.BlockSpec((1, t, ATT), xmap)] * (2 * N_GROUPS),
        out_shape=[oshape] * (2 * N_GROUPS),
        compiler_params=_cparams(("parallel",)),
        name="cache_attention",
    )(q, k, v, *caches)


def _undilate(ref, scr, d):
    if d == 1:
        return ref[0]
    rows = ref.shape[1]
    for r in range(d):
        for kc in range(ATT // 128):
            c0 = r * ATT + kc * 128
            scr[kc, pl.ds(r, rows, stride=d), :] = ref[0, :, c0:c0 + 128]
    return jnp.concatenate([scr[kc] for kc in range(ATT // 128)], axis=1)


def _merge_kernel(o0, o1, o2, l0, l1, l2, x_ref, gate_ref, wo_ref, xo_ref, scr, *, dils):
    la, lb, lc = [_undilate(r, scr, d) for r, d in zip((l0, l1, l2), dils)]
    m = jnp.maximum(jnp.maximum(la, lb), lc)
    ea, eb, ec = jnp.exp(la - m), jnp.exp(lb - m), jnp.exp(lc - m)
    den = ea + eb + ec
    o = (ea / den) * _undilate(o0, scr, dils[0])
    o = o + (eb / den) * _undilate(o1, scr, dils[1])
    o = o + (ec / den) * _undilate(o2, scr, dils[2])
    out = jnp.dot(o.astype(BF16), wo_ref[...], preferred_element_type=F32)
    xo_ref[0] = x_ref[0] + gate_ref[0] * out


def _merge(outs, lses, x3, gate, w_o_bf, *, t_tile, dils):
    bq, sq, _ = x3.shape
    mrows = gate.shape[1]
    mt = t_tile if mrows == sq else 1
    mmap = (lambda b, i: (b, i, 0)) if mrows == sq else (lambda b, i: (b, 0, 0))
    tmap = lambda b, i: (b, i, 0)
    ablk = [pl.BlockSpec((1, t_tile // d, d * ATT), tmap) for d in dils]
    return pl.pallas_call(
        functools.partial(_merge_kernel, dils=dils),
        grid=(bq, sq // t_tile),
        in_specs=ablk + ablk + [pl.BlockSpec((1, t_tile, D_MODEL), tmap), pl.BlockSpec((1, mt, D_MODEL), mmap),
                                pl.BlockSpec((ATT, D_MODEL), lambda b, i: (0, 0))],
        out_specs=pl.BlockSpec((1, t_tile, D_MODEL), tmap),
        out_shape=jax.ShapeDtypeStruct(x3.shape, F32),
        scratch_shapes=[pltpu.VMEM((ATT // 128, t_tile, 128), F32)],
        compiler_params=_cparams(("parallel", "parallel")),
        name="attn_merge",
    )(*outs, *lses, x3, gate, w_o_bf)


def _top16(s, exact):
    vals = s
    rank = jnp.full(s.shape, float(PEER_N_KEYS), F32)
    key = lax.broadcasted_iota(jnp.int32, s.shape, 0).astype(F32) if exact else None
    tops = []
    for r in range(PEER_TOPK):
        m = jnp.max(vals, axis=0, keepdims=True)
        hit = vals == m
        if exact:
            first = jnp.min(jnp.where(hit, key, float(PEER_N_KEYS)), axis=0, keepdims=True)
            hit = key == first
        rank = jnp.where(hit, float(r), rank)
        vals = jnp.where(hit, -jnp.inf, vals)
        tops.append(m)
    return tops, rank


def _count_taken(rank):
    return jnp.sum(jnp.where(rank < float(PEER_N_KEYS), 1.0, 0.0), axis=0, keepdims=True)


def _route_head(s1, s2, exact):
    n_tok = s1.shape[1]
    top1, rank1 = _top16(s1, exact)
    top2, rank2 = _top16(s2, exact)
    row = lax.broadcasted_iota(jnp.int32, (PEER_TOPK, n_tok), 0)
    v1 = jnp.zeros((PEER_TOPK, n_tok), F32)
    for r in range(PEER_TOPK):
        v1 = jnp.where(row == r, top1[r], v1)
    best = top1[0] + top2[0]
    zsum = jnp.zeros_like(best)
    if exact:
        cands = [v1 + top2[b] for b in range(PEER_TOPK)]
        flat = [(row * PEER_TOPK + b).astype(F32) for b in range(PEER_TOPK)]
    else:
        v2 = jnp.zeros((PEER_TOPK, n_tok), F32)
        for r in range(PEER_TOPK):
            v2 = jnp.where(row == r, top2[r], v2)
        cands = [v1[:8] + top2[b] for b in range(8)] + [v1[8:] + top2[0], top1[0] + v2[8:]]
    taken = [jnp.zeros_like(c) for c in cands]
    for _ in range(PEER_TOPK):
        m = cands[0]
        for c in cands[1:]:
            m = jnp.maximum(m, c)
        m = jnp.max(m, axis=0, keepdims=True)
        if exact:
            fmin = jnp.full_like(cands[0], float(PEER_TOPK * PEER_TOPK))
            for c, f in zip(cands, flat):
                fmin = jnp.minimum(fmin, jnp.where(c == m, f, float(PEER_TOPK * PEER_TOPK)))
            fmin = jnp.min(fmin, axis=0, keepdims=True)
        for j in range(len(cands)):
            hit = (flat[j] == fmin) if exact else (cands[j] == m)
            taken[j] = jnp.where(hit, 1.0, taken[j])
            cands[j] = jnp.where(hit, -jnp.inf, cands[j])
        zsum = zsum + jnp.exp(m - best)
    tie = jnp.zeros_like(best)
    if exact:
        cnt = taken[0]
        for j in range(1, PEER_TOPK):
            cnt = cnt + taken[j]
    else:
        cnt = taken[0]
        for j in range(1, 8):
            cnt = cnt + taken[j]
        cnt = jnp.concatenate([cnt, taken[8]], axis=0)
        cnt = cnt + jnp.where(row == 0, jnp.sum(taken[9], axis=0, keepdims=True), 0.0)
        n_pairs = jnp.sum(cnt, axis=0, keepdims=True)
        bad = ((_count_taken(rank1) != float(PEER_TOPK)) | (_count_taken(rank2) != float(PEER_TOPK))
               | (n_pairs != float(PEER_TOPK)))
        tie = jnp.where(bad, 1.0, 0.0)
    n2 = jnp.where(rank2 < 1.0, jnp.sum(cnt[8:], axis=0, keepdims=True), 0.0)
    for a in range(8):
        n2 = n2 + jnp.where(rank2 < cnt[a:a + 1], 1.0, 0.0)
    return rank1, jnp.exp(s1 - top1[0]), n2, jnp.exp(s2 - top2[0]) / zsum, tie


def _peer_route_kernel(x_ref, shift_ref, scale_ref, gain_ref, wq_ref, sk_ref, ones_ref,
                       xt_ref, r1_ref, c1_ref, nf_ref, s_scr, tie_scr):
    hm = _modulate(x_ref[...], gain_ref[...], shift_ref[0], scale_ref[0])
    hb = hm.astype(BF16)
    xt_ref[...] = hm.T.astype(BF16)
    q = jnp.dot(hb, wq_ref[...], preferred_element_type=F32)
    nt_dims = (((1,), (1,)), ((), ()))
    q2 = q * q
    q2_hi = q2.astype(BF16)
    q2_lo = (q2 - q2_hi.astype(F32)).astype(BF16)
    ms_t = (lax.dot_general(ones_ref[...], q2_hi, nt_dims, preferred_element_type=F32)
            + lax.dot_general(ones_ref[...], q2_lo, nt_dims, preferred_element_type=F32)) * (1.0 / PEER_HALF)
    inv_t = lax.rsqrt(ms_t + EPS)
    qb = q.astype(BF16)
    for hc in range(2 * PEER_HEADS):
        raw = lax.dot_general(sk_ref[hc], qb[:, hc * PEER_HALF:(hc + 1) * PEER_HALF], nt_dims,
                              preferred_element_type=F32)
        s_scr[hc * PEER_N_KEYS:(hc + 1) * PEER_N_KEYS, :] = raw * inv_t[hc:hc + 1, :]

    def head(h, tie, exact):
        base = pl.multiple_of(h * 2 * PEER_N_KEYS, 2 * PEER_N_KEYS)
        s1 = s_scr[pl.ds(base, PEER_N_KEYS), :]
        s2 = s_scr[pl.ds(base + PEER_N_KEYS, PEER_N_KEYS), :]
        rank1, c1, n2, f2, tie_h = _route_head(s1, s2, exact)
        r1_ref[0, h] = rank1
        c1_ref[0, h] = c1
        nf_ref[0, h, :, 0] = pltpu.bitcast(n2.astype(BF16), jnp.int32).reshape(PEER_N_KEYS // 16, 8, PEER_SEL_TOK)
        nf_ref[0, h, :, 1] = pltpu.bitcast(f2.astype(BF16), jnp.int32).reshape(PEER_N_KEYS // 16, 8, PEER_SEL_TOK)
        if not exact:
            tie_scr[h] = tie_h
        return jnp.maximum(tie, tie_h)

    no_tie = jnp.zeros((1, s_scr.shape[1]), F32)
    tie = lax.fori_loop(0, PEER_HEADS, functools.partial(head, exact=False), no_tie, unroll=8)

    @pl.when(jnp.max(tie) > 0.0)
    def _():
        def redo(h, carry):
            @pl.when(jnp.max(tie_scr[h]) > 0.0)
            def _():
                head(h, no_tie, True)
            return carry

        lax.fori_loop(0, PEER_HEADS, redo, 0)


def _peer_route(x2, shift, scale, gain, wq_bf, sk_bf, ones_t):
    n = x2.shape[0]
    tt = PEER_SEL_TOK
    nt = n // tt
    per_mod = nt // shift.shape[0]
    mrows = shift.shape[1]
    if mrows == 1:
        mblk, mmap = (1, 1, D_MODEL), (lambda i: (i // per_mod, 0, 0))
    else:
        mblk, mmap = (1, tt, D_MODEL), (lambda i: (0, i, 0))
    cmap = lambda i: (0, 0)
    nrt = PEER_N_KEYS // 16
    hshape = jax.ShapeDtypeStruct((nt, PEER_HEADS, PEER_N_KEYS, tt), F32)
    hblk = pl.BlockSpec((1, PEER_HEADS, PEER_N_KEYS, tt), lambda i: (i, 0, 0, 0))
    nfshape = jax.ShapeDtypeStruct((nt, PEER_HEADS, nrt, 2, 8, tt), jnp.int32)
    nfblk = pl.BlockSpec((1, PEER_HEADS, nrt, 2, 8, tt), lambda i: (i, 0, 0, 0, 0, 0))
    return pl.pallas_call(
        _peer_route_kernel,
        grid=(nt,),
        in_specs=[
            pl.BlockSpec((tt, D_MODEL), lambda i: (i, 0)),
            pl.BlockSpec(mblk, mmap), pl.BlockSpec(mblk, mmap),
            pl.BlockSpec((1, D_MODEL), cmap),
            pl.BlockSpec((D_MODEL, D_MODEL), cmap),
            pl.BlockSpec((2 * PEER_HEADS, PEER_N_KEYS, PEER_HALF), lambda i: (0, 0, 0)),
            pl.BlockSpec((2 * PEER_HEADS, D_MODEL), cmap),
        ],
        out_specs=[pl.BlockSpec((D_MODEL, tt), lambda i: (0, i)), hblk, hblk, nfblk],
        out_shape=[jax.ShapeDtypeStruct((D_MODEL, n), BF16), hshape, hshape, nfshape],
        scratch_shapes=[pltpu.VMEM((2 * PEER_HEADS * PEER_N_KEYS, tt), F32), pltpu.VMEM((PEER_HEADS, 1, tt), F32)],
        compiler_params=_cparams(("parallel",)),
        name="peer_route",
    )(x2, shift, scale, gain.reshape(1, D_MODEL), wq_bf, sk_bf, ones_t)


def _gelu(x):
    return 0.5 * x * (1.0 + lax.erf(x * np.float32(math.sqrt(0.5))))


def _peer_expert_kernel(xt_ref, u_ref, un_ref, vt_ref, vp_ref, r1_ref, c1_ref, nf_ref, x_ref, gate_ref, xo_ref,
                        act_scr, w_scr, acc_scr, *, t_tile):
    e = pl.program_id(1)
    nq = PEER_EXP_TILE // PEER_CHUNK
    last = slice((nq - 1) * PEER_CHUNK, nq * PEER_CHUNK)

    @pl.when(e == 0)
    def _():
        acc_scr[...] = jnp.zeros_like(acc_scr)
        w_scr[nq - 1] = jnp.zeros((PEER_CHUNK, t_tile), BF16)
        act_scr[0] = jnp.dot(u_ref[0, :PEER_CHUNK, :], xt_ref[...], preferred_element_type=F32)

    def stage_a(q):
        qs = slice(q * PEER_CHUNK, (q + 1) * PEER_CHUNK)
        act_scr[q] = jnp.dot(u_ref[0, qs, :], xt_ref[...], preferred_element_type=F32)

    def stage_a_next():
        act_scr[0] = jnp.dot(un_ref[0], xt_ref[...], preferred_element_type=F32)

    def stage_b(q):
        for i1l in range(PEER_CHUNK // PEER_N_KEYS):
            i1 = q * (PEER_CHUNK // PEER_N_KEYS) + i1l
            for lc in range(t_tile // PEER_SEL_TOK):
                ls = slice(lc * PEER_SEL_TOK, (lc + 1) * PEER_SEL_TOK)
                r1 = [jnp.broadcast_to(r1_ref[lc, h, i1:i1 + 1, :], (16, PEER_SEL_TOK)).astype(BF16)
                      for h in range(PEER_HEADS)]
                c1 = [jnp.broadcast_to(c1_ref[lc, h, i1:i1 + 1, :], (16, PEER_SEL_TOK)).astype(BF16)
                      for h in range(PEER_HEADS)]
                for j in range(PEER_N_KEYS // 16):
                    g = jnp.zeros((16, PEER_SEL_TOK), BF16)
                    for h in range(PEER_HEADS):
                        n2 = pltpu.bitcast(nf_ref[lc, h, j, 0], BF16)
                        f2 = pltpu.bitcast(nf_ref[lc, h, j, 1], BF16)
                        g = g + jnp.where(n2 > r1[h], f2, 0.0) * c1[h]
                    rs = slice(i1l * PEER_N_KEYS + j * 16, i1l * PEER_N_KEYS + (j + 1) * 16)
                    w_scr[q, rs, ls] = g * _gelu(act_scr[q, rs, ls]).astype(BF16)

    def stage_c(q):
        qs = slice(q * PEER_CHUNK, (q + 1) * PEER_CHUNK)
        acc_scr[...] += jnp.dot(vt_ref[0, :, qs], w_scr[q], preferred_element_type=F32)

    def stage_c_prev():
        acc_scr[...] += jnp.dot(vp_ref[0], w_scr[nq - 1], preferred_element_type=F32)

    stage_c_prev()
    for q in range(nq):
        if q + 1 < nq:
            stage_a(q + 1)
        else:
            stage_a_next()
        stage_b(q)
        if q > 0:
            stage_c(q - 1)

    @pl.when(e == pl.num_programs(1) - 1)
    def _():
        acc = acc_scr[...] + jnp.dot(vt_ref[0, :, last], w_scr[nq - 1], preferred_element_type=F32)
        xo_ref[...] = x_ref[...] + gate_ref[0] * acc.T


def _peer_expert(xt, u_bf, vt_bf, layer, r1, c1, nf, x2, gate, *, t_tile):
    n = x2.shape[0]
    nt = n // t_tile
    nlc = t_tile // PEER_SEL_TOK
    ne = u_bf.shape[1] // PEER_EXP_TILE
    nq = PEER_EXP_TILE // PEER_CHUNK
    per_mod = max(nt // gate.shape[0], 1)
    mrows = gate.shape[1]
    if mrows == 1:
        mblk, mmap = (1, 1, D_MODEL), (lambda i, e: (i // per_mod, 0, 0))
    else:
        mblk, mmap = (1, t_tile, D_MODEL), (lambda i, e: (0, i, 0))
    i1blk = pl.BlockSpec((nlc, PEER_HEADS, PEER_I1_TILE, PEER_SEL_TOK), lambda i, e: (i, 0, e, 0))
    nfblk = pl.BlockSpec((nlc, PEER_HEADS, PEER_N_KEYS // 16, 2, 8, PEER_SEL_TOK), lambda i, e: (i, 0, 0, 0, 0, 0))
    kern = functools.partial(_peer_expert_kernel, t_tile=t_tile)
    return pl.pallas_call(
        kern,
        grid=(nt, ne),
        in_specs=[
            pl.BlockSpec((D_MODEL, t_tile), lambda i, e: (0, i)),
            pl.BlockSpec((1, PEER_EXP_TILE, D_MODEL), lambda i, e: (layer, e, 0)),
            pl.BlockSpec((1, PEER_CHUNK, D_MODEL), lambda i, e: (layer, jnp.minimum(e + 1, ne - 1) * nq, 0)),
            pl.BlockSpec((1, D_MODEL, PEER_EXP_TILE), lambda i, e: (layer, 0, e)),
            pl.BlockSpec((1, D_MODEL, PEER_CHUNK), lambda i, e: (layer, 0, jnp.maximum(e, 1) * nq - 1)),
            i1blk, i1blk, nfblk,
            pl.BlockSpec((t_tile, D_MODEL), lambda i, e: (i, 0)),
            pl.BlockSpec(mblk, mmap),
        ],
        out_specs=pl.BlockSpec((t_tile, D_MODEL), lambda i, e: (i, 0)),
        out_shape=jax.ShapeDtypeStruct(x2.shape, F32),
        scratch_shapes=[pltpu.VMEM((PEER_EXP_TILE // PEER_CHUNK, PEER_CHUNK, t_tile), F32),
                        pltpu.VMEM((PEER_EXP_TILE // PEER_CHUNK, PEER_CHUNK, t_tile), BF16),
                        pltpu.VMEM((D_MODEL, t_tile), F32)],
        compiler_params=_cparams(("parallel", "arbitrary")),
        name="peer_expert",
    )(xt, u_bf, u_bf, vt_bf, vt_bf, r1, c1, nf, x2, gate)


def _peer(x2, shift, scale, gate, gain, pw, tables, layer, *, t_tile):
    wq_bf, sk_bf, ones_t = pw
    u_bf, vt_bf = tables
    xt, r1, c1, nf = _peer_route(x2, shift, scale, gain, wq_bf, sk_bf, ones_t)
    return _peer_expert(xt, u_bf, vt_bf, layer, r1, c1, nf, x2, gate, t_tile=t_tile)


def _peer_weights(w_q, sub_keys):
    hc = 2 * PEER_HEADS
    seg = np.arange(D_MODEL) // PEER_HALF
    ones_t = jnp.asarray((seg[None, :] == np.arange(hc)[:, None]).astype(np.float32), dtype=BF16)
    return (w_q.astype(BF16), sub_keys.reshape(hc, PEER_N_KEYS, PEER_HALF).astype(BF16), ones_t)


def kernel(x_prompt, x_sample, c_prompt, c_sample, state_ssm, cache_kv_w128, cache_kv_w512, cache_kv_w2048,
           norm_mix, norm_ffn, ada_w, ada_b, ssm_lam_re, ssm_lam_im, ssm_log_dt, ssm_b_re, ssm_b_im, ssm_c_re,
           ssm_c_im, ssm_d, ssm_w_glu, attn_w_qkv, attn_q_norm, attn_k_norm, attn_w_o, peer_w_q,
           peer_sub_keys, peer_u, peer_v):
    bp, sp, _ = x_prompt.shape
    bs, ts, _ = x_sample.shape
    ns = bs * ts
    ones_att = _ones_blockdiag(ATT)

    mods = _adaln(jnp.concatenate([c_prompt, c_sample], axis=0), ada_w, ada_b)
    caches = (cache_kv_w128, cache_kv_w512, cache_kv_w2048)
    tab_p = _rope_tables(jnp.arange(sp, dtype=jnp.int32))
    pos_s = PAST_LEN + jnp.arange(ts, dtype=jnp.int32)
    tab_s = tuple(jnp.tile(t, (bs, 1)) for t in _rope_tables(pos_s))

    peer_tabs = (peer_u.astype(BF16), peer_v.astype(BF16).transpose(0, 2, 1))
    xp, xs = x_prompt, x_sample
    ssm_p, ssm_s = [], []
    kv_p = [[] for _ in range(N_GROUPS)]
    kv_s = [[] for _ in range(N_GROUPS)]
    for i in range(DEPTH):
        j = i // 2
        mp = [m.reshape(bp, 1, D_MODEL) for m in jnp.split(mods[i, :bp], 6, axis=-1)]
        ms_b = jnp.split(mods[i, bp:], 6, axis=-1)
        ms_rows = [jnp.repeat(m, ts, axis=0).reshape(1, ns, D_MODEL) for m in ms_b]
        if i % 2 == 0:
            sw = _s5_weights(ssm_lam_re[j], ssm_lam_im[j], ssm_log_dt[j], ssm_b_re[j], ssm_b_im[j],
                             ssm_c_re[j], ssm_c_im[j])
            wglu = ssm_w_glu[j].astype(BF16)
            mp_t = [m.reshape(1, bp, D_MODEL) for m in mp[:3]]
            xp_t, st_p = _s5_layer(xp.transpose(1, 0, 2), mp_t[0], mp_t[1], mp_t[2], norm_mix[i],
                                   jnp.zeros((bp, SSM_STATE_W), F32), sw, ssm_d[j], wglu, t_chunk=32)
            xp = xp_t.transpose(1, 0, 2)
            ms_t = [m.reshape(1, bs, D_MODEL) for m in ms_b[:3]]
            xs_t, st_s = _s5_layer(xs.transpose(1, 0, 2), ms_t[0], ms_t[1], ms_t[2], norm_mix[i],
                                   _state_to_rows(state_ssm[j]), sw, ssm_d[j], wglu, t_chunk=ts)
            xs = xs_t.transpose(1, 0, 2)
            ssm_p.append(_rows_to_state(st_p))
            ssm_s.append(_rows_to_state(st_s))
        else:
            wqkv = attn_w_qkv[j].astype(BF16)
            wo = attn_w_o[j].astype(BF16)
            k, v, q0, q1, k1, v1, q2, k2, v2 = _qkv_dilated(xp, mp[0], mp[1], norm_mix[i], wqkv, attn_q_norm[j],
                                                            attn_k_norm[j], tab_p, ones_att, t_tile=512)
            res = [_band_attention(q0, k, v, 1, 1, 0, N_GROUPS, 0),
                   _band_attention(q1, k1, v1, DILATIONS[1], 1, 0, 1, 0),
                   _band_attention(q2, k2, v2, DILATIONS[2], 1, 0, 1, 0)]
            outs, lses = zip(*res)
            xp = _merge(outs, lses, xp, mp[2], wo, t_tile=512, dils=DILATIONS)
            for g in range(N_GROUPS):
                keep = min(WINDOWS[g], sp)
                gs = slice(g * ATT, (g + 1) * ATT)
                kv_p[g].append(jnp.stack([k[:, sp - keep:, gs], v[:, sp - keep:, gs]], axis=1)
                               .reshape(bp, 2, keep, N_HEADS, HEAD_DIM))
            xs_rows = xs.reshape(1, ns, D_MODEL)
            qs, ks, vs = _qkv(xs_rows, ms_rows[0], ms_rows[1], norm_mix[i], wqkv, attn_q_norm[j],
                              attn_k_norm[j], tab_s, ones_att, t_tile=ns)
            qs, ks, vs = (t.reshape(bs, ts, N_GROUPS * ATT) for t in (qs, ks, vs))
            res = _cache_attention(qs, ks, vs, [c.reshape(c.shape[:4] + (ATT,)) for c in caches], j)
            outs_s = [r.reshape(1, ns, ATT) for r in res[:N_GROUPS]]
            lses_s = [r.reshape(1, ns, ATT) for r in res[N_GROUPS:]]
            xs = _merge(outs_s, lses_s, xs_rows, ms_rows[2], wo, t_tile=ns,
                        dils=(1, 1, 1)).reshape(bs, ts, D_MODEL)
            for g in range(N_GROUPS):
                gs = slice(g * ATT, (g + 1) * ATT)
                kv_s[g].append(jnp.stack([ks[:, :, gs], vs[:, :, gs]], axis=1)
                               .reshape(bs, 2, ts, N_HEADS, HEAD_DIM))
        pw = _peer_weights(peer_w_q[i], peer_sub_keys[i])
        xp = _peer(xp.reshape(bp * sp, D_MODEL), mp[3], mp[4], mp[5], norm_ffn[i], pw, peer_tabs, i,
                   t_tile=512).reshape(bp, sp, D_MODEL)
        xs = _peer(xs.reshape(ns, D_MODEL), ms_rows[3], ms_rows[4], ms_rows[5], norm_ffn[i], pw, peer_tabs, i,
                   t_tile=ns).reshape(bs, ts, D_MODEL)
    return (xp, xs,
            jnp.stack(ssm_p), jnp.stack(kv_p[0]), jnp.stack(kv_p[1]), jnp.stack(kv_p[2]),
            jnp.stack(ssm_s), jnp.stack(kv_s[0]), jnp.stack(kv_s[1]), jnp.stack(kv_s[2]))
```

```python
import functools
import math

import jax
import jax.numpy as jnp
import numpy as np
from jax import lax
from jax.experimental import pallas as pl
from jax.experimental.pallas import tpu as pltpu

F32 = jnp.float32
BF16 = jnp.bfloat16

D_MODEL = 1024
DEPTH = 4
PAST_LEN = 8192
EPS = 1e-6
NEG_INF = -1e30

SSM_P = 16
SSM_GROUPS = D_MODEL // SSM_P
SSM_N = 64
SSM_GB = 4
SSM_GB_CH = D_MODEL // SSM_GB
SSM_GB_ST = SSM_GROUPS * SSM_N // SSM_GB
SSM_STATE_W = 2 * SSM_GROUPS * SSM_N

N_HEADS = 8
HEAD_DIM = 64
ATT = N_HEADS * HEAD_DIM
WINDOWS = (128, 512, 2048)
DILATIONS = (1, 4, 16)
N_GROUPS = 3
N_BACK = 128
ROPE_DIM = HEAD_DIM // 4
ROPE_THETA = 500000.0

PEER_HEADS = 8
PEER_N_KEYS = 128
PEER_HALF = 64
PEER_TOPK = 16
PEER_SEL_TOK = 128
PEER_EXP_TILE = 1024
PEER_I1_TILE = PEER_EXP_TILE // PEER_N_KEYS
PEER_CHUNK = 256

VMEM_LIMIT = 56 * 1024 * 1024


def _cparams(sem):
    return pltpu.CompilerParams(dimension_semantics=sem, vmem_limit_bytes=VMEM_LIMIT)


def _modulate(x, gain, shift, scale):
    ms = jnp.mean(x * x, axis=-1, keepdims=True)
    return x * lax.rsqrt(ms + EPS) * gain * (1.0 + scale) + shift


def _seg_mean(x2, ones_bd):
    hi = x2.astype(BF16)
    lo = (x2 - hi.astype(F32)).astype(BF16)
    s = jnp.dot(hi, ones_bd, preferred_element_type=F32) + jnp.dot(lo, ones_bd, preferred_element_type=F32)
    return s * (1.0 / PEER_HALF)


def _ones_blockdiag(n):
    seg = np.arange(n) // 64
    return jnp.asarray((seg[:, None] == seg[None, :]).astype(np.float32), dtype=BF16)


def _adaln_kernel(c_ref, w_ref, b_ref, o_ref):
    s = jax.nn.silu(c_ref[...])
    o_ref[0] = jnp.dot(s.astype(BF16), w_ref[0].astype(BF16), preferred_element_type=F32) + b_ref[0]


def _adaln(c_all, ada_w, ada_b):
    nb = c_all.shape[0]
    nt = 6 * D_MODEL // 1024
    return pl.pallas_call(
        _adaln_kernel,
        grid=(DEPTH, nt),
        in_specs=[
            pl.BlockSpec((nb, D_MODEL), lambda l, j: (0, 0)),
            pl.BlockSpec((1, D_MODEL, 1024), lambda l, j: (l, 0, j)),
            pl.BlockSpec((1, 1, 1024), lambda l, j: (l, 0, j)),
        ],
        out_specs=pl.BlockSpec((1, nb, 1024), lambda l, j: (l, 0, j)),
        out_shape=jax.ShapeDtypeStruct((DEPTH, nb, 6 * D_MODEL), F32),
        compiler_params=_cparams(("parallel", "parallel")),
        name="adaln",
    )(c_all, ada_w, ada_b.reshape(DEPTH, 1, 6 * D_MODEL))


def _s5_kernel(x_ref, shift_ref, scale_ref, gate_ref, gain_ref, h0_ref, bm_ref, cm_ref, are_ref, aim_ref,
               d_ref, wglu_ref, xo_ref, st_ref, bu_scr, state_scr, *, n_batch, n_time):
    i = pl.program_id(0)

    @pl.when(i == 0)
    def _():
        state_scr[...] = h0_ref[...]

    x3 = x_ref[...]
    a0, a1, _ = x3.shape
    rows = a0 * a1
    h3 = _modulate(x3, gain_ref[...], shift_ref[...], scale_ref[...])
    u = h3.reshape(rows, D_MODEL)
    ub = u.astype(BF16)
    for gb in range(SSM_GB):
        bu_scr[:, gb * 2 * SSM_GB_ST:(gb + 1) * 2 * SSM_GB_ST] = jnp.dot(
            ub[:, gb * SSM_GB_CH:(gb + 1) * SSM_GB_CH], bm_ref[gb], preferred_element_type=F32)

    for bb in range(n_batch // 8):
        for gb in range(SSM_GB):
            c_re = gb * 2 * SSM_GB_ST
            c_im = c_re + SSM_GB_ST
            ar = jnp.broadcast_to(are_ref[gb], (8, SSM_GB_ST))
            ai = jnp.broadcast_to(aim_ref[gb], (8, SSM_GB_ST))
            xr = state_scr[bb * 8:(bb + 1) * 8, c_re:c_re + SSM_GB_ST]
            xi = state_scr[bb * 8:(bb + 1) * 8, c_im:c_im + SSM_GB_ST]
            for t in range(n_time):
                rsel = pl.ds(t * n_batch + bb * 8, 8)
                br = bu_scr[rsel, c_re:c_re + SSM_GB_ST]
                bi = bu_scr[rsel, c_im:c_im + SSM_GB_ST]
                nr = ar * xr - ai * xi + br
                ni = ar * xi + ai * xr + bi
                bu_scr[rsel, c_re:c_re + SSM_GB_ST] = nr
                bu_scr[rsel, c_im:c_im + SSM_GB_ST] = ni
                xr, xi = nr, ni
            state_scr[bb * 8:(bb + 1) * 8, c_re:c_re + SSM_GB_ST] = xr
            state_scr[bb * 8:(bb + 1) * 8, c_im:c_im + SSM_GB_ST] = xi

    ys = []
    for gb in range(SSM_GB):
        st = bu_scr[:, gb * 2 * SSM_GB_ST:(gb + 1) * 2 * SSM_GB_ST].astype(BF16)
        ys.append(jnp.dot(st, cm_ref[gb], preferred_element_type=F32))
    y = jnp.concatenate(ys, axis=-1) + d_ref[...] * u
    z = jnp.dot(y.astype(BF16), wglu_ref[...], preferred_element_type=F32)
    out = z[:, :D_MODEL] * jax.nn.sigmoid(z[:, D_MODEL:])
    xo_ref[...] = x3 + gate_ref[...] * out.reshape(a0, a1, D_MODEL)

    @pl.when(i == pl.num_programs(0) - 1)
    def _():
        st_ref[...] = state_scr[...]


def _s5_weights(lam_re, lam_im, log_dt, b_re, b_im, c_re, c_im):
    dt = jnp.exp(log_dt)[:, None]
    mag = jnp.exp(lam_re * dt)
    a_re = mag * jnp.cos(lam_im * dt)
    a_im = mag * jnp.sin(lam_im * dt)
    den = lam_re * lam_re + lam_im * lam_im
    f_re = ((a_re - 1.0) * lam_re + a_im * lam_im) / den
    f_im = (a_im * lam_re - (a_re - 1.0) * lam_im) / den
    bb_re = f_re[..., None] * b_re - f_im[..., None] * b_im
    bb_im = f_re[..., None] * b_im + f_im[..., None] * b_re
    eye = jnp.eye(16, dtype=F32)

    def in_blk(b):
        b4 = b.reshape(SSM_GB, 16, SSM_N, SSM_P)
        return jnp.einsum('bgnp,gh->bgphn', b4, eye).reshape(SSM_GB, SSM_GB_CH, SSM_GB_ST)

    def out_blk(c):
        c4 = c.reshape(SSM_GB, 16, SSM_P, SSM_N)
        return jnp.einsum('bgpn,gh->bgnhp', c4, eye).reshape(SSM_GB, SSM_GB_ST, SSM_GB_CH)

    bm = jnp.concatenate([in_blk(bb_re), in_blk(bb_im)], axis=2).astype(BF16)
    cm = jnp.concatenate([out_blk(c_re), -out_blk(c_im)], axis=1).astype(BF16)
    return bm, cm, a_re.reshape(SSM_GB, 1, SSM_GB_ST), a_im.reshape(SSM_GB, 1, SSM_GB_ST)


def _state_to_rows(st):
    b = st.shape[0]
    return st.reshape(b, 2, SSM_GB, SSM_GB_ST).transpose(0, 2, 1, 3).reshape(b, SSM_STATE_W)


def _rows_to_state(rows):
    b = rows.shape[0]
    return rows.reshape(b, SSM_GB, 2, SSM_GB_ST).transpose(0, 2, 1, 3).reshape(b, 2, SSM_GROUPS, SSM_N)


def _s5_layer(x3, shift, scale, gate, gain, h0_rows, weights, d_skip, w_glu, *, t_chunk):
    bm, cm, a_re, a_im = weights
    seq, n_batch = x3.shape[0], x3.shape[1]
    n_time = t_chunk
    blk = (n_time, n_batch, D_MODEL)
    xmap = lambda i: (i, 0, 0)
    mblk = (1, n_batch, D_MODEL)
    grid = seq // n_time
    rows = n_batch * n_time
    const3 = lambda i: (0, 0, 0)
    const2 = lambda i: (0, 0)
    kern = functools.partial(_s5_kernel, n_batch=n_batch, n_time=n_time)
    return pl.pallas_call(
        kern,
        grid=(grid,),
        in_specs=[
            pl.BlockSpec(blk, xmap),
            pl.BlockSpec(mblk, const3), pl.BlockSpec(mblk, const3), pl.BlockSpec(mblk, const3),
            pl.BlockSpec((1, 1, D_MODEL), const3),
            pl.BlockSpec((n_batch, SSM_STATE_W), const2),
            pl.BlockSpec(bm.shape, const3), pl.BlockSpec(cm.shape, const3),
            pl.BlockSpec(a_re.shape, const3), pl.BlockSpec(a_im.shape, const3),
            pl.BlockSpec((1, D_MODEL), const2),
            pl.BlockSpec((D_MODEL, 2 * D_MODEL), const2),
        ],
        out_specs=[pl.BlockSpec(blk, xmap), pl.BlockSpec((n_batch, SSM_STATE_W), const2)],
        out_shape=[jax.ShapeDtypeStruct(x3.shape, F32), jax.ShapeDtypeStruct((n_batch, SSM_STATE_W), F32)],
        scratch_shapes=[pltpu.VMEM((rows, SSM_STATE_W), F32), pltpu.VMEM((n_batch, SSM_STATE_W), F32)],
        compiler_params=_cparams(("arbitrary",)),
        name="s5_layer",
    )(x3, shift, scale, gate, gain.reshape(1, 1, D_MODEL), h0_rows, bm, cm, a_re, a_im,
      d_skip.reshape(1, D_MODEL), w_glu)


def _qkv_kernel(x_ref, shift_ref, scale_ref, gain_ref, w_ref, qg_ref, kg_ref, rc_ref, rs1_ref, rs2_ref,
                ones_ref, q_ref, k_ref, v_ref):
    h = _modulate(x_ref[0], gain_ref[...], shift_ref[0], scale_ref[0])
    qkv = jnp.dot(h.astype(BF16), w_ref[...], preferred_element_type=F32)
    rc, rs1, rs2 = rc_ref[...], rs1_ref[...], rs2_ref[...]
    ones_bd = ones_ref[...]

    def norm_rope(t, g):
        tn = t * lax.rsqrt(_seg_mean(t * t, ones_bd) + EPS) * g
        return tn * rc + pltpu.roll(tn, ATT - ROPE_DIM // 2, 1) * rs1 + pltpu.roll(tn, ROPE_DIM // 2, 1) * rs2

    q_ref[0] = norm_rope(qkv[:, :ATT], qg_ref[...])
    k_ref[0] = norm_rope(qkv[:, ATT:2 * ATT], kg_ref[...])
    v_ref[0] = qkv[:, 2 * ATT:]


def _dilate(x, scr, out_ref, d):
    t = x.shape[0]
    for kc in range(ATT // 128):
        scr[kc] = x[:, kc * 128:(kc + 1) * 128]
    for r in range(d):
        for kc in range(ATT // 128):
            c0 = r * ATT + kc * 128
            out_ref[0, :, c0:c0 + 128] = scr[kc, pl.ds(r, t // d, stride=d), :]


def _qkv_dil_kernel(x_ref, shift_ref, scale_ref, gain_ref, w_ref, qg_ref, kg_ref, rc_ref, rs1_ref, rs2_ref,
                    ones_ref, kn_ref, vn_ref, q0_ref, q1_ref, k1_ref, v1_ref, q2_ref, k2_ref, v2_ref, scr):
    g = pl.program_id(2)
    h = _modulate(x_ref[0], gain_ref[...], shift_ref[0], scale_ref[0])
    qkv = jnp.dot(h.astype(BF16), w_ref[...], preferred_element_type=F32)
    rc, rs1, rs2 = rc_ref[...], rs1_ref[...], rs2_ref[...]
    ones_bd = ones_ref[...]

    def norm_rope(t, gn):
        tn = t * lax.rsqrt(_seg_mean(t * t, ones_bd) + EPS) * gn
        return tn * rc + pltpu.roll(tn, ATT - ROPE_DIM // 2, 1) * rs1 + pltpu.roll(tn, ROPE_DIM // 2, 1) * rs2

    q = norm_rope(qkv[:, :ATT], qg_ref[...])
    k = norm_rope(qkv[:, ATT:2 * ATT], kg_ref[...])
    v = qkv[:, 2 * ATT:]
    kn_ref[0] = k
    vn_ref[0] = v

    @pl.when(g == 0)
    def _():
        q0_ref[0] = q

    for gi, refs in ((1, (q1_ref, k1_ref, v1_ref)), (2, (q2_ref, k2_ref, v2_ref))):
        @pl.when(g == gi)
        def _():
            for val, ref in zip((q, k, v), refs):
                _dilate(val, scr, ref, DILATIONS[gi])


def _qkv_dilated(x3, shift, scale, gain, w_qkv_bf, q_gain, k_gain, tables, ones_bd, *, t_tile):
    bq, sq, _ = x3.shape
    rc, rs1, rs2 = tables
    qg = jnp.tile(q_gain, N_HEADS).reshape(1, ATT)
    kg = jnp.tile(k_gain, N_HEADS).reshape(1, ATT)
    mmap = lambda b, i, g: (b, 0, 0)
    tmap = lambda b, i, g: (i, 0)
    cmap = lambda b, i, g: (0, 0)
    gmap = lambda b, i, g: (b, i, g)
    bmap = lambda b, i, g: (b, i, 0)
    nat = jax.ShapeDtypeStruct((bq, sq, N_GROUPS * ATT), F32)
    out_shape = [nat, nat, jax.ShapeDtypeStruct((bq, sq, ATT), F32)]
    out_specs = [pl.BlockSpec((1, t_tile, ATT), gmap)] * 2 + [pl.BlockSpec((1, t_tile, ATT), bmap)]
    for gi in (1, 2):
        d = DILATIONS[gi]
        out_shape += [jax.ShapeDtypeStruct((bq, sq // d, d * ATT), F32)] * 3
        out_specs += [pl.BlockSpec((1, t_tile // d, d * ATT), bmap)] * 3
    return pl.pallas_call(
        _qkv_dil_kernel,
        grid=(bq, sq // t_tile, N_GROUPS),
        in_specs=[
            pl.BlockSpec((1, t_tile, D_MODEL), bmap),
            pl.BlockSpec((1, 1, D_MODEL), mmap), pl.BlockSpec((1, 1, D_MODEL), mmap),
            pl.BlockSpec((1, D_MODEL), cmap),
            pl.BlockSpec((D_MODEL, 3 * ATT), lambda b, i, g: (0, g)),
            pl.BlockSpec((1, ATT), cmap), pl.BlockSpec((1, ATT), cmap),
            pl.BlockSpec((t_tile, ATT), tmap), pl.BlockSpec((t_tile, ATT), tmap), pl.BlockSpec((t_tile, ATT), tmap),
            pl.BlockSpec((ATT, ATT), cmap),
        ],
        out_specs=out_specs,
        out_shape=out_shape,
        scratch_shapes=[pltpu.VMEM((ATT // 128, t_tile, 128), F32)],
        compiler_params=_cparams(("parallel", "parallel", "arbitrary")),
        name="qkv_proj_dilated",
    )(x3, shift, scale, gain.reshape(1, D_MODEL), w_qkv_bf, qg, kg, rc, rs1, rs2, ones_bd)


def _rope_tables(pos):
    half = ROPE_DIM // 2
    inv = ROPE_THETA ** (-jnp.arange(half, dtype=F32) / half)
    ang = pos.astype(F32)[:, None] * inv[None, :]
    cos, sin = jnp.cos(ang), jnp.sin(ang)
    lane = np.arange(ATT) % HEAD_DIM
    fidx = lane % half
    first = jnp.asarray(lane < half)
    second = jnp.asarray((lane >= half) & (lane < ROPE_DIM))
    cl, sl = cos[:, fidx], sin[:, fidx]
    rc = jnp.where(first | second, cl, 1.0)
    rs1 = jnp.where(first, -sl, 0.0)
    rs2 = jnp.where(second, sl, 0.0)
    return rc, rs1, rs2


def _qkv(x3, shift, scale, gain, w_qkv_bf, q_gain, k_gain, tables, ones_bd, *, t_tile):
    bq, sq, _ = x3.shape
    mrows = shift.shape[1]
    mt = t_tile if mrows == sq else 1
    mmap = (lambda b, i, g: (b, i, 0)) if mrows == sq else (lambda b, i, g: (b, 0, 0))
    rc, rs1, rs2 = tables
    qg = jnp.tile(q_gain, N_HEADS).reshape(1, ATT)
    kg = jnp.tile(k_gain, N_HEADS).reshape(1, ATT)
    tmap = lambda b, i, g: (i, 0)
    cmap = lambda b, i, g: (0, 0)
    omap = lambda b, i, g: (b, i, g)
    oshape = jax.ShapeDtypeStruct((bq, sq, N_GROUPS * ATT), F32)
    return pl.pallas_call(
        _qkv_kernel,
        grid=(bq, sq // t_tile, N_GROUPS),
        in_specs=[
            pl.BlockSpec((1, t_tile, D_MODEL), lambda b, i, g: (b, i, 0)),
            pl.BlockSpec((1, mt, D_MODEL), mmap), pl.BlockSpec((1, mt, D_MODEL), mmap),
            pl.BlockSpec((1, D_MODEL), cmap),
            pl.BlockSpec((D_MODEL, 3 * ATT), lambda b, i, g: (0, g)),
            pl.BlockSpec((1, ATT), cmap), pl.BlockSpec((1, ATT), cmap),
            pl.BlockSpec((t_tile, ATT), tmap), pl.BlockSpec((t_tile, ATT), tmap), pl.BlockSpec((t_tile, ATT), tmap),
            pl.BlockSpec((ATT, ATT), cmap),
        ],
        out_specs=[pl.BlockSpec((1, t_tile, ATT), omap)] * 3,
        out_shape=[oshape] * 3,
        compiler_params=_cparams(("parallel", "parallel", "arbitrary")),
        name="qkv_proj",
    )(x3, shift, scale, gain.reshape(1, D_MODEL), w_qkv_bf, qg, kg, rc, rs1, rs2, ones_bd)


def _band_kernel(q_ref, kp_ref, kc_ref, vp_ref, vc_ref, o_ref, l_ref, *, n_sub, n_col):
    i = pl.program_id(2)
    iq = lax.broadcasted_iota(jnp.int32, (N_BACK, 2 * N_BACK), 0)
    ik = lax.broadcasted_iota(jnp.int32, (N_BACK, 2 * N_BACK), 1)
    dist = iq + N_BACK - ik
    band = (dist >= 0) & (dist <= N_BACK)
    band_first = band & ((ik >= N_BACK) | (i > 0))
    for c in range(n_col):
        c0 = c * ATT
        k_all = jnp.concatenate([kp_ref[0, :, c0:c0 + ATT], kc_ref[0, :, c0:c0 + ATT]], axis=0).astype(BF16)
        v_all = jnp.concatenate([vp_ref[0, :, c0:c0 + ATT], vc_ref[0, :, c0:c0 + ATT]], axis=0).astype(BF16)
        for u in range(n_sub):
            rows = slice(u * N_BACK, (u + 1) * N_BACK)
            q = q_ref[0, rows, c0:c0 + ATT].astype(BF16)
            k = k_all[u * N_BACK:(u + 2) * N_BACK]
            v = v_all[u * N_BACK:(u + 2) * N_BACK]
            mask = band_first if u == 0 else band
            for h in range(N_HEADS):
                sl = slice(h * HEAD_DIM, (h + 1) * HEAD_DIM)
                ol = slice(c0 + h * HEAD_DIM, c0 + (h + 1) * HEAD_DIM)
                s = lax.dot_general(q[:, sl], k[:, sl], (((1,), (1,)), ((), ())),
                                    preferred_element_type=F32) * (HEAD_DIM ** -0.5)
                s = jnp.where(mask, s, NEG_INF)
                m = jnp.max(s, axis=-1, keepdims=True)
                p = jnp.exp(s - m)
                den = jnp.sum(p, axis=-1, keepdims=True)
                o = jnp.dot(p.astype(BF16), v[:, sl], preferred_element_type=F32) / den
                o_ref[0, rows, ol] = o
                l_ref[0, rows, ol] = jnp.broadcast_to(m + jnp.log(den), (N_BACK, HEAD_DIM))


def _band_attention(qv, kv, vv, d, kcols, kcol0, *, n_sub, n_col):
    b, sub, _ = qv.shape
    nb = sub // N_BACK
    assert nb % n_sub == 0 and d % n_col == 0 and (n_col == 1 or kcols == 1)
    qcur = lambda bi, r, i: (bi, i, r)
    cur = lambda bi, r, i: (bi, i, r * kcols + kcol0)
    prev = lambda bi, r, i: (bi, jnp.maximum(i * n_sub - 1, 0), r * kcols + kcol0)
    blk = (1, n_sub * N_BACK, n_col * ATT)
    pblk = (1, N_BACK, n_col * ATT)
    oshape = jax.ShapeDtypeStruct((b, sub, d * ATT), F32)
    return pl.pallas_call(
        functools.partial(_band_kernel, n_sub=n_sub, n_col=n_col),
        grid=(b, d // n_col, nb // n_sub),
        in_specs=[pl.BlockSpec(blk, qcur), pl.BlockSpec(pblk, prev), pl.BlockSpec(blk, cur),
                  pl.BlockSpec(pblk, prev), pl.BlockSpec(blk, cur)],
        out_specs=[pl.BlockSpec(blk, qcur), pl.BlockSpec(blk, qcur)],
        out_shape=[oshape, oshape],
        compiler_params=_cparams(("parallel", "parallel", "arbitrary")),
        name="band_attention",
    )(qv, kv, kv, vv, vv)


def _cache_attn_kernel(q_ref, k_ref, v_ref, c0_ref, c1_ref, c2_ref, *out_refs):
    n_new = q_ref.shape[1]
    rows = n_new * N_HEADS
    lane_head = lax.broadcasted_iota(jnp.int32, (N_HEADS, ATT), 1) // HEAD_DIM
    head_mask = (lane_head == lax.broadcasted_iota(jnp.int32, (N_HEADS, ATT), 0)).astype(F32)
    row_t = lax.broadcasted_iota(jnp.int32, (rows, 1), 0) // N_HEADS
    caches = (c0_ref, c1_ref, c2_ref)
    for g in range(N_GROUPS):
        d, win = DILATIONS[g], WINDOWS[g]
        gs = slice(g * ATT, (g + 1) * ATT)
        q = q_ref[0][:, gs]
        kn = k_ref[0][:, gs]
        vn = v_ref[0][:, gs]
        kc = caches[g][0, 0, 0].astype(BF16)
        vc = caches[g][0, 0, 1].astype(BF16)
        qbd = jnp.concatenate([q[t:t + 1, :] * head_mask for t in range(n_new)], axis=0)
        s_c = lax.dot_general(qbd.astype(BF16), kc, (((1,), (1,)), ((), ())),
                              preferred_element_type=F32) * (HEAD_DIM ** -0.5)
        col = lax.broadcasted_iota(jnp.int32, (rows, win), 1)
        valid_c = (col >= row_t) & (((col - row_t) & (d - 1)) == 0)
        s_c = jnp.where(valid_c, s_c, NEG_INF)
        s_n = []
        for t2 in range(n_new):
            sn = jnp.sum(qbd * kn[t2:t2 + 1, :], axis=-1, keepdims=True) * (HEAD_DIM ** -0.5)
            valid_n = (row_t >= t2) & (((row_t - t2) & (d - 1)) == 0)
            s_n.append(jnp.where(valid_n, sn, NEG_INF))
        m = jnp.max(s_c, axis=-1, keepdims=True)
        for sn in s_n:
            m = jnp.maximum(m, sn)
        p_c = jnp.exp(s_c - m)
        den = jnp.sum(p_c, axis=-1, keepdims=True)
        o = jnp.dot(p_c.astype(BF16), vc, preferred_element_type=F32)
        for t2 in range(n_new):
            p_n = jnp.exp(s_n[t2] - m)
            den = den + p_n
            o = o + p_n * vn[t2:t2 + 1, :]
        o = o / den
        lse = m + jnp.log(den)
        for t in range(n_new):
            rs = slice(t * N_HEADS, (t + 1) * N_HEADS)
            out_refs[g][0, t:t + 1, :] = jnp.sum(o[rs] * head_mask, axis=0, keepdims=True)
            out_refs[N_GROUPS + g][0, t:t + 1, :] = jnp.sum(lse[rs] * head_mask, axis=0, keepdims=True)


def _cache_attention(q, k, v, caches, layer):
    bd, t, w = q.shape
    xmap = lambda b: (b, 0, 0)
    cmap = lambda b: (layer, b, 0, 0, 0)
    oshape = jax.ShapeDtypeStruct((bd, t, ATT), F32)
    return pl.pallas_call(
        _cache_attn_kernel,
        grid=(bd,),
        in_specs=[pl.BlockSpec((1, t, w), xmap)] * 3
        + [pl.BlockSpec((1, 1, 2, WINDOWS[g], ATT), cmap) for g in range(N_GROUPS)],
        out_specs=[pl.BlockSpec((1, t, ATT), xmap)] * (2 * N_GROUPS),
        out_shape=[oshape] * (2 * N_GROUPS),
        compiler_params=_cparams(("parallel",)),
        name="cache_attention",
    )(q, k, v, *caches)


def _undilate(ref, scr, d):
    if d == 1:
        return ref[0]
    rows = ref.shape[1]
    for r in range(d):
        for kc in range(ATT // 128):
            c0 = r * ATT + kc * 128
            scr[kc, pl.ds(r, rows, stride=d), :] = ref[0, :, c0:c0 + 128]
    return jnp.concatenate([scr[kc] for kc in range(ATT // 128)], axis=1)


def _merge_kernel(o0, o1, o2, l0, l1, l2, x_ref, gate_ref, wo_ref, xo_ref, scr, *, dils):
    la, lb, lc = [_undilate(r, scr, d) for r, d in zip((l0, l1, l2), dils)]
    m = jnp.maximum(jnp.maximum(la, lb), lc)
    ea, eb, ec = jnp.exp(la - m), jnp.exp(lb - m), jnp.exp(lc - m)
    den = ea + eb + ec
    o = (ea / den) * _undilate(o0, scr, dils[0])
    o = o + (eb / den) * _undilate(o1, scr, dils[1])
    o = o + (ec / den) * _undilate(o2, scr, dils[2])
    out = jnp.dot(o.astype(BF16), wo_ref[...], preferred_element_type=F32)
    xo_ref[0] = x_ref[0] + gate_ref[0] * out


def _merge(outs, lses, x3, gate, w_o_bf, *, t_tile, dils):
    bq, sq, _ = x3.shape
    mrows = gate.shape[1]
    mt = t_tile if mrows == sq else 1
    mmap = (lambda b, i: (b, i, 0)) if mrows == sq else (lambda b, i: (b, 0, 0))
    tmap = lambda b, i: (b, i, 0)
    ablk = [pl.BlockSpec((1, t_tile // d, d * ATT), tmap) for d in dils]
    return pl.pallas_call(
        functools.partial(_merge_kernel, dils=dils),
        grid=(bq, sq // t_tile),
        in_specs=ablk + ablk + [pl.BlockSpec((1, t_tile, D_MODEL), tmap), pl.BlockSpec((1, mt, D_MODEL), mmap),
                                pl.BlockSpec((ATT, D_MODEL), lambda b, i: (0, 0))],
        out_specs=pl.BlockSpec((1, t_tile, D_MODEL), tmap),
        out_shape=jax.ShapeDtypeStruct(x3.shape, F32),
        scratch_shapes=[pltpu.VMEM((ATT // 128, t_tile, 128), F32)],
        compiler_params=_cparams(("parallel", "parallel")),
        name="attn_merge",
    )(*outs, *lses, x3, gate, w_o_bf)


def _top16(s, exact):
    vals = s
    rank = jnp.full(s.shape, float(PEER_N_KEYS), F32)
    key = lax.broadcasted_iota(jnp.int32, s.shape, 0).astype(F32) if exact else None
    tops = []
    for r in range(PEER_TOPK):
        m = jnp.max(vals, axis=0, keepdims=True)
        hit = vals == m
        if exact:
            first = jnp.min(jnp.where(hit, key, float(PEER_N_KEYS)), axis=0, keepdims=True)
            hit = key == first
        rank = jnp.where(hit, float(r), rank)
        vals = jnp.where(hit, -jnp.inf, vals)
        tops.append(m)
    return tops, rank


def _count_taken(rank):
    return jnp.sum(jnp.where(rank < float(PEER_N_KEYS), 1.0, 0.0), axis=0, keepdims=True)


def _route_head(s1, s2, exact):
    n_tok = s1.shape[1]
    top1, rank1 = _top16(s1, exact)
    top2, rank2 = _top16(s2, exact)
    row = lax.broadcasted_iota(jnp.int32, (PEER_TOPK, n_tok), 0)
    v1 = jnp.zeros((PEER_TOPK, n_tok), F32)
    for r in range(PEER_TOPK):
        v1 = jnp.where(row == r, top1[r], v1)
    best = top1[0] + top2[0]
    zsum = jnp.zeros_like(best)
    if exact:
        cands = [v1 + top2[b] for b in range(PEER_TOPK)]
        flat = [(row * PEER_TOPK + b).astype(F32) for b in range(PEER_TOPK)]
    else:
        v2 = jnp.zeros((PEER_TOPK, n_tok), F32)
        for r in range(PEER_TOPK):
            v2 = jnp.where(row == r, top2[r], v2)
        cands = [v1[:8] + top2[b] for b in range(8)] + [v1[8:] + top2[0], top1[0] + v2[8:]]
    taken = [jnp.zeros_like(c) for c in cands]
    for _ in range(PEER_TOPK):
        m = cands[0]
        for c in cands[1:]:
            m = jnp.maximum(m, c)
        m = jnp.max(m, axis=0, keepdims=True)
        if exact:
            fmin = jnp.full_like(cands[0], float(PEER_TOPK * PEER_TOPK))
            for c, f in zip(cands, flat):
                fmin = jnp.minimum(fmin, jnp.where(c == m, f, float(PEER_TOPK * PEER_TOPK)))
            fmin = jnp.min(fmin, axis=0, keepdims=True)
        for j in range(len(cands)):
            hit = (flat[j] == fmin) if exact else (cands[j] == m)
            taken[j] = jnp.where(hit, 1.0, taken[j])
            cands[j] = jnp.where(hit, -jnp.inf, cands[j])
        zsum = zsum + jnp.exp(m - best)
    tie = jnp.zeros_like(best)
    if exact:
        cnt = taken[0]
        for j in range(1, PEER_TOPK):
            cnt = cnt + taken[j]
    else:
        cnt = taken[0]
        for j in range(1, 8):
            cnt = cnt + taken[j]
        cnt = jnp.concatenate([cnt, taken[8]], axis=0)
        cnt = cnt + jnp.where(row == 0, jnp.sum(taken[9], axis=0, keepdims=True), 0.0)
        n_pairs = jnp.sum(cnt, axis=0, keepdims=True)
        bad = ((_count_taken(rank1) != float(PEER_TOPK)) | (_count_taken(rank2) != float(PEER_TOPK))
               | (n_pairs != float(PEER_TOPK)))
        tie = jnp.where(bad, 1.0, 0.0)
    n2 = jnp.where(rank2 < 1.0, jnp.sum(cnt[8:], axis=0, keepdims=True), 0.0)
    for a in range(8):
        n2 = n2 + jnp.where(rank2 < cnt[a:a + 1], 1.0, 0.0)
    return rank1, jnp.exp(s1 - top1[0]), n2, jnp.exp(s2 - top2[0]) / zsum, tie


def _peer_route_kernel(x_ref, shift_ref, scale_ref, gain_ref, wq_ref, sk_ref, ones_ref,
                       xt_ref, r1_ref, c1_ref, nf_ref, s_scr, tie_scr):
    hm = _modulate(x_ref[...], gain_ref[...], shift_ref[0], scale_ref[0])
    hb = hm.astype(BF16)
    xt_ref[...] = hm.T.astype(BF16)
    q = jnp.dot(hb, wq_ref[...], preferred_element_type=F32)
    nt_dims = (((1,), (1,)), ((), ()))
    q2 = q * q
    q2_hi = q2.astype(BF16)
    q2_lo = (q2 - q2_hi.astype(F32)).astype(BF16)
    ms_t = (lax.dot_general(ones_ref[...], q2_hi, nt_dims, preferred_element_type=F32)
            + lax.dot_general(ones_ref[...], q2_lo, nt_dims, preferred_element_type=F32)) * (1.0 / PEER_HALF)
    inv_t = lax.rsqrt(ms_t + EPS)
    qb = q.astype(BF16)
    for hc in range(2 * PEER_HEADS):
        raw = lax.dot_general(sk_ref[hc], qb[:, hc * PEER_HALF:(hc + 1) * PEER_HALF], nt_dims,
                              preferred_element_type=F32)
        s_scr[hc * PEER_N_KEYS:(hc + 1) * PEER_N_KEYS, :] = raw * inv_t[hc:hc + 1, :]

    def head(h, tie, exact):
        base = pl.multiple_of(h * 2 * PEER_N_KEYS, 2 * PEER_N_KEYS)
        s1 = s_scr[pl.ds(base, PEER_N_KEYS), :]
        s2 = s_scr[pl.ds(base + PEER_N_KEYS, PEER_N_KEYS), :]
        rank1, c1, n2, f2, tie_h = _route_head(s1, s2, exact)
        r1_ref[0, h] = rank1
        c1_ref[0, h] = c1
        nf_ref[0, h, :, 0] = pltpu.bitcast(n2.astype(BF16), jnp.int32).reshape(PEER_N_KEYS // 16, 8, PEER_SEL_TOK)
        nf_ref[0, h, :, 1] = pltpu.bitcast(f2.astype(BF16), jnp.int32).reshape(PEER_N_KEYS // 16, 8, PEER_SEL_TOK)
        if not exact:
            tie_scr[h] = tie_h
        return jnp.maximum(tie, tie_h)

    no_tie = jnp.zeros((1, s_scr.shape[1]), F32)
    tie = lax.fori_loop(0, PEER_HEADS, functools.partial(head, exact=False), no_tie, unroll=8)

    @pl.when(jnp.max(tie) > 0.0)
    def _():
        def redo(h, carry):
            @pl.when(jnp.max(tie_scr[h]) > 0.0)
            def _():
                head(h, no_tie, True)
            return carry

        lax.fori_loop(0, PEER_HEADS, redo, 0)


def _peer_route(x2, shift, scale, gain, wq_bf, sk_bf, ones_t):
    n = x2.shape[0]
    tt = PEER_SEL_TOK
    nt = n // tt
    per_mod = nt // shift.shape[0]
    mrows = shift.shape[1]
    if mrows == 1:
        mblk, mmap = (1, 1, D_MODEL), (lambda i: (i // per_mod, 0, 0))
    else:
        mblk, mmap = (1, tt, D_MODEL), (lambda i: (0, i, 0))
    cmap = lambda i: (0, 0)
    nrt = PEER_N_KEYS // 16
    hshape = jax.ShapeDtypeStruct((nt, PEER_HEADS, PEER_N_KEYS, tt), F32)
    hblk = pl.BlockSpec((1, PEER_HEADS, PEER_N_KEYS, tt), lambda i: (i, 0, 0, 0))
    nfshape = jax.ShapeDtypeStruct((nt, PEER_HEADS, nrt, 2, 8, tt), jnp.int32)
    nfblk = pl.BlockSpec((1, PEER_HEADS, nrt, 2, 8, tt), lambda i: (i, 0, 0, 0, 0, 0))
    return pl.pallas_call(
        _peer_route_kernel,
        grid=(nt,),
        in_specs=[
            pl.BlockSpec((tt, D_MODEL), lambda i: (i, 0)),
            pl.BlockSpec(mblk, mmap), pl.BlockSpec(mblk, mmap),
            pl.BlockSpec((1, D_MODEL), cmap),
            pl.BlockSpec((D_MODEL, D_MODEL), cmap),
            pl.BlockSpec((2 * PEER_HEADS, PEER_N_KEYS, PEER_HALF), lambda i: (0, 0, 0)),
            pl.BlockSpec((2 * PEER_HEADS, D_MODEL), cmap),
        ],
        out_specs=[pl.BlockSpec((D_MODEL, tt), lambda i: (0, i)), hblk, hblk, nfblk],
        out_shape=[jax.ShapeDtypeStruct((D_MODEL, n), BF16), hshape, hshape, nfshape],
        scratch_shapes=[pltpu.VMEM((2 * PEER_HEADS * PEER_N_KEYS, tt), F32), pltpu.VMEM((PEER_HEADS, 1, tt), F32)],
        compiler_params=_cparams(("parallel",)),
        name="peer_route",
    )(x2, shift, scale, gain.reshape(1, D_MODEL), wq_bf, sk_bf, ones_t)


def _gelu(x):
    return 0.5 * x * (1.0 + lax.erf(x * np.float32(math.sqrt(0.5))))


def _peer_expert_kernel(xt_ref, u_ref, un_ref, vt_ref, vp_ref, r1_ref, c1_ref, nf_ref, x_ref, gate_ref, xo_ref,
                        act_scr, w_scr, acc_scr, *, t_tile):
    e = pl.program_id(1)
    nq = PEER_EXP_TILE // PEER_CHUNK
    last = slice((nq - 1) * PEER_CHUNK, nq * PEER_CHUNK)

    @pl.when(e == 0)
    def _():
        acc_scr[...] = jnp.zeros_like(acc_scr)
        w_scr[nq - 1] = jnp.zeros((PEER_CHUNK, t_tile), BF16)
        act_scr[0] = jnp.dot(u_ref[0, :PEER_CHUNK, :], xt_ref[...], preferred_element_type=F32)

    def stage_a(q):
        qs = slice(q * PEER_CHUNK, (q + 1) * PEER_CHUNK)
        act_scr[q] = jnp.dot(u_ref[0, qs, :], xt_ref[...], preferred_element_type=F32)

    def stage_a_next():
        act_scr[0] = jnp.dot(un_ref[0], xt_ref[...], preferred_element_type=F32)

    def stage_b(q):
        for i1l in range(PEER_CHUNK // PEER_N_KEYS):
            i1 = q * (PEER_CHUNK // PEER_N_KEYS) + i1l
            for lc in range(t_tile // PEER_SEL_TOK):
                ls = slice(lc * PEER_SEL_TOK, (lc + 1) * PEER_SEL_TOK)
                r1 = [jnp.broadcast_to(r1_ref[lc, h, i1:i1 + 1, :], (16, PEER_SEL_TOK)).astype(BF16)
                      for h in range(PEER_HEADS)]
                c1 = [jnp.broadcast_to(c1_ref[lc, h, i1:i1 + 1, :], (16, PEER_SEL_TOK)).astype(BF16)
                      for h in range(PEER_HEADS)]
                for j in range(PEER_N_KEYS // 16):
                    g = jnp.zeros((16, PEER_SEL_TOK), BF16)
                    for h in range(PEER_HEADS):
                        n2 = pltpu.bitcast(nf_ref[lc, h, j, 0], BF16)
                        f2 = pltpu.bitcast(nf_ref[lc, h, j, 1], BF16)
                        g = g + jnp.where(n2 > r1[h], f2, 0.0) * c1[h]
                    rs = slice(i1l * PEER_N_KEYS + j * 16, i1l * PEER_N_KEYS + (j + 1) * 16)
                    w_scr[q, rs, ls] = g * _gelu(act_scr[q, rs, ls]).astype(BF16)

    def stage_c(q):
        qs = slice(q * PEER_CHUNK, (q + 1) * PEER_CHUNK)
        acc_scr[...] += jnp.dot(vt_ref[0, :, qs], w_scr[q], preferred_element_type=F32)

    def stage_c_prev():
        acc_scr[...] += jnp.dot(vp_ref[0], w_scr[nq - 1], preferred_element_type=F32)

    stage_c_prev()
    for q in range(nq):
        if q + 1 < nq:
            stage_a(q + 1)
        else:
            stage_a_next()
        stage_b(q)
        if q > 0:
            stage_c(q - 1)

    @pl.when(e == pl.num_programs(1) - 1)
    def _():
        acc = acc_scr[...] + jnp.dot(vt_ref[0, :, last], w_scr[nq - 1], preferred_element_type=F32)
        xo_ref[...] = x_ref[...] + gate_ref[0] * acc.T


def _peer_expert(xt, u_bf, vt_bf, layer, r1, c1, nf, x2, gate, *, t_tile):
    n = x2.shape[0]
    nt = n // t_tile
    nlc = t_tile // PEER_SEL_TOK
    ne = u_bf.shape[1] // PEER_EXP_TILE
    nq = PEER_EXP_TILE // PEER_CHUNK
    per_mod = max(nt // gate.shape[0], 1)
    mrows = gate.shape[1]
    if mrows == 1:
        mblk, mmap = (1, 1, D_MODEL), (lambda i, e: (i // per_mod, 0, 0))
    else:
        mblk, mmap = (1, t_tile, D_MODEL), (lambda i, e: (0, i, 0))
    i1blk = pl.BlockSpec((nlc, PEER_HEADS, PEER_I1_TILE, PEER_SEL_TOK), lambda i, e: (i, 0, e, 0))
    nfblk = pl.BlockSpec((nlc, PEER_HEADS, PEER_N_KEYS // 16, 2, 8, PEER_SEL_TOK), lambda i, e: (i, 0, 0, 0, 0, 0))
    kern = functools.partial(_peer_expert_kernel, t_tile=t_tile)
    return pl.pallas_call(
        kern,
        grid=(nt, ne),
        in_specs=[
            pl.BlockSpec((D_MODEL, t_tile), lambda i, e: (0, i)),
            pl.BlockSpec((1, PEER_EXP_TILE, D_MODEL), lambda i, e: (layer, e, 0)),
            pl.BlockSpec((1, PEER_CHUNK, D_MODEL), lambda i, e: (layer, jnp.minimum(e + 1, ne - 1) * nq, 0)),
            pl.BlockSpec((1, D_MODEL, PEER_EXP_TILE), lambda i, e: (layer, 0, e)),
            pl.BlockSpec((1, D_MODEL, PEER_CHUNK), lambda i, e: (layer, 0, jnp.maximum(e, 1) * nq - 1)),
            i1blk, i1blk, nfblk,
            pl.BlockSpec((t_tile, D_MODEL), lambda i, e: (i, 0)),
            pl.BlockSpec(mblk, mmap),
        ],
        out_specs=pl.BlockSpec((t_tile, D_MODEL), lambda i, e: (i, 0)),
        out_shape=jax.ShapeDtypeStruct(x2.shape, F32),
        scratch_shapes=[pltpu.VMEM((PEER_EXP_TILE // PEER_CHUNK, PEER_CHUNK, t_tile), F32),
                        pltpu.VMEM((PEER_EXP_TILE // PEER_CHUNK, PEER_CHUNK, t_tile), BF16),
                        pltpu.VMEM((D_MODEL, t_tile), F32)],
        compiler_params=_cparams(("parallel", "arbitrary")),
        name="peer_expert",
    )(xt, u_bf, u_bf, vt_bf, vt_bf, r1, c1, nf, x2, gate)


def _peer(x2, shift, scale, gate, gain, pw, tables, layer, *, t_tile):
    wq_bf, sk_bf, ones_t = pw
    u_bf, vt_bf = tables
    xt, r1, c1, nf = _peer_route(x2, shift, scale, gain, wq_bf, sk_bf, ones_t)
    return _peer_expert(xt, u_bf, vt_bf, layer, r1, c1, nf, x2, gate, t_tile=t_tile)


def _peer_weights(w_q, sub_keys):
    hc = 2 * PEER_HEADS
    seg = np.arange(D_MODEL) // PEER_HALF
    ones_t = jnp.asarray((seg[None, :] == np.arange(hc)[:, None]).astype(np.float32), dtype=BF16)
    return (w_q.astype(BF16), sub_keys.reshape(hc, PEER_N_KEYS, PEER_HALF).astype(BF16), ones_t)


def kernel(x_prompt, x_sample, c_prompt, c_sample, state_ssm, cache_kv_w128, cache_kv_w512, cache_kv_w2048,
           norm_mix, norm_ffn, ada_w, ada_b, ssm_lam_re, ssm_lam_im, ssm_log_dt, ssm_b_re, ssm_b_im, ssm_c_re,
           ssm_c_im, ssm_d, ssm_w_glu, attn_w_qkv, attn_q_norm, attn_k_norm, attn_w_o, peer_w_q,
           peer_sub_keys, peer_u, peer_v):
    bp, sp, _ = x_prompt.shape
    bs, ts, _ = x_sample.shape
    ns = bs * ts
    ones_att = _ones_blockdiag(ATT)

    mods = _adaln(jnp.concatenate([c_prompt, c_sample], axis=0), ada_w, ada_b)
    caches = (cache_kv_w128, cache_kv_w512, cache_kv_w2048)
    tab_p = _rope_tables(jnp.arange(sp, dtype=jnp.int32))
    pos_s = PAST_LEN + jnp.arange(ts, dtype=jnp.int32)
    tab_s = tuple(jnp.tile(t, (bs, 1)) for t in _rope_tables(pos_s))

    peer_tabs = (peer_u.astype(BF16), peer_v.astype(BF16).transpose(0, 2, 1))
    xp, xs = x_prompt, x_sample
    ssm_p, ssm_s = [], []
    kv_p = [[] for _ in range(N_GROUPS)]
    kv_s = [[] for _ in range(N_GROUPS)]
    for i in range(DEPTH):
        j = i // 2
        mp = [m.reshape(bp, 1, D_MODEL) for m in jnp.split(mods[i, :bp], 6, axis=-1)]
        ms_b = jnp.split(mods[i, bp:], 6, axis=-1)
        ms_rows = [jnp.repeat(m, ts, axis=0).reshape(1, ns, D_MODEL) for m in ms_b]
        if i % 2 == 0:
            sw = _s5_weights(ssm_lam_re[j], ssm_lam_im[j], ssm_log_dt[j], ssm_b_re[j], ssm_b_im[j],
                             ssm_c_re[j], ssm_c_im[j])
            wglu = ssm_w_glu[j].astype(BF16)
            mp_t = [m.reshape(1, bp, D_MODEL) for m in mp[:3]]
            xp_t, st_p = _s5_layer(xp.transpose(1, 0, 2), mp_t[0], mp_t[1], mp_t[2], norm_mix[i],
                                   jnp.zeros((bp, SSM_STATE_W), F32), sw, ssm_d[j], wglu, t_chunk=32)
            xp = xp_t.transpose(1, 0, 2)
            ms_t = [m.reshape(1, bs, D_MODEL) for m in ms_b[:3]]
            xs_t, st_s = _s5_layer(xs.transpose(1, 0, 2), ms_t[0], ms_t[1], ms_t[2], norm_mix[i],
                                   _state_to_rows(state_ssm[j]), sw, ssm_d[j], wglu, t_chunk=ts)
            xs = xs_t.transpose(1, 0, 2)
            ssm_p.append(_rows_to_state(st_p))
            ssm_s.append(_rows_to_state(st_s))
        else:
            wqkv = attn_w_qkv[j].astype(BF16)
            wo = attn_w_o[j].astype(BF16)
            k, v, q0, q1, k1, v1, q2, k2, v2 = _qkv_dilated(xp, mp[0], mp[1], norm_mix[i], wqkv, attn_q_norm[j],
                                                            attn_k_norm[j], tab_p, ones_att, t_tile=512)
            res = [_band_attention(q0, k, v, 1, N_GROUPS, 0, n_sub=2, n_col=1),
                   _band_attention(q1, k1, v1, DILATIONS[1], 1, 0, n_sub=2, n_col=1),
                   _band_attention(q2, k2, v2, DILATIONS[2], 1, 0, n_sub=1, n_col=2)]
            outs, lses = zip(*res)
            xp = _merge(outs, lses, xp, mp[2], wo, t_tile=512, dils=DILATIONS)
            for g in range(N_GROUPS):
                keep = min(WINDOWS[g], sp)
                gs = slice(g * ATT, (g + 1) * ATT)
                kv_p[g].append(jnp.stack([k[:, sp - keep:, gs], v[:, sp - keep:, gs]], axis=1)
                               .reshape(bp, 2, keep, N_HEADS, HEAD_DIM))
            xs_rows = xs.reshape(1, ns, D_MODEL)
            qs, ks, vs = _qkv(xs_rows, ms_rows[0], ms_rows[1], norm_mix[i], wqkv, attn_q_norm[j],
                              attn_k_norm[j], tab_s, ones_att, t_tile=ns)
            qs, ks, vs = (t.reshape(bs, ts, N_GROUPS * ATT) for t in (qs, ks, vs))
            res = _cache_attention(qs, ks, vs, [c.reshape(c.shape[:4] + (ATT,)) for c in caches], j)
            outs_s = [r.reshape(1, ns, ATT) for r in res[:N_GROUPS]]
            lses_s = [r.reshape(1, ns, ATT) for r in res[N_GROUPS:]]
            xs = _merge(outs_s, lses_s, xs_rows, ms_rows[2], wo, t_tile=ns,
                        dils=(1, 1, 1)).reshape(bs, ts, D_MODEL)
            for g in range(N_GROUPS):
                gs = slice(g * ATT, (g + 1) * ATT)
                kv_s[g].append(jnp.stack([ks[:, :, gs], vs[:, :, gs]], axis=1)
                               .reshape(bs, 2, ts, N_HEADS, HEAD_DIM))
        pw = _peer_weights(peer_w_q[i], peer_sub_keys[i])
        xp = _peer(xp.reshape(bp * sp, D_MODEL), mp[3], mp[4], mp[5], norm_ffn[i], pw, peer_tabs, i,
                   t_tile=512).reshape(bp, sp, D_MODEL)
        xs = _peer(xs.reshape(ns, D_MODEL), ms_rows[3], ms_rows[4], ms_rows[5], norm_ffn[i], pw, peer_tabs, i,
                   t_tile=ns).reshape(bs, ts, D_MODEL)
    return (xp, xs,
            jnp.stack(ssm_p), jnp.stack(kv_p[0]), jnp.stack(kv_p[1]), jnp.stack(kv_p[2]),
            jnp.stack(ssm_s), jnp.stack(kv_s[0]), jnp.stack(kv_s[1]), jnp.stack(kv_s[2]))
```

```python
import functools
import math

import jax
import jax.numpy as jnp
import numpy as np
from jax import lax
from jax.experimental import pallas as pl
from jax.experimental.pallas import tpu as pltpu

F32 = jnp.float32
BF16 = jnp.bfloat16

SUBLANES = 8
LANES = 128
BF16_ROWS = 16

D_MODEL = 1024
DEPTH = 4
PAST_LEN = 8192
EPS = 1e-6
NEG_INF = -1e30

SSM_P = 16
SSM_GROUPS = D_MODEL // SSM_P
SSM_N = 64
SSM_GB = 4
SSM_GB_CH = D_MODEL // SSM_GB
SSM_GB_ST = SSM_GROUPS * SSM_N // SSM_GB
SSM_STATE_W = 2 * SSM_GROUPS * SSM_N

N_HEADS = 8
HEAD_DIM = 64
ATT = N_HEADS * HEAD_DIM
WINDOWS = (128, 512, 2048)
DILATIONS = (1, 4, 16)
N_GROUPS = 3
N_BACK = 128
ROPE_DIM = HEAD_DIM // 4
ROPE_THETA = 500000.0

PEER_HEADS = 8
PEER_N_KEYS = 128
PEER_HALF = 64
PEER_TOPK = 16
PEER_SEL_TOK = LANES
PEER_EXP_TILE = 1024
PEER_I1_TILE = PEER_EXP_TILE // PEER_N_KEYS
PEER_CHUNK = 256

ADA_COL_TILE = 1024

VMEM_LIMIT = 56 * 1024 * 1024


def _cparams(sem):
    return pltpu.CompilerParams(dimension_semantics=sem, vmem_limit_bytes=VMEM_LIMIT)


def _modulate(x, gain, shift, scale):
    ms = jnp.mean(x * x, axis=-1, keepdims=True)
    return x * lax.rsqrt(ms + EPS) * gain * (1.0 + scale) + shift


def _seg_mean(x2, ones_bd):
    hi = x2.astype(BF16)
    lo = (x2 - hi.astype(F32)).astype(BF16)
    s = jnp.dot(hi, ones_bd, preferred_element_type=F32) + jnp.dot(lo, ones_bd, preferred_element_type=F32)
    return s * (1.0 / HEAD_DIM)


def _ones_blockdiag(n):
    seg = np.arange(n) // HEAD_DIM
    return jnp.asarray((seg[:, None] == seg[None, :]).astype(np.float32), dtype=BF16)


def _adaln_kernel(c_ref, w_ref, b_ref, o_ref):
    s = jax.nn.silu(c_ref[...])
    o_ref[0] = jnp.dot(s.astype(BF16), w_ref[0].astype(BF16), preferred_element_type=F32) + b_ref[0]


def _adaln(c_all, ada_w, ada_b):
    nb = c_all.shape[0]
    nt = 6 * D_MODEL // ADA_COL_TILE
    return pl.pallas_call(
        _adaln_kernel,
        grid=(DEPTH, nt),
        in_specs=[
            pl.BlockSpec((nb, D_MODEL), lambda l, j: (0, 0)),
            pl.BlockSpec((1, D_MODEL, ADA_COL_TILE), lambda l, j: (l, 0, j)),
            pl.BlockSpec((1, 1, ADA_COL_TILE), lambda l, j: (l, 0, j)),
        ],
        out_specs=pl.BlockSpec((1, nb, ADA_COL_TILE), lambda l, j: (l, 0, j)),
        out_shape=jax.ShapeDtypeStruct((DEPTH, nb, 6 * D_MODEL), F32),
        compiler_params=_cparams(("parallel", "parallel")),
        name="adaln",
    )(c_all, ada_w, ada_b.reshape(DEPTH, 1, 6 * D_MODEL))


def _s5_kernel(x_ref, shift_ref, scale_ref, gate_ref, gain_ref, h0_ref, bm_ref, cm_ref, are_ref, aim_ref,
               d_ref, wglu_ref, xo_ref, st_ref, bu_scr, state_scr, *, n_batch, n_time):
    i = pl.program_id(0)

    @pl.when(i == 0)
    def _():
        state_scr[...] = h0_ref[...]

    x3 = x_ref[...]
    a0, a1, _ = x3.shape
    rows = a0 * a1
    h3 = _modulate(x3, gain_ref[...], shift_ref[...], scale_ref[...])
    u = h3.reshape(rows, D_MODEL)
    ub = u.astype(BF16)
    for gb in range(SSM_GB):
        bu_scr[:, gb * 2 * SSM_GB_ST:(gb + 1) * 2 * SSM_GB_ST] = jnp.dot(
            ub[:, gb * SSM_GB_CH:(gb + 1) * SSM_GB_CH], bm_ref[gb], preferred_element_type=F32)

    for bb in range(n_batch // SUBLANES):
        for gb in range(SSM_GB):
            c_re = gb * 2 * SSM_GB_ST
            c_im = c_re + SSM_GB_ST
            ar = jnp.broadcast_to(are_ref[gb], (SUBLANES, SSM_GB_ST))
            ai = jnp.broadcast_to(aim_ref[gb], (SUBLANES, SSM_GB_ST))
            xr = state_scr[bb * 8:(bb + 1) * 8, c_re:c_re + SSM_GB_ST]
            xi = state_scr[bb * 8:(bb + 1) * 8, c_im:c_im + SSM_GB_ST]
            for t in range(n_time):
                rsel = pl.ds(t * n_batch + bb * 8, 8)
                br = bu_scr[rsel, c_re:c_re + SSM_GB_ST]
                bi = bu_scr[rsel, c_im:c_im + SSM_GB_ST]
                nr = ar * xr - ai * xi + br
                ni = ar * xi + ai * xr + bi
                bu_scr[rsel, c_re:c_re + SSM_GB_ST] = nr
                bu_scr[rsel, c_im:c_im + SSM_GB_ST] = ni
                xr, xi = nr, ni
            state_scr[bb * 8:(bb + 1) * 8, c_re:c_re + SSM_GB_ST] = xr
            state_scr[bb * 8:(bb + 1) * 8, c_im:c_im + SSM_GB_ST] = xi

    ys = []
    for gb in range(SSM_GB):
        st = bu_scr[:, gb * 2 * SSM_GB_ST:(gb + 1) * 2 * SSM_GB_ST].astype(BF16)
        ys.append(jnp.dot(st, cm_ref[gb], preferred_element_type=F32))
    y = jnp.concatenate(ys, axis=-1) + d_ref[...] * u
    z = jnp.dot(y.astype(BF16), wglu_ref[...], preferred_element_type=F32)
    out = z[:, :D_MODEL] * jax.nn.sigmoid(z[:, D_MODEL:])
    xo_ref[...] = x3 + gate_ref[...] * out.reshape(a0, a1, D_MODEL)

    @pl.when(i == pl.num_programs(0) - 1)
    def _():
        st_ref[...] = state_scr[...]


def _s5_weights(lam_re, lam_im, log_dt, b_re, b_im, c_re, c_im):
    dt = jnp.exp(log_dt)[:, None]
    mag = jnp.exp(lam_re * dt)
    a_re = mag * jnp.cos(lam_im * dt)
    a_im = mag * jnp.sin(lam_im * dt)
    den = lam_re * lam_re + lam_im * lam_im
    f_re = ((a_re - 1.0) * lam_re + a_im * lam_im) / den
    f_im = (a_im * lam_re - (a_re - 1.0) * lam_im) / den
    bb_re = f_re[..., None] * b_re - f_im[..., None] * b_im
    bb_im = f_re[..., None] * b_im + f_im[..., None] * b_re
    eye = jnp.eye(16, dtype=F32)

    def in_blk(b):
        b4 = b.reshape(SSM_GB, 16, SSM_N, SSM_P)
        return jnp.einsum('bgnp,gh->bgphn', b4, eye).reshape(SSM_GB, SSM_GB_CH, SSM_GB_ST)

    def out_blk(c):
        c4 = c.reshape(SSM_GB, 16, SSM_P, SSM_N)
        return jnp.einsum('bgpn,gh->bgnhp', c4, eye).reshape(SSM_GB, SSM_GB_ST, SSM_GB_CH)

    bm = jnp.concatenate([in_blk(bb_re), in_blk(bb_im)], axis=2).astype(BF16)
    cm = jnp.concatenate([out_blk(c_re), -out_blk(c_im)], axis=1).astype(BF16)
    return bm, cm, a_re.reshape(SSM_GB, 1, SSM_GB_ST), a_im.reshape(SSM_GB, 1, SSM_GB_ST)


def _state_to_rows(st):
    b = st.shape[0]
    return st.reshape(b, 2, SSM_GB, SSM_GB_ST).transpose(0, 2, 1, 3).reshape(b, SSM_STATE_W)


def _rows_to_state(rows):
    b = rows.shape[0]
    return rows.reshape(b, SSM_GB, 2, SSM_GB_ST).transpose(0, 2, 1, 3).reshape(b, 2, SSM_GROUPS, SSM_N)


def _s5_layer(x3, shift, scale, gate, gain, h0_rows, weights, d_skip, w_glu, *, t_chunk):
    bm, cm, a_re, a_im = weights
    seq, n_batch = x3.shape[0], x3.shape[1]
    n_time = t_chunk
    blk = (n_time, n_batch, D_MODEL)
    xmap = lambda i: (i, 0, 0)
    mblk = (1, n_batch, D_MODEL)
    grid = seq // n_time
    rows = n_batch * n_time
    const3 = lambda i: (0, 0, 0)
    const2 = lambda i: (0, 0)
    kern = functools.partial(_s5_kernel, n_batch=n_batch, n_time=n_time)
    return pl.pallas_call(
        kern,
        grid=(grid,),
        in_specs=[
            pl.BlockSpec(blk, xmap),
            pl.BlockSpec(mblk, const3), pl.BlockSpec(mblk, const3), pl.BlockSpec(mblk, const3),
            pl.BlockSpec((1, 1, D_MODEL), const3),
            pl.BlockSpec((n_batch, SSM_STATE_W), const2),
            pl.BlockSpec(bm.shape, const3), pl.BlockSpec(cm.shape, const3),
            pl.BlockSpec(a_re.shape, const3), pl.BlockSpec(a_im.shape, const3),
            pl.BlockSpec((1, D_MODEL), const2),
            pl.BlockSpec((D_MODEL, 2 * D_MODEL), const2),
        ],
        out_specs=[pl.BlockSpec(blk, xmap), pl.BlockSpec((n_batch, SSM_STATE_W), const2)],
        out_shape=[jax.ShapeDtypeStruct(x3.shape, F32), jax.ShapeDtypeStruct((n_batch, SSM_STATE_W), F32)],
        scratch_shapes=[pltpu.VMEM((rows, SSM_STATE_W), F32), pltpu.VMEM((n_batch, SSM_STATE_W), F32)],
        compiler_params=_cparams(("arbitrary",)),
        name="s5_layer",
    )(x3, shift, scale, gate, gain.reshape(1, 1, D_MODEL), h0_rows, bm, cm, a_re, a_im,
      d_skip.reshape(1, D_MODEL), w_glu)


def _qkv_kernel(x_ref, shift_ref, scale_ref, gain_ref, w_ref, qg_ref, kg_ref, rc_ref, rs1_ref, rs2_ref,
                ones_ref, q_ref, k_ref, v_ref):
    h = _modulate(x_ref[0], gain_ref[...], shift_ref[0], scale_ref[0])
    qkv = jnp.dot(h.astype(BF16), w_ref[...], preferred_element_type=F32)
    rc, rs1, rs2 = rc_ref[...], rs1_ref[...], rs2_ref[...]
    ones_bd = ones_ref[...]

    def norm_rope(t, g):
        tn = t * lax.rsqrt(_seg_mean(t * t, ones_bd) + EPS) * g
        return tn * rc + pltpu.roll(tn, ATT - ROPE_DIM // 2, 1) * rs1 + pltpu.roll(tn, ROPE_DIM // 2, 1) * rs2

    q_ref[0] = norm_rope(qkv[:, :ATT], qg_ref[...])
    k_ref[0] = norm_rope(qkv[:, ATT:2 * ATT], kg_ref[...])
    v_ref[0] = qkv[:, 2 * ATT:]


def _dilate(x, scr, out_ref, d):
    t = x.shape[0]
    for kc in range(ATT // LANES):
        scr[kc] = x[:, kc * LANES:(kc + 1) * LANES]
    for r in range(d):
        for kc in range(ATT // LANES):
            c0 = r * ATT + kc * LANES
            out_ref[0, :, c0:c0 + LANES] = scr[kc, pl.ds(r, t // d, stride=d), :]


def _qkv_dil_kernel(x_ref, shift_ref, scale_ref, gain_ref, w_ref, qg_ref, kg_ref, rc_ref, rs1_ref, rs2_ref,
                    ones_ref, kn_ref, vn_ref, q0_ref, q1_ref, k1_ref, v1_ref, q2_ref, k2_ref, v2_ref, scr):
    g = pl.program_id(2)
    h = _modulate(x_ref[0], gain_ref[...], shift_ref[0], scale_ref[0])
    qkv = jnp.dot(h.astype(BF16), w_ref[...], preferred_element_type=F32)
    rc, rs1, rs2 = rc_ref[...], rs1_ref[...], rs2_ref[...]
    ones_bd = ones_ref[...]

    def norm_rope(t, gn):
        tn = t * lax.rsqrt(_seg_mean(t * t, ones_bd) + EPS) * gn
        return tn * rc + pltpu.roll(tn, ATT - ROPE_DIM // 2, 1) * rs1 + pltpu.roll(tn, ROPE_DIM // 2, 1) * rs2

    q = norm_rope(qkv[:, :ATT], qg_ref[...])
    k = norm_rope(qkv[:, ATT:2 * ATT], kg_ref[...])
    v = qkv[:, 2 * ATT:]
    kn_ref[0] = k
    vn_ref[0] = v

    @pl.when(g == 0)
    def _():
        q0_ref[0] = q

    for gi, refs in ((1, (q1_ref, k1_ref, v1_ref)), (2, (q2_ref, k2_ref, v2_ref))):
        @pl.when(g == gi)
        def _():
            for val, ref in zip((q, k, v), refs):
                _dilate(val, scr, ref, DILATIONS[gi])


def _qkv_dilated(x3, shift, scale, gain, w_qkv_bf, q_gain, k_gain, tables, ones_bd, *, t_tile):
    bq, sq, _ = x3.shape
    rc, rs1, rs2 = tables
    qg = jnp.tile(q_gain, N_HEADS).reshape(1, ATT)
    kg = jnp.tile(k_gain, N_HEADS).reshape(1, ATT)
    mmap = lambda b, i, g: (b, 0, 0)
    tmap = lambda b, i, g: (i, 0)
    cmap = lambda b, i, g: (0, 0)
    gmap = lambda b, i, g: (b, i, g)
    bmap = lambda b, i, g: (b, i, 0)
    nat = jax.ShapeDtypeStruct((bq, sq, N_GROUPS * ATT), F32)
    out_shape = [nat, nat, jax.ShapeDtypeStruct((bq, sq, ATT), F32)]
    out_specs = [pl.BlockSpec((1, t_tile, ATT), gmap)] * 2 + [pl.BlockSpec((1, t_tile, ATT), bmap)]
    for gi in (1, 2):
        d = DILATIONS[gi]
        out_shape += [jax.ShapeDtypeStruct((bq, sq // d, d * ATT), F32)] * 3
        out_specs += [pl.BlockSpec((1, t_tile // d, d * ATT), bmap)] * 3
    return pl.pallas_call(
        _qkv_dil_kernel,
        grid=(bq, sq // t_tile, N_GROUPS),
        in_specs=[
            pl.BlockSpec((1, t_tile, D_MODEL), bmap),
            pl.BlockSpec((1, 1, D_MODEL), mmap), pl.BlockSpec((1, 1, D_MODEL), mmap),
            pl.BlockSpec((1, D_MODEL), cmap),
            pl.BlockSpec((D_MODEL, 3 * ATT), lambda b, i, g: (0, g)),
            pl.BlockSpec((1, ATT), cmap), pl.BlockSpec((1, ATT), cmap),
            pl.BlockSpec((t_tile, ATT), tmap), pl.BlockSpec((t_tile, ATT), tmap), pl.BlockSpec((t_tile, ATT), tmap),
            pl.BlockSpec((ATT, ATT), cmap),
        ],
        out_specs=out_specs,
        out_shape=out_shape,
        scratch_shapes=[pltpu.VMEM((ATT // LANES, t_tile, LANES), F32)],
        compiler_params=_cparams(("parallel", "parallel", "arbitrary")),
        name="qkv_proj_dilated",
    )(x3, shift, scale, gain.reshape(1, D_MODEL), w_qkv_bf, qg, kg, rc, rs1, rs2, ones_bd)


def _rope_tables(pos):
    half = ROPE_DIM // 2
    inv = ROPE_THETA ** (-jnp.arange(half, dtype=F32) / half)
    ang = pos.astype(F32)[:, None] * inv[None, :]
    cos, sin = jnp.cos(ang), jnp.sin(ang)
    lane = np.arange(ATT) % HEAD_DIM
    fidx = lane % half
    first = jnp.asarray(lane < half)
    second = jnp.asarray((lane >= half) & (lane < ROPE_DIM))
    cl, sl = cos[:, fidx], sin[:, fidx]
    rc = jnp.where(first | second, cl, 1.0)
    rs1 = jnp.where(first, -sl, 0.0)
    rs2 = jnp.where(second, sl, 0.0)
    return rc, rs1, rs2


def _qkv(x3, shift, scale, gain, w_qkv_bf, q_gain, k_gain, tables, ones_bd, *, t_tile):
    bq, sq, _ = x3.shape
    mrows = shift.shape[1]
    mt = t_tile if mrows == sq else 1
    mmap = (lambda b, i, g: (b, i, 0)) if mrows == sq else (lambda b, i, g: (b, 0, 0))
    rc, rs1, rs2 = tables
    qg = jnp.tile(q_gain, N_HEADS).reshape(1, ATT)
    kg = jnp.tile(k_gain, N_HEADS).reshape(1, ATT)
    tmap = lambda b, i, g: (i, 0)
    cmap = lambda b, i, g: (0, 0)
    omap = lambda b, i, g: (b, i, g)
    oshape = jax.ShapeDtypeStruct((bq, sq, N_GROUPS * ATT), F32)
    return pl.pallas_call(
        _qkv_kernel,
        grid=(bq, sq // t_tile, N_GROUPS),
        in_specs=[
            pl.BlockSpec((1, t_tile, D_MODEL), lambda b, i, g: (b, i, 0)),
            pl.BlockSpec((1, mt, D_MODEL), mmap), pl.BlockSpec((1, mt, D_MODEL), mmap),
            pl.BlockSpec((1, D_MODEL), cmap),
            pl.BlockSpec((D_MODEL, 3 * ATT), lambda b, i, g: (0, g)),
            pl.BlockSpec((1, ATT), cmap), pl.BlockSpec((1, ATT), cmap),
            pl.BlockSpec((t_tile, ATT), tmap), pl.BlockSpec((t_tile, ATT), tmap), pl.BlockSpec((t_tile, ATT), tmap),
            pl.BlockSpec((ATT, ATT), cmap),
        ],
        out_specs=[pl.BlockSpec((1, t_tile, ATT), omap)] * 3,
        out_shape=[oshape] * 3,
        compiler_params=_cparams(("parallel", "parallel", "arbitrary")),
        name="qkv_proj",
    )(x3, shift, scale, gain.reshape(1, D_MODEL), w_qkv_bf, qg, kg, rc, rs1, rs2, ones_bd)


def _band_kernel(q_ref, kp_ref, kc_ref, vp_ref, vc_ref, o_ref, l_ref, *, n_sub, n_col):
    i = pl.program_id(2)
    iq = lax.broadcasted_iota(jnp.int32, (N_BACK, 2 * N_BACK), 0)
    ik = lax.broadcasted_iota(jnp.int32, (N_BACK, 2 * N_BACK), 1)
    dist = iq + N_BACK - ik
    band = (dist >= 0) & (dist <= N_BACK)
    band_first = band & ((ik >= N_BACK) | (i > 0))
    for c in range(n_col):
        c0 = c * ATT
        k_all = jnp.concatenate([kp_ref[0, :, c0:c0 + ATT], kc_ref[0, :, c0:c0 + ATT]], axis=0).astype(BF16)
        v_all = jnp.concatenate([vp_ref[0, :, c0:c0 + ATT], vc_ref[0, :, c0:c0 + ATT]], axis=0).astype(BF16)
        for u in range(n_sub):
            rows = slice(u * N_BACK, (u + 1) * N_BACK)
            q = q_ref[0, rows, c0:c0 + ATT].astype(BF16)
            k = k_all[u * N_BACK:(u + 2) * N_BACK]
            v = v_all[u * N_BACK:(u + 2) * N_BACK]
            mask = band_first if u == 0 else band
            for h in range(N_HEADS):
                sl = slice(h * HEAD_DIM, (h + 1) * HEAD_DIM)
                ol = slice(c0 + h * HEAD_DIM, c0 + (h + 1) * HEAD_DIM)
                s = lax.dot_general(q[:, sl], k[:, sl], (((1,), (1,)), ((), ())),
                                    preferred_element_type=F32) * (HEAD_DIM ** -0.5)
                s = jnp.where(mask, s, NEG_INF)
                m = jnp.max(s, axis=-1, keepdims=True)
                p = jnp.exp(s - m)
                den = jnp.sum(p, axis=-1, keepdims=True)
                o = jnp.dot(p.astype(BF16), v[:, sl], preferred_element_type=F32) / den
                o_ref[0, rows, ol] = o
                l_ref[0, rows, ol] = jnp.broadcast_to(m + jnp.log(den), (N_BACK, HEAD_DIM))


def _band_attention(qv, kv, vv, d, kcols, kcol0, *, n_sub, n_col):
    b, sub, _ = qv.shape
    nb = sub // N_BACK
    assert nb % n_sub == 0 and d % n_col == 0 and (n_col == 1 or kcols == 1)
    qcur = lambda bi, r, i: (bi, i, r)
    cur = lambda bi, r, i: (bi, i, r * kcols + kcol0)
    prev = lambda bi, r, i: (bi, jnp.maximum(i * n_sub - 1, 0), r * kcols + kcol0)
    blk = (1, n_sub * N_BACK, n_col * ATT)
    pblk = (1, N_BACK, n_col * ATT)
    oshape = jax.ShapeDtypeStruct((b, sub, d * ATT), F32)
    return pl.pallas_call(
        functools.partial(_band_kernel, n_sub=n_sub, n_col=n_col),
        grid=(b, d // n_col, nb // n_sub),
        in_specs=[pl.BlockSpec(blk, qcur), pl.BlockSpec(pblk, prev), pl.BlockSpec(blk, cur),
                  pl.BlockSpec(pblk, prev), pl.BlockSpec(blk, cur)],
        out_specs=[pl.BlockSpec(blk, qcur), pl.BlockSpec(blk, qcur)],
        out_shape=[oshape, oshape],
        compiler_params=_cparams(("parallel", "parallel", "arbitrary")),
        name="band_attention",
    )(qv, kv, kv, vv, vv)


def _cache_attn_kernel(q_ref, k_ref, v_ref, c0_ref, c1_ref, c2_ref, *out_refs):
    n_new = q_ref.shape[1]
    rows = n_new * N_HEADS
    lane_head = lax.broadcasted_iota(jnp.int32, (N_HEADS, ATT), 1) // HEAD_DIM
    head_mask = (lane_head == lax.broadcasted_iota(jnp.int32, (N_HEADS, ATT), 0)).astype(F32)
    row_t = lax.broadcasted_iota(jnp.int32, (rows, 1), 0) // N_HEADS
    caches = (c0_ref, c1_ref, c2_ref)
    for g in range(N_GROUPS):
        d, win = DILATIONS[g], WINDOWS[g]
        gs = slice(g * ATT, (g + 1) * ATT)
        q = q_ref[0][:, gs]
        kn = k_ref[0][:, gs]
        vn = v_ref[0][:, gs]
        kc = caches[g][0, 0, 0].astype(BF16)
        vc = caches[g][0, 0, 1].astype(BF16)
        qbd = jnp.concatenate([q[t:t + 1, :] * head_mask for t in range(n_new)], axis=0)
        s_c = lax.dot_general(qbd.astype(BF16), kc, (((1,), (1,)), ((), ())),
                              preferred_element_type=F32) * (HEAD_DIM ** -0.5)
        col = lax.broadcasted_iota(jnp.int32, (rows, win), 1)
        valid_c = (col >= row_t) & (((col - row_t) & (d - 1)) == 0)
        s_c = jnp.where(valid_c, s_c, NEG_INF)
        s_n = []
        for t2 in range(n_new):
            sn = jnp.sum(qbd * kn[t2:t2 + 1, :], axis=-1, keepdims=True) * (HEAD_DIM ** -0.5)
            valid_n = (row_t >= t2) & (((row_t - t2) & (d - 1)) == 0)
            s_n.append(jnp.where(valid_n, sn, NEG_INF))
        m = jnp.max(s_c, axis=-1, keepdims=True)
        for sn in s_n:
            m = jnp.maximum(m, sn)
        p_c = jnp.exp(s_c - m)
        den = jnp.sum(p_c, axis=-1, keepdims=True)
        o = jnp.dot(p_c.astype(BF16), vc, preferred_element_type=F32)
        for t2 in range(n_new):
            p_n = jnp.exp(s_n[t2] - m)
            den = den + p_n
            o = o + p_n * vn[t2:t2 + 1, :]
        o = o / den
        lse = m + jnp.log(den)
        for t in range(n_new):
            rs = slice(t * N_HEADS, (t + 1) * N_HEADS)
            out_refs[g][0, t:t + 1, :] = jnp.sum(o[rs] * head_mask, axis=0, keepdims=True)
            out_refs[N_GROUPS + g][0, t:t + 1, :] = jnp.sum(lse[rs] * head_mask, axis=0, keepdims=True)


def _cache_attention(q, k, v, caches, layer):
    bd, t, w = q.shape
    xmap = lambda b: (b, 0, 0)
    cmap = lambda b: (layer, b, 0, 0, 0)
    oshape = jax.ShapeDtypeStruct((bd, t, ATT), F32)
    return pl.pallas_call(
        _cache_attn_kernel,
        grid=(bd,),
        in_specs=[pl.BlockSpec((1, t, w), xmap)] * 3
        + [pl.BlockSpec((1, 1, 2, WINDOWS[g], ATT), cmap) for g in range(N_GROUPS)],
        out_specs=[pl.BlockSpec((1, t, ATT), xmap)] * (2 * N_GROUPS),
        out_shape=[oshape] * (2 * N_GROUPS),
        compiler_params=_cparams(("parallel",)),
        name="cache_attention",
    )(q, k, v, *caches)


def _undilate(ref, scr, d):
    if d == 1:
        return ref[0]
    rows = ref.shape[1]
    for r in range(d):
        for kc in range(ATT // LANES):
            c0 = r * ATT + kc * LANES
            scr[kc, pl.ds(r, rows, stride=d), :] = ref[0, :, c0:c0 + LANES]
    return jnp.concatenate([scr[kc] for kc in range(ATT // LANES)], axis=1)


def _merge_kernel(o0, o1, o2, l0, l1, l2, x_ref, gate_ref, wo_ref, xo_ref, scr, *, dils):
    la, lb, lc = [_undilate(r, scr, d) for r, d in zip((l0, l1, l2), dils)]
    m = jnp.maximum(jnp.maximum(la, lb), lc)
    ea, eb, ec = jnp.exp(la - m), jnp.exp(lb - m), jnp.exp(lc - m)
    den = ea + eb + ec
    o = (ea / den) * _undilate(o0, scr, dils[0])
    o = o + (eb / den) * _undilate(o1, scr, dils[1])
    o = o + (ec / den) * _undilate(o2, scr, dils[2])
    out = jnp.dot(o.astype(BF16), wo_ref[...], preferred_element_type=F32)
    xo_ref[0] = x_ref[0] + gate_ref[0] * out


def _merge(outs, lses, x3, gate, w_o_bf, *, t_tile, dils):
    bq, sq, _ = x3.shape
    mrows = gate.shape[1]
    mt = t_tile if mrows == sq else 1
    mmap = (lambda b, i: (b, i, 0)) if mrows == sq else (lambda b, i: (b, 0, 0))
    tmap = lambda b, i: (b, i, 0)
    ablk = [pl.BlockSpec((1, t_tile // d, d * ATT), tmap) for d in dils]
    return pl.pallas_call(
        functools.partial(_merge_kernel, dils=dils),
        grid=(bq, sq // t_tile),
        in_specs=ablk + ablk + [pl.BlockSpec((1, t_tile, D_MODEL), tmap), pl.BlockSpec((1, mt, D_MODEL), mmap),
                                pl.BlockSpec((ATT, D_MODEL), lambda b, i: (0, 0))],
        out_specs=pl.BlockSpec((1, t_tile, D_MODEL), tmap),
        out_shape=jax.ShapeDtypeStruct(x3.shape, F32),
        scratch_shapes=[pltpu.VMEM((ATT // LANES, t_tile, LANES), F32)],
        compiler_params=_cparams(("parallel", "parallel")),
        name="attn_merge",
    )(*outs, *lses, x3, gate, w_o_bf)


def _top16(s, exact):
    vals = s
    rank = jnp.full(s.shape, float(PEER_N_KEYS), F32)
    key = lax.broadcasted_iota(jnp.int32, s.shape, 0).astype(F32) if exact else None
    tops = []
    for r in range(PEER_TOPK):
        m = jnp.max(vals, axis=0, keepdims=True)
        hit = vals == m
        if exact:
            first = jnp.min(jnp.where(hit, key, float(PEER_N_KEYS)), axis=0, keepdims=True)
            hit = key == first
        rank = jnp.where(hit, float(r), rank)
        vals = jnp.where(hit, -jnp.inf, vals)
        tops.append(m)
    return tops, rank


def _count_taken(rank):
    return jnp.sum(jnp.where(rank < float(PEER_N_KEYS), 1.0, 0.0), axis=0, keepdims=True)


def _route_head(s1, s2, exact):
    n_tok = s1.shape[1]
    top1, rank1 = _top16(s1, exact)
    top2, rank2 = _top16(s2, exact)
    row = lax.broadcasted_iota(jnp.int32, (PEER_TOPK, n_tok), 0)
    v1 = jnp.zeros((PEER_TOPK, n_tok), F32)
    for r in range(PEER_TOPK):
        v1 = jnp.where(row == r, top1[r], v1)
    best = top1[0] + top2[0]
    zsum = jnp.zeros_like(best)
    if exact:
        cands = [v1 + top2[b] for b in range(PEER_TOPK)]
        flat = [(row * PEER_TOPK + b).astype(F32) for b in range(PEER_TOPK)]
    else:
        v2 = jnp.zeros((PEER_TOPK, n_tok), F32)
        for r in range(PEER_TOPK):
            v2 = jnp.where(row == r, top2[r], v2)
        cands = [v1[:8] + top2[b] for b in range(8)] + [v1[8:] + top2[0], top1[0] + v2[8:]]
    taken = [jnp.zeros_like(c) for c in cands]
    for _ in range(PEER_TOPK):
        m = cands[0]
        for c in cands[1:]:
            m = jnp.maximum(m, c)
        m = jnp.max(m, axis=0, keepdims=True)
        if exact:
            fmin = jnp.full_like(cands[0], float(PEER_TOPK * PEER_TOPK))
            for c, f in zip(cands, flat):
                fmin = jnp.minimum(fmin, jnp.where(c == m, f, float(PEER_TOPK * PEER_TOPK)))
            fmin = jnp.min(fmin, axis=0, keepdims=True)
        for j in range(len(cands)):
            hit = (flat[j] == fmin) if exact else (cands[j] == m)
            taken[j] = jnp.where(hit, 1.0, taken[j])
            cands[j] = jnp.where(hit, -jnp.inf, cands[j])
        zsum = zsum + jnp.exp(m - best)
    tie = jnp.zeros_like(best)
    if exact:
        cnt = taken[0]
        for j in range(1, PEER_TOPK):
            cnt = cnt + taken[j]
    else:
        cnt = taken[0]
        for j in range(1, 8):
            cnt = cnt + taken[j]
        cnt = jnp.concatenate([cnt, taken[8]], axis=0)
        cnt = cnt + jnp.where(row == 0, jnp.sum(taken[9], axis=0, keepdims=True), 0.0)
        n_pairs = jnp.sum(cnt, axis=0, keepdims=True)
        bad = ((_count_taken(rank1) != float(PEER_TOPK)) | (_count_taken(rank2) != float(PEER_TOPK))
               | (n_pairs != float(PEER_TOPK)))
        tie = jnp.where(bad, 1.0, 0.0)
    n2 = jnp.where(rank2 < 1.0, jnp.sum(cnt[8:], axis=0, keepdims=True), 0.0)
    for a in range(8):
        n2 = n2 + jnp.where(rank2 < cnt[a:a + 1], 1.0, 0.0)
    return rank1, jnp.exp(s1 - top1[0]), n2, jnp.exp(s2 - top2[0]) / zsum, tie


def _peer_route_kernel(x_ref, shift_ref, scale_ref, gain_ref, wq_ref, sk_ref, ones_ref,
                       xt_ref, r1_ref, c1_ref, nf_ref, s_scr, tie_scr):
    hm = _modulate(x_ref[...], gain_ref[...], shift_ref[0], scale_ref[0])
    hb = hm.astype(BF16)
    xt_ref[...] = hm.T.astype(BF16)
    q = jnp.dot(hb, wq_ref[...], preferred_element_type=F32)
    nt_dims = (((1,), (1,)), ((), ()))
    q2 = q * q
    q2_hi = q2.astype(BF16)
    q2_lo = (q2 - q2_hi.astype(F32)).astype(BF16)
    ms_t = (lax.dot_general(ones_ref[...], q2_hi, nt_dims, preferred_element_type=F32)
            + lax.dot_general(ones_ref[...], q2_lo, nt_dims, preferred_element_type=F32)) * (1.0 / PEER_HALF)
    inv_t = lax.rsqrt(ms_t + EPS)
    qb = q.astype(BF16)
    for hc in range(2 * PEER_HEADS):
        raw = lax.dot_general(sk_ref[hc], qb[:, hc * PEER_HALF:(hc + 1) * PEER_HALF], nt_dims,
                              preferred_element_type=F32)
        s_scr[hc * PEER_N_KEYS:(hc + 1) * PEER_N_KEYS, :] = raw * inv_t[hc:hc + 1, :]

    def head(h, tie, exact):
        base = pl.multiple_of(h * 2 * PEER_N_KEYS, 2 * PEER_N_KEYS)
        s1 = s_scr[pl.ds(base, PEER_N_KEYS), :]
        s2 = s_scr[pl.ds(base + PEER_N_KEYS, PEER_N_KEYS), :]
        rank1, c1, n2, f2, tie_h = _route_head(s1, s2, exact)
        r1_ref[0, h] = rank1
        c1_ref[0, h] = c1
        nf_ref[0, h, :, 0] = pltpu.bitcast(n2.astype(BF16), jnp.int32).reshape(PEER_N_KEYS // BF16_ROWS, SUBLANES, PEER_SEL_TOK)
        nf_ref[0, h, :, 1] = pltpu.bitcast(f2.astype(BF16), jnp.int32).reshape(PEER_N_KEYS // BF16_ROWS, SUBLANES, PEER_SEL_TOK)
        if not exact:
            tie_scr[h] = tie_h
        return jnp.maximum(tie, tie_h)

    no_tie = jnp.zeros((1, s_scr.shape[1]), F32)
    tie = lax.fori_loop(0, PEER_HEADS, functools.partial(head, exact=False), no_tie, unroll=8)

    @pl.when(jnp.max(tie) > 0.0)
    def _():
        def redo(h, carry):
            @pl.when(jnp.max(tie_scr[h]) > 0.0)
            def _():
                head(h, no_tie, True)
            return carry

        lax.fori_loop(0, PEER_HEADS, redo, 0)


def _peer_route(x2, shift, scale, gain, wq_bf, sk_bf, ones_t):
    n = x2.shape[0]
    tt = PEER_SEL_TOK
    nt = n // tt
    per_mod = nt // shift.shape[0]
    mrows = shift.shape[1]
    if mrows == 1:
        mblk, mmap = (1, 1, D_MODEL), (lambda i: (i // per_mod, 0, 0))
    else:
        mblk, mmap = (1, tt, D_MODEL), (lambda i: (0, i, 0))
    cmap = lambda i: (0, 0)
    nrt = PEER_N_KEYS // BF16_ROWS
    hshape = jax.ShapeDtypeStruct((nt, PEER_HEADS, PEER_N_KEYS, tt), F32)
    hblk = pl.BlockSpec((1, PEER_HEADS, PEER_N_KEYS, tt), lambda i: (i, 0, 0, 0))
    nfshape = jax.ShapeDtypeStruct((nt, PEER_HEADS, nrt, 2, SUBLANES, tt), jnp.int32)
    nfblk = pl.BlockSpec((1, PEER_HEADS, nrt, 2, SUBLANES, tt), lambda i: (i, 0, 0, 0, 0, 0))
    return pl.pallas_call(
        _peer_route_kernel,
        grid=(nt,),
        in_specs=[
            pl.BlockSpec((tt, D_MODEL), lambda i: (i, 0)),
            pl.BlockSpec(mblk, mmap), pl.BlockSpec(mblk, mmap),
            pl.BlockSpec((1, D_MODEL), cmap),
            pl.BlockSpec((D_MODEL, D_MODEL), cmap),
            pl.BlockSpec((2 * PEER_HEADS, PEER_N_KEYS, PEER_HALF), lambda i: (0, 0, 0)),
            pl.BlockSpec((2 * PEER_HEADS, D_MODEL), cmap),
        ],
        out_specs=[pl.BlockSpec((D_MODEL, tt), lambda i: (0, i)), hblk, hblk, nfblk],
        out_shape=[jax.ShapeDtypeStruct((D_MODEL, n), BF16), hshape, hshape, nfshape],
        scratch_shapes=[pltpu.VMEM((2 * PEER_HEADS * PEER_N_KEYS, tt), F32), pltpu.VMEM((PEER_HEADS, 1, tt), F32)],
        compiler_params=_cparams(("parallel",)),
        name="peer_route",
    )(x2, shift, scale, gain.reshape(1, D_MODEL), wq_bf, sk_bf, ones_t)


def _gelu(x):
    return 0.5 * x * (1.0 + lax.erf(x * np.float32(math.sqrt(0.5))))


def _peer_expert_kernel(xt_ref, u_ref, un_ref, vt_ref, vp_ref, r1_ref, c1_ref, nf_ref, x_ref, gate_ref, xo_ref,
                        act_scr, w_scr, acc_scr, *, t_tile):
    e = pl.program_id(1)
    nq = PEER_EXP_TILE // PEER_CHUNK
    last = slice((nq - 1) * PEER_CHUNK, nq * PEER_CHUNK)

    @pl.when(e == 0)
    def _():
        acc_scr[...] = jnp.zeros_like(acc_scr)
        w_scr[nq - 1] = jnp.zeros((PEER_CHUNK, t_tile), BF16)
        act_scr[0] = jnp.dot(u_ref[0, :PEER_CHUNK, :], xt_ref[...], preferred_element_type=F32)

    def stage_a(q):
        qs = slice(q * PEER_CHUNK, (q + 1) * PEER_CHUNK)
        act_scr[q] = jnp.dot(u_ref[0, qs, :], xt_ref[...], preferred_element_type=F32)

    def stage_a_next():
        act_scr[0] = jnp.dot(un_ref[0], xt_ref[...], preferred_element_type=F32)

    def stage_b(q):
        for i1l in range(PEER_CHUNK // PEER_N_KEYS):
            i1 = q * (PEER_CHUNK // PEER_N_KEYS) + i1l
            for lc in range(t_tile // PEER_SEL_TOK):
                ls = slice(lc * PEER_SEL_TOK, (lc + 1) * PEER_SEL_TOK)
                r1 = [jnp.broadcast_to(r1_ref[lc, h, i1:i1 + 1, :], (BF16_ROWS, PEER_SEL_TOK)).astype(BF16)
                      for h in range(PEER_HEADS)]
                c1 = [jnp.broadcast_to(c1_ref[lc, h, i1:i1 + 1, :], (BF16_ROWS, PEER_SEL_TOK)).astype(BF16)
                      for h in range(PEER_HEADS)]
                for j in range(PEER_N_KEYS // BF16_ROWS):
                    g = jnp.zeros((BF16_ROWS, PEER_SEL_TOK), BF16)
                    for h in range(PEER_HEADS):
                        n2 = pltpu.bitcast(nf_ref[lc, h, j, 0], BF16)
                        f2 = pltpu.bitcast(nf_ref[lc, h, j, 1], BF16)
                        g = g + jnp.where(n2 > r1[h], f2, 0.0) * c1[h]
                    rs = slice(i1l * PEER_N_KEYS + j * BF16_ROWS, i1l * PEER_N_KEYS + (j + 1) * BF16_ROWS)
                    w_scr[q, rs, ls] = g * _gelu(act_scr[q, rs, ls]).astype(BF16)

    def stage_c(q):
        qs = slice(q * PEER_CHUNK, (q + 1) * PEER_CHUNK)
        acc_scr[...] += jnp.dot(vt_ref[0, :, qs], w_scr[q], preferred_element_type=F32)

    def stage_c_prev():
        acc_scr[...] += jnp.dot(vp_ref[0], w_scr[nq - 1], preferred_element_type=F32)

    stage_c_prev()
    for q in range(nq):
        if q + 1 < nq:
            stage_a(q + 1)
        else:
            stage_a_next()
        stage_b(q)
        if q > 0:
            stage_c(q - 1)

    @pl.when(e == pl.num_programs(1) - 1)
    def _():
        acc = acc_scr[...] + jnp.dot(vt_ref[0, :, last], w_scr[nq - 1], preferred_element_type=F32)
        xo_ref[...] = x_ref[...] + gate_ref[0] * acc.T


def _peer_expert(xt, u_bf, vt_bf, layer, r1, c1, nf, x2, gate, *, t_tile):
    n = x2.shape[0]
    nt = n // t_tile
    nlc = t_tile // PEER_SEL_TOK
    ne = u_bf.shape[1] // PEER_EXP_TILE
    nq = PEER_EXP_TILE // PEER_CHUNK
    per_mod = max(nt // gate.shape[0], 1)
    mrows = gate.shape[1]
    if mrows == 1:
        mblk, mmap = (1, 1, D_MODEL), (lambda i, e: (i // per_mod, 0, 0))
    else:
        mblk, mmap = (1, t_tile, D_MODEL), (lambda i, e: (0, i, 0))
    i1blk = pl.BlockSpec((nlc, PEER_HEADS, PEER_I1_TILE, PEER_SEL_TOK), lambda i, e: (i, 0, e, 0))
    nfblk = pl.BlockSpec((nlc, PEER_HEADS, PEER_N_KEYS // BF16_ROWS, 2, SUBLANES, PEER_SEL_TOK),
                         lambda i, e: (i, 0, 0, 0, 0, 0))
    kern = functools.partial(_peer_expert_kernel, t_tile=t_tile)
    return pl.pallas_call(
        kern,
        grid=(nt, ne),
        in_specs=[
            pl.BlockSpec((D_MODEL, t_tile), lambda i, e: (0, i)),
            pl.BlockSpec((1, PEER_EXP_TILE, D_MODEL), lambda i, e: (layer, e, 0)),
            pl.BlockSpec((1, PEER_CHUNK, D_MODEL), lambda i, e: (layer, jnp.minimum(e + 1, ne - 1) * nq, 0)),
            pl.BlockSpec((1, D_MODEL, PEER_EXP_TILE), lambda i, e: (layer, 0, e)),
            pl.BlockSpec((1, D_MODEL, PEER_CHUNK), lambda i, e: (layer, 0, jnp.maximum(e, 1) * nq - 1)),
            i1blk, i1blk, nfblk,
            pl.BlockSpec((t_tile, D_MODEL), lambda i, e: (i, 0)),
            pl.BlockSpec(mblk, mmap),
        ],
        out_specs=pl.BlockSpec((t_tile, D_MODEL), lambda i, e: (i, 0)),
        out_shape=jax.ShapeDtypeStruct(x2.shape, F32),
        scratch_shapes=[pltpu.VMEM((PEER_EXP_TILE // PEER_CHUNK, PEER_CHUNK, t_tile), F32),
                        pltpu.VMEM((PEER_EXP_TILE // PEER_CHUNK, PEER_CHUNK, t_tile), BF16),
                        pltpu.VMEM((D_MODEL, t_tile), F32)],
        compiler_params=_cparams(("parallel", "arbitrary")),
        name="peer_expert",
    )(xt, u_bf, u_bf, vt_bf, vt_bf, r1, c1, nf, x2, gate)


def _peer(x2, shift, scale, gate, gain, pw, tables, layer, *, t_tile):
    wq_bf, sk_bf, ones_t = pw
    u_bf, vt_bf = tables
    xt, r1, c1, nf = _peer_route(x2, shift, scale, gain, wq_bf, sk_bf, ones_t)
    return _peer_expert(xt, u_bf, vt_bf, layer, r1, c1, nf, x2, gate, t_tile=t_tile)


def _peer_weights(w_q, sub_keys):
    hc = 2 * PEER_HEADS
    seg = np.arange(D_MODEL) // PEER_HALF
    ones_t = jnp.asarray((seg[None, :] == np.arange(hc)[:, None]).astype(np.float32), dtype=BF16)
    return (w_q.astype(BF16), sub_keys.reshape(hc, PEER_N_KEYS, PEER_HALF).astype(BF16), ones_t)


def kernel(x_prompt, x_sample, c_prompt, c_sample, state_ssm, cache_kv_w128, cache_kv_w512, cache_kv_w2048,
           norm_mix, norm_ffn, ada_w, ada_b, ssm_lam_re, ssm_lam_im, ssm_log_dt, ssm_b_re, ssm_b_im, ssm_c_re,
           ssm_c_im, ssm_d, ssm_w_glu, attn_w_qkv, attn_q_norm, attn_k_norm, attn_w_o, peer_w_q,
           peer_sub_keys, peer_u, peer_v):
    bp, sp, _ = x_prompt.shape
    bs, ts, _ = x_sample.shape
    ns = bs * ts
    ones_att = _ones_blockdiag(ATT)

    mods = _adaln(jnp.concatenate([c_prompt, c_sample], axis=0), ada_w, ada_b)
    caches = (cache_kv_w128, cache_kv_w512, cache_kv_w2048)
    tab_p = _rope_tables(jnp.arange(sp, dtype=jnp.int32))
    pos_s = PAST_LEN + jnp.arange(ts, dtype=jnp.int32)
    tab_s = tuple(jnp.tile(t, (bs, 1)) for t in _rope_tables(pos_s))

    peer_tabs = (peer_u.astype(BF16), peer_v.astype(BF16).transpose(0, 2, 1))
    xp, xs = x_prompt, x_sample
    ssm_p, ssm_s = [], []
    kv_p = [[] for _ in range(N_GROUPS)]
    kv_s = [[] for _ in range(N_GROUPS)]
    for i in range(DEPTH):
        j = i // 2
        mp = [m.reshape(bp, 1, D_MODEL) for m in jnp.split(mods[i, :bp], 6, axis=-1)]
        ms_b = jnp.split(mods[i, bp:], 6, axis=-1)
        ms_rows = [jnp.repeat(m, ts, axis=0).reshape(1, ns, D_MODEL) for m in ms_b]
        if i % 2 == 0:
            sw = _s5_weights(ssm_lam_re[j], ssm_lam_im[j], ssm_log_dt[j], ssm_b_re[j], ssm_b_im[j],
                             ssm_c_re[j], ssm_c_im[j])
            wglu = ssm_w_glu[j].astype(BF16)
            mp_t = [m.reshape(1, bp, D_MODEL) for m in mp[:3]]
            xp_t, st_p = _s5_layer(xp.transpose(1, 0, 2), mp_t[0], mp_t[1], mp_t[2], norm_mix[i],
                                   jnp.zeros((bp, SSM_STATE_W), F32), sw, ssm_d[j], wglu, t_chunk=32)
            xp = xp_t.transpose(1, 0, 2)
            ms_t = [m.reshape(1, bs, D_MODEL) for m in ms_b[:3]]
            xs_t, st_s = _s5_layer(xs.transpose(1, 0, 2), ms_t[0], ms_t[1], ms_t[2], norm_mix[i],
                                   _state_to_rows(state_ssm[j]), sw, ssm_d[j], wglu, t_chunk=ts)
            xs = xs_t.transpose(1, 0, 2)
            ssm_p.append(_rows_to_state(st_p))
            ssm_s.append(_rows_to_state(st_s))
        else:
            wqkv = attn_w_qkv[j].astype(BF16)
            wo = attn_w_o[j].astype(BF16)
            k, v, q0, q1, k1, v1, q2, k2, v2 = _qkv_dilated(xp, mp[0], mp[1], norm_mix[i], wqkv, attn_q_norm[j],
                                                            attn_k_norm[j], tab_p, ones_att, t_tile=512)
            res = [_band_attention(q0, k, v, 1, N_GROUPS, 0, n_sub=2, n_col=1),
                   _band_attention(q1, k1, v1, DILATIONS[1], 1, 0, n_sub=2, n_col=1),
                   _band_attention(q2, k2, v2, DILATIONS[2], 1, 0, n_sub=1, n_col=2)]
            outs, lses = zip(*res)
            xp = _merge(outs, lses, xp, mp[2], wo, t_tile=512, dils=DILATIONS)
            for g in range(N_GROUPS):
                keep = min(WINDOWS[g], sp)
                gs = slice(g * ATT, (g + 1) * ATT)
                kv_p[g].append(jnp.stack([k[:, sp - keep:, gs], v[:, sp - keep:, gs]], axis=1)
                               .reshape(bp, 2, keep, N_HEADS, HEAD_DIM))
            xs_rows = xs.reshape(1, ns, D_MODEL)
            qs, ks, vs = _qkv(xs_rows, ms_rows[0], ms_rows[1], norm_mix[i], wqkv, attn_q_norm[j],
                              attn_k_norm[j], tab_s, ones_att, t_tile=ns)
            qs, ks, vs = (t.reshape(bs, ts, N_GROUPS * ATT) for t in (qs, ks, vs))
            res = _cache_attention(qs, ks, vs, [c.reshape(c.shape[:4] + (ATT,)) for c in caches], j)
            outs_s = [r.reshape(1, ns, ATT) for r in res[:N_GROUPS]]
            lses_s = [r.reshape(1, ns, ATT) for r in res[N_GROUPS:]]
            xs = _merge(outs_s, lses_s, xs_rows, ms_rows[2], wo, t_tile=ns,
                        dils=(1, 1, 1)).reshape(bs, ts, D_MODEL)
            for g in range(N_GROUPS):
                gs = slice(g * ATT, (g + 1) * ATT)
                kv_s[g].append(jnp.stack([ks[:, :, gs], vs[:, :, gs]], axis=1)
                               .reshape(bs, 2, ts, N_HEADS, HEAD_DIM))
        pw = _peer_weights(peer_w_q[i], peer_sub_keys[i])
        xp = _peer(xp.reshape(bp * sp, D_MODEL), mp[3], mp[4], mp[5], norm_ffn[i], pw, peer_tabs, i,
                   t_tile=512).reshape(bp, sp, D_MODEL)
        xs = _peer(xs.reshape(ns, D_MODEL), ms_rows[3], ms_rows[4], ms_rows[5], norm_ffn[i], pw, peer_tabs, i,
                   t_tile=ns).reshape(bs, ts, D_MODEL)
    return (xp, xs,
            jnp.stack(ssm_p), jnp.stack(kv_p[0]), jnp.stack(kv_p[1]), jnp.stack(kv_p[2]),
            jnp.stack(ssm_s), jnp.stack(kv_s[0]), jnp.stack(kv_s[1]), jnp.stack(kv_s[2]))
```

```python
import functools
import math

import jax
import jax.numpy as jnp
import numpy as np
from jax import lax
from jax.experimental import pallas as pl
from jax.experimental.pallas import tpu as pltpu

F32 = jnp.float32
BF16 = jnp.bfloat16

SUBLANES = 8
LANES = 128
BF16_ROWS = 16

D_MODEL = 1024
DEPTH = 4
PAST_LEN = 8192
EPS = 1e-6
NEG_INF = -1e30

SSM_P = 16
SSM_GROUPS = D_MODEL // SSM_P
SSM_N = 64
SSM_GB = 4
SSM_GB_CH = D_MODEL // SSM_GB
SSM_GB_ST = SSM_GROUPS * SSM_N // SSM_GB
SSM_STATE_W = 2 * SSM_GROUPS * SSM_N

N_HEADS = 8
HEAD_DIM = 64
ATT = N_HEADS * HEAD_DIM
WINDOWS = (128, 512, 2048)
DILATIONS = (1, 4, 16)
N_GROUPS = 3
N_BACK = 128
ROPE_DIM = HEAD_DIM // 4
ROPE_THETA = 500000.0

PEER_HEADS = 8
PEER_N_KEYS = 128
PEER_HALF = 64
PEER_TOPK = 16
PEER_SEL_TOK = LANES
PEER_EXP_TILE = 1024
PEER_I1_TILE = PEER_EXP_TILE // PEER_N_KEYS
PEER_MARK = 2.0 ** 100
PEER_CHUNK = 256

ADA_COL_TILE = 1024

VMEM_LIMIT = 56 * 1024 * 1024


def _cparams(sem):
    return pltpu.CompilerParams(dimension_semantics=sem, vmem_limit_bytes=VMEM_LIMIT)


def _modulate(x, gain, shift, scale):
    ms = jnp.mean(x * x, axis=-1, keepdims=True)
    return x * lax.rsqrt(ms + EPS) * gain * (1.0 + scale) + shift


def _seg_mean(x2, ones_bd):
    hi = x2.astype(BF16)
    lo = (x2 - hi.astype(F32)).astype(BF16)
    s = jnp.dot(hi, ones_bd, preferred_element_type=F32) + jnp.dot(lo, ones_bd, preferred_element_type=F32)
    return s * (1.0 / HEAD_DIM)


def _ones_blockdiag(n):
    seg = np.arange(n) // HEAD_DIM
    return jnp.asarray((seg[:, None] == seg[None, :]).astype(np.float32), dtype=BF16)


def _adaln_kernel(c_ref, w_ref, b_ref, o_ref):
    s = jax.nn.silu(c_ref[...])
    o_ref[0] = jnp.dot(s.astype(BF16), w_ref[0].astype(BF16), preferred_element_type=F32) + b_ref[0]


def _adaln(c_all, ada_w, ada_b):
    nb = c_all.shape[0]
    nt = 6 * D_MODEL // ADA_COL_TILE
    return pl.pallas_call(
        _adaln_kernel,
        grid=(DEPTH, nt),
        in_specs=[
            pl.BlockSpec((nb, D_MODEL), lambda l, j: (0, 0)),
            pl.BlockSpec((1, D_MODEL, ADA_COL_TILE), lambda l, j: (l, 0, j)),
            pl.BlockSpec((1, 1, ADA_COL_TILE), lambda l, j: (l, 0, j)),
        ],
        out_specs=pl.BlockSpec((1, nb, ADA_COL_TILE), lambda l, j: (l, 0, j)),
        out_shape=jax.ShapeDtypeStruct((DEPTH, nb, 6 * D_MODEL), F32),
        compiler_params=_cparams(("parallel", "parallel")),
        name="adaln",
    )(c_all, ada_w, ada_b.reshape(DEPTH, 1, 6 * D_MODEL))


def _s5_kernel(x_ref, shift_ref, scale_ref, gate_ref, gain_ref, h0_ref, bm_ref, cm_ref, are_ref, aim_ref,
               d_ref, wglu_ref, xo_ref, st_ref, bu_scr, state_scr, *, n_batch, n_time):
    i = pl.program_id(0)

    @pl.when(i == 0)
    def _():
        state_scr[...] = h0_ref[...]

    x3 = x_ref[...]
    a0, a1, _ = x3.shape
    rows = a0 * a1
    h3 = _modulate(x3, gain_ref[...], shift_ref[...], scale_ref[...])
    u = h3.reshape(rows, D_MODEL)
    ub = u.astype(BF16)
    for gb in range(SSM_GB):
        bu_scr[:, gb * 2 * SSM_GB_ST:(gb + 1) * 2 * SSM_GB_ST] = jnp.dot(
            ub[:, gb * SSM_GB_CH:(gb + 1) * SSM_GB_CH], bm_ref[gb], preferred_element_type=F32)

    for bb in range(n_batch // SUBLANES):
        for gb in range(SSM_GB):
            c_re = gb * 2 * SSM_GB_ST
            c_im = c_re + SSM_GB_ST
            ar = jnp.broadcast_to(are_ref[gb], (SUBLANES, SSM_GB_ST))
            ai = jnp.broadcast_to(aim_ref[gb], (SUBLANES, SSM_GB_ST))
            xr = state_scr[bb * 8:(bb + 1) * 8, c_re:c_re + SSM_GB_ST]
            xi = state_scr[bb * 8:(bb + 1) * 8, c_im:c_im + SSM_GB_ST]
            for t in range(n_time):
                rsel = pl.ds(t * n_batch + bb * 8, 8)
                br = bu_scr[rsel, c_re:c_re + SSM_GB_ST]
                bi = bu_scr[rsel, c_im:c_im + SSM_GB_ST]
                nr = ar * xr - ai * xi + br
                ni = ar * xi + ai * xr + bi
                bu_scr[rsel, c_re:c_re + SSM_GB_ST] = nr
                bu_scr[rsel, c_im:c_im + SSM_GB_ST] = ni
                xr, xi = nr, ni
            state_scr[bb * 8:(bb + 1) * 8, c_re:c_re + SSM_GB_ST] = xr
            state_scr[bb * 8:(bb + 1) * 8, c_im:c_im + SSM_GB_ST] = xi

    ys = []
    for gb in range(SSM_GB):
        st = bu_scr[:, gb * 2 * SSM_GB_ST:(gb + 1) * 2 * SSM_GB_ST].astype(BF16)
        ys.append(jnp.dot(st, cm_ref[gb], preferred_element_type=F32))
    y = jnp.concatenate(ys, axis=-1) + d_ref[...] * u
    z = jnp.dot(y.astype(BF16), wglu_ref[...], preferred_element_type=F32)
    out = z[:, :D_MODEL] * jax.nn.sigmoid(z[:, D_MODEL:])
    xo_ref[...] = x3 + gate_ref[...] * out.reshape(a0, a1, D_MODEL)

    @pl.when(i == pl.num_programs(0) - 1)
    def _():
        st_ref[...] = state_scr[...]


def _s5_weights(lam_re, lam_im, log_dt, b_re, b_im, c_re, c_im):
    dt = jnp.exp(log_dt)[:, None]
    mag = jnp.exp(lam_re * dt)
    a_re = mag * jnp.cos(lam_im * dt)
    a_im = mag * jnp.sin(lam_im * dt)
    den = lam_re * lam_re + lam_im * lam_im
    f_re = ((a_re - 1.0) * lam_re + a_im * lam_im) / den
    f_im = (a_im * lam_re - (a_re - 1.0) * lam_im) / den
    bb_re = f_re[..., None] * b_re - f_im[..., None] * b_im
    bb_im = f_re[..., None] * b_im + f_im[..., None] * b_re
    eye = jnp.eye(16, dtype=F32)

    def in_blk(b):
        b4 = b.reshape(SSM_GB, 16, SSM_N, SSM_P)
        return jnp.einsum('bgnp,gh->bgphn', b4, eye).reshape(SSM_GB, SSM_GB_CH, SSM_GB_ST)

    def out_blk(c):
        c4 = c.reshape(SSM_GB, 16, SSM_P, SSM_N)
        return jnp.einsum('bgpn,gh->bgnhp', c4, eye).reshape(SSM_GB, SSM_GB_ST, SSM_GB_CH)

    bm = jnp.concatenate([in_blk(bb_re), in_blk(bb_im)], axis=2).astype(BF16)
    cm = jnp.concatenate([out_blk(c_re), -out_blk(c_im)], axis=1).astype(BF16)
    return bm, cm, a_re.reshape(SSM_GB, 1, SSM_GB_ST), a_im.reshape(SSM_GB, 1, SSM_GB_ST)


def _state_to_rows(st):
    b = st.shape[0]
    return st.reshape(b, 2, SSM_GB, SSM_GB_ST).transpose(0, 2, 1, 3).reshape(b, SSM_STATE_W)


def _rows_to_state(rows):
    b = rows.shape[0]
    return rows.reshape(b, SSM_GB, 2, SSM_GB_ST).transpose(0, 2, 1, 3).reshape(b, 2, SSM_GROUPS, SSM_N)


def _s5_layer(x3, shift, scale, gate, gain, h0_rows, weights, d_skip, w_glu, *, t_chunk):
    bm, cm, a_re, a_im = weights
    seq, n_batch = x3.shape[0], x3.shape[1]
    n_time = t_chunk
    blk = (n_time, n_batch, D_MODEL)
    xmap = lambda i: (i, 0, 0)
    mblk = (1, n_batch, D_MODEL)
    grid = seq // n_time
    rows = n_batch * n_time
    const3 = lambda i: (0, 0, 0)
    const2 = lambda i: (0, 0)
    kern = functools.partial(_s5_kernel, n_batch=n_batch, n_time=n_time)
    return pl.pallas_call(
        kern,
        grid=(grid,),
        in_specs=[
            pl.BlockSpec(blk, xmap),
            pl.BlockSpec(mblk, const3), pl.BlockSpec(mblk, const3), pl.BlockSpec(mblk, const3),
            pl.BlockSpec((1, 1, D_MODEL), const3),
            pl.BlockSpec((n_batch, SSM_STATE_W), const2),
            pl.BlockSpec(bm.shape, const3), pl.BlockSpec(cm.shape, const3),
            pl.BlockSpec(a_re.shape, const3), pl.BlockSpec(a_im.shape, const3),
            pl.BlockSpec((1, D_MODEL), const2),
            pl.BlockSpec((D_MODEL, 2 * D_MODEL), const2),
        ],
        out_specs=[pl.BlockSpec(blk, xmap), pl.BlockSpec((n_batch, SSM_STATE_W), const2)],
        out_shape=[jax.ShapeDtypeStruct(x3.shape, F32), jax.ShapeDtypeStruct((n_batch, SSM_STATE_W), F32)],
        scratch_shapes=[pltpu.VMEM((rows, SSM_STATE_W), F32), pltpu.VMEM((n_batch, SSM_STATE_W), F32)],
        compiler_params=_cparams(("arbitrary",)),
        name="s5_layer",
    )(x3, shift, scale, gate, gain.reshape(1, 1, D_MODEL), h0_rows, bm, cm, a_re, a_im,
      d_skip.reshape(1, D_MODEL), w_glu)


def _qkv_kernel(x_ref, shift_ref, scale_ref, gain_ref, w_ref, qg_ref, kg_ref, rc_ref, rs1_ref, rs2_ref,
                ones_ref, q_ref, k_ref, v_ref):
    h = _modulate(x_ref[0], gain_ref[...], shift_ref[0], scale_ref[0])
    qkv = jnp.dot(h.astype(BF16), w_ref[...], preferred_element_type=F32)
    rc, rs1, rs2 = rc_ref[...], rs1_ref[...], rs2_ref[...]
    ones_bd = ones_ref[...]

    def norm_rope(t, g):
        tn = t * lax.rsqrt(_seg_mean(t * t, ones_bd) + EPS) * g
        return tn * rc + pltpu.roll(tn, ATT - ROPE_DIM // 2, 1) * rs1 + pltpu.roll(tn, ROPE_DIM // 2, 1) * rs2

    q_ref[0] = norm_rope(qkv[:, :ATT], qg_ref[...])
    k_ref[0] = norm_rope(qkv[:, ATT:2 * ATT], kg_ref[...])
    v_ref[0] = qkv[:, 2 * ATT:]


def _dilate(x, scr, out_ref, d):
    t = x.shape[0]
    for kc in range(ATT // LANES):
        scr[kc] = x[:, kc * LANES:(kc + 1) * LANES]
    for r in range(d):
        for kc in range(ATT // LANES):
            c0 = r * ATT + kc * LANES
            out_ref[0, :, c0:c0 + LANES] = scr[kc, pl.ds(r, t // d, stride=d), :]


def _qkv_dil_kernel(x_ref, shift_ref, scale_ref, gain_ref, w_ref, qg_ref, kg_ref, rc_ref, rs1_ref, rs2_ref,
                    ones_ref, kn_ref, vn_ref, q0_ref, q1_ref, k1_ref, v1_ref, q2_ref, k2_ref, v2_ref, scr):
    g = pl.program_id(2)
    h = _modulate(x_ref[0], gain_ref[...], shift_ref[0], scale_ref[0])
    qkv = jnp.dot(h.astype(BF16), w_ref[...], preferred_element_type=F32)
    rc, rs1, rs2 = rc_ref[...], rs1_ref[...], rs2_ref[...]
    ones_bd = ones_ref[...]

    def norm_rope(t, gn):
        tn = t * lax.rsqrt(_seg_mean(t * t, ones_bd) + EPS) * gn
        return tn * rc + pltpu.roll(tn, ATT - ROPE_DIM // 2, 1) * rs1 + pltpu.roll(tn, ROPE_DIM // 2, 1) * rs2

    q = norm_rope(qkv[:, :ATT], qg_ref[...])
    k = norm_rope(qkv[:, ATT:2 * ATT], kg_ref[...])
    v = qkv[:, 2 * ATT:]
    kn_ref[0] = k
    vn_ref[0] = v

    @pl.when(g == 0)
    def _():
        q0_ref[0] = q

    for gi, refs in ((1, (q1_ref, k1_ref, v1_ref)), (2, (q2_ref, k2_ref, v2_ref))):
        @pl.when(g == gi)
        def _():
            for val, ref in zip((q, k, v), refs):
                _dilate(val, scr, ref, DILATIONS[gi])


def _qkv_dilated(x3, shift, scale, gain, w_qkv_bf, q_gain, k_gain, tables, ones_bd, *, t_tile):
    bq, sq, _ = x3.shape
    rc, rs1, rs2 = tables
    qg = jnp.tile(q_gain, N_HEADS).reshape(1, ATT)
    kg = jnp.tile(k_gain, N_HEADS).reshape(1, ATT)
    mmap = lambda b, i, g: (b, 0, 0)
    tmap = lambda b, i, g: (i, 0)
    cmap = lambda b, i, g: (0, 0)
    gmap = lambda b, i, g: (b, i, g)
    bmap = lambda b, i, g: (b, i, 0)
    nat = jax.ShapeDtypeStruct((bq, sq, N_GROUPS * ATT), F32)
    out_shape = [nat, nat, jax.ShapeDtypeStruct((bq, sq, ATT), F32)]
    out_specs = [pl.BlockSpec((1, t_tile, ATT), gmap)] * 2 + [pl.BlockSpec((1, t_tile, ATT), bmap)]
    for gi in (1, 2):
        d = DILATIONS[gi]
        out_shape += [jax.ShapeDtypeStruct((bq, sq // d, d * ATT), F32)] * 3
        out_specs += [pl.BlockSpec((1, t_tile // d, d * ATT), bmap)] * 3
    return pl.pallas_call(
        _qkv_dil_kernel,
        grid=(bq, sq // t_tile, N_GROUPS),
        in_specs=[
            pl.BlockSpec((1, t_tile, D_MODEL), bmap),
            pl.BlockSpec((1, 1, D_MODEL), mmap), pl.BlockSpec((1, 1, D_MODEL), mmap),
            pl.BlockSpec((1, D_MODEL), cmap),
            pl.BlockSpec((D_MODEL, 3 * ATT), lambda b, i, g: (0, g)),
            pl.BlockSpec((1, ATT), cmap), pl.BlockSpec((1, ATT), cmap),
            pl.BlockSpec((t_tile, ATT), tmap), pl.BlockSpec((t_tile, ATT), tmap), pl.BlockSpec((t_tile, ATT), tmap),
            pl.BlockSpec((ATT, ATT), cmap),
        ],
        out_specs=out_specs,
        out_shape=out_shape,
        scratch_shapes=[pltpu.VMEM((ATT // LANES, t_tile, LANES), F32)],
        compiler_params=_cparams(("parallel", "parallel", "arbitrary")),
        name="qkv_proj_dilated",
    )(x3, shift, scale, gain.reshape(1, D_MODEL), w_qkv_bf, qg, kg, rc, rs1, rs2, ones_bd)


def _rope_tables(pos):
    half = ROPE_DIM // 2
    inv = ROPE_THETA ** (-jnp.arange(half, dtype=F32) / half)
    ang = pos.astype(F32)[:, None] * inv[None, :]
    cos, sin = jnp.cos(ang), jnp.sin(ang)
    lane = np.arange(ATT) % HEAD_DIM
    fidx = lane % half
    first = jnp.asarray(lane < half)
    second = jnp.asarray((lane >= half) & (lane < ROPE_DIM))
    cl, sl = cos[:, fidx], sin[:, fidx]
    rc = jnp.where(first | second, cl, 1.0)
    rs1 = jnp.where(first, -sl, 0.0)
    rs2 = jnp.where(second, sl, 0.0)
    return rc, rs1, rs2


def _qkv(x3, shift, scale, gain, w_qkv_bf, q_gain, k_gain, tables, ones_bd, *, t_tile):
    bq, sq, _ = x3.shape
    mrows = shift.shape[1]
    mt = t_tile if mrows == sq else 1
    mmap = (lambda b, i, g: (b, i, 0)) if mrows == sq else (lambda b, i, g: (b, 0, 0))
    rc, rs1, rs2 = tables
    qg = jnp.tile(q_gain, N_HEADS).reshape(1, ATT)
    kg = jnp.tile(k_gain, N_HEADS).reshape(1, ATT)
    tmap = lambda b, i, g: (i, 0)
    cmap = lambda b, i, g: (0, 0)
    omap = lambda b, i, g: (b, i, g)
    oshape = jax.ShapeDtypeStruct((bq, sq, N_GROUPS * ATT), F32)
    return pl.pallas_call(
        _qkv_kernel,
        grid=(bq, sq // t_tile, N_GROUPS),
        in_specs=[
            pl.BlockSpec((1, t_tile, D_MODEL), lambda b, i, g: (b, i, 0)),
            pl.BlockSpec((1, mt, D_MODEL), mmap), pl.BlockSpec((1, mt, D_MODEL), mmap),
            pl.BlockSpec((1, D_MODEL), cmap),
            pl.BlockSpec((D_MODEL, 3 * ATT), lambda b, i, g: (0, g)),
            pl.BlockSpec((1, ATT), cmap), pl.BlockSpec((1, ATT), cmap),
            pl.BlockSpec((t_tile, ATT), tmap), pl.BlockSpec((t_tile, ATT), tmap), pl.BlockSpec((t_tile, ATT), tmap),
            pl.BlockSpec((ATT, ATT), cmap),
        ],
        out_specs=[pl.BlockSpec((1, t_tile, ATT), omap)] * 3,
        out_shape=[oshape] * 3,
        compiler_params=_cparams(("parallel", "parallel", "arbitrary")),
        name="qkv_proj",
    )(x3, shift, scale, gain.reshape(1, D_MODEL), w_qkv_bf, qg, kg, rc, rs1, rs2, ones_bd)


def _band_kernel(q_ref, kp_ref, kc_ref, vp_ref, vc_ref, o_ref, l_ref, *, n_sub, n_col):
    i = pl.program_id(2)
    iq = lax.broadcasted_iota(jnp.int32, (N_BACK, 2 * N_BACK), 0)
    ik = lax.broadcasted_iota(jnp.int32, (N_BACK, 2 * N_BACK), 1)
    dist = iq + N_BACK - ik
    band = (dist >= 0) & (dist <= N_BACK)
    band_first = band & ((ik >= N_BACK) | (i > 0))
    for c in range(n_col):
        c0 = c * ATT
        k_all = jnp.concatenate([kp_ref[0, :, c0:c0 + ATT], kc_ref[0, :, c0:c0 + ATT]], axis=0).astype(BF16)
        v_all = jnp.concatenate([vp_ref[0, :, c0:c0 + ATT], vc_ref[0, :, c0:c0 + ATT]], axis=0).astype(BF16)
        for u in range(n_sub):
            rows = slice(u * N_BACK, (u + 1) * N_BACK)
            q = q_ref[0, rows, c0:c0 + ATT].astype(BF16)
            k = k_all[u * N_BACK:(u + 2) * N_BACK]
            v = v_all[u * N_BACK:(u + 2) * N_BACK]
            mask = band_first if u == 0 else band
            for h in range(N_HEADS):
                sl = slice(h * HEAD_DIM, (h + 1) * HEAD_DIM)
                ol = slice(c0 + h * HEAD_DIM, c0 + (h + 1) * HEAD_DIM)
                s = lax.dot_general(q[:, sl], k[:, sl], (((1,), (1,)), ((), ())),
                                    preferred_element_type=F32) * (HEAD_DIM ** -0.5)
                s = jnp.where(mask, s, NEG_INF)
                m = jnp.max(s, axis=-1, keepdims=True)
                p = jnp.exp(s - m)
                den = jnp.sum(p, axis=-1, keepdims=True)
                o = jnp.dot(p.astype(BF16), v[:, sl], preferred_element_type=F32) / den
                o_ref[0, rows, ol] = o
                l_ref[0, rows, ol] = jnp.broadcast_to(m + jnp.log(den), (N_BACK, HEAD_DIM))


def _band_attention(qv, kv, vv, d, kcols, kcol0, *, n_sub, n_col):
    b, sub, _ = qv.shape
    nb = sub // N_BACK
    assert nb % n_sub == 0 and d % n_col == 0 and (n_col == 1 or kcols == 1)
    qcur = lambda bi, r, i: (bi, i, r)
    cur = lambda bi, r, i: (bi, i, r * kcols + kcol0)
    prev = lambda bi, r, i: (bi, jnp.maximum(i * n_sub - 1, 0), r * kcols + kcol0)
    blk = (1, n_sub * N_BACK, n_col * ATT)
    pblk = (1, N_BACK, n_col * ATT)
    oshape = jax.ShapeDtypeStruct((b, sub, d * ATT), F32)
    return pl.pallas_call(
        functools.partial(_band_kernel, n_sub=n_sub, n_col=n_col),
        grid=(b, d // n_col, nb // n_sub),
        in_specs=[pl.BlockSpec(blk, qcur), pl.BlockSpec(pblk, prev), pl.BlockSpec(blk, cur),
                  pl.BlockSpec(pblk, prev), pl.BlockSpec(blk, cur)],
        out_specs=[pl.BlockSpec(blk, qcur), pl.BlockSpec(blk, qcur)],
        out_shape=[oshape, oshape],
        compiler_params=_cparams(("parallel", "parallel", "arbitrary")),
        name="band_attention",
    )(qv, kv, kv, vv, vv)


def _cache_attn_kernel(q_ref, k_ref, v_ref, c0_ref, c1_ref, c2_ref, *out_refs):
    n_new = q_ref.shape[1]
    rows = n_new * N_HEADS
    lane_head = lax.broadcasted_iota(jnp.int32, (N_HEADS, ATT), 1) // HEAD_DIM
    head_mask = (lane_head == lax.broadcasted_iota(jnp.int32, (N_HEADS, ATT), 0)).astype(F32)
    row_t = lax.broadcasted_iota(jnp.int32, (rows, 1), 0) // N_HEADS
    caches = (c0_ref, c1_ref, c2_ref)
    for g in range(N_GROUPS):
        d, win = DILATIONS[g], WINDOWS[g]
        gs = slice(g * ATT, (g + 1) * ATT)
        q = q_ref[0][:, gs]
        kn = k_ref[0][:, gs]
        vn = v_ref[0][:, gs]
        kc = caches[g][0, 0, 0].astype(BF16)
        vc = caches[g][0, 0, 1].astype(BF16)
        qbd = jnp.concatenate([q[t:t + 1, :] * head_mask for t in range(n_new)], axis=0)
        s_c = lax.dot_general(qbd.astype(BF16), kc, (((1,), (1,)), ((), ())),
                              preferred_element_type=F32) * (HEAD_DIM ** -0.5)
        col = lax.broadcasted_iota(jnp.int32, (rows, win), 1)
        valid_c = (col >= row_t) & (((col - row_t) & (d - 1)) == 0)
        s_c = jnp.where(valid_c, s_c, NEG_INF)
        s_n = []
        for t2 in range(n_new):
            sn = jnp.sum(qbd * kn[t2:t2 + 1, :], axis=-1, keepdims=True) * (HEAD_DIM ** -0.5)
            valid_n = (row_t >= t2) & (((row_t - t2) & (d - 1)) == 0)
            s_n.append(jnp.where(valid_n, sn, NEG_INF))
        m = jnp.max(s_c, axis=-1, keepdims=True)
        for sn in s_n:
            m = jnp.maximum(m, sn)
        p_c = jnp.exp(s_c - m)
        den = jnp.sum(p_c, axis=-1, keepdims=True)
        o = jnp.dot(p_c.astype(BF16), vc, preferred_element_type=F32)
        for t2 in range(n_new):
            p_n = jnp.exp(s_n[t2] - m)
            den = den + p_n
            o = o + p_n * vn[t2:t2 + 1, :]
        o = o / den
        lse = m + jnp.log(den)
        for t in range(n_new):
            rs = slice(t * N_HEADS, (t + 1) * N_HEADS)
            out_refs[g][0, t:t + 1, :] = jnp.sum(o[rs] * head_mask, axis=0, keepdims=True)
            out_refs[N_GROUPS + g][0, t:t + 1, :] = jnp.sum(lse[rs] * head_mask, axis=0, keepdims=True)


def _cache_attention(q, k, v, caches, layer):
    bd, t, w = q.shape
    xmap = lambda b: (b, 0, 0)
    cmap = lambda b: (layer, b, 0, 0, 0)
    oshape = jax.ShapeDtypeStruct((bd, t, ATT), F32)
    return pl.pallas_call(
        _cache_attn_kernel,
        grid=(bd,),
        in_specs=[pl.BlockSpec((1, t, w), xmap)] * 3
        + [pl.BlockSpec((1, 1, 2, WINDOWS[g], ATT), cmap) for g in range(N_GROUPS)],
        out_specs=[pl.BlockSpec((1, t, ATT), xmap)] * (2 * N_GROUPS),
        out_shape=[oshape] * (2 * N_GROUPS),
        compiler_params=_cparams(("parallel",)),
        name="cache_attention",
    )(q, k, v, *caches)


def _undilate(ref, scr, d):
    if d == 1:
        return ref[0]
    rows = ref.shape[1]
    for r in range(d):
        for kc in range(ATT // LANES):
            c0 = r * ATT + kc * LANES
            scr[kc, pl.ds(r, rows, stride=d), :] = ref[0, :, c0:c0 + LANES]
    return jnp.concatenate([scr[kc] for kc in range(ATT // LANES)], axis=1)


def _merge_kernel(o0, o1, o2, l0, l1, l2, x_ref, gate_ref, wo_ref, xo_ref, scr, *, dils):
    la, lb, lc = [_undilate(r, scr, d) for r, d in zip((l0, l1, l2), dils)]
    m = jnp.maximum(jnp.maximum(la, lb), lc)
    ea, eb, ec = jnp.exp(la - m), jnp.exp(lb - m), jnp.exp(lc - m)
    den = ea + eb + ec
    o = (ea / den) * _undilate(o0, scr, dils[0])
    o = o + (eb / den) * _undilate(o1, scr, dils[1])
    o = o + (ec / den) * _undilate(o2, scr, dils[2])
    out = jnp.dot(o.astype(BF16), wo_ref[...], preferred_element_type=F32)
    xo_ref[0] = x_ref[0] + gate_ref[0] * out


def _merge(outs, lses, x3, gate, w_o_bf, *, t_tile, dils):
    bq, sq, _ = x3.shape
    mrows = gate.shape[1]
    mt = t_tile if mrows == sq else 1
    mmap = (lambda b, i: (b, i, 0)) if mrows == sq else (lambda b, i: (b, 0, 0))
    tmap = lambda b, i: (b, i, 0)
    ablk = [pl.BlockSpec((1, t_tile // d, d * ATT), tmap) for d in dils]
    return pl.pallas_call(
        functools.partial(_merge_kernel, dils=dils),
        grid=(bq, sq // t_tile),
        in_specs=ablk + ablk + [pl.BlockSpec((1, t_tile, D_MODEL), tmap), pl.BlockSpec((1, mt, D_MODEL), mmap),
                                pl.BlockSpec((ATT, D_MODEL), lambda b, i: (0, 0))],
        out_specs=pl.BlockSpec((1, t_tile, D_MODEL), tmap),
        out_shape=jax.ShapeDtypeStruct(x3.shape, F32),
        scratch_shapes=[pltpu.VMEM((ATT // LANES, t_tile, LANES), F32)],
        compiler_params=_cparams(("parallel", "parallel")),
        name="attn_merge",
    )(*outs, *lses, x3, gate, w_o_bf)


def _top16(s, exact):
    vals = s
    tops = []
    if not exact:
        for r in range(PEER_TOPK):
            m = jnp.max(vals, axis=0, keepdims=True)
            vals = jnp.where(vals == m, -(r + 1.0) * PEER_MARK, vals)
            tops.append(m)
        t = vals * (-1.0 / PEER_MARK)
        return tops, jnp.where(t >= 0.5, t - 1.0, float(PEER_N_KEYS))
    rank = jnp.full(s.shape, float(PEER_N_KEYS), F32)
    key = lax.broadcasted_iota(jnp.int32, s.shape, 0).astype(F32)
    for r in range(PEER_TOPK):
        m = jnp.max(vals, axis=0, keepdims=True)
        first = jnp.min(jnp.where(vals == m, key, float(PEER_N_KEYS)), axis=0, keepdims=True)
        hit = key == first
        rank = jnp.where(hit, float(r), rank)
        vals = jnp.where(hit, -jnp.inf, vals)
        tops.append(m)
    return tops, rank


def _count_taken(rank):
    return jnp.sum(jnp.where(rank < float(PEER_N_KEYS), 1.0, 0.0), axis=0, keepdims=True)


def _route_head(s1, s2, exact):
    n_tok = s1.shape[1]
    top1, rank1 = _top16(s1, exact)
    top2, rank2 = _top16(s2, exact)
    row = lax.broadcasted_iota(jnp.int32, (PEER_TOPK, n_tok), 0)
    v1 = jnp.zeros((PEER_TOPK, n_tok), F32)
    for r in range(PEER_TOPK):
        v1 = jnp.where(row == r, top1[r], v1)
    best = top1[0] + top2[0]
    zsum = jnp.zeros_like(best)
    if exact:
        cands = [v1 + top2[b] for b in range(PEER_TOPK)]
        flat = [(row * PEER_TOPK + b).astype(F32) for b in range(PEER_TOPK)]
    else:
        v2 = jnp.zeros((PEER_TOPK, n_tok), F32)
        for r in range(PEER_TOPK):
            v2 = jnp.where(row == r, top2[r], v2)
        cands = [v1[:8] + top2[b] for b in range(8)] + [v1[8:] + top2[0], top1[0] + v2[8:]]
    taken = [jnp.zeros_like(c) for c in cands]
    for _ in range(PEER_TOPK):
        m = cands[0]
        for c in cands[1:]:
            m = jnp.maximum(m, c)
        m = jnp.max(m, axis=0, keepdims=True)
        if exact:
            fmin = jnp.full_like(cands[0], float(PEER_TOPK * PEER_TOPK))
            for c, f in zip(cands, flat):
                fmin = jnp.minimum(fmin, jnp.where(c == m, f, float(PEER_TOPK * PEER_TOPK)))
            fmin = jnp.min(fmin, axis=0, keepdims=True)
            for j in range(len(cands)):
                hit = flat[j] == fmin
                taken[j] = jnp.where(hit, 1.0, taken[j])
                cands[j] = jnp.where(hit, -jnp.inf, cands[j])
        else:
            cands = [jnp.where(c == m, -PEER_MARK, c) for c in cands]
        zsum = zsum + jnp.exp(m - best)
    if not exact:
        taken = [jnp.where(c <= -0.5 * PEER_MARK, 1.0, 0.0) for c in cands]
    tie = jnp.zeros_like(best)
    if exact:
        cnt = taken[0]
        for j in range(1, PEER_TOPK):
            cnt = cnt + taken[j]
    else:
        cnt = taken[0]
        for j in range(1, 8):
            cnt = cnt + taken[j]
        cnt = jnp.concatenate([cnt, taken[8]], axis=0)
        cnt = cnt + jnp.where(row == 0, jnp.sum(taken[9], axis=0, keepdims=True), 0.0)
        n_pairs = jnp.sum(cnt, axis=0, keepdims=True)
        low = jnp.minimum(jnp.min(s1, axis=0, keepdims=True), jnp.min(s2, axis=0, keepdims=True))
        bad = ((_count_taken(rank1) != float(PEER_TOPK)) | (_count_taken(rank2) != float(PEER_TOPK))
               | (n_pairs != float(PEER_TOPK)) | (low <= -0.25 * PEER_MARK))
        tie = jnp.where(bad, 1.0, 0.0)
    n2 = jnp.where(rank2 < 1.0, jnp.sum(cnt[8:], axis=0, keepdims=True), 0.0)
    for a in range(8):
        n2 = n2 + jnp.where(rank2 < cnt[a:a + 1], 1.0, 0.0)
    return rank1, jnp.exp(s1 - top1[0]), n2, jnp.exp(s2 - top2[0]) / zsum, tie


def _peer_route_kernel(x_ref, shift_ref, scale_ref, gain_ref, wq_ref, sk_ref, ones_ref,
                       xt_ref, r1_ref, c1_ref, nf_ref, s_scr, tie_scr):
    hm = _modulate(x_ref[...], gain_ref[...], shift_ref[0], scale_ref[0])
    hb = hm.astype(BF16)
    xt_ref[...] = hm.T.astype(BF16)
    q = jnp.dot(hb, wq_ref[...], preferred_element_type=F32)
    nt_dims = (((1,), (1,)), ((), ()))
    q2 = q * q
    q2_hi = q2.astype(BF16)
    q2_lo = (q2 - q2_hi.astype(F32)).astype(BF16)
    ms_t = (lax.dot_general(ones_ref[...], q2_hi, nt_dims, preferred_element_type=F32)
            + lax.dot_general(ones_ref[...], q2_lo, nt_dims, preferred_element_type=F32)) * (1.0 / PEER_HALF)
    inv_t = lax.rsqrt(ms_t + EPS)
    qb = q.astype(BF16)
    for hc in range(2 * PEER_HEADS):
        raw = lax.dot_general(sk_ref[hc], qb[:, hc * PEER_HALF:(hc + 1) * PEER_HALF], nt_dims,
                              preferred_element_type=F32)
        s_scr[hc * PEER_N_KEYS:(hc + 1) * PEER_N_KEYS, :] = raw * inv_t[hc:hc + 1, :]

    def head(h, tie, exact):
        base = pl.multiple_of(h * 2 * PEER_N_KEYS, 2 * PEER_N_KEYS)
        s1 = s_scr[pl.ds(base, PEER_N_KEYS), :]
        s2 = s_scr[pl.ds(base + PEER_N_KEYS, PEER_N_KEYS), :]
        rank1, c1, n2, f2, tie_h = _route_head(s1, s2, exact)
        r1_ref[0, h] = rank1
        c1_ref[0, h] = c1
        nf_ref[0, h, :, 0] = pltpu.bitcast(n2.astype(BF16), jnp.int32).reshape(PEER_N_KEYS // BF16_ROWS, SUBLANES, PEER_SEL_TOK)
        nf_ref[0, h, :, 1] = pltpu.bitcast(f2.astype(BF16), jnp.int32).reshape(PEER_N_KEYS // BF16_ROWS, SUBLANES, PEER_SEL_TOK)
        if not exact:
            tie_scr[h] = tie_h
        return jnp.maximum(tie, tie_h)

    no_tie = jnp.zeros((1, s_scr.shape[1]), F32)
    tie = lax.fori_loop(0, PEER_HEADS, functools.partial(head, exact=False), no_tie, unroll=8)

    @pl.when(jnp.max(tie) > 0.0)
    def _():
        def redo(h, carry):
            @pl.when(jnp.max(tie_scr[h]) > 0.0)
            def _():
                head(h, no_tie, True)
            return carry

        lax.fori_loop(0, PEER_HEADS, redo, 0)


def _peer_route(x2, shift, scale, gain, wq_bf, sk_bf, ones_t):
    n = x2.shape[0]
    tt = PEER_SEL_TOK
    nt = n // tt
    per_mod = nt // shift.shape[0]
    mrows = shift.shape[1]
    if mrows == 1:
        mblk, mmap = (1, 1, D_MODEL), (lambda i: (i // per_mod, 0, 0))
    else:
        mblk, mmap = (1, tt, D_MODEL), (lambda i: (0, i, 0))
    cmap = lambda i: (0, 0)
    nrt = PEER_N_KEYS // BF16_ROWS
    hshape = jax.ShapeDtypeStruct((nt, PEER_HEADS, PEER_N_KEYS, tt), F32)
    hblk = pl.BlockSpec((1, PEER_HEADS, PEER_N_KEYS, tt), lambda i: (i, 0, 0, 0))
    nfshape = jax.ShapeDtypeStruct((nt, PEER_HEADS, nrt, 2, SUBLANES, tt), jnp.int32)
    nfblk = pl.BlockSpec((1, PEER_HEADS, nrt, 2, SUBLANES, tt), lambda i: (i, 0, 0, 0, 0, 0))
    return pl.pallas_call(
        _peer_route_kernel,
        grid=(nt,),
        in_specs=[
            pl.BlockSpec((tt, D_MODEL), lambda i: (i, 0)),
            pl.BlockSpec(mblk, mmap), pl.BlockSpec(mblk, mmap),
            pl.BlockSpec((1, D_MODEL), cmap),
            pl.BlockSpec((D_MODEL, D_MODEL), cmap),
            pl.BlockSpec((2 * PEER_HEADS, PEER_N_KEYS, PEER_HALF), lambda i: (0, 0, 0)),
            pl.BlockSpec((2 * PEER_HEADS, D_MODEL), cmap),
        ],
        out_specs=[pl.BlockSpec((D_MODEL, tt), lambda i: (0, i)), hblk, hblk, nfblk],
        out_shape=[jax.ShapeDtypeStruct((D_MODEL, n), BF16), hshape, hshape, nfshape],
        scratch_shapes=[pltpu.VMEM((2 * PEER_HEADS * PEER_N_KEYS, tt), F32), pltpu.VMEM((PEER_HEADS, 1, tt), F32)],
        compiler_params=_cparams(("parallel",)),
        name="peer_route",
    )(x2, shift, scale, gain.reshape(1, D_MODEL), wq_bf, sk_bf, ones_t)


def _gelu(x):
    return 0.5 * x * (1.0 + lax.erf(x * np.float32(math.sqrt(0.5))))


def _peer_expert_kernel(xt_ref, u_ref, un_ref, vt_ref, vp_ref, r1_ref, c1_ref, nf_ref, x_ref, gate_ref, xo_ref,
                        act_scr, w_scr, acc_scr, *, t_tile):
    e = pl.program_id(1)
    nq = PEER_EXP_TILE // PEER_CHUNK
    last = slice((nq - 1) * PEER_CHUNK, nq * PEER_CHUNK)

    @pl.when(e == 0)
    def _():
        acc_scr[...] = jnp.zeros_like(acc_scr)
        w_scr[nq - 1] = jnp.zeros((PEER_CHUNK, t_tile), BF16)
        act_scr[0] = jnp.dot(u_ref[0, :PEER_CHUNK, :], xt_ref[...], preferred_element_type=F32)

    def stage_a(q):
        qs = slice(q * PEER_CHUNK, (q + 1) * PEER_CHUNK)
        act_scr[q] = jnp.dot(u_ref[0, qs, :], xt_ref[...], preferred_element_type=F32)

    def stage_a_next():
        act_scr[0] = jnp.dot(un_ref[0], xt_ref[...], preferred_element_type=F32)

    def stage_b(q):
        for i1l in range(PEER_CHUNK // PEER_N_KEYS):
            i1 = q * (PEER_CHUNK // PEER_N_KEYS) + i1l
            for lc in range(t_tile // PEER_SEL_TOK):
                ls = slice(lc * PEER_SEL_TOK, (lc + 1) * PEER_SEL_TOK)
                r1 = [jnp.broadcast_to(r1_ref[lc, h, i1:i1 + 1, :], (BF16_ROWS, PEER_SEL_TOK)).astype(BF16)
                      for h in range(PEER_HEADS)]
                c1 = [jnp.broadcast_to(c1_ref[lc, h, i1:i1 + 1, :], (BF16_ROWS, PEER_SEL_TOK)).astype(BF16)
                      for h in range(PEER_HEADS)]
                for j in range(PEER_N_KEYS // BF16_ROWS):
                    g = jnp.zeros((BF16_ROWS, PEER_SEL_TOK), BF16)
                    for h in range(PEER_HEADS):
                        n2 = pltpu.bitcast(nf_ref[lc, h, j, 0], BF16)
                        f2 = pltpu.bitcast(nf_ref[lc, h, j, 1], BF16)
                        g = g + jnp.where(n2 > r1[h], f2, 0.0) * c1[h]
                    rs = slice(i1l * PEER_N_KEYS + j * BF16_ROWS, i1l * PEER_N_KEYS + (j + 1) * BF16_ROWS)
                    w_scr[q, rs, ls] = g * _gelu(act_scr[q, rs, ls]).astype(BF16)

    def stage_c(q):
        qs = slice(q * PEER_CHUNK, (q + 1) * PEER_CHUNK)
        acc_scr[...] += jnp.dot(vt_ref[0, :, qs], w_scr[q], preferred_element_type=F32)

    def stage_c_prev():
        acc_scr[...] += jnp.dot(vp_ref[0], w_scr[nq - 1], preferred_element_type=F32)

    stage_c_prev()
    for q in range(nq):
        if q + 1 < nq:
            stage_a(q + 1)
        else:
            stage_a_next()
        stage_b(q)
        if q > 0:
            stage_c(q - 1)

    @pl.when(e == pl.num_programs(1) - 1)
    def _():
        acc = acc_scr[...] + jnp.dot(vt_ref[0, :, last], w_scr[nq - 1], preferred_element_type=F32)
        xo_ref[...] = x_ref[...] + gate_ref[0] * acc.T


def _peer_expert(xt, u_bf, vt_bf, layer, r1, c1, nf, x2, gate, *, t_tile):
    n = x2.shape[0]
    nt = n // t_tile
    nlc = t_tile // PEER_SEL_TOK
    ne = u_bf.shape[1] // PEER_EXP_TILE
    nq = PEER_EXP_TILE // PEER_CHUNK
    per_mod = max(nt // gate.shape[0], 1)
    mrows = gate.shape[1]
    if mrows == 1:
        mblk, mmap = (1, 1, D_MODEL), (lambda i, e: (i // per_mod, 0, 0))
    else:
        mblk, mmap = (1, t_tile, D_MODEL), (lambda i, e: (0, i, 0))
    i1blk = pl.BlockSpec((nlc, PEER_HEADS, PEER_I1_TILE, PEER_SEL_TOK), lambda i, e: (i, 0, e, 0))
    nfblk = pl.BlockSpec((nlc, PEER_HEADS, PEER_N_KEYS // BF16_ROWS, 2, SUBLANES, PEER_SEL_TOK),
                         lambda i, e: (i, 0, 0, 0, 0, 0))
    kern = functools.partial(_peer_expert_kernel, t_tile=t_tile)
    return pl.pallas_call(
        kern,
        grid=(nt, ne),
        in_specs=[
            pl.BlockSpec((D_MODEL, t_tile), lambda i, e: (0, i)),
            pl.BlockSpec((1, PEER_EXP_TILE, D_MODEL), lambda i, e: (layer, e, 0)),
            pl.BlockSpec((1, PEER_CHUNK, D_MODEL), lambda i, e: (layer, jnp.minimum(e + 1, ne - 1) * nq, 0)),
            pl.BlockSpec((1, D_MODEL, PEER_EXP_TILE), lambda i, e: (layer, 0, e)),
            pl.BlockSpec((1, D_MODEL, PEER_CHUNK), lambda i, e: (layer, 0, jnp.maximum(e, 1) * nq - 1)),
            i1blk, i1blk, nfblk,
            pl.BlockSpec((t_tile, D_MODEL), lambda i, e: (i, 0)),
            pl.BlockSpec(mblk, mmap),
        ],
        out_specs=pl.BlockSpec((t_tile, D_MODEL), lambda i, e: (i, 0)),
        out_shape=jax.ShapeDtypeStruct(x2.shape, F32),
        scratch_shapes=[pltpu.VMEM((PEER_EXP_TILE // PEER_CHUNK, PEER_CHUNK, t_tile), F32),
                        pltpu.VMEM((PEER_EXP_TILE // PEER_CHUNK, PEER_CHUNK, t_tile), BF16),
                        pltpu.VMEM((D_MODEL, t_tile), F32)],
        compiler_params=_cparams(("parallel", "arbitrary")),
        name="peer_expert",
    )(xt, u_bf, u_bf, vt_bf, vt_bf, r1, c1, nf, x2, gate)


def _peer(x2, shift, scale, gate, gain, pw, tables, layer, *, t_tile):
    wq_bf, sk_bf, ones_t = pw
    u_bf, vt_bf = tables
    xt, r1, c1, nf = _peer_route(x2, shift, scale, gain, wq_bf, sk_bf, ones_t)
    return _peer_expert(xt, u_bf, vt_bf, layer, r1, c1, nf, x2, gate, t_tile=t_tile)


def _peer_weights(w_q, sub_keys):
    hc = 2 * PEER_HEADS
    seg = np.arange(D_MODEL) // PEER_HALF
    ones_t = jnp.asarray((seg[None, :] == np.arange(hc)[:, None]).astype(np.float32), dtype=BF16)
    return (w_q.astype(BF16), sub_keys.reshape(hc, PEER_N_KEYS, PEER_HALF).astype(BF16), ones_t)


def kernel(x_prompt, x_sample, c_prompt, c_sample, state_ssm, cache_kv_w128, cache_kv_w512, cache_kv_w2048,
           norm_mix, norm_ffn, ada_w, ada_b, ssm_lam_re, ssm_lam_im, ssm_log_dt, ssm_b_re, ssm_b_im, ssm_c_re,
           ssm_c_im, ssm_d, ssm_w_glu, attn_w_qkv, attn_q_norm, attn_k_norm, attn_w_o, peer_w_q,
           peer_sub_keys, peer_u, peer_v):
    bp, sp, _ = x_prompt.shape
    bs, ts, _ = x_sample.shape
    ns = bs * ts
    ones_att = _ones_blockdiag(ATT)

    mods = _adaln(jnp.concatenate([c_prompt, c_sample], axis=0), ada_w, ada_b)
    caches = (cache_kv_w128, cache_kv_w512, cache_kv_w2048)
    tab_p = _rope_tables(jnp.arange(sp, dtype=jnp.int32))
    pos_s = PAST_LEN + jnp.arange(ts, dtype=jnp.int32)
    tab_s = tuple(jnp.tile(t, (bs, 1)) for t in _rope_tables(pos_s))

    peer_tabs = (peer_u.astype(BF16), peer_v.astype(BF16).transpose(0, 2, 1))
    xp, xs = x_prompt, x_sample
    ssm_p, ssm_s = [], []
    kv_p = [[] for _ in range(N_GROUPS)]
    kv_s = [[] for _ in range(N_GROUPS)]
    for i in range(DEPTH):
        j = i // 2
        mp = [m.reshape(bp, 1, D_MODEL) for m in jnp.split(mods[i, :bp], 6, axis=-1)]
        ms_b = jnp.split(mods[i, bp:], 6, axis=-1)
        ms_rows = [jnp.repeat(m, ts, axis=0).reshape(1, ns, D_MODEL) for m in ms_b]
        if i % 2 == 0:
            sw = _s5_weights(ssm_lam_re[j], ssm_lam_im[j], ssm_log_dt[j], ssm_b_re[j], ssm_b_im[j],
                             ssm_c_re[j], ssm_c_im[j])
            wglu = ssm_w_glu[j].astype(BF16)
            mp_t = [m.reshape(1, bp, D_MODEL) for m in mp[:3]]
            xp_t, st_p = _s5_layer(xp.transpose(1, 0, 2), mp_t[0], mp_t[1], mp_t[2], norm_mix[i],
                                   jnp.zeros((bp, SSM_STATE_W), F32), sw, ssm_d[j], wglu, t_chunk=32)
            xp = xp_t.transpose(1, 0, 2)
            ms_t = [m.reshape(1, bs, D_MODEL) for m in ms_b[:3]]
            xs_t, st_s = _s5_layer(xs.transpose(1, 0, 2), ms_t[0], ms_t[1], ms_t[2], norm_mix[i],
                                   _state_to_rows(state_ssm[j]), sw, ssm_d[j], wglu, t_chunk=ts)
            xs = xs_t.transpose(1, 0, 2)
            ssm_p.append(_rows_to_state(st_p))
            ssm_s.append(_rows_to_state(st_s))
        else:
            wqkv = attn_w_qkv[j].astype(BF16)
            wo = attn_w_o[j].astype(BF16)
            k, v, q0, q1, k1, v1, q2, k2, v2 = _qkv_dilated(xp, mp[0], mp[1], norm_mix[i], wqkv, attn_q_norm[j],
                                                            attn_k_norm[j], tab_p, ones_att, t_tile=512)
            res = [_band_attention(q0, k, v, 1, N_GROUPS, 0, n_sub=2, n_col=1),
                   _band_attention(q1, k1, v1, DILATIONS[1], 1, 0, n_sub=2, n_col=1),
                   _band_attention(q2, k2, v2, DILATIONS[2], 1, 0, n_sub=1, n_col=2)]
            outs, lses = zip(*res)
            xp = _merge(outs, lses, xp, mp[2], wo, t_tile=512, dils=DILATIONS)
            for g in range(N_GROUPS):
                keep = min(WINDOWS[g], sp)
                gs = slice(g * ATT, (g + 1) * ATT)
                kv_p[g].append(jnp.stack([k[:, sp - keep:, gs], v[:, sp - keep:, gs]], axis=1)
                               .reshape(bp, 2, keep, N_HEADS, HEAD_DIM))
            xs_rows = xs.reshape(1, ns, D_MODEL)
            qs, ks, vs = _qkv(xs_rows, ms_rows[0], ms_rows[1], norm_mix[i], wqkv, attn_q_norm[j],
                              attn_k_norm[j], tab_s, ones_att, t_tile=ns)
            qs, ks, vs = (t.reshape(bs, ts, N_GROUPS * ATT) for t in (qs, ks, vs))
            res = _cache_attention(qs, ks, vs, [c.reshape(c.shape[:4] + (ATT,)) for c in caches], j)
            outs_s = [r.reshape(1, ns, ATT) for r in res[:N_GROUPS]]
            lses_s = [r.reshape(1, ns, ATT) for r in res[N_GROUPS:]]
            xs = _merge(outs_s, lses_s, xs_rows, ms_rows[2], wo, t_tile=ns,
                        dils=(1, 1, 1)).reshape(bs, ts, D_MODEL)
            for g in range(N_GROUPS):
                gs = slice(g * ATT, (g + 1) * ATT)
                kv_s[g].append(jnp.stack([ks[:, :, gs], vs[:, :, gs]], axis=1)
                               .reshape(bs, 2, ts, N_HEADS, HEAD_DIM))
        pw = _peer_weights(peer_w_q[i], peer_sub_keys[i])
        xp = _peer(xp.reshape(bp * sp, D_MODEL), mp[3], mp[4], mp[5], norm_ffn[i], pw, peer_tabs, i,
                   t_tile=512).reshape(bp, sp, D_MODEL)
        xs = _peer(xs.reshape(ns, D_MODEL), ms_rows[3], ms_rows[4], ms_rows[5], norm_ffn[i], pw, peer_tabs, i,
                   t_tile=ns).reshape(bs, ts, D_MODEL)
    return (xp, xs,
            jnp.stack(ssm_p), jnp.stack(kv_p[0]), jnp.stack(kv_p[1]), jnp.stack(kv_p[2]),
            jnp.stack(ssm_s), jnp.stack(kv_s[0]), jnp.stack(kv_s[1]), jnp.stack(kv_s[2]))
```

```python
import functools
import math

import jax
import jax.numpy as jnp
import numpy as np
from jax import lax
from jax.experimental import pallas as pl
from jax.experimental.pallas import tpu as pltpu

F32 = jnp.float32
BF16 = jnp.bfloat16

SUBLANES = 8
LANES = 128
BF16_ROWS = 16

D_MODEL = 1024
DEPTH = 4
PAST_LEN = 8192
EPS = 1e-6
NEG_INF = -1e30

SSM_P = 16
SSM_GROUPS = D_MODEL // SSM_P
SSM_N = 64
SSM_GB = 4
SSM_GB_CH = D_MODEL // SSM_GB
SSM_GB_ST = SSM_GROUPS * SSM_N // SSM_GB
SSM_STATE_W = 2 * SSM_GROUPS * SSM_N

N_HEADS = 8
HEAD_DIM = 64
ATT = N_HEADS * HEAD_DIM
WINDOWS = (128, 512, 2048)
DILATIONS = (1, 4, 16)
N_GROUPS = 3
N_BACK = 128
ROPE_DIM = HEAD_DIM // 4
ROPE_THETA = 500000.0

PEER_HEADS = 8
PEER_N_KEYS = 128
PEER_HALF = 64
PEER_TOPK = 16
PEER_SEL_TOK = LANES
PEER_EXP_TILE = 1024
PEER_I1_TILE = PEER_EXP_TILE // PEER_N_KEYS
PEER_MARK = 2.0 ** 100
PEER_CHUNK = 256

ADA_COL_TILE = 1024

VMEM_LIMIT = 56 * 1024 * 1024


def _cparams(sem):
    return pltpu.CompilerParams(dimension_semantics=sem, vmem_limit_bytes=VMEM_LIMIT)


def _modulate(x, gain, shift, scale):
    ms = jnp.mean(x * x, axis=-1, keepdims=True)
    return x * lax.rsqrt(ms + EPS) * gain * (1.0 + scale) + shift


def _seg_mean(x2, ones_bd):
    hi = x2.astype(BF16)
    lo = (x2 - hi.astype(F32)).astype(BF16)
    s = jnp.dot(hi, ones_bd, preferred_element_type=F32) + jnp.dot(lo, ones_bd, preferred_element_type=F32)
    return s * (1.0 / HEAD_DIM)


def _ones_blockdiag(n):
    seg = np.arange(n) // HEAD_DIM
    return jnp.asarray((seg[:, None] == seg[None, :]).astype(np.float32), dtype=BF16)


def _adaln_kernel(c_ref, w_ref, b_ref, o_ref):
    s = jax.nn.silu(c_ref[...])
    o_ref[0] = jnp.dot(s.astype(BF16), w_ref[0].astype(BF16), preferred_element_type=F32) + b_ref[0]


def _adaln(c_all, ada_w, ada_b):
    nb = c_all.shape[0]
    nt = 6 * D_MODEL // ADA_COL_TILE
    return pl.pallas_call(
        _adaln_kernel,
        grid=(DEPTH, nt),
        in_specs=[
            pl.BlockSpec((nb, D_MODEL), lambda l, j: (0, 0)),
            pl.BlockSpec((1, D_MODEL, ADA_COL_TILE), lambda l, j: (l, 0, j)),
            pl.BlockSpec((1, 1, ADA_COL_TILE), lambda l, j: (l, 0, j)),
        ],
        out_specs=pl.BlockSpec((1, nb, ADA_COL_TILE), lambda l, j: (l, 0, j)),
        out_shape=jax.ShapeDtypeStruct((DEPTH, nb, 6 * D_MODEL), F32),
        compiler_params=_cparams(("parallel", "parallel")),
        name="adaln",
    )(c_all, ada_w, ada_b.reshape(DEPTH, 1, 6 * D_MODEL))


def _s5_kernel(x_ref, shift_ref, scale_ref, gate_ref, gain_ref, h0_ref, bm_ref, cm_ref, are_ref, aim_ref,
               d_ref, wglu_ref, xo_ref, st_ref, bu_scr, state_scr, *, n_batch, n_time):
    i = pl.program_id(0)

    @pl.when(i == 0)
    def _():
        state_scr[...] = h0_ref[...]

    x3 = x_ref[...]
    a0, a1, _ = x3.shape
    rows = a0 * a1
    h3 = _modulate(x3, gain_ref[...], shift_ref[...], scale_ref[...])
    u = h3.reshape(rows, D_MODEL)
    ub = u.astype(BF16)
    for gb in range(SSM_GB):
        bu_scr[:, gb * 2 * SSM_GB_ST:(gb + 1) * 2 * SSM_GB_ST] = jnp.dot(
            ub[:, gb * SSM_GB_CH:(gb + 1) * SSM_GB_CH], bm_ref[gb], preferred_element_type=F32)

    for bb in range(n_batch // SUBLANES):
        for gb in range(SSM_GB):
            c_re = gb * 2 * SSM_GB_ST
            c_im = c_re + SSM_GB_ST
            ar = jnp.broadcast_to(are_ref[gb], (SUBLANES, SSM_GB_ST))
            ai = jnp.broadcast_to(aim_ref[gb], (SUBLANES, SSM_GB_ST))
            xr = state_scr[bb * 8:(bb + 1) * 8, c_re:c_re + SSM_GB_ST]
            xi = state_scr[bb * 8:(bb + 1) * 8, c_im:c_im + SSM_GB_ST]
            for t in range(n_time):
                rsel = pl.ds(t * n_batch + bb * 8, 8)
                br = bu_scr[rsel, c_re:c_re + SSM_GB_ST]
                bi = bu_scr[rsel, c_im:c_im + SSM_GB_ST]
                nr = ar * xr - ai * xi + br
                ni = ar * xi + ai * xr + bi
                bu_scr[rsel, c_re:c_re + SSM_GB_ST] = nr
                bu_scr[rsel, c_im:c_im + SSM_GB_ST] = ni
                xr, xi = nr, ni
            state_scr[bb * 8:(bb + 1) * 8, c_re:c_re + SSM_GB_ST] = xr
            state_scr[bb * 8:(bb + 1) * 8, c_im:c_im + SSM_GB_ST] = xi

    ys = []
    for gb in range(SSM_GB):
        st = bu_scr[:, gb * 2 * SSM_GB_ST:(gb + 1) * 2 * SSM_GB_ST].astype(BF16)
        ys.append(jnp.dot(st, cm_ref[gb], preferred_element_type=F32))
    y = jnp.concatenate(ys, axis=-1) + d_ref[...] * u
    z = jnp.dot(y.astype(BF16), wglu_ref[...], preferred_element_type=F32)
    out = z[:, :D_MODEL] * jax.nn.sigmoid(z[:, D_MODEL:])
    xo_ref[...] = x3 + gate_ref[...] * out.reshape(a0, a1, D_MODEL)

    @pl.when(i == pl.num_programs(0) - 1)
    def _():
        st_ref[...] = state_scr[...]


def _s5_weights(lam_re, lam_im, log_dt, b_re, b_im, c_re, c_im):
    dt = jnp.exp(log_dt)[:, None]
    mag = jnp.exp(lam_re * dt)
    a_re = mag * jnp.cos(lam_im * dt)
    a_im = mag * jnp.sin(lam_im * dt)
    den = lam_re * lam_re + lam_im * lam_im
    f_re = ((a_re - 1.0) * lam_re + a_im * lam_im) / den
    f_im = (a_im * lam_re - (a_re - 1.0) * lam_im) / den
    bb_re = f_re[..., None] * b_re - f_im[..., None] * b_im
    bb_im = f_re[..., None] * b_im + f_im[..., None] * b_re
    eye = jnp.eye(16, dtype=F32)

    def in_blk(b):
        b4 = b.reshape(SSM_GB, 16, SSM_N, SSM_P)
        return jnp.einsum('bgnp,gh->bgphn', b4, eye).reshape(SSM_GB, SSM_GB_CH, SSM_GB_ST)

    def out_blk(c):
        c4 = c.reshape(SSM_GB, 16, SSM_P, SSM_N)
        return jnp.einsum('bgpn,gh->bgnhp', c4, eye).reshape(SSM_GB, SSM_GB_ST, SSM_GB_CH)

    bm = jnp.concatenate([in_blk(bb_re), in_blk(bb_im)], axis=2).astype(BF16)
    cm = jnp.concatenate([out_blk(c_re), -out_blk(c_im)], axis=1).astype(BF16)
    return bm, cm, a_re.reshape(SSM_GB, 1, SSM_GB_ST), a_im.reshape(SSM_GB, 1, SSM_GB_ST)


def _state_to_rows(st):
    b = st.shape[0]
    return st.reshape(b, 2, SSM_GB, SSM_GB_ST).transpose(0, 2, 1, 3).reshape(b, SSM_STATE_W)


def _rows_to_state(rows):
    b = rows.shape[0]
    return rows.reshape(b, SSM_GB, 2, SSM_GB_ST).transpose(0, 2, 1, 3).reshape(b, 2, SSM_GROUPS, SSM_N)


def _s5_layer(x3, shift, scale, gate, gain, h0_rows, weights, d_skip, w_glu, *, t_chunk):
    bm, cm, a_re, a_im = weights
    seq, n_batch = x3.shape[0], x3.shape[1]
    n_time = t_chunk
    blk = (n_time, n_batch, D_MODEL)
    xmap = lambda i: (i, 0, 0)
    mblk = (1, n_batch, D_MODEL)
    grid = seq // n_time
    rows = n_batch * n_time
    const3 = lambda i: (0, 0, 0)
    const2 = lambda i: (0, 0)
    kern = functools.partial(_s5_kernel, n_batch=n_batch, n_time=n_time)
    return pl.pallas_call(
        kern,
        grid=(grid,),
        in_specs=[
            pl.BlockSpec(blk, xmap),
            pl.BlockSpec(mblk, const3), pl.BlockSpec(mblk, const3), pl.BlockSpec(mblk, const3),
            pl.BlockSpec((1, 1, D_MODEL), const3),
            pl.BlockSpec((n_batch, SSM_STATE_W), const2),
            pl.BlockSpec(bm.shape, const3), pl.BlockSpec(cm.shape, const3),
            pl.BlockSpec(a_re.shape, const3), pl.BlockSpec(a_im.shape, const3),
            pl.BlockSpec((1, D_MODEL), const2),
            pl.BlockSpec((D_MODEL, 2 * D_MODEL), const2),
        ],
        out_specs=[pl.BlockSpec(blk, xmap), pl.BlockSpec((n_batch, SSM_STATE_W), const2)],
        out_shape=[jax.ShapeDtypeStruct(x3.shape, F32), jax.ShapeDtypeStruct((n_batch, SSM_STATE_W), F32)],
        scratch_shapes=[pltpu.VMEM((rows, SSM_STATE_W), F32), pltpu.VMEM((n_batch, SSM_STATE_W), F32)],
        compiler_params=_cparams(("arbitrary",)),
        name="s5_layer",
    )(x3, shift, scale, gate, gain.reshape(1, 1, D_MODEL), h0_rows, bm, cm, a_re, a_im,
      d_skip.reshape(1, D_MODEL), w_glu)


def _qkv_kernel(x_ref, shift_ref, scale_ref, gain_ref, w_ref, qg_ref, kg_ref, rc_ref, rs1_ref, rs2_ref,
                ones_ref, q_ref, k_ref, v_ref):
    h = _modulate(x_ref[0], gain_ref[...], shift_ref[0], scale_ref[0])
    qkv = jnp.dot(h.astype(BF16), w_ref[...], preferred_element_type=F32)
    rc, rs1, rs2 = rc_ref[...], rs1_ref[...], rs2_ref[...]
    ones_bd = ones_ref[...]

    def norm_rope(t, g):
        tn = t * lax.rsqrt(_seg_mean(t * t, ones_bd) + EPS) * g
        return tn * rc + pltpu.roll(tn, ATT - ROPE_DIM // 2, 1) * rs1 + pltpu.roll(tn, ROPE_DIM // 2, 1) * rs2

    q_ref[0] = norm_rope(qkv[:, :ATT], qg_ref[...])
    k_ref[0] = norm_rope(qkv[:, ATT:2 * ATT], kg_ref[...])
    v_ref[0] = qkv[:, 2 * ATT:]


def _dilate(x, scr, out_ref, d):
    t = x.shape[0]
    for kc in range(ATT // LANES):
        scr[kc] = x[:, kc * LANES:(kc + 1) * LANES]
    for r in range(d):
        for kc in range(ATT // LANES):
            c0 = r * ATT + kc * LANES
            out_ref[0, :, c0:c0 + LANES] = scr[kc, pl.ds(r, t // d, stride=d), :]


def _qkv_dil_kernel(x_ref, shift_ref, scale_ref, gain_ref, w_ref, qg_ref, kg_ref, rc_ref, rs1_ref, rs2_ref,
                    ones_ref, kn_ref, vn_ref, q0_ref, q1_ref, k1_ref, v1_ref, q2_ref, k2_ref, v2_ref, scr):
    g = pl.program_id(2)
    h = _modulate(x_ref[0], gain_ref[...], shift_ref[0], scale_ref[0])
    qkv = jnp.dot(h.astype(BF16), w_ref[...], preferred_element_type=F32)
    rc, rs1, rs2 = rc_ref[...], rs1_ref[...], rs2_ref[...]
    ones_bd = ones_ref[...]

    def norm_rope(t, gn):
        tn = t * lax.rsqrt(_seg_mean(t * t, ones_bd) + EPS) * gn
        return tn * rc + pltpu.roll(tn, ATT - ROPE_DIM // 2, 1) * rs1 + pltpu.roll(tn, ROPE_DIM // 2, 1) * rs2

    q = norm_rope(qkv[:, :ATT], qg_ref[...])
    k = norm_rope(qkv[:, ATT:2 * ATT], kg_ref[...])
    v = qkv[:, 2 * ATT:]
    kn_ref[0] = k
    vn_ref[0] = v

    @pl.when(g == 0)
    def _():
        q0_ref[0] = q

    for gi, refs in ((1, (q1_ref, k1_ref, v1_ref)), (2, (q2_ref, k2_ref, v2_ref))):
        @pl.when(g == gi)
        def _():
            for val, ref in zip((q, k, v), refs):
                _dilate(val, scr, ref, DILATIONS[gi])


def _qkv_dilated(x3, shift, scale, gain, w_qkv_bf, q_gain, k_gain, tables, ones_bd, *, t_tile):
    bq, sq, _ = x3.shape
    rc, rs1, rs2 = tables
    qg = jnp.tile(q_gain, N_HEADS).reshape(1, ATT)
    kg = jnp.tile(k_gain, N_HEADS).reshape(1, ATT)
    mmap = lambda b, i, g: (b, 0, 0)
    tmap = lambda b, i, g: (i, 0)
    cmap = lambda b, i, g: (0, 0)
    gmap = lambda b, i, g: (b, i, g)
    bmap = lambda b, i, g: (b, i, 0)
    nat = jax.ShapeDtypeStruct((bq, sq, N_GROUPS * ATT), F32)
    out_shape = [nat, nat, jax.ShapeDtypeStruct((bq, sq, ATT), F32)]
    out_specs = [pl.BlockSpec((1, t_tile, ATT), gmap)] * 2 + [pl.BlockSpec((1, t_tile, ATT), bmap)]
    for gi in (1, 2):
        d = DILATIONS[gi]
        out_shape += [jax.ShapeDtypeStruct((bq, sq // d, d * ATT), F32)] * 3
        out_specs += [pl.BlockSpec((1, t_tile // d, d * ATT), bmap)] * 3
    return pl.pallas_call(
        _qkv_dil_kernel,
        grid=(bq, sq // t_tile, N_GROUPS),
        in_specs=[
            pl.BlockSpec((1, t_tile, D_MODEL), bmap),
            pl.BlockSpec((1, 1, D_MODEL), mmap), pl.BlockSpec((1, 1, D_MODEL), mmap),
            pl.BlockSpec((1, D_MODEL), cmap),
            pl.BlockSpec((D_MODEL, 3 * ATT), lambda b, i, g: (0, g)),
            pl.BlockSpec((1, ATT), cmap), pl.BlockSpec((1, ATT), cmap),
            pl.BlockSpec((t_tile, ATT), tmap), pl.BlockSpec((t_tile, ATT), tmap), pl.BlockSpec((t_tile, ATT), tmap),
            pl.BlockSpec((ATT, ATT), cmap),
        ],
        out_specs=out_specs,
        out_shape=out_shape,
        scratch_shapes=[pltpu.VMEM((ATT // LANES, t_tile, LANES), F32)],
        compiler_params=_cparams(("parallel", "parallel", "arbitrary")),
        name="qkv_proj_dilated",
    )(x3, shift, scale, gain.reshape(1, D_MODEL), w_qkv_bf, qg, kg, rc, rs1, rs2, ones_bd)


def _rope_tables(pos):
    half = ROPE_DIM // 2
    inv = ROPE_THETA ** (-jnp.arange(half, dtype=F32) / half)
    ang = pos.astype(F32)[:, None] * inv[None, :]
    cos, sin = jnp.cos(ang), jnp.sin(ang)
    lane = np.arange(ATT) % HEAD_DIM
    fidx = lane % half
    first = jnp.asarray(lane < half)
    second = jnp.asarray((lane >= half) & (lane < ROPE_DIM))
    cl, sl = cos[:, fidx], sin[:, fidx]
    rc = jnp.where(first | second, cl, 1.0)
    rs1 = jnp.where(first, -sl, 0.0)
    rs2 = jnp.where(second, sl, 0.0)
    return rc, rs1, rs2


def _qkv(x3, shift, scale, gain, w_qkv_bf, q_gain, k_gain, tables, ones_bd, *, t_tile):
    bq, sq, _ = x3.shape
    mrows = shift.shape[1]
    mt = t_tile if mrows == sq else 1
    mmap = (lambda b, i, g: (b, i, 0)) if mrows == sq else (lambda b, i, g: (b, 0, 0))
    rc, rs1, rs2 = tables
    qg = jnp.tile(q_gain, N_HEADS).reshape(1, ATT)
    kg = jnp.tile(k_gain, N_HEADS).reshape(1, ATT)
    tmap = lambda b, i, g: (i, 0)
    cmap = lambda b, i, g: (0, 0)
    omap = lambda b, i, g: (b, i, g)
    oshape = jax.ShapeDtypeStruct((bq, sq, N_GROUPS * ATT), F32)
    return pl.pallas_call(
        _qkv_kernel,
        grid=(bq, sq // t_tile, N_GROUPS),
        in_specs=[
            pl.BlockSpec((1, t_tile, D_MODEL), lambda b, i, g: (b, i, 0)),
            pl.BlockSpec((1, mt, D_MODEL), mmap), pl.BlockSpec((1, mt, D_MODEL), mmap),
            pl.BlockSpec((1, D_MODEL), cmap),
            pl.BlockSpec((D_MODEL, 3 * ATT), lambda b, i, g: (0, g)),
            pl.BlockSpec((1, ATT), cmap), pl.BlockSpec((1, ATT), cmap),
            pl.BlockSpec((t_tile, ATT), tmap), pl.BlockSpec((t_tile, ATT), tmap), pl.BlockSpec((t_tile, ATT), tmap),
            pl.BlockSpec((ATT, ATT), cmap),
        ],
        out_specs=[pl.BlockSpec((1, t_tile, ATT), omap)] * 3,
        out_shape=[oshape] * 3,
        compiler_params=_cparams(("parallel", "parallel", "arbitrary")),
        name="qkv_proj",
    )(x3, shift, scale, gain.reshape(1, D_MODEL), w_qkv_bf, qg, kg, rc, rs1, rs2, ones_bd)


def _band_kernel(q_ref, kp_ref, kc_ref, vp_ref, vc_ref, o_ref, l_ref, *, n_sub, n_col):
    i = pl.program_id(2)
    iq = lax.broadcasted_iota(jnp.int32, (N_BACK, 2 * N_BACK), 0)
    ik = lax.broadcasted_iota(jnp.int32, (N_BACK, 2 * N_BACK), 1)
    dist = iq + N_BACK - ik
    band = (dist >= 0) & (dist <= N_BACK)
    band_first = band & ((ik >= N_BACK) | (i > 0))
    for c in range(n_col):
        c0 = c * ATT
        k_all = jnp.concatenate([kp_ref[0, :, c0:c0 + ATT], kc_ref[0, :, c0:c0 + ATT]], axis=0).astype(BF16)
        v_all = jnp.concatenate([vp_ref[0, :, c0:c0 + ATT], vc_ref[0, :, c0:c0 + ATT]], axis=0).astype(BF16)
        for u in range(n_sub):
            rows = slice(u * N_BACK, (u + 1) * N_BACK)
            q = q_ref[0, rows, c0:c0 + ATT].astype(BF16)
            k = k_all[u * N_BACK:(u + 2) * N_BACK]
            v = v_all[u * N_BACK:(u + 2) * N_BACK]
            mask = band_first if u == 0 else band
            for h in range(N_HEADS):
                sl = slice(h * HEAD_DIM, (h + 1) * HEAD_DIM)
                ol = slice(c0 + h * HEAD_DIM, c0 + (h + 1) * HEAD_DIM)
                s = lax.dot_general(q[:, sl], k[:, sl], (((1,), (1,)), ((), ())),
                                    preferred_element_type=F32) * (HEAD_DIM ** -0.5)
                s = jnp.where(mask, s, NEG_INF)
                m = jnp.max(s, axis=-1, keepdims=True)
                p = jnp.exp(s - m)
                den = jnp.sum(p, axis=-1, keepdims=True)
                o = jnp.dot(p.astype(BF16), v[:, sl], preferred_element_type=F32) / den
                o_ref[0, rows, ol] = o
                l_ref[0, rows, ol] = jnp.broadcast_to(m + jnp.log(den), (N_BACK, HEAD_DIM))


def _band_attention(qv, kv, vv, d, kcols, kcol0, *, n_sub, n_col):
    b, sub, _ = qv.shape
    nb = sub // N_BACK
    assert nb % n_sub == 0 and d % n_col == 0 and (n_col == 1 or kcols == 1)
    qcur = lambda bi, r, i: (bi, i, r)
    cur = lambda bi, r, i: (bi, i, r * kcols + kcol0)
    prev = lambda bi, r, i: (bi, jnp.maximum(i * n_sub - 1, 0), r * kcols + kcol0)
    blk = (1, n_sub * N_BACK, n_col * ATT)
    pblk = (1, N_BACK, n_col * ATT)
    oshape = jax.ShapeDtypeStruct((b, sub, d * ATT), F32)
    return pl.pallas_call(
        functools.partial(_band_kernel, n_sub=n_sub, n_col=n_col),
        grid=(b, d // n_col, nb // n_sub),
        in_specs=[pl.BlockSpec(blk, qcur), pl.BlockSpec(pblk, prev), pl.BlockSpec(blk, cur),
                  pl.BlockSpec(pblk, prev), pl.BlockSpec(blk, cur)],
        out_specs=[pl.BlockSpec(blk, qcur), pl.BlockSpec(blk, qcur)],
        out_shape=[oshape, oshape],
        compiler_params=_cparams(("parallel", "parallel", "arbitrary")),
        name="band_attention",
    )(qv, kv, kv, vv, vv)


def _cache_attn_kernel(q_ref, k_ref, v_ref, c0_ref, c1_ref, c2_ref, *out_refs):
    n_new = q_ref.shape[1]
    rows = n_new * N_HEADS
    lane_head = lax.broadcasted_iota(jnp.int32, (N_HEADS, ATT), 1) // HEAD_DIM
    head_mask = (lane_head == lax.broadcasted_iota(jnp.int32, (N_HEADS, ATT), 0)).astype(F32)
    row_t = lax.broadcasted_iota(jnp.int32, (rows, 1), 0) // N_HEADS
    caches = (c0_ref, c1_ref, c2_ref)
    for g in range(N_GROUPS):
        d, win = DILATIONS[g], WINDOWS[g]
        gs = slice(g * ATT, (g + 1) * ATT)
        q = q_ref[0][:, gs]
        kn = k_ref[0][:, gs]
        vn = v_ref[0][:, gs]
        kc = caches[g][0, 0, 0].astype(BF16)
        vc = caches[g][0, 0, 1].astype(BF16)
        qbd = jnp.concatenate([q[t:t + 1, :] * head_mask for t in range(n_new)], axis=0)
        s_c = lax.dot_general(qbd.astype(BF16), kc, (((1,), (1,)), ((), ())),
                              preferred_element_type=F32) * (HEAD_DIM ** -0.5)
        col = lax.broadcasted_iota(jnp.int32, (rows, win), 1)
        valid_c = (col >= row_t) & (((col - row_t) & (d - 1)) == 0)
        s_c = jnp.where(valid_c, s_c, NEG_INF)
        s_n = []
        for t2 in range(n_new):
            sn = jnp.sum(qbd * kn[t2:t2 + 1, :], axis=-1, keepdims=True) * (HEAD_DIM ** -0.5)
            valid_n = (row_t >= t2) & (((row_t - t2) & (d - 1)) == 0)
            s_n.append(jnp.where(valid_n, sn, NEG_INF))
        m = jnp.max(s_c, axis=-1, keepdims=True)
        for sn in s_n:
            m = jnp.maximum(m, sn)
        p_c = jnp.exp(s_c - m)
        den = jnp.sum(p_c, axis=-1, keepdims=True)
        o = jnp.dot(p_c.astype(BF16), vc, preferred_element_type=F32)
        for t2 in range(n_new):
            p_n = jnp.exp(s_n[t2] - m)
            den = den + p_n
            o = o + p_n * vn[t2:t2 + 1, :]
        o = o / den
        lse = m + jnp.log(den)
        for t in range(n_new):
            rs = slice(t * N_HEADS, (t + 1) * N_HEADS)
            out_refs[g][0, t:t + 1, :] = jnp.sum(o[rs] * head_mask, axis=0, keepdims=True)
            out_refs[N_GROUPS + g][0, t:t + 1, :] = jnp.sum(lse[rs] * head_mask, axis=0, keepdims=True)


def _cache_attention(q, k, v, caches, layer):
    bd, t, w = q.shape
    xmap = lambda b: (b, 0, 0)
    cmap = lambda b: (layer, b, 0, 0, 0)
    oshape = jax.ShapeDtypeStruct((bd, t, ATT), F32)
    return pl.pallas_call(
        _cache_attn_kernel,
        grid=(bd,),
        in_specs=[pl.BlockSpec((1, t, w), xmap)] * 3
        + [pl.BlockSpec((1, 1, 2, WINDOWS[g], ATT), cmap) for g in range(N_GROUPS)],
        out_specs=[pl.BlockSpec((1, t, ATT), xmap)] * (2 * N_GROUPS),
        out_shape=[oshape] * (2 * N_GROUPS),
        compiler_params=_cparams(("parallel",)),
        name="cache_attention",
    )(q, k, v, *caches)


def _undilate(ref, scr, d):
    if d == 1:
        return ref[0]
    rows = ref.shape[1]
    for r in range(d):
        for kc in range(ATT // LANES):
            c0 = r * ATT + kc * LANES
            scr[kc, pl.ds(r, rows, stride=d), :] = ref[0, :, c0:c0 + LANES]
    return jnp.concatenate([scr[kc] for kc in range(ATT // LANES)], axis=1)


def _merge_kernel(o0, o1, o2, l0, l1, l2, x_ref, gate_ref, wo_ref, xo_ref, scr, *, dils):
    la, lb, lc = [_undilate(r, scr, d) for r, d in zip((l0, l1, l2), dils)]
    m = jnp.maximum(jnp.maximum(la, lb), lc)
    ea, eb, ec = jnp.exp(la - m), jnp.exp(lb - m), jnp.exp(lc - m)
    den = ea + eb + ec
    o = (ea / den) * _undilate(o0, scr, dils[0])
    o = o + (eb / den) * _undilate(o1, scr, dils[1])
    o = o + (ec / den) * _undilate(o2, scr, dils[2])
    out = jnp.dot(o.astype(BF16), wo_ref[...], preferred_element_type=F32)
    xo_ref[0] = x_ref[0] + gate_ref[0] * out


def _merge(outs, lses, x3, gate, w_o_bf, *, t_tile, dils):
    bq, sq, _ = x3.shape
    mrows = gate.shape[1]
    mt = t_tile if mrows == sq else 1
    mmap = (lambda b, i: (b, i, 0)) if mrows == sq else (lambda b, i: (b, 0, 0))
    tmap = lambda b, i: (b, i, 0)
    ablk = [pl.BlockSpec((1, t_tile // d, d * ATT), tmap) for d in dils]
    return pl.pallas_call(
        functools.partial(_merge_kernel, dils=dils),
        grid=(bq, sq // t_tile),
        in_specs=ablk + ablk + [pl.BlockSpec((1, t_tile, D_MODEL), tmap), pl.BlockSpec((1, mt, D_MODEL), mmap),
                                pl.BlockSpec((ATT, D_MODEL), lambda b, i: (0, 0))],
        out_specs=pl.BlockSpec((1, t_tile, D_MODEL), tmap),
        out_shape=jax.ShapeDtypeStruct(x3.shape, F32),
        scratch_shapes=[pltpu.VMEM((ATT // LANES, t_tile, LANES), F32)],
        compiler_params=_cparams(("parallel", "parallel")),
        name="attn_merge",
    )(*outs, *lses, x3, gate, w_o_bf)


def _top16(s, exact):
    vals = s
    tops = []
    if not exact:
        for r in range(PEER_TOPK):
            m = jnp.max(vals, axis=0, keepdims=True)
            vals = jnp.where(vals == m, -(r + 1.0) * PEER_MARK, vals)
            tops.append(m)
        t = vals * (-1.0 / PEER_MARK)
        return tops, jnp.where(t >= 0.5, t - 1.0, float(PEER_N_KEYS))
    rank = jnp.full(s.shape, float(PEER_N_KEYS), F32)
    key = lax.broadcasted_iota(jnp.int32, s.shape, 0).astype(F32)
    for r in range(PEER_TOPK):
        m = jnp.max(vals, axis=0, keepdims=True)
        first = jnp.min(jnp.where(vals == m, key, float(PEER_N_KEYS)), axis=0, keepdims=True)
        hit = key == first
        rank = jnp.where(hit, float(r), rank)
        vals = jnp.where(hit, -jnp.inf, vals)
        tops.append(m)
    return tops, rank


def _count_taken(rank):
    return jnp.sum(jnp.where(rank < float(PEER_N_KEYS), 1.0, 0.0), axis=0, keepdims=True)


def _route_head(s1, s2, exact):
    n_tok = s1.shape[1]
    top1, rank1 = _top16(s1, exact)
    top2, rank2 = _top16(s2, exact)
    row = lax.broadcasted_iota(jnp.int32, (PEER_TOPK, n_tok), 0)
    v1 = jnp.zeros((PEER_TOPK, n_tok), F32)
    for r in range(PEER_TOPK):
        v1 = jnp.where(row == r, top1[r], v1)
    best = top1[0] + top2[0]
    zsum = jnp.zeros_like(best)
    if exact:
        cands = [v1 + top2[b] for b in range(PEER_TOPK)]
        flat = [(row * PEER_TOPK + b).astype(F32) for b in range(PEER_TOPK)]
    else:
        v2 = jnp.zeros((PEER_TOPK, n_tok), F32)
        for r in range(PEER_TOPK):
            v2 = jnp.where(row == r, top2[r], v2)
        cands = [v1[:8] + top2[b] for b in range(8)] + [v1[8:] + top2[0], top1[0] + v2[8:]]
    taken = [jnp.zeros_like(c) for c in cands]
    for _ in range(PEER_TOPK):
        m = cands[0]
        for c in cands[1:]:
            m = jnp.maximum(m, c)
        m = jnp.max(m, axis=0, keepdims=True)
        if exact:
            fmin = jnp.full_like(cands[0], float(PEER_TOPK * PEER_TOPK))
            for c, f in zip(cands, flat):
                fmin = jnp.minimum(fmin, jnp.where(c == m, f, float(PEER_TOPK * PEER_TOPK)))
            fmin = jnp.min(fmin, axis=0, keepdims=True)
            for j in range(len(cands)):
                hit = flat[j] == fmin
                taken[j] = jnp.where(hit, 1.0, taken[j])
                cands[j] = jnp.where(hit, -jnp.inf, cands[j])
        else:
            cands = [jnp.where(c == m, -PEER_MARK, c) for c in cands]
        zsum = zsum + jnp.exp(m - best)
    if not exact:
        taken = [jnp.where(c <= -0.5 * PEER_MARK, 1.0, 0.0) for c in cands]
    tie = jnp.zeros_like(best)
    if exact:
        cnt = taken[0]
        for j in range(1, PEER_TOPK):
            cnt = cnt + taken[j]
    else:
        cnt = taken[0]
        for j in range(1, 8):
            cnt = cnt + taken[j]
        cnt = jnp.concatenate([cnt, taken[8]], axis=0)
        cnt = cnt + jnp.where(row == 0, jnp.sum(taken[9], axis=0, keepdims=True), 0.0)
        n_pairs = jnp.sum(cnt, axis=0, keepdims=True)
        low = jnp.minimum(jnp.min(s1, axis=0, keepdims=True), jnp.min(s2, axis=0, keepdims=True))
        bad = ((_count_taken(rank1) != float(PEER_TOPK)) | (_count_taken(rank2) != float(PEER_TOPK))
               | (n_pairs != float(PEER_TOPK)) | (low <= -0.25 * PEER_MARK))
        tie = jnp.where(bad, 1.0, 0.0)
    n2 = jnp.where(rank2 < 1.0, jnp.sum(cnt[8:], axis=0, keepdims=True), 0.0)
    for a in range(8):
        n2 = n2 + jnp.where(rank2 < cnt[a:a + 1], 1.0, 0.0)
    return rank1, jnp.exp(s1 - top1[0]), n2, jnp.exp(s2 - top2[0]) / zsum, tie


def _peer_route_kernel(x_ref, shift_ref, scale_ref, gain_ref, wq_ref, sk_ref, ones_ref,
                       xt_ref, r1_ref, c1_ref, nf_ref, s_scr, tie_scr):
    hm = _modulate(x_ref[...], gain_ref[...], shift_ref[0], scale_ref[0])
    hb = hm.astype(BF16)
    xt_ref[...] = hm.T.astype(BF16)
    q = jnp.dot(hb, wq_ref[...], preferred_element_type=F32)
    nt_dims = (((1,), (1,)), ((), ()))
    q2 = q * q
    q2_hi = q2.astype(BF16)
    q2_lo = (q2 - q2_hi.astype(F32)).astype(BF16)
    ms_t = (lax.dot_general(ones_ref[...], q2_hi, nt_dims, preferred_element_type=F32)
            + lax.dot_general(ones_ref[...], q2_lo, nt_dims, preferred_element_type=F32)) * (1.0 / PEER_HALF)
    inv_t = lax.rsqrt(ms_t + EPS)
    qb = q.astype(BF16)
    for hc in range(2 * PEER_HEADS):
        raw = lax.dot_general(sk_ref[hc], qb[:, hc * PEER_HALF:(hc + 1) * PEER_HALF], nt_dims,
                              preferred_element_type=F32)
        s_scr[hc * PEER_N_KEYS:(hc + 1) * PEER_N_KEYS, :] = raw * inv_t[hc:hc + 1, :]

    def head(h, tie, exact):
        base = pl.multiple_of(h * 2 * PEER_N_KEYS, 2 * PEER_N_KEYS)
        s1 = s_scr[pl.ds(base, PEER_N_KEYS), :]
        s2 = s_scr[pl.ds(base + PEER_N_KEYS, PEER_N_KEYS), :]
        rank1, c1, n2, f2, tie_h = _route_head(s1, s2, exact)
        r1_ref[0, h] = rank1
        c1_ref[0, h] = c1
        nf_ref[0, h, :, 0] = pltpu.bitcast(n2.astype(BF16), jnp.int32).reshape(PEER_N_KEYS // BF16_ROWS, SUBLANES, PEER_SEL_TOK)
        nf_ref[0, h, :, 1] = pltpu.bitcast(f2.astype(BF16), jnp.int32).reshape(PEER_N_KEYS // BF16_ROWS, SUBLANES, PEER_SEL_TOK)
        if not exact:
            tie_scr[h] = tie_h
        return jnp.maximum(tie, tie_h)

    no_tie = jnp.zeros((1, s_scr.shape[1]), F32)
    tie = lax.fori_loop(0, PEER_HEADS, functools.partial(head, exact=False), no_tie, unroll=8)

    @pl.when(jnp.max(tie) > 0.0)
    def _():
        def redo(h, carry):
            @pl.when(jnp.max(tie_scr[h]) > 0.0)
            def _():
                head(h, no_tie, True)
            return carry

        lax.fori_loop(0, PEER_HEADS, redo, 0)


def _peer_route(x2, shift, scale, gain, wq_bf, sk_bf, ones_t):
    n = x2.shape[0]
    tt = PEER_SEL_TOK
    nt = n // tt
    per_mod = nt // shift.shape[0]
    mrows = shift.shape[1]
    if mrows == 1:
        mblk, mmap = (1, 1, D_MODEL), (lambda i: (i // per_mod, 0, 0))
    else:
        mblk, mmap = (1, tt, D_MODEL), (lambda i: (0, i, 0))
    cmap = lambda i: (0, 0)
    nrt = PEER_N_KEYS // BF16_ROWS
    hshape = jax.ShapeDtypeStruct((nt, PEER_HEADS, PEER_N_KEYS, tt), F32)
    hblk = pl.BlockSpec((1, PEER_HEADS, PEER_N_KEYS, tt), lambda i: (i, 0, 0, 0))
    nfshape = jax.ShapeDtypeStruct((nt, PEER_HEADS, nrt, 2, SUBLANES, tt), jnp.int32)
    nfblk = pl.BlockSpec((1, PEER_HEADS, nrt, 2, SUBLANES, tt), lambda i: (i, 0, 0, 0, 0, 0))
    return pl.pallas_call(
        _peer_route_kernel,
        grid=(nt,),
        in_specs=[
            pl.BlockSpec((tt, D_MODEL), lambda i: (i, 0)),
            pl.BlockSpec(mblk, mmap), pl.BlockSpec(mblk, mmap),
            pl.BlockSpec((1, D_MODEL), cmap),
            pl.BlockSpec((D_MODEL, D_MODEL), cmap),
            pl.BlockSpec((2 * PEER_HEADS, PEER_N_KEYS, PEER_HALF), lambda i: (0, 0, 0)),
            pl.BlockSpec((2 * PEER_HEADS, D_MODEL), cmap),
        ],
        out_specs=[pl.BlockSpec((D_MODEL, tt), lambda i: (0, i)), hblk, hblk, nfblk],
        out_shape=[jax.ShapeDtypeStruct((D_MODEL, n), BF16), hshape, hshape, nfshape],
        scratch_shapes=[pltpu.VMEM((2 * PEER_HEADS * PEER_N_KEYS, tt), F32), pltpu.VMEM((PEER_HEADS, 1, tt), F32)],
        compiler_params=_cparams(("parallel",)),
        name="peer_route",
    )(x2, shift, scale, gain.reshape(1, D_MODEL), wq_bf, sk_bf, ones_t)


def _gelu(x):
    return 0.5 * x * (1.0 + lax.erf(x * np.float32(math.sqrt(0.5))))


def _peer_expert_kernel(xt_ref, u_ref, un_ref, vt_ref, vp_ref, r1_ref, c1_ref, nf_ref, x_ref, gate_ref, xo_ref,
                        act_scr, w_scr, acc_scr, *, t_tile):
    e = pl.program_id(1)
    nq = PEER_EXP_TILE // PEER_CHUNK

    @pl.when(e == 0)
    def _():
        acc_scr[...] = jnp.zeros_like(acc_scr)
        w_scr[nq - 1] = jnp.zeros((PEER_CHUNK, t_tile), BF16)
        act_scr[0] = jnp.dot(u_ref[0, :PEER_CHUNK, :], xt_ref[...], preferred_element_type=F32)

    def stage_a(q):
        qs = slice(q * PEER_CHUNK, (q + 1) * PEER_CHUNK)
        act_scr[q] = jnp.dot(u_ref[0, qs, :], xt_ref[...], preferred_element_type=F32)

    def stage_a_next():
        act_scr[0] = jnp.dot(un_ref[0], xt_ref[...], preferred_element_type=F32)

    def stage_b(q):
        for i1l in range(PEER_CHUNK // PEER_N_KEYS):
            i1 = q * (PEER_CHUNK // PEER_N_KEYS) + i1l
            for lc in range(t_tile // PEER_SEL_TOK):
                ls = slice(lc * PEER_SEL_TOK, (lc + 1) * PEER_SEL_TOK)
                r1 = [jnp.broadcast_to(r1_ref[lc, h, i1:i1 + 1, :], (BF16_ROWS, PEER_SEL_TOK)).astype(BF16)
                      for h in range(PEER_HEADS)]
                c1 = [jnp.broadcast_to(c1_ref[lc, h, i1:i1 + 1, :], (BF16_ROWS, PEER_SEL_TOK)).astype(BF16)
                      for h in range(PEER_HEADS)]
                for j in range(PEER_N_KEYS // BF16_ROWS):
                    g = jnp.zeros((BF16_ROWS, PEER_SEL_TOK), BF16)
                    for h in range(PEER_HEADS):
                        n2 = pltpu.bitcast(nf_ref[lc, h, j, 0], BF16)
                        f2 = pltpu.bitcast(nf_ref[lc, h, j, 1], BF16)
                        g = g + jnp.where(n2 > r1[h], f2, 0.0) * c1[h]
                    rs = slice(i1l * PEER_N_KEYS + j * BF16_ROWS, i1l * PEER_N_KEYS + (j + 1) * BF16_ROWS)
                    w_scr[q, rs, ls] = g * _gelu(act_scr[q, rs, ls]).astype(BF16)

    def stage_c(q):
        acc_scr[...] += jnp.dot(vt_ref[0, q], w_scr[q], preferred_element_type=F32)

    def stage_c_prev():
        acc_scr[...] += jnp.dot(vp_ref[0, 0], w_scr[nq - 1], preferred_element_type=F32)

    stage_c_prev()
    for q in range(nq):
        if q + 1 < nq:
            stage_a(q + 1)
        else:
            stage_a_next()
        stage_b(q)
        if q > 0:
            stage_c(q - 1)

    @pl.when(e == pl.num_programs(1) - 1)
    def _():
        acc = acc_scr[...] + jnp.dot(vt_ref[0, nq - 1], w_scr[nq - 1], preferred_element_type=F32)
        xo_ref[...] = x_ref[...] + gate_ref[0] * acc.T


def _peer_expert(xt, u_bf, vt_bf, layer, r1, c1, nf, x2, gate, *, t_tile):
    n = x2.shape[0]
    nt = n // t_tile
    nlc = t_tile // PEER_SEL_TOK
    ne = u_bf.shape[1] // PEER_EXP_TILE
    nq = PEER_EXP_TILE // PEER_CHUNK
    per_mod = max(nt // gate.shape[0], 1)
    mrows = gate.shape[1]
    if mrows == 1:
        mblk, mmap = (1, 1, D_MODEL), (lambda i, e: (i // per_mod, 0, 0))
    else:
        mblk, mmap = (1, t_tile, D_MODEL), (lambda i, e: (0, i, 0))
    i1blk = pl.BlockSpec((nlc, PEER_HEADS, PEER_I1_TILE, PEER_SEL_TOK), lambda i, e: (i, 0, e, 0))
    nfblk = pl.BlockSpec((nlc, PEER_HEADS, PEER_N_KEYS // BF16_ROWS, 2, SUBLANES, PEER_SEL_TOK),
                         lambda i, e: (i, 0, 0, 0, 0, 0))
    kern = functools.partial(_peer_expert_kernel, t_tile=t_tile)
    return pl.pallas_call(
        kern,
        grid=(nt, ne),
        in_specs=[
            pl.BlockSpec((D_MODEL, t_tile), lambda i, e: (0, i)),
            pl.BlockSpec((1, PEER_EXP_TILE, D_MODEL), lambda i, e: (layer, e, 0)),
            pl.BlockSpec((1, PEER_CHUNK, D_MODEL), lambda i, e: (layer, jnp.minimum(e + 1, ne - 1) * nq, 0)),
            pl.BlockSpec((1, nq, D_MODEL, PEER_CHUNK), lambda i, e: (layer, e, 0, 0)),
            pl.BlockSpec((1, 1, D_MODEL, PEER_CHUNK), lambda i, e: (layer, jnp.maximum(e, 1) * nq - 1, 0, 0)),
            i1blk, i1blk, nfblk,
            pl.BlockSpec((t_tile, D_MODEL), lambda i, e: (i, 0)),
            pl.BlockSpec(mblk, mmap),
        ],
        out_specs=pl.BlockSpec((t_tile, D_MODEL), lambda i, e: (i, 0)),
        out_shape=jax.ShapeDtypeStruct(x2.shape, F32),
        scratch_shapes=[pltpu.VMEM((PEER_EXP_TILE // PEER_CHUNK, PEER_CHUNK, t_tile), F32),
                        pltpu.VMEM((PEER_EXP_TILE // PEER_CHUNK, PEER_CHUNK, t_tile), BF16),
                        pltpu.VMEM((D_MODEL, t_tile), F32)],
        compiler_params=_cparams(("parallel", "arbitrary")),
        name="peer_expert",
    )(xt, u_bf, u_bf, vt_bf, vt_bf, r1, c1, nf, x2, gate)


def _peer(x2, shift, scale, gate, gain, pw, tables, layer, *, t_tile):
    wq_bf, sk_bf, ones_t = pw
    u_bf, vt_bf = tables
    xt, r1, c1, nf = _peer_route(x2, shift, scale, gain, wq_bf, sk_bf, ones_t)
    return _peer_expert(xt, u_bf, vt_bf, layer, r1, c1, nf, x2, gate, t_tile=t_tile)


def _peer_weights(w_q, sub_keys):
    hc = 2 * PEER_HEADS
    seg = np.arange(D_MODEL) // PEER_HALF
    ones_t = jnp.asarray((seg[None, :] == np.arange(hc)[:, None]).astype(np.float32), dtype=BF16)
    return (w_q.astype(BF16), sub_keys.reshape(hc, PEER_N_KEYS, PEER_HALF).astype(BF16), ones_t)


def kernel(x_prompt, x_sample, c_prompt, c_sample, state_ssm, cache_kv_w128, cache_kv_w512, cache_kv_w2048,
           norm_mix, norm_ffn, ada_w, ada_b, ssm_lam_re, ssm_lam_im, ssm_log_dt, ssm_b_re, ssm_b_im, ssm_c_re,
           ssm_c_im, ssm_d, ssm_w_glu, attn_w_qkv, attn_q_norm, attn_k_norm, attn_w_o, peer_w_q,
           peer_sub_keys, peer_u, peer_v):
    bp, sp, _ = x_prompt.shape
    bs, ts, _ = x_sample.shape
    ns = bs * ts
    ones_att = _ones_blockdiag(ATT)

    mods = _adaln(jnp.concatenate([c_prompt, c_sample], axis=0), ada_w, ada_b)
    caches = (cache_kv_w128, cache_kv_w512, cache_kv_w2048)
    tab_p = _rope_tables(jnp.arange(sp, dtype=jnp.int32))
    pos_s = PAST_LEN + jnp.arange(ts, dtype=jnp.int32)
    tab_s = tuple(jnp.tile(t, (bs, 1)) for t in _rope_tables(pos_s))

    n_chunks = peer_v.shape[1] // PEER_CHUNK
    peer_tabs = (peer_u.astype(BF16),
                 peer_v.astype(BF16).reshape(DEPTH, n_chunks, PEER_CHUNK, D_MODEL).transpose(0, 1, 3, 2))
    xp, xs = x_prompt, x_sample
    ssm_p, ssm_s = [], []
    kv_p = [[] for _ in range(N_GROUPS)]
    kv_s = [[] for _ in range(N_GROUPS)]
    for i in range(DEPTH):
        j = i // 2
        mp = [m.reshape(bp, 1, D_MODEL) for m in jnp.split(mods[i, :bp], 6, axis=-1)]
        ms_b = jnp.split(mods[i, bp:], 6, axis=-1)
        ms_rows = [jnp.repeat(m, ts, axis=0).reshape(1, ns, D_MODEL) for m in ms_b]
        if i % 2 == 0:
            sw = _s5_weights(ssm_lam_re[j], ssm_lam_im[j], ssm_log_dt[j], ssm_b_re[j], ssm_b_im[j],
                             ssm_c_re[j], ssm_c_im[j])
            wglu = ssm_w_glu[j].astype(BF16)
            mp_t = [m.reshape(1, bp, D_MODEL) for m in mp[:3]]
            xp_t, st_p = _s5_layer(xp.transpose(1, 0, 2), mp_t[0], mp_t[1], mp_t[2], norm_mix[i],
                                   jnp.zeros((bp, SSM_STATE_W), F32), sw, ssm_d[j], wglu, t_chunk=32)
            xp = xp_t.transpose(1, 0, 2)
            ms_t = [m.reshape(1, bs, D_MODEL) for m in ms_b[:3]]
            xs_t, st_s = _s5_layer(xs.transpose(1, 0, 2), ms_t[0], ms_t[1], ms_t[2], norm_mix[i],
                                   _state_to_rows(state_ssm[j]), sw, ssm_d[j], wglu, t_chunk=ts)
            xs = xs_t.transpose(1, 0, 2)
            ssm_p.append(_rows_to_state(st_p))
            ssm_s.append(_rows_to_state(st_s))
        else:
            wqkv = attn_w_qkv[j].astype(BF16)
            wo = attn_w_o[j].astype(BF16)
            k, v, q0, q1, k1, v1, q2, k2, v2 = _qkv_dilated(xp, mp[0], mp[1], norm_mix[i], wqkv, attn_q_norm[j],
                                                            attn_k_norm[j], tab_p, ones_att, t_tile=512)
            res = [_band_attention(q0, k, v, 1, N_GROUPS, 0, n_sub=2, n_col=1),
                   _band_attention(q1, k1, v1, DILATIONS[1], 1, 0, n_sub=2, n_col=1),
                   _band_attention(q2, k2, v2, DILATIONS[2], 1, 0, n_sub=1, n_col=2)]
            outs, lses = zip(*res)
            xp = _merge(outs, lses, xp, mp[2], wo, t_tile=512, dils=DILATIONS)
            for g in range(N_GROUPS):
                keep = min(WINDOWS[g], sp)
                gs = slice(g * ATT, (g + 1) * ATT)
                kv_p[g].append(jnp.stack([k[:, sp - keep:, gs], v[:, sp - keep:, gs]], axis=1)
                               .reshape(bp, 2, keep, N_HEADS, HEAD_DIM))
            xs_rows = xs.reshape(1, ns, D_MODEL)
            qs, ks, vs = _qkv(xs_rows, ms_rows[0], ms_rows[1], norm_mix[i], wqkv, attn_q_norm[j],
                              attn_k_norm[j], tab_s, ones_att, t_tile=ns)
            qs, ks, vs = (t.reshape(bs, ts, N_GROUPS * ATT) for t in (qs, ks, vs))
            res = _cache_attention(qs, ks, vs, [c.reshape(c.shape[:4] + (ATT,)) for c in caches], j)
            outs_s = [r.reshape(1, ns, ATT) for r in res[:N_GROUPS]]
            lses_s = [r.reshape(1, ns, ATT) for r in res[N_GROUPS:]]
            xs = _merge(outs_s, lses_s, xs_rows, ms_rows[2], wo, t_tile=ns,
                        dils=(1, 1, 1)).reshape(bs, ts, D_MODEL)
            for g in range(N_GROUPS):
                gs = slice(g * ATT, (g + 1) * ATT)
                kv_s[g].append(jnp.stack([ks[:, :, gs], vs[:, :, gs]], axis=1)
                               .reshape(bs, 2, ts, N_HEADS, HEAD_DIM))
        pw = _peer_weights(peer_w_q[i], peer_sub_keys[i])
        xp = _peer(xp.reshape(bp * sp, D_MODEL), mp[3], mp[4], mp[5], norm_ffn[i], pw, peer_tabs, i,
                   t_tile=512).reshape(bp, sp, D_MODEL)
        xs = _peer(xs.reshape(ns, D_MODEL), ms_rows[3], ms_rows[4], ms_rows[5], norm_ffn[i], pw, peer_tabs, i,
                   t_tile=ns).reshape(bs, ts, D_MODEL)
    return (xp, xs,
            jnp.stack(ssm_p), jnp.stack(kv_p[0]), jnp.stack(kv_p[1]), jnp.stack(kv_p[2]),
            jnp.stack(ssm_s), jnp.stack(kv_s[0]), jnp.stack(kv_s[1]), jnp.stack(kv_s[2]))
```
